```python
import math
import jax, jax.numpy as jnp
from jax import lax
import numpy as np

D_MODEL = 2048
BATCH = 8
SEQ = 4096
DEPTH = 1

N_META = 16
CHUNK = 64
NORM_EPS = 1e-6
DN_HEADS = D_MODEL // 128
DN_DK = 128
DN_DV = 128
DN_W = DN_HEADS * DN_DV
DN_CONV = 4
M2_HEAD_DIM = 64
M2_HEADS = D_MODEL // M2_HEAD_DIM
M2_GROUPS = 4
M2_HPG = M2_HEADS // M2_GROUPS
M2_STATE = 128
M2_W = M2_HEADS * M2_HEAD_DIM
M2_CONV = 4
D_MIX = DN_W + M2_W
D_IN_PROJ = 4 * DN_W + 2 * DN_HEADS + 2 * M2_W + 2 * M2_GROUPS * M2_STATE + M2_HEADS
D_FF = 11 * D_MODEL // 4
FFN_CONV = 3

kernel_name = "hybrid_gdn_mamba2_convffn_meta"


def rms_norm(x, w):
    xf = x.astype(jnp.float32)
    y = xf * lax.rsqrt(jnp.mean(xf * xf, axis=-1, keepdims=True) + NORM_EPS)
    return (y * w.astype(jnp.float32)).astype(x.dtype)


def causal_dwconv(x, w):
    k, c = w.shape
    return lax.conv_general_dilated(x, w[:, None, :].astype(x.dtype), window_strides=(1,),
                                    padding=[(k - 1, 0)],
                                    dimension_numbers=('NWC', 'WIO', 'NWC'),
                                    feature_group_count=c)


def to_chunks(t):
    pad = [(0, 0), (CHUNK - N_META, 0)] + [(0, 0)] * (t.ndim - 2)
    t = jnp.pad(t, pad)
    b, lp = t.shape[:2]
    t = t.reshape((b, lp // CHUNK, CHUNK) + t.shape[2:])
    return jnp.moveaxis(t, 1, 0)


def from_chunks(t):
    t = jnp.moveaxis(t, 0, 1)
    t = t.reshape((t.shape[0], t.shape[1] * t.shape[2]) + t.shape[3:])
    return t[:, CHUNK - N_META:]


def gated_delta_rule(q, k, v, beta, g):
    causal = jnp.tril(jnp.ones((CHUNK, CHUNK), dtype=bool))
    strict = jnp.tril(jnp.ones((CHUNK, CHUNK), dtype=bool), -1)

    def step(state, inp):
        qc, kc, vc, bc, gc = inp
        gcum = jnp.cumsum(gc, axis=1)
        gh = jnp.swapaxes(gcum, 1, 2)
        decay = jnp.exp(jnp.where(causal, gh[..., :, None] - gh[..., None, :], -jnp.inf))
        kk = jnp.einsum('blhd,bshd->bhls', kc, kc)
        bh = jnp.swapaxes(bc, 1, 2)
        a_mat = jnp.where(strict, bh[..., :, None] * kk * decay, 0.0)
        rhs = jnp.concatenate([vc * bc[..., None], kc * (bc * jnp.exp(gcum))[..., None]], axis=-1)
        rhs = jnp.swapaxes(rhs, 1, 2)
        sol = lax.linalg.triangular_solve(a_mat, rhs, left_side=True, lower=True,
                                          unit_diagonal=True)
        u, w = sol[..., :DN_DV], sol[..., DN_DV:]
        v_new = u - jnp.einsum('bhlk,bhkv->bhlv', w, state)
        o_inter = jnp.einsum('blhk,bhkv->bhlv', qc * jnp.exp(gcum)[..., None], state)
        qk = jnp.einsum('blhd,bshd->bhls', qc, kc) * decay
        o = o_inter + jnp.einsum('bhls,bhsv->bhlv', qk, v_new)
        g_last = gcum[:, -1:, :]
        state = (state * jnp.exp(g_last[:, 0])[..., None, None]
                 + jnp.einsum('bshk,bhsv->bhkv', kc * jnp.exp(g_last - gcum)[..., None], v_new))
        return state, jnp.swapaxes(o, 1, 2)

    b = q.shape[0]
    s0 = jnp.zeros((b, DN_HEADS, DN_DK, DN_DV), jnp.float32)
    _, ys = lax.scan(step, s0, (to_chunks(q), to_chunks(k), to_chunks(v),
                                to_chunks(beta), to_chunks(g)))
    return from_chunks(ys)


def ssd_chunked(xs, dt, a, bm, cm):
    causal = jnp.tril(jnp.ones((CHUNK, CHUNK), dtype=bool))[None, :, :, None, None]

    def step(state, inp):
        xc, dtc, ac, bc, cc = inp
        acs = jnp.cumsum(ac, axis=1)
        lmat = jnp.exp(jnp.where(causal, acs[:, :, None] - acs[:, None, :], -jnp.inf))
        xdt = xc * dtc[..., None]
        cb = jnp.einsum('blgn,bsgn->blsg', cc, bc)
        y_diag = jnp.einsum('blsg,blsgr,bsgrp->blgrp', cb, lmat, xdt)
        y_off = jnp.einsum('blgn,bgrpn->blgrp', cc, state) * jnp.exp(acs)[..., None]
        a_last = acs[:, -1]
        state = (state * jnp.exp(a_last)[..., None, None]
                 + jnp.einsum('bsgn,bsgr,bsgrp->bgrpn', bc, jnp.exp(a_last[:, None] - acs), xdt))
        return state, y_diag + y_off

    b = xs.shape[0]
    s0 = jnp.zeros((b, M2_GROUPS, M2_HPG, M2_HEAD_DIM, M2_STATE), jnp.float32)
    _, ys = lax.scan(step, s0, (to_chunks(xs), to_chunks(dt), to_chunks(a),
                                to_chunks(bm), to_chunks(cm)))
    return from_chunks(ys)


def hybrid_mixer(h, w_in, dn_conv_w, dn_a_log, dn_dt_bias, dn_norm_w,
                 m2_conv_w, m2_conv_b, m2_a_log, m2_dt_bias, m2_d, m2_norm_w, w_out):
    b, l, _ = h.shape
    f32 = jnp.float32
    proj = h @ w_in
    sizes = [3 * DN_W, DN_W, DN_HEADS, DN_HEADS, M2_W, M2_W + 2 * M2_GROUPS * M2_STATE]
    dn_qkv, dn_z, dn_b, dn_a, m2_z, m2_xbc, m2_dt = jnp.split(proj, list(np.cumsum(sizes)), axis=-1)

    qkv = jax.nn.silu(causal_dwconv(dn_qkv, dn_conv_w)).astype(f32)
    q, k, v = jnp.split(qkv, 3, axis=-1)
    q = q.reshape(b, l, DN_HEADS, DN_DK)
    k = k.reshape(b, l, DN_HEADS, DN_DK)
    v = v.reshape(b, l, DN_HEADS, DN_DV)
    q = q * lax.rsqrt(jnp.sum(q * q, -1, keepdims=True) + NORM_EPS) * (DN_DK ** -0.5)
    k = k * lax.rsqrt(jnp.sum(k * k, -1, keepdims=True) + NORM_EPS)
    beta = jax.nn.sigmoid(dn_b.astype(f32))
    g = -jnp.exp(dn_a_log.astype(f32)) * jax.nn.softplus(dn_a.astype(f32) + dn_dt_bias.astype(f32))
    o = gated_delta_rule(q, k, v, beta, g)
    z = dn_z.astype(f32).reshape(b, l, DN_HEADS, DN_DV)
    o = (o * lax.rsqrt(jnp.mean(o * o, -1, keepdims=True) + NORM_EPS)
         * dn_norm_w.astype(f32) * jax.nn.silu(z)).reshape(b, l, DN_W)

    xbc = jax.nn.silu(causal_dwconv(m2_xbc, m2_conv_w) + m2_conv_b.astype(h.dtype)).astype(f32)
    xs, bm, cm = jnp.split(xbc, [M2_W, M2_W + M2_GROUPS * M2_STATE], axis=-1)
    xs = xs.reshape(b, l, M2_GROUPS, M2_HPG, M2_HEAD_DIM)
    bm = bm.reshape(b, l, M2_GROUPS, M2_STATE)
    cm = cm.reshape(b, l, M2_GROUPS, M2_STATE)
    dt = jax.nn.softplus(m2_dt.astype(f32) + m2_dt_bias.astype(f32)).reshape(b, l, M2_GROUPS, M2_HPG)
    a_head = (-jnp.exp(m2_a_log.astype(f32))).reshape(M2_GROUPS, M2_HPG)
    y = ssd_chunked(xs, dt, dt * a_head, bm, cm)
    y = y + m2_d.astype(f32).reshape(M2_GROUPS, M2_HPG)[:, :, None] * xs
    y = y.reshape(b, l, M2_W) * jax.nn.silu(m2_z.astype(f32))
    y = y.reshape(b, l, M2_GROUPS, M2_W // M2_GROUPS)
    y = (y * lax.rsqrt(jnp.mean(y * y, -1, keepdims=True) + NORM_EPS)).reshape(b, l, M2_W)
    y = y * m2_norm_w.astype(f32)

    mixed = jnp.concatenate([o, y], axis=-1).astype(h.dtype)
    return mixed @ w_out


def conv_ffn(h, w_up, conv_w, w_down):
    u = causal_dwconv(h @ w_up, conv_w)
    gate, val = jnp.split(u, 2, axis=-1)
    return (jax.nn.silu(gate) * val) @ w_down


def _fwd_setup_inputs(seed: int = 0) -> dict:
    key = jax.random.key(seed)
    ks = jax.random.split(key, 24)
    nrm = jax.random.normal

    def dt_bias(k, n):
        dt = jnp.exp(jax.random.uniform(k, (DEPTH, n), minval=math.log(1e-3), maxval=math.log(1e-1)))
        return dt + jnp.log(-jnp.expm1(-dt))

    def a_log(k, n):
        return jnp.log(jax.random.uniform(k, (DEPTH, n), minval=1.0, maxval=16.0))

    return {
        "x": nrm(ks[0], (BATCH, SEQ, D_MODEL), jnp.float32),
        "meta_tokens": nrm(ks[1], (N_META, D_MODEL), jnp.float32),
        "norm_mix_w": 1.0 + 0.02 * nrm(ks[2], (DEPTH, D_MODEL), jnp.float32),
        "w_in": nrm(ks[3], (DEPTH, D_MODEL, D_IN_PROJ), jnp.float32) * D_MODEL ** -0.5,
        "dn_conv_w": nrm(ks[4], (DEPTH, DN_CONV, 3 * DN_W), jnp.float32) * DN_CONV ** -0.5,
        "dn_a_log": a_log(ks[5], DN_HEADS),
        "dn_dt_bias": dt_bias(ks[6], DN_HEADS),
        "dn_norm_w": 1.0 + 0.02 * nrm(ks[7], (DEPTH, DN_DV), jnp.float32),
        "m2_conv_w": nrm(ks[8], (DEPTH, M2_CONV, M2_W + 2 * M2_GROUPS * M2_STATE), jnp.float32) * M2_CONV ** -0.5,
        "m2_conv_b": 0.02 * nrm(ks[9], (DEPTH, M2_W + 2 * M2_GROUPS * M2_STATE), jnp.float32),
        "m2_a_log": a_log(ks[10], M2_HEADS),
        "m2_dt_bias": dt_bias(ks[11], M2_HEADS),
        "m2_d": 1.0 + 0.1 * nrm(ks[12], (DEPTH, M2_HEADS), jnp.float32),
        "m2_norm_w": 1.0 + 0.02 * nrm(ks[13], (DEPTH, M2_W), jnp.float32),
        "w_out": nrm(ks[14], (DEPTH, D_MIX, D_MODEL), jnp.float32) * D_MIX ** -0.5,
        "norm_ffn_w": 1.0 + 0.02 * nrm(ks[15], (DEPTH, D_MODEL), jnp.float32),
        "ffn_up": nrm(ks[16], (DEPTH, D_MODEL, 2 * D_FF), jnp.float32) * D_MODEL ** -0.5,
        "ffn_conv_w": nrm(ks[17], (DEPTH, FFN_CONV, 2 * D_FF), jnp.float32) * FFN_CONV ** -0.5,
        "ffn_down": nrm(ks[18], (DEPTH, D_FF, D_MODEL), jnp.float32) * D_FF ** -0.5,
        "norm_final_w": 1.0 + 0.02 * nrm(ks[19], (D_MODEL,), jnp.float32),
    }


def _fwd_reference(x, meta_tokens, norm_mix_w, w_in, dn_conv_w, dn_a_log, dn_dt_bias, dn_norm_w,
              m2_conv_w, m2_conv_b, m2_a_log, m2_dt_bias, m2_d, m2_norm_w, w_out,
              norm_ffn_w, ffn_up, ffn_conv_w, ffn_down, norm_final_w):
    b = x.shape[0]
    meta = jnp.broadcast_to(meta_tokens[None].astype(x.dtype), (b, N_META, D_MODEL))
    h = jnp.concatenate([meta, x], axis=1)
    for layer in range(DEPTH):
        h = h + hybrid_mixer(rms_norm(h, norm_mix_w[layer]), w_in[layer], dn_conv_w[layer],
                             dn_a_log[layer], dn_dt_bias[layer], dn_norm_w[layer],
                             m2_conv_w[layer], m2_conv_b[layer], m2_a_log[layer],
                             m2_dt_bias[layer], m2_d[layer], m2_norm_w[layer], w_out[layer])
        h = h + conv_ffn(rms_norm(h, norm_ffn_w[layer]), ffn_up[layer], ffn_conv_w[layer],
                         ffn_down[layer])
    return rms_norm(h, norm_final_w)[:, N_META:]


import jax as _jax
import jax.numpy as _jnp

TWIN_FORMAT = 'train_step'
FWD_PARAMS = ['x', 'meta_tokens', 'norm_mix_w', 'w_in', 'dn_conv_w', 'dn_a_log', 'dn_dt_bias', 'dn_norm_w', 'm2_conv_w', 'm2_conv_b', 'm2_a_log', 'm2_dt_bias', 'm2_d', 'm2_norm_w', 'w_out', 'norm_ffn_w', 'ffn_up', 'ffn_conv_w', 'ffn_down', 'norm_final_w']
TWIN_WEIGHTS = ['meta_tokens', 'norm_mix_w', 'w_in', 'dn_conv_w', 'dn_a_log', 'dn_dt_bias', 'dn_norm_w', 'm2_conv_w', 'm2_conv_b', 'm2_a_log', 'm2_dt_bias', 'm2_d', 'm2_norm_w', 'w_out', 'norm_ffn_w', 'ffn_up', 'ffn_conv_w', 'ffn_down', 'norm_final_w']
TWIN_DIFF_INPUT = 'x'
TWIN_INPUTS = ['x', 'meta_tokens', 'norm_mix_w', 'w_in', 'dn_conv_w', 'dn_a_log', 'dn_dt_bias', 'dn_norm_w', 'm2_conv_w', 'm2_conv_b', 'm2_a_log', 'm2_dt_bias', 'm2_d', 'm2_norm_w', 'w_out', 'norm_ffn_w', 'ffn_up', 'ffn_conv_w', 'ffn_down', 'norm_final_w', 'loss_target', 'm_meta_tokens', 'm_norm_mix_w', 'm_w_in', 'm_dn_conv_w', 'm_dn_a_log', 'm_dn_dt_bias', 'm_dn_norm_w', 'm_m2_conv_w', 'm_m2_conv_b', 'm_m2_a_log', 'm_m2_dt_bias', 'm_m2_d', 'm_m2_norm_w', 'm_w_out', 'm_norm_ffn_w', 'm_ffn_up', 'm_ffn_conv_w', 'm_ffn_down', 'm_norm_final_w', 'v_meta_tokens', 'v_norm_mix_w', 'v_w_in', 'v_dn_conv_w', 'v_dn_a_log', 'v_dn_dt_bias', 'v_dn_norm_w', 'v_m2_conv_w', 'v_m2_conv_b', 'v_m2_a_log', 'v_m2_dt_bias', 'v_m2_d', 'v_m2_norm_w', 'v_w_out', 'v_norm_ffn_w', 'v_ffn_up', 'v_ffn_conv_w', 'v_ffn_down', 'v_norm_final_w']
TWIN_OUTPUTS = ['loss', 'grad_x', 'grad_meta_tokens', 'grad_norm_mix_w', 'grad_w_in', 'grad_dn_conv_w', 'grad_dn_a_log', 'grad_dn_dt_bias', 'grad_dn_norm_w', 'grad_m2_conv_w', 'grad_m2_conv_b', 'grad_m2_a_log', 'grad_m2_dt_bias', 'grad_m2_d', 'grad_m2_norm_w', 'grad_w_out', 'grad_norm_ffn_w', 'grad_ffn_up', 'grad_ffn_conv_w', 'grad_ffn_down', 'grad_norm_final_w', 'delta_meta_tokens', 'delta_norm_mix_w', 'delta_w_in', 'delta_dn_conv_w', 'delta_dn_a_log', 'delta_dn_dt_bias', 'delta_dn_norm_w', 'delta_m2_conv_w', 'delta_m2_conv_b', 'delta_m2_a_log', 'delta_m2_dt_bias', 'delta_m2_d', 'delta_m2_norm_w', 'delta_w_out', 'delta_norm_ffn_w', 'delta_ffn_up', 'delta_ffn_conv_w', 'delta_ffn_down', 'delta_norm_final_w', 'new_m_meta_tokens', 'new_m_norm_mix_w', 'new_m_w_in', 'new_m_dn_conv_w', 'new_m_dn_a_log', 'new_m_dn_dt_bias', 'new_m_dn_norm_w', 'new_m_m2_conv_w', 'new_m_m2_conv_b', 'new_m_m2_a_log', 'new_m_m2_dt_bias', 'new_m_m2_d', 'new_m_m2_norm_w', 'new_m_w_out', 'new_m_norm_ffn_w', 'new_m_ffn_up', 'new_m_ffn_conv_w', 'new_m_ffn_down', 'new_m_norm_final_w', 'new_v_meta_tokens', 'new_v_norm_mix_w', 'new_v_w_in', 'new_v_dn_conv_w', 'new_v_dn_a_log', 'new_v_dn_dt_bias', 'new_v_dn_norm_w', 'new_v_m2_conv_w', 'new_v_m2_conv_b', 'new_v_m2_a_log', 'new_v_m2_dt_bias', 'new_v_m2_d', 'new_v_m2_norm_w', 'new_v_w_out', 'new_v_norm_ffn_w', 'new_v_ffn_up', 'new_v_ffn_conv_w', 'new_v_ffn_down', 'new_v_norm_final_w']
TWIN_LEAF_KINDS = {'loss': 'loss', 'grad_x': 'grad_x', 'grad_meta_tokens': 'grad_w', 'grad_norm_mix_w': 'grad_w', 'grad_w_in': 'grad_w', 'grad_dn_conv_w': 'grad_w', 'grad_dn_a_log': 'grad_w', 'grad_dn_dt_bias': 'grad_w', 'grad_dn_norm_w': 'grad_w', 'grad_m2_conv_w': 'grad_w', 'grad_m2_conv_b': 'grad_w', 'grad_m2_a_log': 'grad_w', 'grad_m2_dt_bias': 'grad_w', 'grad_m2_d': 'grad_w', 'grad_m2_norm_w': 'grad_w', 'grad_w_out': 'grad_w', 'grad_norm_ffn_w': 'grad_w', 'grad_ffn_up': 'grad_w', 'grad_ffn_conv_w': 'grad_w', 'grad_ffn_down': 'grad_w', 'grad_norm_final_w': 'grad_w', 'delta_meta_tokens': 'delta_w', 'delta_norm_mix_w': 'delta_w', 'delta_w_in': 'delta_w', 'delta_dn_conv_w': 'delta_w', 'delta_dn_a_log': 'delta_w', 'delta_dn_dt_bias': 'delta_w', 'delta_dn_norm_w': 'delta_w', 'delta_m2_conv_w': 'delta_w', 'delta_m2_conv_b': 'delta_w', 'delta_m2_a_log': 'delta_w', 'delta_m2_dt_bias': 'delta_w', 'delta_m2_d': 'delta_w', 'delta_m2_norm_w': 'delta_w', 'delta_w_out': 'delta_w', 'delta_norm_ffn_w': 'delta_w', 'delta_ffn_up': 'delta_w', 'delta_ffn_conv_w': 'delta_w', 'delta_ffn_down': 'delta_w', 'delta_norm_final_w': 'delta_w', 'new_m_meta_tokens': 'new_m', 'new_m_norm_mix_w': 'new_m', 'new_m_w_in': 'new_m', 'new_m_dn_conv_w': 'new_m', 'new_m_dn_a_log': 'new_m', 'new_m_dn_dt_bias': 'new_m', 'new_m_dn_norm_w': 'new_m', 'new_m_m2_conv_w': 'new_m', 'new_m_m2_conv_b': 'new_m', 'new_m_m2_a_log': 'new_m', 'new_m_m2_dt_bias': 'new_m', 'new_m_m2_d': 'new_m', 'new_m_m2_norm_w': 'new_m', 'new_m_w_out': 'new_m', 'new_m_norm_ffn_w': 'new_m', 'new_m_ffn_up': 'new_m', 'new_m_ffn_conv_w': 'new_m', 'new_m_ffn_down': 'new_m', 'new_m_norm_final_w': 'new_m', 'new_v_meta_tokens': 'new_v', 'new_v_norm_mix_w': 'new_v', 'new_v_w_in': 'new_v', 'new_v_dn_conv_w': 'new_v', 'new_v_dn_a_log': 'new_v', 'new_v_dn_dt_bias': 'new_v', 'new_v_dn_norm_w': 'new_v', 'new_v_m2_conv_w': 'new_v', 'new_v_m2_conv_b': 'new_v', 'new_v_m2_a_log': 'new_v', 'new_v_m2_dt_bias': 'new_v', 'new_v_m2_d': 'new_v', 'new_v_m2_norm_w': 'new_v', 'new_v_w_out': 'new_v', 'new_v_norm_ffn_w': 'new_v', 'new_v_ffn_up': 'new_v', 'new_v_ffn_conv_w': 'new_v', 'new_v_ffn_down': 'new_v', 'new_v_norm_final_w': 'new_v'}


def _forward(args):
    return _fwd_reference(*[args[k] for k in FWD_PARAMS])


def _output_shape():
    out = _jax.eval_shape(lambda: _forward(_fwd_setup_inputs(0)))
    return out.shape, out.dtype

N_MICROBATCH = 1
ADAM_LR = 0.001
ADAM_B1 = 0.9
ADAM_B2 = 0.999
ADAM_EPS = 1e-08
ADAM_WD = 0.01
ADAM_STEP = 10
PER_EXAMPLE_BATCH_AXIS = {'x': 0, 'loss_target': 0}
SHARED_INPUTS = []
_WEIGHT_DTYPES = {'meta_tokens': _jnp.float32, 'norm_mix_w': _jnp.float32, 'w_in': _jnp.float32, 'dn_conv_w': _jnp.float32, 'dn_a_log': _jnp.float32, 'dn_dt_bias': _jnp.float32, 'dn_norm_w': _jnp.float32, 'm2_conv_w': _jnp.float32, 'm2_conv_b': _jnp.float32, 'm2_a_log': _jnp.float32, 'm2_dt_bias': _jnp.float32, 'm2_d': _jnp.float32, 'm2_norm_w': _jnp.float32, 'w_out': _jnp.float32, 'norm_ffn_w': _jnp.float32, 'ffn_up': _jnp.float32, 'ffn_conv_w': _jnp.float32, 'ffn_down': _jnp.float32, 'norm_final_w': _jnp.float32}
MOMENT_SCALE = {'meta_tokens': 2.157219e-03, 'norm_mix_w': 9.715325e-02, 'w_in': 3.703126e-02, 'dn_conv_w': 2.386173e-02, 'dn_a_log': 1.553616e-01, 'dn_dt_bias': 1.477903e-01, 'dn_norm_w': 1.353520e-01, 'm2_conv_w': 4.733171e-02, 'm2_conv_b': 6.103589e-02, 'm2_a_log': 1.502275e-01, 'm2_dt_bias': 1.012813e-01, 'm2_d': 5.491286e-01, 'm2_norm_w': 5.482815e-02, 'w_out': 6.159232e-02, 'norm_ffn_w': 5.491568e-02, 'ffn_up': 2.321660e-02, 'ffn_conv_w': 2.386427e-02, 'ffn_down': 3.792685e-02, 'norm_final_w': 1.599403e+01}


def _to_microbatches(a, axis):
    t = _jnp.moveaxis(a, axis, 0)
    t = t.reshape((N_MICROBATCH, t.shape[0] // N_MICROBATCH) + t.shape[1:])
    return _jnp.moveaxis(t, 1, axis + 1)


def setup_inputs(seed: int = 0) -> dict:
    inp = _fwd_setup_inputs(seed)
    key = _jax.random.fold_in(_jax.random.key(seed), 7919)
    shape, _ = _output_shape()
    out = dict(inp)
    out["loss_target"] = _jax.random.normal(_jax.random.fold_in(key, 0), shape, _jnp.float32)
    for i, name in enumerate(TWIN_WEIGHTS):
        w = inp[name].astype(_jnp.float32)
        if MOMENT_SCALE is None:
            s = _jnp.sqrt(_jnp.mean(_jnp.square(w)) + 1e-30)
        else:
            s = MOMENT_SCALE[name]
        km, kv = _jax.random.split(_jax.random.fold_in(key, i + 1))
        out[name] = w
        out["m_" + name] = s * _jax.random.normal(km, w.shape, _jnp.float32)
        out["v_" + name] = (s * s) * _jax.random.uniform(kv, w.shape, _jnp.float32, 0.5, 1.5)
    if N_MICROBATCH > 1:
        for name, axis in PER_EXAMPLE_BATCH_AXIS.items():
            out[name] = _to_microbatches(out[name], axis)
    return {'x': out['x'], 'meta_tokens': out['meta_tokens'], 'norm_mix_w': out['norm_mix_w'], 'w_in': out['w_in'], 'dn_conv_w': out['dn_conv_w'], 'dn_a_log': out['dn_a_log'], 'dn_dt_bias': out['dn_dt_bias'], 'dn_norm_w': out['dn_norm_w'], 'm2_conv_w': out['m2_conv_w'], 'm2_conv_b': out['m2_conv_b'], 'm2_a_log': out['m2_a_log'], 'm2_dt_bias': out['m2_dt_bias'], 'm2_d': out['m2_d'], 'm2_norm_w': out['m2_norm_w'], 'w_out': out['w_out'], 'norm_ffn_w': out['norm_ffn_w'], 'ffn_up': out['ffn_up'], 'ffn_conv_w': out['ffn_conv_w'], 'ffn_down': out['ffn_down'], 'norm_final_w': out['norm_final_w'], 'loss_target': out['loss_target'], 'm_meta_tokens': out['m_meta_tokens'], 'm_norm_mix_w': out['m_norm_mix_w'], 'm_w_in': out['m_w_in'], 'm_dn_conv_w': out['m_dn_conv_w'], 'm_dn_a_log': out['m_dn_a_log'], 'm_dn_dt_bias': out['m_dn_dt_bias'], 'm_dn_norm_w': out['m_dn_norm_w'], 'm_m2_conv_w': out['m_m2_conv_w'], 'm_m2_conv_b': out['m_m2_conv_b'], 'm_m2_a_log': out['m_m2_a_log'], 'm_m2_dt_bias': out['m_m2_dt_bias'], 'm_m2_d': out['m_m2_d'], 'm_m2_norm_w': out['m_m2_norm_w'], 'm_w_out': out['m_w_out'], 'm_norm_ffn_w': out['m_norm_ffn_w'], 'm_ffn_up': out['m_ffn_up'], 'm_ffn_conv_w': out['m_ffn_conv_w'], 'm_ffn_down': out['m_ffn_down'], 'm_norm_final_w': out['m_norm_final_w'], 'v_meta_tokens': out['v_meta_tokens'], 'v_norm_mix_w': out['v_norm_mix_w'], 'v_w_in': out['v_w_in'], 'v_dn_conv_w': out['v_dn_conv_w'], 'v_dn_a_log': out['v_dn_a_log'], 'v_dn_dt_bias': out['v_dn_dt_bias'], 'v_dn_norm_w': out['v_dn_norm_w'], 'v_m2_conv_w': out['v_m2_conv_w'], 'v_m2_conv_b': out['v_m2_conv_b'], 'v_m2_a_log': out['v_m2_a_log'], 'v_m2_dt_bias': out['v_m2_dt_bias'], 'v_m2_d': out['v_m2_d'], 'v_m2_norm_w': out['v_m2_norm_w'], 'v_w_out': out['v_w_out'], 'v_norm_ffn_w': out['v_norm_ffn_w'], 'v_ffn_up': out['v_ffn_up'], 'v_ffn_conv_w': out['v_ffn_conv_w'], 'v_ffn_down': out['v_ffn_down'], 'v_norm_final_w': out['v_norm_final_w']}


def _loss(weights, diff, rest, loss_target):
    with _jax.named_scope("forward"):
        args = {**rest, TWIN_DIFF_INPUT: diff, **{k: w.astype(_WEIGHT_DTYPES[k]) for k, w in weights.items()}}
        y = _forward(args)
    with _jax.named_scope("loss_head"):
        err = _jnp.square(y.astype(_jnp.float32) - loss_target)
        return 0.5 * _jnp.sum(_jnp.mean(err, axis=-1)) if err.ndim else 0.5 * err


def _adamw(w, g, m, v):
    m = ADAM_B1 * m + (1.0 - ADAM_B1) * g
    v = ADAM_B2 * v + (1.0 - ADAM_B2) * _jnp.square(g)
    m_hat = m / (1.0 - ADAM_B1 ** ADAM_STEP)
    v_hat = v / (1.0 - ADAM_B2 ** ADAM_STEP)
    delta = -ADAM_LR * (m_hat / (_jnp.sqrt(v_hat) + ADAM_EPS) + ADAM_WD * w)
    return delta, m, v


def reference(x, meta_tokens, norm_mix_w, w_in, dn_conv_w, dn_a_log, dn_dt_bias, dn_norm_w, m2_conv_w, m2_conv_b, m2_a_log, m2_dt_bias, m2_d, m2_norm_w, w_out, norm_ffn_w, ffn_up, ffn_conv_w, ffn_down, norm_final_w, loss_target, m_meta_tokens, m_norm_mix_w, m_w_in, m_dn_conv_w, m_dn_a_log, m_dn_dt_bias, m_dn_norm_w, m_m2_conv_w, m_m2_conv_b, m_m2_a_log, m_m2_dt_bias, m_m2_d, m_m2_norm_w, m_w_out, m_norm_ffn_w, m_ffn_up, m_ffn_conv_w, m_ffn_down, m_norm_final_w, v_meta_tokens, v_norm_mix_w, v_w_in, v_dn_conv_w, v_dn_a_log, v_dn_dt_bias, v_dn_norm_w, v_m2_conv_w, v_m2_conv_b, v_m2_a_log, v_m2_dt_bias, v_m2_d, v_m2_norm_w, v_w_out, v_norm_ffn_w, v_ffn_up, v_ffn_conv_w, v_ffn_down, v_norm_final_w):
    given = dict(x=x, meta_tokens=meta_tokens, norm_mix_w=norm_mix_w, w_in=w_in, dn_conv_w=dn_conv_w, dn_a_log=dn_a_log, dn_dt_bias=dn_dt_bias, dn_norm_w=dn_norm_w, m2_conv_w=m2_conv_w, m2_conv_b=m2_conv_b, m2_a_log=m2_a_log, m2_dt_bias=m2_dt_bias, m2_d=m2_d, m2_norm_w=m2_norm_w, w_out=w_out, norm_ffn_w=norm_ffn_w, ffn_up=ffn_up, ffn_conv_w=ffn_conv_w, ffn_down=ffn_down, norm_final_w=norm_final_w, loss_target=loss_target, m_meta_tokens=m_meta_tokens, m_norm_mix_w=m_norm_mix_w, m_w_in=m_w_in, m_dn_conv_w=m_dn_conv_w, m_dn_a_log=m_dn_a_log, m_dn_dt_bias=m_dn_dt_bias, m_dn_norm_w=m_dn_norm_w, m_m2_conv_w=m_m2_conv_w, m_m2_conv_b=m_m2_conv_b, m_m2_a_log=m_m2_a_log, m_m2_dt_bias=m_m2_dt_bias, m_m2_d=m_m2_d, m_m2_norm_w=m_m2_norm_w, m_w_out=m_w_out, m_norm_ffn_w=m_norm_ffn_w, m_ffn_up=m_ffn_up, m_ffn_conv_w=m_ffn_conv_w, m_ffn_down=m_ffn_down, m_norm_final_w=m_norm_final_w, v_meta_tokens=v_meta_tokens, v_norm_mix_w=v_norm_mix_w, v_w_in=v_w_in, v_dn_conv_w=v_dn_conv_w, v_dn_a_log=v_dn_a_log, v_dn_dt_bias=v_dn_dt_bias, v_dn_norm_w=v_dn_norm_w, v_m2_conv_w=v_m2_conv_w, v_m2_conv_b=v_m2_conv_b, v_m2_a_log=v_m2_a_log, v_m2_dt_bias=v_m2_dt_bias, v_m2_d=v_m2_d, v_m2_norm_w=v_m2_norm_w, v_w_out=v_w_out, v_norm_ffn_w=v_norm_ffn_w, v_ffn_up=v_ffn_up, v_ffn_conv_w=v_ffn_conv_w, v_ffn_down=v_ffn_down, v_norm_final_w=v_norm_final_w)
    weights = {n: given[n] for n in TWIN_WEIGHTS}
    shared = {n: given[n] for n in SHARED_INPUTS}
    per_example = {n: given[n] for n in ['x']}
    grad_fn = _jax.value_and_grad(_loss, argnums=(0, 1))

    def one_microbatch(ex, loss_target):
        ex = dict(ex)
        diff = ex.pop(TWIN_DIFF_INPUT)
        return grad_fn(weights, diff, {**shared, **ex}, loss_target)

    if N_MICROBATCH == 1:
        loss, (grad_w, grad_x) = one_microbatch(per_example, given["loss_target"])
    else:
        def body(carry, xs):
            loss_sum, grad_sum = carry
            l_k, (gw_k, gx_k) = one_microbatch(xs[0], xs[1])
            with _jax.named_scope("update"):
                return (loss_sum + l_k, _jax.tree.map(_jnp.add, grad_sum, gw_k)), gx_k

        init = (_jnp.zeros((), _jnp.float32), _jax.tree.map(_jnp.zeros_like, weights))
        (loss, grad_w), grad_x = _jax.lax.scan(body, init, (per_example, given["loss_target"]))
    with _jax.named_scope("update"):
        delta_w, new_m, new_v = {}, {}, {}
        for n in TWIN_WEIGHTS:
            delta_w[n], new_m[n], new_v[n] = _adamw(weights[n], grad_w[n], given["m_" + n], given["v_" + n])
    return (loss, grad_x, *[grad_w[n] for n in TWIN_WEIGHTS], *[delta_w[n] for n in TWIN_WEIGHTS],
            *[new_m[n] for n in TWIN_WEIGHTS], *[new_v[n] for n in TWIN_WEIGHTS])
```

```python
import functools

import jax
import jax.numpy as jnp
from jax import lax
from jax.experimental import pallas as pl
from jax.experimental.pallas import tpu as pltpu

F32 = jnp.float32
BF16 = jnp.bfloat16
HI = lax.Precision.HIGHEST

CHUNK = 64
NORM_EPS = 1e-6
DN_DK = 128
M2_P = 64
M2_N = 128
M2_GROUPS = 4
HEAD_BLOCK = 8
VMEM_LIMIT = 56 * 1024 * 1024

ADAM_LR, ADAM_B1, ADAM_B2, ADAM_EPS, ADAM_WD, ADAM_STEP = 0.001, 0.9, 0.999, 1e-08, 0.01, 10


def _cparams(sem=None):
    return pltpu.CompilerParams(dimension_semantics=sem, vmem_limit_bytes=VMEM_LIMIT)


def _silu(x):
    return x / (1.0 + jnp.exp(-x))


def _sigmoid(x):
    return 1.0 / (1.0 + jnp.exp(-x))


def _softplus(x):
    return jnp.maximum(x, 0.0) + jnp.log(1.0 + jnp.exp(-jnp.abs(x)))


def _tri_masks():
    r = lax.broadcasted_iota(jnp.int32, (CHUNK, CHUNK), 0)
    c = lax.broadcasted_iota(jnp.int32, (CHUNK, CHUNK), 1)
    return (r >= c)[None], (r > c)[None], (r == c)[None]


def _col2row(col, eye):
    return jnp.sum(jnp.where(eye, col, 0.0), axis=1, keepdims=True)


def _cumsum_col(col, causal, eye):
    row = _col2row(col, eye)
    return jnp.sum(jnp.where(causal, row, 0.0), axis=2, keepdims=True)


def _bdot(a, b, dims, precision=None):
    (ca, cb) = dims
    if precision is None:
        a = a.astype(BF16)
        b = b.astype(BF16)
    return lax.dot_general(a, b, (((ca,), (cb,)), ((0,), (0,))), precision=precision,
                           preferred_element_type=F32)


def _unit_lower_inverse(a_mat, eye):
    inv = jnp.where(eye, 1.0, 0.0) - a_mat
    pw = a_mat
    n = 2
    while n < CHUNK:
        pw = _bdot(pw, pw, (2, 1), HI)
        inv = inv + _bdot(inv, pw, (2, 1), HI)
        n *= 2
    return inv


def _gdn_step(state, qa, ka, va, z, braw, araw, a_log, dt_bias, norm_w, vm):
    causal, strict, eye = _tri_masks()
    qa, ka, va = qa * vm, ka * vm, va * vm
    q = qa * lax.rsqrt(jnp.sum(qa * qa, -1, keepdims=True) + NORM_EPS) * (DN_DK ** -0.5)
    k = ka * lax.rsqrt(jnp.sum(ka * ka, -1, keepdims=True) + NORM_EPS)
    beta = _sigmoid(braw) * vm
    g = -jnp.exp(a_log) * _softplus(araw + dt_bias) * vm
    gcum = _cumsum_col(g, causal, eye)
    grow = _col2row(gcum, eye)
    decay = jnp.where(causal, jnp.exp(jnp.where(causal, gcum - grow, 0.0)), 0.0)
    kk = _bdot(k, k, (2, 2))
    a_mat = jnp.where(strict, beta * kk * decay, 0.0)
    tinv = _unit_lower_inverse(a_mat, eye)
    egc = jnp.exp(gcum)
    u = _bdot(tinv, va * beta, (2, 1), HI)
    w = _bdot(tinv, k * (beta * egc), (2, 1), HI)
    v_new = u - _bdot(w, state, (2, 1))
    o_inter = _bdot(q * egc, state, (2, 1))
    qk = _bdot(q, k, (2, 2)) * decay
    o = o_inter + _bdot(qk, v_new, (2, 1))
    g_last = jnp.sum(g, axis=1, keepdims=True)
    new_state = state * jnp.exp(g_last) + _bdot(k * jnp.exp(g_last - gcum), v_new, (1, 1))
    o = o * lax.rsqrt(jnp.mean(o * o, -1, keepdims=True) + NORM_EPS) * norm_w * _silu(z)
    return new_state, o


def _valid_rows(chunk, n_pad):
    r = lax.broadcasted_iota(jnp.int32, (1, CHUNK, 1), 1)
    return jnp.where((chunk > 0) | (r >= n_pad), 1.0, 0.0).astype(F32)


def _split_heads(x, n, w):
    return jnp.stack([x[:, i * w:(i + 1) * w] for i in range(n)], axis=0)


def _merge_heads(x):
    return jnp.concatenate([x[i] for i in range(x.shape[0])], axis=-1)


def gdn_forward(act_qkv, proj, bcol, acol, a_log3, dt_bias3, norm_w3, *, n_pad, z_off):
    t, w3 = act_qkv.shape
    wdn = w3 // 3
    heads = wdn // DN_DK
    hb = min(HEAD_BLOCK, heads)
    nhb = heads // hb
    nc = t // CHUNK
    bw = hb * DN_DK
    nqb = wdn // bw

    def body(q_ref, k_ref, v_ref, z_ref, b_ref, a_ref, al_ref, db_ref, nw_ref, out_ref, sall_ref, st_ref):
        n = pl.program_id(1)

        @pl.when(n == 0)
        def _():
            st_ref[...] = jnp.zeros_like(st_ref)

        state = st_ref[...]
        sall_ref[...] = state
        vm = _valid_rows(n, n_pad)
        new_state, o = _gdn_step(state, _split_heads(q_ref[...], hb, DN_DK), _split_heads(k_ref[...], hb, DN_DK),
                                 _split_heads(v_ref[...], hb, DN_DK), _split_heads(z_ref[...], hb, DN_DK),
                                 b_ref[...], a_ref[...], al_ref[...], db_ref[...], nw_ref[...], vm)
        st_ref[...] = new_state
        out_ref[...] = _merge_heads(o).astype(out_ref.dtype)

    col = pl.BlockSpec((hb, CHUNK, 1), lambda h, n: (h, n, 0))
    par = pl.BlockSpec((hb, 1, 1), lambda h, n: (h, 0, 0))
    return pl.pallas_call(
        body, name="gdn_fwd",
        grid=(nhb, nc),
        in_specs=[pl.BlockSpec((CHUNK, bw), lambda h, n: (n, h)),
                  pl.BlockSpec((CHUNK, bw), lambda h, n: (n, nqb + h)),
                  pl.BlockSpec((CHUNK, bw), lambda h, n: (n, 2 * nqb + h)),
                  pl.BlockSpec((CHUNK, bw), lambda h, n: (n, z_off // bw + h)),
                  col, col, par, par,
                  pl.BlockSpec((1, 1, DN_DK), lambda h, n: (0, 0, 0))],
        out_specs=[pl.BlockSpec((CHUNK, bw), lambda h, n: (n, h)),
                   pl.BlockSpec((None, hb, DN_DK, DN_DK), lambda h, n: (n, h, 0, 0))],
        out_shape=[jax.ShapeDtypeStruct((t, wdn), BF16),
                   jax.ShapeDtypeStruct((nc, heads, DN_DK, DN_DK), F32)],
        scratch_shapes=[pltpu.VMEM((hb, DN_DK, DN_DK), F32)],
        compiler_params=_cparams(("arbitrary", "arbitrary")),
    )(act_qkv, act_qkv, act_qkv, proj, bcol, acol, a_log3, dt_bias3, norm_w3)


def gdn_backward(act_qkv, proj, bcol, acol, a_log3, dt_bias3, norm_w3, s_all, d_mixed, *, n_pad, z_off):
    t, w3 = act_qkv.shape
    wdn = w3 // 3
    heads = wdn // DN_DK
    hb = min(HEAD_BLOCK, heads)
    nhb = heads // hb
    nc = t // CHUNK
    bw = hb * DN_DK
    nqb = wdn // bw

    def body(q_ref, k_ref, v_ref, z_ref, b_ref, a_ref, al_ref, db_ref, nw_ref, s_ref, do_ref,
             dq_ref, dk_ref, dv_ref, dz_ref, dbc_ref, dac_ref, dal_ref, ddb_ref, dnw_ref, ds_ref):
        i = pl.program_id(1)
        n = nc - 1 - i

        @pl.when(i == 0)
        def _():
            ds_ref[...] = jnp.zeros_like(ds_ref)
            dal_ref[...] = jnp.zeros_like(dal_ref)
            ddb_ref[...] = jnp.zeros_like(ddb_ref)
            dnw_ref[...] = jnp.zeros_like(dnw_ref)

        vm = _valid_rows(n, n_pad)
        step = functools.partial(_gdn_step, vm=vm)
        _, vjp = jax.vjp(step, s_ref[...], _split_heads(q_ref[...], hb, DN_DK), _split_heads(k_ref[...], hb, DN_DK),
                         _split_heads(v_ref[...], hb, DN_DK), _split_heads(z_ref[...], hb, DN_DK),
                         b_ref[...], a_ref[...], al_ref[...], db_ref[...], nw_ref[...])
        ds, dq, dk, dv, dz, dbr, dar, dal, ddb, dnw = vjp((ds_ref[...], _split_heads(do_ref[...], hb, DN_DK)))
        ds_ref[...] = ds
        dq_ref[...] = _merge_heads(dq)
        dk_ref[...] = _merge_heads(dk)
        dv_ref[...] = _merge_heads(dv)
        dz_ref[...] = _merge_heads(dz).astype(dz_ref.dtype)
        dbc_ref[...] = dbr
        dac_ref[...] = dar
        dal_ref[...] += dal
        ddb_ref[...] += ddb
        dnw_ref[...] += dnw[0]

    rev = lambda n: nc - 1 - n
    col = pl.BlockSpec((hb, CHUNK, 1), lambda h, n: (h, rev(n), 0))
    par = pl.BlockSpec((hb, 1, 1), lambda h, n: (h, 0, 0))
    blk = lambda off: pl.BlockSpec((CHUNK, bw), lambda h, n: (rev(n), off + h))
    return pl.pallas_call(
        body, name="gdn_bwd",
        grid=(nhb, nc),
        in_specs=[blk(0), blk(nqb), blk(2 * nqb), blk(z_off // bw), col, col, par, par,
                  pl.BlockSpec((1, 1, DN_DK), lambda h, n: (0, 0, 0)),
                  pl.BlockSpec((None, hb, DN_DK, DN_DK), lambda h, n: (rev(n), h, 0, 0)),
                  blk(0)],
        out_specs=[blk(0), blk(0), blk(0), blk(0), col, col, par, par,
                   pl.BlockSpec((None, 1, DN_DK), lambda h, n: (h, 0, 0))],
        out_shape=[jax.ShapeDtypeStruct((t, wdn), F32)] * 3
        + [jax.ShapeDtypeStruct((t, wdn), BF16),
           jax.ShapeDtypeStruct((heads, t, 1), F32), jax.ShapeDtypeStruct((heads, t, 1), F32),
           jax.ShapeDtypeStruct((heads, 1, 1), F32), jax.ShapeDtypeStruct((heads, 1, 1), F32),
           jax.ShapeDtypeStruct((nhb, 1, DN_DK), F32)],
        scratch_shapes=[pltpu.VMEM((hb, DN_DK, DN_DK), F32)],
        compiler_params=_cparams(("arbitrary", "arbitrary")),
    )(act_qkv, act_qkv, act_qkv, proj, bcol, acol, a_log3, dt_bias3, norm_w3, s_all, d_mixed)


def _dot2(a, b, ca, cb):
    return lax.dot_general(a.astype(BF16), b.astype(BF16), (((ca,), (cb,)), ((), ())),
                           preferred_element_type=F32)


def _ssd_step(state, xa, bmat, cmat, z, dtraw, a_log, dt_bias, dskip, norm_w, vm):
    causal, _, eye = _tri_masks()
    r_heads, p, n_state = state.shape
    gw = r_heads * p
    vm2 = vm[0]
    xa, bmat, cmat = xa * vm2, bmat * vm2, cmat * vm2
    dt = _softplus(dtraw + dt_bias) * vm
    a = dt * (-jnp.exp(a_log))
    acs = _cumsum_col(a, causal, eye)
    arow = _col2row(acs, eye)
    lmat = jnp.where(causal, jnp.exp(jnp.where(causal, acs - arow, 0.0)), 0.0)
    hsel = (lax.broadcasted_iota(jnp.int32, (r_heads, 1, gw), 2) // p
            == lax.broadcasted_iota(jnp.int32, (r_heads, 1, gw), 0))

    def spread(col):
        return jnp.sum(jnp.where(hsel, col, 0.0), axis=0)

    xdt = xa * spread(dt)
    cb = _dot2(cmat, bmat, 1, 1)
    m = (cb[None] * lmat).reshape(r_heads * CHUNK, CHUNK)
    yb = _dot2(m, xdt, 1, 0).reshape(r_heads, CHUNK, gw)
    y_diag = jnp.sum(jnp.where(hsel, yb, 0.0), axis=0)
    s2 = state.reshape(gw, n_state)
    y_off = _dot2(cmat, s2, 1, 1) * spread(jnp.exp(acs))
    a_last = jnp.sum(a, axis=1, keepdims=True)
    upd = _dot2(xdt * spread(jnp.exp(a_last - acs)), bmat, 0, 0)
    new_state = state * jnp.exp(a_last) + upd.reshape(r_heads, p, n_state)
    y = y_diag + y_off + xa * spread(dskip)
    y = y * _silu(z)
    y = y * lax.rsqrt(jnp.mean(y * y, -1, keepdims=True) + NORM_EPS) * norm_w
    return new_state, y


def _ssd_dims(act_xbc):
    t, wx = act_xbc.shape
    wm = wx - 2 * M2_GROUPS * M2_N
    gw = wm // M2_GROUPS
    return t, wm, gw, gw // M2_P, wm // M2_P, t // CHUNK


def ssd_forward(act_xbc, proj, dtcol, a_log3, dt_bias3, dskip3, norm_w, *, n_pad, z_off):
    t, wm, gw, rh, heads, nc = _ssd_dims(act_xbc)
    nb = wm // M2_N

    def body(x_ref, b_ref, c_ref, z_ref, dt_ref, al_ref, db_ref, dk_ref, nw_ref, out_ref, sall_ref, st_ref):
        n = pl.program_id(1)

        @pl.when(n == 0)
        def _():
            st_ref[...] = jnp.zeros_like(st_ref)

        state = st_ref[...]
        sall_ref[...] = state
        new_state, y = _ssd_step(state, x_ref[...], b_ref[...], c_ref[...], z_ref[...], dt_ref[...],
                                 al_ref[...], db_ref[...], dk_ref[...], nw_ref[...], _valid_rows(n, n_pad))
        st_ref[...] = new_state
        out_ref[...] = y.astype(out_ref.dtype)

    par = pl.BlockSpec((rh, 1, 1), lambda g, n: (g, 0, 0))
    return pl.pallas_call(
        body, name="ssd_fwd",
        grid=(M2_GROUPS, nc),
        in_specs=[pl.BlockSpec((CHUNK, gw), lambda g, n: (n, g)),
                  pl.BlockSpec((CHUNK, M2_N), lambda g, n: (n, nb + g)),
                  pl.BlockSpec((CHUNK, M2_N), lambda g, n: (n, nb + M2_GROUPS + g)),
                  pl.BlockSpec((CHUNK, gw), lambda g, n: (n, z_off // gw + g)),
                  pl.BlockSpec((rh, CHUNK, 1), lambda g, n: (g, n, 0)),
                  par, par, par,
                  pl.BlockSpec((1, gw), lambda g, n: (0, g))],
        out_specs=[pl.BlockSpec((CHUNK, gw), lambda g, n: (n, g)),
                   pl.BlockSpec((None, rh, M2_P, M2_N), lambda g, n: (n, g, 0, 0))],
        out_shape=[jax.ShapeDtypeStruct((t, wm), BF16),
                   jax.ShapeDtypeStruct((nc, heads, M2_P, M2_N), F32)],
        scratch_shapes=[pltpu.VMEM((rh, M2_P, M2_N), F32)],
        compiler_params=_cparams(("arbitrary", "arbitrary")),
    )(act_xbc, act_xbc, act_xbc, proj, dtcol, a_log3, dt_bias3, dskip3, norm_w)


def ssd_backward(act_xbc, proj, dtcol, a_log3, dt_bias3, dskip3, norm_w, s_all, d_mixed, *, n_pad, z_off):
    t, wm, gw, rh, heads, nc = _ssd_dims(act_xbc)
    nb = wm // M2_N

    def body(x_ref, b_ref, c_ref, z_ref, dt_ref, al_ref, db_ref, dk_ref, nw_ref, s_ref, dy_ref,
             dx_ref, dbm_ref, dcm_ref, dz_ref, ddt_ref, dal_ref, ddb_ref, ddk_ref, dnw_ref, ds_ref):
        i = pl.program_id(1)
        n = nc - 1 - i

        @pl.when(i == 0)
        def _():
            ds_ref[...] = jnp.zeros_like(ds_ref)
            dal_ref[...] = jnp.zeros_like(dal_ref)
            ddb_ref[...] = jnp.zeros_like(ddb_ref)
            ddk_ref[...] = jnp.zeros_like(ddk_ref)
            dnw_ref[...] = jnp.zeros_like(dnw_ref)

        step = functools.partial(_ssd_step, vm=_valid_rows(n, n_pad))
        _, vjp = jax.vjp(step, s_ref[...], x_ref[...], b_ref[...], c_ref[...], z_ref[...], dt_ref[...],
                         al_ref[...], db_ref[...], dk_ref[...], nw_ref[...])
        ds, dx, dbm, dcm, dz, ddt, dal, ddb, ddk, dnw = vjp((ds_ref[...], dy_ref[...]))
        ds_ref[...] = ds
        dx_ref[...] = dx
        dbm_ref[...] = dbm
        dcm_ref[...] = dcm
        dz_ref[...] = dz.astype(dz_ref.dtype)
        ddt_ref[...] = ddt
        dal_ref[...] += dal
        ddb_ref[...] += ddb
        ddk_ref[...] += ddk
        dnw_ref[...] += dnw

    rev = lambda n: nc - 1 - n
    par = pl.BlockSpec((rh, 1, 1), lambda g, n: (g, 0, 0))
    wide = lambda off: pl.BlockSpec((CHUNK, gw), lambda g, n: (rev(n), off + g))
    narrow = lambda off: pl.BlockSpec((CHUNK, M2_N), lambda g, n: (rev(n), off + g))
    col = pl.BlockSpec((rh, CHUNK, 1), lambda g, n: (g, rev(n), 0))
    gn = M2_GROUPS * M2_N
    return pl.pallas_call(
        body, name="ssd_bwd",
        grid=(M2_GROUPS, nc),
        in_specs=[wide(0), narrow(nb), narrow(nb + M2_GROUPS), wide(z_off // gw), col, par, par, par,
                  pl.BlockSpec((1, gw), lambda g, n: (0, g)),
                  pl.BlockSpec((None, rh, M2_P, M2_N), lambda g, n: (rev(n), g, 0, 0)),
                  wide(M2_GROUPS)],
        out_specs=[wide(0), narrow(0), narrow(0), wide(0), col, par, par, par,
                   pl.BlockSpec((1, gw), lambda g, n: (0, g))],
        out_shape=[jax.ShapeDtypeStruct((t, wm), F32), jax.ShapeDtypeStruct((t, gn), F32),
                   jax.ShapeDtypeStruct((t, gn), F32), jax.ShapeDtypeStruct((t, wm), BF16),
                   jax.ShapeDtypeStruct((heads, t, 1), F32),
                   jax.ShapeDtypeStruct((heads, 1, 1), F32), jax.ShapeDtypeStruct((heads, 1, 1), F32),
                   jax.ShapeDtypeStruct((heads, 1, 1), F32), jax.ShapeDtypeStruct((1, wm), F32)],
        scratch_shapes=[pltpu.VMEM((rh, M2_P, M2_N), F32)],
        compiler_params=_cparams(("arbitrary", "arbitrary")),
    )(act_xbc, act_xbc, act_xbc, proj, dtcol, a_log3, dt_bias3, dskip3, norm_w, s_all, d_mixed)


SUBLANES = 8
LANES = 128


def _pick(dim, target, align):
    best = None
    for d in range(align, min(dim, target) + 1, align):
        if dim % d == 0:
            best = d
    return dim if best is None else best


def _row_tile(t):
    return _pick(t, 512, 16)


def matmul(a, b, *, mode, out_dtype, name, tm=1040, tn=512, tk=2048, residual=None, out_shards=1):
    if mode == "nn":
        (m, kd), n = a.shape, b.shape[1]
    elif mode == "nt":
        (m, kd), n = a.shape, b.shape[0]
    else:
        (kd, m), n = a.shape, b.shape[1]
    tm, tk = _pick(m, tm, 16), _pick(kd, tk, LANES if mode != "tn" else 16)
    ns = n // out_shards
    tn = _pick(ns, tn, LANES)
    nk = kd // tk
    npb = ns // tn
    if mode == "tn":
        a_spec = pl.BlockSpec((tk, tm), lambda i, j, k: (k, i))
        contract = ((0,), (0,))
    else:
        a_spec = pl.BlockSpec((tm, tk), lambda i, j, k: (i, k))
        contract = ((1,), (1,)) if mode == "nt" else ((1,), (0,))
    if mode == "nt":
        b_spec = pl.BlockSpec((tn, tk), lambda i, j, k: (j, k))
    else:
        b_spec = pl.BlockSpec((tk, tn), lambda i, j, k: (k, j))
    has_res = residual is not None

    def body(*refs):
        a_ref, b_ref = refs[0], refs[1]
        r_ref = refs[2] if has_res else None
        o_ref, acc_ref = refs[-2], refs[-1]
        k = pl.program_id(2)
        part = lax.dot_general(a_ref[...].astype(BF16), b_ref[...].astype(BF16), (contract, ((), ())),
                               preferred_element_type=F32)

        def finish(total):
            if has_res:
                total = total + r_ref[...]
            o_ref[...] = total.astype(o_ref.dtype)

        if nk == 1:
            finish(part)
        else:
            @pl.when(k == 0)
            def _():
                acc_ref[...] = part

            @pl.when((k > 0) & (k < nk - 1))
            def _():
                acc_ref[...] += part

            @pl.when(k == nk - 1)
            def _():
                finish(acc_ref[...] + part)

    in_specs = [a_spec, b_spec]
    args = [a, b]
    if has_res:
        in_specs.append(pl.BlockSpec((tm, tn), lambda i, j, k: (i, j)))
        args.append(residual)
    if out_shards == 1:
        out_spec = pl.BlockSpec((tm, tn), lambda i, j, k: (i, j))
        out_shape = jax.ShapeDtypeStruct((m, n), out_dtype)
    else:
        out_spec = pl.BlockSpec((None, tm, tn), lambda i, j, k: (j // npb, i, j % npb))
        out_shape = jax.ShapeDtypeStruct((out_shards, m, ns), out_dtype)
    return pl.pallas_call(
        body, name=name, grid=(m // tm, n // tn, nk),
        in_specs=in_specs, out_specs=out_spec, out_shape=out_shape,
        scratch_shapes=[pltpu.VMEM((tm, tn), F32)],
        compiler_params=_cparams(("parallel", "parallel", "arbitrary")),
    )(*args)


def rmsnorm_forward(x, w, *, name):
    t, d = x.shape
    tm = _row_tile(t)

    def body(x_ref, w_ref, o_ref):
        xv = x_ref[...]
        r = lax.rsqrt(jnp.mean(xv * xv, -1, keepdims=True) + NORM_EPS)
        o_ref[...] = (xv * r * w_ref[...]).astype(o_ref.dtype)

    return pl.pallas_call(
        body, name=name, grid=(t // tm,),
        in_specs=[pl.BlockSpec((tm, d), lambda i: (i, 0)), pl.BlockSpec((1, d), lambda i: (0, 0))],
        out_specs=pl.BlockSpec((tm, d), lambda i: (i, 0)),
        out_shape=jax.ShapeDtypeStruct((t, d), BF16),
        compiler_params=_cparams(("parallel",)),
    )(x, w)


def _rmsnorm_grads(xv, wv, dy):
    r = lax.rsqrt(jnp.mean(xv * xv, -1, keepdims=True) + NORM_EPS)
    xh = xv * r
    g = dy * wv
    dx = r * (g - xh * jnp.mean(g * xh, -1, keepdims=True))
    return dx, jnp.sum(dy * xh, axis=0, keepdims=True)


def rmsnorm_backward(x, w, dy, dres, *, n_pad, name):
    t, d = x.shape
    tm = _row_tile(t)

    def body(x_ref, w_ref, dy_ref, dr_ref, dx_ref, dw_ref):
        i = pl.program_id(0)

        @pl.when(i == 0)
        def _():
            dw_ref[...] = jnp.zeros_like(dw_ref)

        dx, dw = _rmsnorm_grads(x_ref[...], w_ref[...], dy_ref[...])
        rows = i * tm + lax.broadcasted_iota(jnp.int32, (tm, 1), 0)
        dx_ref[...] = jnp.where(rows >= n_pad, dx + dr_ref[...], 0.0)
        dw_ref[...] += dw

    row = pl.BlockSpec((tm, d), lambda i: (i, 0))
    vec = pl.BlockSpec((1, d), lambda i: (0, 0))
    return pl.pallas_call(
        body, name=name, grid=(t // tm,),
        in_specs=[row, vec, row, row], out_specs=[row, vec],
        out_shape=[jax.ShapeDtypeStruct((t, d), F32), jax.ShapeDtypeStruct((1, d), F32)],
        compiler_params=_cparams(("arbitrary",)),
    )(x, w, dy, dres)


def loss_head(h, w, target, *, n_skip):
    t, d = h.shape
    tm = _row_tile(t)

    def body(x_ref, w_ref, y_ref, loss_ref, dx_ref, dw_ref):
        i = pl.program_id(0)

        @pl.when(i == 0)
        def _():
            dw_ref[...] = jnp.zeros_like(dw_ref)
            loss_ref[...] = jnp.zeros_like(loss_ref)

        xv, wv = x_ref[...], w_ref[...]
        r = lax.rsqrt(jnp.mean(xv * xv, -1, keepdims=True) + NORM_EPS)
        rows = i * tm + lax.broadcasted_iota(jnp.int32, (tm, 1), 0)
        err = jnp.where(rows >= n_skip, xv * r * wv - y_ref[...], 0.0)
        loss_ref[...] += 0.5 * jnp.sum(jnp.mean(err * err, -1, keepdims=True))
        dx, dw = _rmsnorm_grads(xv, wv, err * (1.0 / d))
        dx_ref[...] = dx
        dw_ref[...] += dw

    row = pl.BlockSpec((tm, d), lambda i: (i, 0))
    vec = pl.BlockSpec((1, d), lambda i: (0, 0))
    return pl.pallas_call(
        body, name="loss_head", grid=(t // tm,),
        in_specs=[row, vec, row],
        out_specs=[pl.BlockSpec((1, LANES), lambda i: (0, 0)), row, vec],
        out_shape=[jax.ShapeDtypeStruct((1, LANES), F32), jax.ShapeDtypeStruct((t, d), F32),
                   jax.ShapeDtypeStruct((1, d), F32)],
        compiler_params=_cparams(("arbitrary",)),
    )(h, w, target)


HALO = SUBLANES


def _conv_rows(scr_ref, w, first, rows):
    kk = w.shape[0]
    acc = w[0:1, :] * scr_ref[pl.ds(first, rows), :]
    for j in range(1, kk):
        acc = acc + w[j:j + 1, :] * scr_ref[pl.ds(first + j, rows), :]
    return acc


def _dsilu(p):
    s = _sigmoid(p)
    return s * (1.0 + p * (1.0 - s))


def conv_silu(x, w, b, *, x_off, name):
    t = x.shape[0]
    kk, width = w.shape
    tm = _row_tile(t)
    tc = _pick(width, 512, LANES)
    ob, nh = x_off // tc, tm // HALO

    def body(prev_ref, x_ref, w_ref, b_ref, o_ref, scr):
        i = pl.program_id(1)
        scr[0:HALO, :] = jnp.where(i > 0, prev_ref[...], 0.0)
        scr[HALO:HALO + tm, :] = x_ref[...]
        pre = _conv_rows(scr, w_ref[...], HALO - (kk - 1), tm) + b_ref[...]
        o_ref[...] = _silu(pre)

    return pl.pallas_call(
        body, name=name, grid=(width // tc, t // tm),
        in_specs=[pl.BlockSpec((HALO, tc), lambda j, i: (jnp.maximum(i * nh - 1, 0), ob + j)),
                  pl.BlockSpec((tm, tc), lambda j, i: (i, ob + j)),
                  pl.BlockSpec((kk, tc), lambda j, i: (0, j)),
                  pl.BlockSpec((1, tc), lambda j, i: (0, j))],
        out_specs=pl.BlockSpec((tm, tc), lambda j, i: (i, j)),
        out_shape=jax.ShapeDtypeStruct((t, width), F32),
        scratch_shapes=[pltpu.VMEM((tm + HALO, tc), F32)],
        compiler_params=_cparams(("parallel", "arbitrary")),
    )(x, x, w, b)


def conv_silu_backward(x, w, b, dact, *, x_off, name):
    t = x.shape[0]
    kk, width = w.shape
    tm = _row_tile(t)
    tc = _pick(width, 512, LANES)
    ob, nh, nt = x_off // tc, tm // HALO, t // tm
    last_h = t // HALO - 1

    def body(prev_ref, x_ref, next_ref, w_ref, b_ref, d_ref, dnext_ref, dx_ref, dw_ref, db_ref, scr_x, scr_d):
        i = pl.program_id(1)

        @pl.when(i == 0)
        def _():
            dw_ref[...] = jnp.zeros_like(dw_ref)
            db_ref[...] = jnp.zeros_like(db_ref)

        wv = w_ref[...]
        scr_x[0:HALO, :] = jnp.where(i > 0, prev_ref[...], 0.0)
        scr_x[HALO:HALO + tm, :] = x_ref[...]
        scr_x[HALO + tm:, :] = next_ref[...]
        ext = tm + HALO
        pre = _conv_rows(scr_x, wv, HALO - (kk - 1), ext) + b_ref[...]
        scr_d[0:tm, :] = d_ref[...]
        scr_d[tm:, :] = jnp.where(i < nt - 1, dnext_ref[...], 0.0)
        scr_d[...] = scr_d[...] * _dsilu(pre)
        dx = wv[0:1, :] * scr_d[pl.ds(kk - 1, tm), :]
        for j in range(1, kk):
            dx = dx + wv[j:j + 1, :] * scr_d[pl.ds(kk - 1 - j, tm), :]
        dx_ref[...] = dx.astype(dx_ref.dtype)
        dpre = scr_d[0:tm, :]
        for j in range(kk):
            dw_ref[j:j + 1, :] += jnp.sum(dpre * scr_x[pl.ds(HALO - (kk - 1) + j, tm), :], axis=0, keepdims=True)
        db_ref[0:1, :] += jnp.sum(dpre, axis=0, keepdims=True)

    nxt = lambda j, i: (jnp.minimum((i + 1) * nh, last_h), j)
    acc = pl.BlockSpec((SUBLANES, tc), lambda j, i: (0, j))
    return pl.pallas_call(
        body, name=name, grid=(width // tc, nt),
        in_specs=[pl.BlockSpec((HALO, tc), lambda j, i: (jnp.maximum(i * nh - 1, 0), ob + j)),
                  pl.BlockSpec((tm, tc), lambda j, i: (i, ob + j)),
                  pl.BlockSpec((HALO, tc), lambda j, i: (jnp.minimum((i + 1) * nh, last_h), ob + j)),
                  pl.BlockSpec((kk, tc), lambda j, i: (0, j)),
                  pl.BlockSpec((1, tc), lambda j, i: (0, j)),
                  pl.BlockSpec((tm, tc), lambda j, i: (i, j)),
                  pl.BlockSpec((HALO, tc), nxt)],
        out_specs=[pl.BlockSpec((tm, tc), lambda j, i: (i, j)), acc, acc],
        out_shape=[jax.ShapeDtypeStruct((t, width), BF16), jax.ShapeDtypeStruct((SUBLANES, width), F32),
                   jax.ShapeDtypeStruct((SUBLANES, width), F32)],
        scratch_shapes=[pltpu.VMEM((tm + 2 * HALO, tc), F32), pltpu.VMEM((tm + HALO, tc), F32)],
        compiler_params=_cparams(("parallel", "arbitrary")),
    )(x, x, x, w, b, dact, dact)


def conv_glu(u, w, *, name):
    t, f2 = u.shape
    kk, f = w.shape[0], f2 // 2
    tm = _row_tile(t)
    tc = _pick(f, 512, LANES)
    nf, nh = f // tc, tm // HALO

    def body(gp_ref, g_ref, vp_ref, v_ref, wg_ref, wv_ref, o_ref, scr_g, scr_v):
        i = pl.program_id(1)
        scr_g[0:HALO, :] = jnp.where(i > 0, gp_ref[...], 0.0)
        scr_g[HALO:, :] = g_ref[...]
        scr_v[0:HALO, :] = jnp.where(i > 0, vp_ref[...], 0.0)
        scr_v[HALO:, :] = v_ref[...]
        gate = _conv_rows(scr_g, wg_ref[...], HALO - (kk - 1), tm)
        val = _conv_rows(scr_v, wv_ref[...], HALO - (kk - 1), tm)
        o_ref[...] = (_silu(gate) * val).astype(o_ref.dtype)

    prev = lambda off: pl.BlockSpec((HALO, tc), lambda j, i: (jnp.maximum(i * nh - 1, 0), off + j))
    tile = lambda off: pl.BlockSpec((tm, tc), lambda j, i: (i, off + j))
    wsp = lambda off: pl.BlockSpec((kk, tc), lambda j, i: (0, off + j))
    return pl.pallas_call(
        body, name=name, grid=(nf, t // tm),
        in_specs=[prev(0), tile(0), prev(nf), tile(nf), wsp(0), wsp(nf)],
        out_specs=pl.BlockSpec((tm, tc), lambda j, i: (i, j)),
        out_shape=jax.ShapeDtypeStruct((t, f), BF16),
        scratch_shapes=[pltpu.VMEM((tm + HALO, tc), F32), pltpu.VMEM((tm + HALO, tc), F32)],
        compiler_params=_cparams(("parallel", "arbitrary")),
    )(u, u, u, u, w, w)


def conv_glu_backward(u, w, dact, *, name):
    t, f2 = u.shape
    kk, f = w.shape[0], f2 // 2
    tm = _row_tile(t)
    tc = _pick(f, 512, LANES)
    nf, nh, nt = f // tc, tm // HALO, t // tm
    last_h = t // HALO - 1
    ext = tm + HALO

    def body(pp_ref, p_ref, pn_ref, op_ref, o_ref, on_ref, wp_ref, wo_ref, d_ref, dn_ref, du_ref, dw_ref,
             scr_p, scr_o, scr_d):
        half, i = pl.program_id(1), pl.program_id(2)

        @pl.when(i == 0)
        def _():
            dw_ref[...] = jnp.zeros_like(dw_ref)

        wp = wp_ref[...]
        for scr, a, bb, c in ((scr_p, pp_ref, p_ref, pn_ref), (scr_o, op_ref, o_ref, on_ref)):
            scr[0:HALO, :] = jnp.where(i > 0, a[...], 0.0)
            scr[HALO:HALO + tm, :] = bb[...]
            scr[HALO + tm:, :] = c[...]
        cp = _conv_rows(scr_p, wp, HALO - (kk - 1), ext)
        co = _conv_rows(scr_o, wo_ref[...], HALO - (kk - 1), ext)
        scr_d[0:tm, :] = d_ref[...]
        scr_d[tm:, :] = jnp.where(i < nt - 1, dn_ref[...], 0.0)
        dact_ext = scr_d[...]
        scr_d[...] = jnp.where(half == 0, dact_ext * co * _dsilu(cp), dact_ext * _silu(co))
        du = wp[0:1, :] * scr_d[pl.ds(kk - 1, tm), :]
        for j in range(1, kk):
            du = du + wp[j:j + 1, :] * scr_d[pl.ds(kk - 1 - j, tm), :]
        du_ref[...] = du.astype(du_ref.dtype)
        dc = scr_d[0:tm, :]
        for j in range(kk):
            dw_ref[j:j + 1, :] += jnp.sum(dc * scr_p[pl.ds(HALO - (kk - 1) + j, tm), :], axis=0, keepdims=True)

    mine = lambda j, h: h * nf + j
    other = lambda j, h: (1 - h) * nf + j
    prev = lambda sel: pl.BlockSpec((HALO, tc), lambda j, h, i: (jnp.maximum(i * nh - 1, 0), sel(j, h)))
    tile = lambda sel: pl.BlockSpec((tm, tc), lambda j, h, i: (i, sel(j, h)))
    nxt = lambda sel: pl.BlockSpec((HALO, tc), lambda j, h, i: (jnp.minimum((i + 1) * nh, last_h), sel(j, h)))
    wsp = lambda sel: pl.BlockSpec((kk, tc), lambda j, h, i: (0, sel(j, h)))
    return pl.pallas_call(
        body, name=name, grid=(nf, 2, nt),
        in_specs=[prev(mine), tile(mine), nxt(mine), prev(other), tile(other), nxt(other), wsp(mine), wsp(other),
                  pl.BlockSpec((tm, tc), lambda j, h, i: (i, j)),
                  pl.BlockSpec((HALO, tc), lambda j, h, i: (jnp.minimum((i + 1) * nh, last_h), j))],
        out_specs=[pl.BlockSpec((tm, tc), lambda j, h, i: (i, mine(j, h))),
                   pl.BlockSpec((SUBLANES, tc), lambda j, h, i: (0, mine(j, h)))],
        out_shape=[jax.ShapeDtypeStruct((t, f2), BF16), jax.ShapeDtypeStruct((SUBLANES, f2), F32)],
        scratch_shapes=[pltpu.VMEM((tm + 2 * HALO, tc), F32), pltpu.VMEM((tm + 2 * HALO, tc), F32),
                        pltpu.VMEM((ext, tc), F32)],
        compiler_params=_cparams(("parallel", "arbitrary", "arbitrary")),
    )(u, u, u, u, u, u, w, w, dact, dact)


def adamw(w, g, m, v, *, name):
    r, c = w.shape
    tr = _pick(r, max(SUBLANES, (2 * 1024 * 1024) // (4 * c) // SUBLANES * SUBLANES), SUBLANES)

    def body(w_ref, g_ref, m_ref, v_ref, d_ref, nm_ref, nv_ref):
        gv = g_ref[...]
        nm = ADAM_B1 * m_ref[...] + (1.0 - ADAM_B1) * gv
        nv = ADAM_B2 * v_ref[...] + (1.0 - ADAM_B2) * (gv * gv)
        m_hat = nm / (1.0 - ADAM_B1 ** ADAM_STEP)
        v_hat = nv / (1.0 - ADAM_B2 ** ADAM_STEP)
        d_ref[...] = -ADAM_LR * (m_hat / (jnp.sqrt(v_hat) + ADAM_EPS) + ADAM_WD * w_ref[...])
        nm_ref[...] = nm
        nv_ref[...] = nv

    blk = pl.BlockSpec((tr, c), lambda i: (i, 0))
    shp = jax.ShapeDtypeStruct((r, c), F32)
    return pl.pallas_call(
        body, name=name, grid=(r // tr,),
        in_specs=[blk] * 4, out_specs=[blk] * 3, out_shape=[shp] * 3,
        compiler_params=_cparams(("parallel",)),
    )(w, g, m, v)


def cast_bf16(x, *, name):
    r, c = x.shape
    tr = _pick(r, max(16, (2 * 1024 * 1024) // (4 * c) // 16 * 16), 16)

    def body(x_ref, o_ref):
        o_ref[...] = x_ref[...].astype(o_ref.dtype)

    return pl.pallas_call(
        body, name=name, grid=(r // tr,),
        in_specs=[pl.BlockSpec((tr, c), lambda i: (i, 0))], out_specs=pl.BlockSpec((tr, c), lambda i: (i, 0)),
        out_shape=jax.ShapeDtypeStruct((r, c), BF16),
        compiler_params=_cparams(("parallel",)),
    )(x)


class Layout:
    def __init__(self, d):
        self.d = d
        self.h_dn = d // DN_DK
        self.h_m2 = d // M2_P
        self.gn = M2_GROUPS * M2_N
        self.w_xbc = d + 2 * self.gn
        self.z_off = 3 * d
        self.m2z_off = 4 * d
        self.xbc_off = 5 * d
        self.small_off = 5 * d + self.w_xbc
        self.n_small = 2 * self.h_dn + self.h_m2
        self.p = self.small_off + LANES
        self.p_orig = self.small_off + self.n_small

    def permute_w_in(self, w):
        d, s2 = self.d, 2 * self.h_dn
        pad = jnp.zeros((w.shape[0], LANES - self.n_small), w.dtype)
        return jnp.concatenate([w[:, :4 * d], w[:, 4 * d + s2:4 * d + s2 + d + self.w_xbc],
                                w[:, 4 * d:4 * d + s2], w[:, self.p_orig - self.h_m2:], pad], axis=1)

    def unpermute_w_in(self, w):
        d, s2 = self.d, 2 * self.h_dn
        so = self.small_off
        return jnp.concatenate([w[:, :4 * d], w[:, so:so + s2], w[:, 4 * d:so], w[:, so + s2:so + self.n_small]], axis=1)


def _cols(small, lo, hi):
    return jnp.transpose(small[:, lo:hi])[..., None]


def local_step(h0, target, wts, *, n_pad, n_meta):
    t, d = h0.shape
    lay = Layout(d)
    hd, hm = lay.h_dn, lay.h_m2
    zeros_b = jnp.zeros((1, 3 * d), F32)
    r3 = lambda v, n: v.reshape(n, 1, 1)
    dn_al, dn_db = r3(wts["dn_a_log"], hd), r3(wts["dn_dt_bias"], hd)
    dn_nw = wts["dn_norm_w"].reshape(1, 1, DN_DK)
    m2_al, m2_db, m2_dk = r3(wts["m2_a_log"], hm), r3(wts["m2_dt_bias"], hm), r3(wts["m2_d"], hm)

    hn1 = rmsnorm_forward(h0, wts["norm_mix_w"], name="norm_mix")
    proj = matmul(hn1, wts["w_in"], mode="nn", out_dtype=F32, name="in_proj", tn=640)
    act_qkv = conv_silu(proj, wts["dn_conv_w"], zeros_b, x_off=0, name="dn_conv")
    act_xbc = conv_silu(proj, wts["m2_conv_w"], wts["m2_conv_b"], x_off=lay.xbc_off, name="m2_conv")
    small = proj[:, lay.small_off:]
    bcol, acol, dtcol = _cols(small, 0, hd), _cols(small, hd, 2 * hd), _cols(small, 2 * hd, 2 * hd + hm)
    mixed_dn, s_dn = gdn_forward(act_qkv, proj, bcol, acol, dn_al, dn_db, dn_nw, n_pad=n_pad, z_off=lay.z_off)
    mixed_m2, s_m2 = ssd_forward(act_xbc, proj, dtcol, m2_al, m2_db, m2_dk, wts["m2_norm_w"],
                                 n_pad=n_pad, z_off=lay.m2z_off)
    mixed = jnp.concatenate([mixed_dn, mixed_m2], axis=1)
    h1 = matmul(mixed, wts["w_out"], mode="nn", out_dtype=F32, name="out_proj", residual=h0)
    hn2 = rmsnorm_forward(h1, wts["norm_ffn_w"], name="norm_ffn")
    up = matmul(hn2, wts["ffn_up"], mode="nn", out_dtype=F32, name="ffn_up", tn=1024)
    act = conv_glu(up, wts["ffn_conv_w"], name="ffn_conv")
    h2 = matmul(act, wts["ffn_down"], mode="nn", out_dtype=F32, name="ffn_down", tk=1408, residual=h1)
    loss, dh2, d_nfw = loss_head(h2, wts["norm_final_w"].reshape(1, d), target, n_skip=n_pad + n_meta)

    g = {}
    d_act = matmul(dh2, wts["ffn_down"], mode="nt", out_dtype=F32, name="d_ffn_act", tn=512)
    g["ffn_down"] = matmul(act, dh2, mode="tn", out_dtype=F32, name="dw_ffn_down", tm=512, tn=1024, tk=1040)
    dup, d_fcw = conv_glu_backward(up, wts["ffn_conv_w"], d_act, name="d_ffn_conv")
    g["ffn_conv_w"] = d_fcw[:wts["ffn_conv_w"].shape[0]]
    dhn2 = matmul(dup, wts["ffn_up"], mode="nt", out_dtype=F32, name="d_norm_ffn_out", tk=1024)
    g["ffn_up"] = matmul(hn2, dup, mode="tn", out_dtype=F32, name="dw_ffn_up", tm=512, tn=1024, tk=1040, out_shards=4)
    dh1, g["norm_ffn_w"] = rmsnorm_backward(h1, wts["norm_ffn_w"], dhn2, dh2, n_pad=n_pad, name="d_norm_ffn")
    d_mixed = matmul(dh1, wts["w_out"], mode="nt", out_dtype=F32, name="d_mixed", tn=512)
    g["w_out"] = matmul(mixed, dh1, mode="tn", out_dtype=F32, name="dw_out", tm=512, tn=1024, tk=1040)

    dq, dk, dv, dz_dn, dbc, dac, g_al, g_db, g_nw = gdn_backward(
        act_qkv, proj, bcol, acol, dn_al, dn_db, dn_nw, s_dn, d_mixed, n_pad=n_pad, z_off=lay.z_off)
    g["dn_a_log"], g["dn_dt_bias"] = g_al.reshape(1, hd), g_db.reshape(1, hd)
    g["dn_norm_w"] = jnp.sum(g_nw, axis=0)
    dxs, dbm, dcm, dz_m2, ddt, g_al, g_db, g_dk, g["m2_norm_w"] = ssd_backward(
        act_xbc, proj, dtcol, m2_al, m2_db, m2_dk, wts["m2_norm_w"], s_m2, d_mixed, n_pad=n_pad, z_off=lay.m2z_off)
    g["m2_a_log"], g["m2_dt_bias"], g["m2_d"] = g_al.reshape(1, hm), g_db.reshape(1, hm), g_dk.reshape(1, hm)

    kc = wts["dn_conv_w"].shape[0]
    dproj_parts, dw_parts = [], []
    for idx, dpart in enumerate((dq, dk, dv)):
        dx, dw, _ = conv_silu_backward(proj, wts["dn_conv_w"][:, idx * d:(idx + 1) * d], zeros_b[:, :d], dpart,
                                       x_off=idx * d, name=f"d_dn_conv{idx}")
        dproj_parts.append(dx)
        dw_parts.append(dw[:kc])
    g["dn_conv_w"] = jnp.concatenate(dw_parts, axis=1)
    dproj_parts += [dz_dn, dz_m2]
    dw_parts, db_parts = [], []
    off = 0
    for idx, dpart in enumerate((dxs, dbm, dcm)):
        wd = dpart.shape[1]
        dx, dw, db = conv_silu_backward(proj, wts["m2_conv_w"][:, off:off + wd], wts["m2_conv_b"][:, off:off + wd], dpart,
                                        x_off=lay.xbc_off + off, name=f"d_m2_conv{idx}")
        dproj_parts.append(dx)
        dw_parts.append(dw[:kc])
        db_parts.append(db[:1])
        off += wd
    g["m2_conv_w"] = jnp.concatenate(dw_parts, axis=1)
    g["m2_conv_b"] = jnp.concatenate(db_parts, axis=1)
    dsmall = jnp.concatenate([jnp.transpose(dbc[..., 0]), jnp.transpose(dac[..., 0]), jnp.transpose(ddt[..., 0]),
                              jnp.zeros((t, LANES - lay.n_small), F32)], axis=1)
    dproj_parts.append(dsmall.astype(BF16))
    dproj = jnp.concatenate(dproj_parts, axis=1)
    dhn1 = matmul(dproj, wts["w_in"], mode="nt", out_dtype=F32, name="d_norm_mix_out", tk=640)
    g["w_in"] = matmul(hn1, dproj, mode="tn", out_dtype=F32, name="dw_in", tm=512, tn=640, tk=1040)
    dh0, g["norm_mix_w"] = rmsnorm_backward(h0, wts["norm_mix_w"], dhn1, dh1, n_pad=n_pad, name="d_norm_mix")
    g["norm_final_w"] = d_nfw
    return loss[0, 0], dh0, g


MESH = pl.DeviceIdType.MESH
ANY = pl.BlockSpec(memory_space=pl.ANY)
N_CHIPS = 4
N_DEV = 8


def _mesh_pos():
    return lax.axis_index("x"), lax.axis_index("y"), lax.axis_index("c")


def _other_chips(x, y):
    return [(1 - x, y), (x, 1 - y), (1 - x, 1 - y)]


def _rcopy(src, dst, send_sems, recv_sems, k, to):
    return pltpu.make_async_remote_copy(src_ref=src, dst_ref=dst, send_sem=send_sems.at[k], recv_sem=recv_sems.at[k],
                                        device_id=to, device_id_type=MESH)


def gather_shards(shards, *, name):
    n = len(shards)

    def body(*refs):
        ins, outs = refs[:n], refs[n:2 * n]
        send_sems, recv_sems, local_sems = refs[2 * n:]
        x, y, c = _mesh_pos()
        sib = (x, y, 1 - c)
        chips = _other_chips(x, y)
        me = 2 * x + y
        halves = [r.shape[0] // 2 for r in ins]
        local = [pltpu.make_async_copy(ins[a], outs[a].at[me], local_sems.at[a]) for a in range(n)]
        for cp in local:
            cp.start()
        sent = []
        for a in range(n):
            rows = pl.ds(c * halves[a], halves[a])
            for j, (px, py) in enumerate(chips):
                cp = _rcopy(ins[a].at[rows], outs[a].at[me, rows], send_sems, recv_sems, 6 * a + j, (px, py, c))
                cp.start()
                sent.append(cp)
        for a in range(n):
            rows = pl.ds(c * halves[a], halves[a])
            for j, (px, py) in enumerate(chips):
                landed = outs[a].at[2 * px + py, rows]
                _rcopy(landed, landed, send_sems, recv_sems, 6 * a + j, (px, py, c)).wait_recv()
                cp = _rcopy(landed, landed, send_sems, recv_sems, 6 * a + 3 + j, sib)
                cp.start()
                sent.append(cp)
        for a in range(n):
            rows = pl.ds((1 - c) * halves[a], halves[a])
            for j, (px, py) in enumerate(chips):
                passed = outs[a].at[2 * px + py, rows]
                _rcopy(passed, passed, send_sems, recv_sems, 6 * a + 3 + j, sib).wait_recv()
        for cp in sent:
            cp.wait_send()
        for cp in local:
            cp.wait()

    return pl.pallas_call(
        body, name=name,
        in_specs=[ANY] * n, out_specs=[ANY] * n,
        out_shape=[jax.ShapeDtypeStruct((N_CHIPS,) + s.shape, s.dtype) for s in shards],
        scratch_shapes=[pltpu.SemaphoreType.DMA((6 * n,)), pltpu.SemaphoreType.DMA((6 * n,)),
                        pltpu.SemaphoreType.DMA((n,))],
        compiler_params=pltpu.CompilerParams(has_side_effects=True),
    )(*shards)


def pair_exchange(grads, *, name):
    n = len(grads)

    def body(*refs):
        ins, outs = refs[:n], refs[n:2 * n]
        send_sems, recv_sems = refs[2 * n:]
        x, y, c = _mesh_pos()
        cps = []
        for a in range(n):
            half = ins[a].shape[1] // 2
            cp = _rcopy(ins[a].at[:, pl.ds((1 - c) * half, half), :], outs[a], send_sems, recv_sems, a, (x, y, 1 - c))
            cp.start()
            cps.append(cp)
        for cp in cps:
            cp.wait()

    return pl.pallas_call(
        body, name=name, in_specs=[ANY] * n, out_specs=[ANY] * n,
        out_shape=[jax.ShapeDtypeStruct((s.shape[0], s.shape[1] // 2, s.shape[2]), s.dtype) for s in grads],
        scratch_shapes=[pltpu.SemaphoreType.DMA((n,)), pltpu.SemaphoreType.DMA((n,))],
        compiler_params=pltpu.CompilerParams(has_side_effects=True),
    )(*grads)


def chip_exchange(parts, *, name):
    n = len(parts)

    def body(*refs):
        ins, outs = refs[:n], refs[n:2 * n]
        send_sems, recv_sems = refs[2 * n:]
        x, y, c = _mesh_pos()
        cps = []
        for a in range(n):
            for j, (px, py) in enumerate(_other_chips(x, y)):
                cp = _rcopy(ins[a].at[2 * px + py], outs[a].at[j], send_sems, recv_sems, 3 * a + j, (px, py, c))
                cp.start()
                cps.append(cp)
        for cp in cps:
            cp.wait()

    return pl.pallas_call(
        body, name=name, in_specs=[ANY] * n, out_specs=[ANY] * n,
        out_shape=[jax.ShapeDtypeStruct((3,) + s.shape[1:], s.dtype) for s in parts],
        scratch_shapes=[pltpu.SemaphoreType.DMA((3 * n,)), pltpu.SemaphoreType.DMA((3 * n,))],
        compiler_params=pltpu.CompilerParams(has_side_effects=True),
    )(*parts)


def pair_join(halves, *, name):
    n = len(halves)

    def body(*refs):
        ins, outs = refs[:n], refs[n:2 * n]
        send_sems, recv_sems, local_sems = refs[2 * n:]
        x, y, c = _mesh_pos()
        cps = []
        for a in range(n):
            half = ins[a].shape[0]
            rows = outs[a].at[pl.ds(c * half, half)]
            lc = pltpu.make_async_copy(ins[a], rows, local_sems.at[a])
            lc.start()
            cp = _rcopy(ins[a], rows, send_sems, recv_sems, a, (x, y, 1 - c))
            cp.start()
            cps += [lc, cp]
        for cp in cps:
            cp.wait()

    return pl.pallas_call(
        body, name=name, in_specs=[ANY] * n, out_specs=[ANY] * n,
        out_shape=[jax.ShapeDtypeStruct((2 * s.shape[0], s.shape[1]), s.dtype) for s in halves],
        scratch_shapes=[pltpu.SemaphoreType.DMA((n,)), pltpu.SemaphoreType.DMA((n,)), pltpu.SemaphoreType.DMA((n,))],
        compiler_params=pltpu.CompilerParams(has_side_effects=True),
    )(*halves)


def gather_all(v, *, name):
    def body(in_ref, out_ref, send_sems, recv_sems, local_sem):
        x, y, c = _mesh_pos()
        mine = out_ref.at[4 * x + 2 * y + c]
        lc = pltpu.make_async_copy(in_ref, mine, local_sem)
        lc.start()
        cps = []
        for k in range(1, N_DEV):
            flip = lambda v, bit: 1 - v if (k >> bit) & 1 else v
            cp = _rcopy(in_ref, mine, send_sems, recv_sems, k - 1, (flip(x, 2), flip(y, 1), flip(c, 0)))
            cp.start()
            cps.append(cp)
        for cp in cps:
            cp.wait()
        lc.wait()

    return pl.pallas_call(
        body, name=name, in_specs=[ANY], out_specs=ANY,
        out_shape=jax.ShapeDtypeStruct((N_DEV,) + v.shape, v.dtype),
        scratch_shapes=[pltpu.SemaphoreType.DMA((N_DEV - 1,)), pltpu.SemaphoreType.DMA((N_DEV - 1,)),
                        pltpu.SemaphoreType.DMA(())],
        compiler_params=pltpu.CompilerParams(has_side_effects=True),
    )(v)


def _sum_tile(rows, cols):
    return _pick(rows, max(16, (1024 * 1024) // (4 * cols) // 16 * 16), 16)


def pair_add(g, rb, chip, c, *, name):
    _, r, cols = g.shape
    half = r // 2
    tr = _sum_tile(half, cols)
    nrt = half // tr

    def body(s_ref, g_ref, rb_ref, p16_ref, own_ref):
        v = g_ref[...] + rb_ref[...]
        p16_ref[...] = v.astype(p16_ref.dtype)

        @pl.when(pl.program_id(1) == s_ref[0])
        def _():
            own_ref[...] = v

    grid_spec = pltpu.PrefetchScalarGridSpec(
        num_scalar_prefetch=1, grid=(nrt, N_CHIPS),
        in_specs=[pl.BlockSpec((None, tr, cols), lambda i, k, s: (k, s[1] * nrt + i, 0)),
                  pl.BlockSpec((None, tr, cols), lambda i, k, s: (k, i, 0))],
        out_specs=[pl.BlockSpec((None, tr, cols), lambda i, k, s: (k, i, 0)),
                   pl.BlockSpec((tr, cols), lambda i, k, s: (i, 0))])
    return pl.pallas_call(
        body, name=name, grid_spec=grid_spec,
        out_shape=[jax.ShapeDtypeStruct((N_CHIPS, half, cols), BF16), jax.ShapeDtypeStruct((half, cols), F32)],
        compiler_params=_cparams(("arbitrary", "arbitrary")),
    )(jnp.stack([chip, c]).astype(jnp.int32), g, rb)


def chip_add(own, q, *, name):
    r, cols = own.shape
    tr = _sum_tile(r, cols)

    def body(own_ref, q_ref, o_ref):
        o_ref[...] = ((own_ref[...] + q_ref[0].astype(F32)) + q_ref[1].astype(F32)) + q_ref[2].astype(F32)

    return pl.pallas_call(
        body, name=name, grid=(r // tr,),
        in_specs=[pl.BlockSpec((tr, cols), lambda i: (i, 0)), pl.BlockSpec((3, tr, cols), lambda i: (0, i, 0))],
        out_specs=pl.BlockSpec((tr, cols), lambda i: (i, 0)),
        out_shape=jax.ShapeDtypeStruct((r, cols), F32),
        compiler_params=_cparams(("parallel",)),
    )(own, q)


def sum_slots(v, *, name):
    n, r, cols = v.shape
    tr = _sum_tile(r, cols)

    def body(v_ref, o_ref):
        acc = v_ref[0]
        for k in range(1, n):
            acc = acc + v_ref[k]
        o_ref[...] = acc

    return pl.pallas_call(
        body, name=name, grid=(r // tr,),
        in_specs=[pl.BlockSpec((n, tr, cols), lambda i: (0, i, 0))],
        out_specs=pl.BlockSpec((tr, cols), lambda i: (i, 0)),
        out_shape=jax.ShapeDtypeStruct((r, cols), F32),
        compiler_params=_cparams(("parallel",)),
    )(v)


PACK_ROWS = 16


def _pack(arrays):
    parts = []
    for a in arrays:
        flat = a.reshape(-1).astype(F32)
        size = PACK_ROWS * LANES
        pad = (-flat.shape[0]) % size
        parts.append(jnp.pad(flat, (0, pad)))
    return jnp.concatenate(parts).reshape(-1, LANES)


def _unpack(slab, shapes):
    out, row = [], 0
    for shp in shapes:
        n = 1
        for s in shp:
            n *= s
        rows = -(-n // (PACK_ROWS * LANES)) * PACK_ROWS
        out.append(slab[row:row + rows].reshape(-1)[:n].reshape(shp))
        row += rows
    return out


WEIGHT_NAMES = ("meta_tokens", "norm_mix_w", "w_in", "dn_conv_w", "dn_a_log", "dn_dt_bias", "dn_norm_w", "m2_conv_w",
                "m2_conv_b", "m2_a_log", "m2_dt_bias", "m2_d", "m2_norm_w", "w_out", "norm_ffn_w", "ffn_up",
                "ffn_conv_w", "ffn_down", "norm_final_w")
BIG = ("w_in", "w_out", "ffn_up", "ffn_down")
BIG_COLUMN_SHARDED = ("w_in", "ffn_up")
SMALL_SHARDED = ("meta_tokens", "dn_conv_w", "m2_conv_w", "ffn_conv_w")
SMALL = tuple(n for n in WEIGHT_NAMES if n not in BIG)


def kernel(x, meta_tokens, norm_mix_w, w_in, dn_conv_w, dn_a_log, dn_dt_bias, dn_norm_w, m2_conv_w, m2_conv_b, m2_a_log, m2_dt_bias, m2_d, m2_norm_w, w_out, norm_ffn_w, ffn_up, ffn_conv_w, ffn_down, norm_final_w, loss_target, m_meta_tokens, m_norm_mix_w, m_w_in, m_dn_conv_w, m_dn_a_log, m_dn_dt_bias, m_dn_norm_w, m_m2_conv_w, m_m2_conv_b, m_m2_a_log, m_m2_dt_bias, m_m2_d, m_m2_norm_w, m_w_out, m_norm_ffn_w, m_ffn_up, m_ffn_conv_w, m_ffn_down, m_norm_final_w, v_meta_tokens, v_norm_mix_w, v_w_in, v_dn_conv_w, v_dn_a_log, v_dn_dt_bias, v_dn_norm_w, v_m2_conv_w, v_m2_conv_b, v_m2_a_log, v_m2_dt_bias, v_m2_d, v_m2_norm_w, v_w_out, v_norm_ffn_w, v_ffn_up, v_ffn_conv_w, v_ffn_down, v_norm_final_w):
    args = tuple(locals().values())
    nw = len(WEIGHT_NAMES)
    wt = dict(zip(WEIGHT_NAMES, args[1:1 + nw]))
    mom = dict(zip(WEIGHT_NAMES, args[2 + nw:2 + 2 * nw]))
    var = dict(zip(WEIGHT_NAMES, args[2 + 2 * nw:2 + 3 * nw]))
    xi, yi, ci = _mesh_pos()
    chip = 2 * xi + yi
    d = x.shape[2]
    n_meta = wt["meta_tokens"].shape[0]
    n_pad = CHUNK - n_meta
    lay = Layout(d)

    shard2d = {k: wt[k].reshape(wt[k].shape[-2:]) for k in BIG}
    small_local = [wt[k].reshape(wt[k].shape[-2:]) for k in SMALL_SHARDED]
    gathered = gather_shards([cast_bf16(shard2d[k], name="cast_" + k) for k in BIG] + [_pack(small_local)],
                             name="gather_weights")
    full = {}
    for k, gth in zip(BIG, gathered):
        if k in BIG_COLUMN_SHARDED:
            full[k] = jnp.transpose(gth, (1, 0, 2)).reshape(gth.shape[1], -1)
        else:
            full[k] = gth.reshape(-1, gth.shape[2])
    full["w_in"] = lay.permute_w_in(full["w_in"])
    per_chip = [_unpack(gathered[-1][s], [a.shape for a in small_local]) for s in range(N_CHIPS)]
    for idx, k in enumerate(SMALL_SHARDED):
        full[k] = jnp.concatenate([per_chip[s][idx] for s in range(N_CHIPS)], axis=-1)
    for k in SMALL:
        if k not in SMALL_SHARDED:
            full[k] = wt[k]

    h0 = jnp.concatenate([jnp.zeros((n_pad, d), F32), full["meta_tokens"], x[0]], axis=0)
    target = jnp.concatenate([jnp.zeros((n_pad + n_meta, d), F32), loss_target[0]], axis=0)
    loss_local, dh0, g = local_step(h0, target, full, n_pad=n_pad, n_meta=n_meta)
    grad_x = dh0[n_pad + n_meta:][None]
    g["meta_tokens"] = dh0[n_pad:n_pad + n_meta]
    loss = lax.psum(loss_local, ("x", "y", "c"))

    g["w_in"] = lay.unpermute_w_in(g["w_in"])
    slots = {}
    for k in BIG:
        if k == "ffn_up":
            slots[k] = g[k]
        elif k in BIG_COLUMN_SHARDED:
            slots[k] = jnp.transpose(g[k].reshape(g[k].shape[0], N_CHIPS, -1), (1, 0, 2))
        else:
            slots[k] = g[k].reshape(N_CHIPS, -1, g[k].shape[1])
    from_sibling = pair_exchange([slots[k] for k in BIG], name="grad_pair_exchange")
    p16, own = [], []
    for k, rb in zip(BIG, from_sibling):
        a, b = pair_add(slots[k], rb, chip, ci, name="grad_pair_add_" + k)
        p16.append(a)
        own.append(b)
    from_chips = chip_exchange(p16, name="grad_chip_exchange")
    halves = [chip_add(o, q, name="grad_chip_add_" + k) for k, o, q in zip(BIG, own, from_chips)]
    grad = dict(zip(BIG, pair_join(halves, name="grad_pair_join")))

    small_shapes = [g[k].shape for k in SMALL]
    summed = sum_slots(gather_all(_pack([g[k] for k in SMALL]), name="small_grad_gather"), name="small_grad_sum")
    for k, v in zip(SMALL, _unpack(summed, small_shapes)):
        if k in SMALL_SHARDED:
            width = wt[k].shape[-1]
            v = lax.dynamic_slice_in_dim(v, chip * width, width, axis=v.ndim - 1)
        grad[k] = v

    delta, new_m, new_v = {}, {}, {}
    for k in BIG:
        shp = shard2d[k].shape
        delta[k], new_m[k], new_v[k] = adamw(shard2d[k], grad[k], mom[k].reshape(shp), var[k].reshape(shp),
                                             name="adamw_" + k)
    shapes = [wt[k].shape for k in SMALL]
    packed = adamw(_pack([wt[k] for k in SMALL]), _pack([grad[k] for k in SMALL]), _pack([mom[k] for k in SMALL]),
                   _pack([var[k] for k in SMALL]), name="adamw_small")
    for res, slab in zip((delta, new_m, new_v), packed):
        for k, v in zip(SMALL, _unpack(slab, shapes)):
            res[k] = v
    outs = [loss, grad_x]
    for res in (grad, delta, new_m, new_v):
        outs += [res[k].reshape(wt[k].shape) for k in WEIGHT_NAMES]
    return tuple(outs)
```

```python
import functools
import math

import jax
import jax.numpy as jnp
from jax import lax
from jax.experimental import pallas as pl
from jax.experimental.pallas import tpu as pltpu

F32 = jnp.float32
BF16 = jnp.bfloat16
HI = lax.Precision.HIGHEST

CHUNK = 64
ROW_ALIGN = 128
NORM_EPS = 1e-6
DN_DK = 128
M2_P = 64
M2_N = 128
M2_GROUPS = 4
HEAD_BLOCK = 8
VMEM_LIMIT = 56 * 1024 * 1024

ADAM_LR, ADAM_B1, ADAM_B2, ADAM_EPS, ADAM_WD, ADAM_STEP = 0.001, 0.9, 0.999, 1e-08, 0.01, 10


def _cparams(sem=None):
    return pltpu.CompilerParams(dimension_semantics=sem, vmem_limit_bytes=VMEM_LIMIT)


def _silu(x):
    return x / (1.0 + jnp.exp(-x))


def _sigmoid(x):
    return 1.0 / (1.0 + jnp.exp(-x))


def _softplus(x):
    return jnp.maximum(x, 0.0) + jnp.log(1.0 + jnp.exp(-jnp.abs(x)))


def _tri_masks():
    r = lax.broadcasted_iota(jnp.int32, (CHUNK, CHUNK), 0)
    c = lax.broadcasted_iota(jnp.int32, (CHUNK, CHUNK), 1)
    return (r >= c)[None], (r > c)[None], (r == c)[None]


def _col2row(col, eye):
    return jnp.sum(jnp.where(eye, col, 0.0), axis=1, keepdims=True)


def _cumsum_col(col, causal, eye):
    row = _col2row(col, eye)
    return jnp.sum(jnp.where(causal, row, 0.0), axis=2, keepdims=True)


def _bdot(a, b, dims, precision=None):
    (ca, cb) = dims
    if precision is None:
        a = a.astype(BF16)
        b = b.astype(BF16)
    return lax.dot_general(a, b, (((ca,), (cb,)), ((0,), (0,))), precision=precision,
                           preferred_element_type=F32)


def _dot3(a, b, ca, cb):
    def split(v):
        hi = v.astype(BF16)
        return hi, (v - hi.astype(F32)).astype(BF16)

    def dot(p, q):
        return lax.dot_general(p, q, (((ca,), (cb,)), ((0,), (0,))), preferred_element_type=F32)

    (ah, al), (bh, bl) = split(a), split(b)
    return dot(ah, bh) + (dot(ah, bl) + dot(al, bh))


@jax.custom_vjp
def _bmm3(a, b):
    return _dot3(a, b, 2, 1)


def _bmm3_fwd(a, b):
    return _dot3(a, b, 2, 1), (a, b)


def _bmm3_bwd(res, g):
    a, b = res
    return _dot3(g, b, 2, 2), _dot3(a, g, 1, 1)


_bmm3.defvjp(_bmm3_fwd, _bmm3_bwd)


def _unit_lower_inverse(a_mat, eye):
    inv = jnp.where(eye, 1.0, 0.0) - a_mat
    pw = a_mat
    n = 2
    while n < CHUNK:
        pw = _bmm3(pw, pw)
        inv = inv + _bmm3(inv, pw)
        n *= 2
    return inv


def _gdn_step(state, qa, ka, va, z, braw, araw, a_log, dt_bias, norm_w, vm):
    causal, strict, eye = _tri_masks()
    qa, ka, va = qa * vm, ka * vm, va * vm
    q = qa * lax.rsqrt(jnp.sum(qa * qa, -1, keepdims=True) + NORM_EPS) * (DN_DK ** -0.5)
    k = ka * lax.rsqrt(jnp.sum(ka * ka, -1, keepdims=True) + NORM_EPS)
    beta = _sigmoid(braw) * vm
    g = -jnp.exp(a_log) * _softplus(araw + dt_bias) * vm
    gcum = _cumsum_col(g, causal, eye)
    grow = _col2row(gcum, eye)
    decay = jnp.where(causal, jnp.exp(jnp.where(causal, gcum - grow, 0.0)), 0.0)
    kk = _bdot(k, k, (2, 2))
    a_mat = jnp.where(strict, beta * kk * decay, 0.0)
    tinv = _unit_lower_inverse(a_mat, eye)
    egc = jnp.exp(gcum)
    u = _bmm3(tinv, va * beta)
    w = _bmm3(tinv, k * (beta * egc))
    v_new = u - _bdot(w, state, (2, 1))
    o_inter = _bdot(q * egc, state, (2, 1))
    qk = _bdot(q, k, (2, 2)) * decay
    o = o_inter + _bdot(qk, v_new, (2, 1))
    g_last = jnp.sum(g, axis=1, keepdims=True)
    new_state = state * jnp.exp(g_last) + _bdot(k * jnp.exp(g_last - gcum), v_new, (1, 1))
    o = o * lax.rsqrt(jnp.mean(o * o, -1, keepdims=True) + NORM_EPS) * norm_w * _silu(z)
    return new_state, o


def _valid_rows(chunk, n_pad):
    r = chunk * CHUNK + lax.broadcasted_iota(jnp.int32, (1, CHUNK, 1), 1)
    return jnp.where(r >= n_pad, 1.0, 0.0).astype(F32)


def _split_heads(x, n, w):
    return jnp.stack([x[:, i * w:(i + 1) * w] for i in range(n)], axis=0)


def _merge_heads(x):
    return jnp.concatenate([x[i] for i in range(x.shape[0])], axis=-1)


def gdn_forward(act_qkv, proj, bcol, acol, a_log3, dt_bias3, norm_w3, *, n_pad, z_off, out_cols):
    t, w3 = act_qkv.shape
    wdn = w3 // 3
    heads = wdn // DN_DK
    hb = min(HEAD_BLOCK, heads)
    nhb = heads // hb
    nc = t // CHUNK
    bw = hb * DN_DK
    nqb = wdn // bw

    def body(q_ref, k_ref, v_ref, z_ref, b_ref, a_ref, al_ref, db_ref, nw_ref, out_ref, sall_ref, st_ref):
        n = pl.program_id(1)

        @pl.when(n == 0)
        def _():
            st_ref[...] = jnp.zeros_like(st_ref)

        state = st_ref[...]
        sall_ref[...] = state
        vm = _valid_rows(n, n_pad)
        new_state, o = _gdn_step(state, _split_heads(q_ref[...], hb, DN_DK), _split_heads(k_ref[...], hb, DN_DK),
                                 _split_heads(v_ref[...], hb, DN_DK), _split_heads(z_ref[...], hb, DN_DK),
                                 b_ref[...], a_ref[...], al_ref[...], db_ref[...], nw_ref[...], vm)
        st_ref[...] = new_state
        out_ref[...] = _merge_heads(o).astype(out_ref.dtype)

    col = pl.BlockSpec((hb, CHUNK, 1), lambda h, n: (h, n, 0))
    par = pl.BlockSpec((hb, 1, 1), lambda h, n: (h, 0, 0))
    return pl.pallas_call(
        body, name="gdn_fwd",
        grid=(nhb, nc),
        in_specs=[pl.BlockSpec((CHUNK, bw), lambda h, n: (n, h)),
                  pl.BlockSpec((CHUNK, bw), lambda h, n: (n, nqb + h)),
                  pl.BlockSpec((CHUNK, bw), lambda h, n: (n, 2 * nqb + h)),
                  pl.BlockSpec((CHUNK, bw), lambda h, n: (n, z_off // bw + h)),
                  col, col, par, par,
                  pl.BlockSpec((1, 1, DN_DK), lambda h, n: (0, 0, 0))],
        out_specs=[pl.BlockSpec((CHUNK, bw), lambda h, n: (n, h)),
                   pl.BlockSpec((None, hb, DN_DK, DN_DK), lambda h, n: (n, h, 0, 0))],
        out_shape=[jax.ShapeDtypeStruct((t, out_cols), BF16),
                   jax.ShapeDtypeStruct((nc, heads, DN_DK, DN_DK), F32)],
        scratch_shapes=[pltpu.VMEM((hb, DN_DK, DN_DK), F32)],
        compiler_params=_cparams(("arbitrary", "arbitrary")),
    )(act_qkv, act_qkv, act_qkv, proj, bcol, acol, a_log3, dt_bias3, norm_w3)


def gdn_backward(act_qkv, proj, bcol, acol, a_log3, dt_bias3, norm_w3, s_all, d_mixed, *, n_pad, z_off):
    t, w3 = act_qkv.shape
    wdn = w3 // 3
    heads = wdn // DN_DK
    hb = min(HEAD_BLOCK, heads)
    nhb = heads // hb
    nc = t // CHUNK
    bw = hb * DN_DK
    nqb = wdn // bw

    def body(q_ref, k_ref, v_ref, z_ref, b_ref, a_ref, al_ref, db_ref, nw_ref, s_ref, do_ref,
             dq_ref, dk_ref, dv_ref, dz_ref, dbc_ref, dac_ref, dal_ref, ddb_ref, dnw_ref, ds_ref):
        i = pl.program_id(1)
        n = nc - 1 - i

        @pl.when(i == 0)
        def _():
            ds_ref[...] = jnp.zeros_like(ds_ref)
            dal_ref[...] = jnp.zeros_like(dal_ref)
            ddb_ref[...] = jnp.zeros_like(ddb_ref)
            dnw_ref[...] = jnp.zeros_like(dnw_ref)

        vm = _valid_rows(n, n_pad)
        step = functools.partial(_gdn_step, vm=vm)
        _, vjp = jax.vjp(step, s_ref[...], _split_heads(q_ref[...], hb, DN_DK), _split_heads(k_ref[...], hb, DN_DK),
                         _split_heads(v_ref[...], hb, DN_DK), _split_heads(z_ref[...], hb, DN_DK),
                         b_ref[...], a_ref[...], al_ref[...], db_ref[...], nw_ref[...])
        ds, dq, dk, dv, dz, dbr, dar, dal, ddb, dnw = vjp((ds_ref[...], _split_heads(do_ref[...], hb, DN_DK)))
        ds_ref[...] = ds
        dq_ref[...] = _merge_heads(dq)
        dk_ref[...] = _merge_heads(dk)
        dv_ref[...] = _merge_heads(dv)
        dz_ref[...] = _merge_heads(dz).astype(dz_ref.dtype)
        dbc_ref[...] = dbr
        dac_ref[...] = dar
        dal_ref[...] += dal
        ddb_ref[...] += ddb
        dnw_ref[...] += dnw[0]

    rev = lambda n: nc - 1 - n
    col = pl.BlockSpec((hb, CHUNK, 1), lambda h, n: (h, rev(n), 0))
    par = pl.BlockSpec((hb, 1, 1), lambda h, n: (h, 0, 0))
    blk = lambda off: pl.BlockSpec((CHUNK, bw), lambda h, n: (rev(n), off + h))
    return pl.pallas_call(
        body, name="gdn_bwd",
        grid=(nhb, nc),
        in_specs=[blk(0), blk(nqb), blk(2 * nqb), blk(z_off // bw), col, col, par, par,
                  pl.BlockSpec((1, 1, DN_DK), lambda h, n: (0, 0, 0)),
                  pl.BlockSpec((None, hb, DN_DK, DN_DK), lambda h, n: (rev(n), h, 0, 0)),
                  blk(0)],
        out_specs=[blk(0), blk(0), blk(0), blk(z_off // bw), col, col, par, par,
                   pl.BlockSpec((None, 1, DN_DK), lambda h, n: (h, 0, 0))],
        out_shape=[jax.ShapeDtypeStruct((t, wdn), F32)] * 3
        + [jax.ShapeDtypeStruct((t, proj.shape[1]), BF16),
           jax.ShapeDtypeStruct((heads, t, 1), F32), jax.ShapeDtypeStruct((heads, t, 1), F32),
           jax.ShapeDtypeStruct((heads, 1, 1), F32), jax.ShapeDtypeStruct((heads, 1, 1), F32),
           jax.ShapeDtypeStruct((nhb, 1, DN_DK), F32)],
        scratch_shapes=[pltpu.VMEM((hb, DN_DK, DN_DK), F32)],
        compiler_params=_cparams(("arbitrary", "arbitrary")),
    )(act_qkv, act_qkv, act_qkv, proj, bcol, acol, a_log3, dt_bias3, norm_w3, s_all, d_mixed)


def _dot2(a, b, ca, cb):
    return lax.dot_general(a.astype(BF16), b.astype(BF16), (((ca,), (cb,)), ((), ())),
                           preferred_element_type=F32)


def _ssd_step(state, xa, bmat, cmat, z, dtraw, a_log, dt_bias, dskip, norm_w, vm):
    causal, _, eye = _tri_masks()
    r_heads, p, n_state = state.shape
    gw = r_heads * p
    vm2 = vm[0]
    xa, bmat, cmat = xa * vm2, bmat * vm2, cmat * vm2
    dt = _softplus(dtraw + dt_bias) * vm
    a = dt * (-jnp.exp(a_log))
    acs = _cumsum_col(a, causal, eye)
    arow = _col2row(acs, eye)
    lmat = jnp.where(causal, jnp.exp(jnp.where(causal, acs - arow, 0.0)), 0.0)
    hsel = (lax.broadcasted_iota(jnp.int32, (r_heads, 1, gw), 2) // p
            == lax.broadcasted_iota(jnp.int32, (r_heads, 1, gw), 0))

    def spread(col):
        return jnp.sum(jnp.where(hsel, col, 0.0), axis=0)

    xdt = xa * spread(dt)
    cb = _dot2(cmat, bmat, 1, 1)
    m = (cb[None] * lmat).reshape(r_heads * CHUNK, CHUNK)
    yb = _dot2(m, xdt, 1, 0).reshape(r_heads, CHUNK, gw)
    y_diag = jnp.sum(jnp.where(hsel, yb, 0.0), axis=0)
    s2 = state.reshape(gw, n_state)
    y_off = _dot2(cmat, s2, 1, 1) * spread(jnp.exp(acs))
    a_last = jnp.sum(a, axis=1, keepdims=True)
    upd = _dot2(xdt * spread(jnp.exp(a_last - acs)), bmat, 0, 0)
    new_state = state * jnp.exp(a_last) + upd.reshape(r_heads, p, n_state)
    y = y_diag + y_off + xa * spread(dskip)
    y = y * _silu(z)
    y = y * lax.rsqrt(jnp.mean(y * y, -1, keepdims=True) + NORM_EPS) * norm_w
    return new_state, y


def _ssd_dims(act_xbc):
    t, wx = act_xbc.shape
    wm = wx - 2 * M2_GROUPS * M2_N
    gw = wm // M2_GROUPS
    return t, wm, gw, gw // M2_P, wm // M2_P, t // CHUNK


def ssd_forward(act_xbc, proj, dtcol, a_log3, dt_bias3, dskip3, norm_w, mixed, *, n_pad, z_off):
    t, wm, gw, rh, heads, nc = _ssd_dims(act_xbc)
    nb = wm // M2_N
    ob = (mixed.shape[1] - wm) // gw

    def body(x_ref, b_ref, c_ref, z_ref, dt_ref, al_ref, db_ref, dk_ref, nw_ref, _, out_ref, sall_ref, st_ref):
        n = pl.program_id(1)

        @pl.when(n == 0)
        def _():
            st_ref[...] = jnp.zeros_like(st_ref)

        state = st_ref[...]
        sall_ref[...] = state
        new_state, y = _ssd_step(state, x_ref[...], b_ref[...], c_ref[...], z_ref[...], dt_ref[...],
                                 al_ref[...], db_ref[...], dk_ref[...], nw_ref[...], _valid_rows(n, n_pad))
        st_ref[...] = new_state
        out_ref[...] = y.astype(out_ref.dtype)

    par = pl.BlockSpec((rh, 1, 1), lambda g, n: (g, 0, 0))
    return pl.pallas_call(
        body, name="ssd_fwd",
        grid=(M2_GROUPS, nc),
        in_specs=[pl.BlockSpec((CHUNK, gw), lambda g, n: (n, g)),
                  pl.BlockSpec((CHUNK, M2_N), lambda g, n: (n, nb + g)),
                  pl.BlockSpec((CHUNK, M2_N), lambda g, n: (n, nb + M2_GROUPS + g)),
                  pl.BlockSpec((CHUNK, gw), lambda g, n: (n, z_off // gw + g)),
                  pl.BlockSpec((rh, CHUNK, 1), lambda g, n: (g, n, 0)),
                  par, par, par,
                  pl.BlockSpec((1, gw), lambda g, n: (0, g)),
                  pl.BlockSpec(memory_space=pl.ANY)],
        out_specs=[pl.BlockSpec((CHUNK, gw), lambda g, n: (n, ob + g)),
                   pl.BlockSpec((None, rh, M2_P, M2_N), lambda g, n: (n, g, 0, 0))],
        out_shape=[jax.ShapeDtypeStruct(mixed.shape, mixed.dtype),
                   jax.ShapeDtypeStruct((nc, heads, M2_P, M2_N), F32)],
        input_output_aliases={9: 0},
        scratch_shapes=[pltpu.VMEM((rh, M2_P, M2_N), F32)],
        compiler_params=_cparams(("arbitrary", "arbitrary")),
    )(act_xbc, act_xbc, act_xbc, proj, dtcol, a_log3, dt_bias3, dskip3, norm_w, mixed)


def ssd_backward(act_xbc, proj, dtcol, a_log3, dt_bias3, dskip3, norm_w, s_all, d_mixed, dproj, *, n_pad, z_off):
    t, wm, gw, rh, heads, nc = _ssd_dims(act_xbc)
    nb = wm // M2_N

    def body(x_ref, b_ref, c_ref, z_ref, dt_ref, al_ref, db_ref, dk_ref, nw_ref, s_ref, dy_ref, _,
             dx_ref, dbm_ref, dcm_ref, dz_ref, ddt_ref, dal_ref, ddb_ref, ddk_ref, dnw_ref, ds_ref):
        i = pl.program_id(1)
        n = nc - 1 - i

        @pl.when(i == 0)
        def _():
            ds_ref[...] = jnp.zeros_like(ds_ref)
            dal_ref[...] = jnp.zeros_like(dal_ref)
            ddb_ref[...] = jnp.zeros_like(ddb_ref)
            ddk_ref[...] = jnp.zeros_like(ddk_ref)
            dnw_ref[...] = jnp.zeros_like(dnw_ref)

        step = functools.partial(_ssd_step, vm=_valid_rows(n, n_pad))
        _, vjp = jax.vjp(step, s_ref[...], x_ref[...], b_ref[...], c_ref[...], z_ref[...], dt_ref[...],
                         al_ref[...], db_ref[...], dk_ref[...], nw_ref[...])
        ds, dx, dbm, dcm, dz, ddt, dal, ddb, ddk, dnw = vjp((ds_ref[...], dy_ref[...]))
        ds_ref[...] = ds
        dx_ref[...] = dx
        dbm_ref[...] = dbm
        dcm_ref[...] = dcm
        dz_ref[...] = dz.astype(dz_ref.dtype)
        ddt_ref[...] = ddt
        dal_ref[...] += dal
        ddb_ref[...] += ddb
        ddk_ref[...] += ddk
        dnw_ref[...] += dnw

    rev = lambda n: nc - 1 - n
    par = pl.BlockSpec((rh, 1, 1), lambda g, n: (g, 0, 0))
    wide = lambda off: pl.BlockSpec((CHUNK, gw), lambda g, n: (rev(n), off + g))
    narrow = lambda off: pl.BlockSpec((CHUNK, M2_N), lambda g, n: (rev(n), off + g))
    col = pl.BlockSpec((rh, CHUNK, 1), lambda g, n: (g, rev(n), 0))
    gn = M2_GROUPS * M2_N
    return pl.pallas_call(
        body, name="ssd_bwd",
        grid=(M2_GROUPS, nc),
        in_specs=[wide(0), narrow(nb), narrow(nb + M2_GROUPS), wide(z_off // gw), col, par, par, par,
                  pl.BlockSpec((1, gw), lambda g, n: (0, g)),
                  pl.BlockSpec((None, rh, M2_P, M2_N), lambda g, n: (rev(n), g, 0, 0)),
                  wide(M2_GROUPS), pl.BlockSpec(memory_space=pl.ANY)],
        out_specs=[wide(0), narrow(0), narrow(0), wide(z_off // gw), col, par, par, par,
                   pl.BlockSpec((1, gw), lambda g, n: (0, g))],
        out_shape=[jax.ShapeDtypeStruct((t, wm), F32), jax.ShapeDtypeStruct((t, gn), F32),
                   jax.ShapeDtypeStruct((t, gn), F32), jax.ShapeDtypeStruct(dproj.shape, dproj.dtype),
                   jax.ShapeDtypeStruct((heads, t, 1), F32),
                   jax.ShapeDtypeStruct((heads, 1, 1), F32), jax.ShapeDtypeStruct((heads, 1, 1), F32),
                   jax.ShapeDtypeStruct((heads, 1, 1), F32), jax.ShapeDtypeStruct((1, wm), F32)],
        input_output_aliases={11: 3},
        scratch_shapes=[pltpu.VMEM((rh, M2_P, M2_N), F32)],
        compiler_params=_cparams(("arbitrary", "arbitrary")),
    )(act_xbc, act_xbc, act_xbc, proj, dtcol, a_log3, dt_bias3, dskip3, norm_w, s_all, d_mixed, dproj)


SUBLANES = 8
LANES = 128


def _pick(dim, target, align):
    best = None
    for d in range(align, min(dim, target) + 1, align):
        if dim % d == 0:
            best = d
    return dim if best is None else best


def _row_tile(t):
    return _pick(t, 512, 16)


def matmul(a, b, *, mode, out_dtype, name, tm=1056, tn=512, tk=2048, residual=None, out_shards=1,
           a_shards=1, b_shards=1):
    if mode == "tn":
        kd, m = a.shape
        n = b.shape[-1] * b_shards
    else:
        m, kd = a.shape[-2], a.shape[-1] * a_shards
        n = b.shape[1] if mode == "nn" else b.shape[0]
    tm = _pick(m, tm, LANES if mode == "tn" else 16)
    ks = kd // a_shards
    tk = _pick(ks, tk, LANES)
    nkb = ks // tk
    nk = kd // tk
    ns_o, ns_b = n // out_shards, n // b_shards
    tn = _pick(math.gcd(ns_o, ns_b), tn, LANES)
    npb_o, npb_b = ns_o // tn, ns_b // tn
    if mode == "tn":
        a_spec = pl.BlockSpec((tk, tm), lambda i, j, k: (k, i))
    elif a_shards == 1:
        a_spec = pl.BlockSpec((tm, tk), lambda i, j, k: (i, k))
    else:
        a_spec = pl.BlockSpec((None, tm, tk), lambda i, j, k: (k // nkb, i, k % nkb))
    contract = ((1,), (1,)) if mode == "nt" else ((1,), (0,))
    if mode == "nt":
        b_spec = pl.BlockSpec((tn, tk), lambda i, j, k: (j, k))
    elif b_shards == 1:
        b_spec = pl.BlockSpec((tk, tn), lambda i, j, k: (k, j))
    else:
        b_spec = pl.BlockSpec((None, tk, tn), lambda i, j, k: (j // npb_b, k, j % npb_b))
    has_res = residual is not None

    def body(*refs):
        a_ref, b_ref = refs[0], refs[1]
        r_ref = refs[2] if has_res else None
        o_ref = refs[3] if has_res else refs[2]
        acc_ref = refs[-1]
        k = pl.program_id(2)
        if mode == "tn":
            at_ref = refs[4] if has_res else refs[3]

            @pl.when(pl.program_id(1) == 0)
            def _():
                at_ref[k] = jnp.transpose(a_ref[...].astype(F32)).astype(BF16)

            lhs = at_ref[k]
        else:
            lhs = a_ref[...].astype(BF16)
        part = lax.dot_general(lhs, b_ref[...].astype(BF16), (contract, ((), ())), preferred_element_type=F32)

        def finish(total):
            if has_res:
                total = total + r_ref[...]
            o_ref[...] = total.astype(o_ref.dtype)

        if nk == 1:
            finish(part)
        else:
            @pl.when(k == 0)
            def _():
                acc_ref[...] = part

            @pl.when((k > 0) & (k < nk - 1))
            def _():
                acc_ref[...] += part

            @pl.when(k == nk - 1)
            def _():
                finish(acc_ref[...] + part)

    in_specs = [a_spec, b_spec]
    args = [a, b]
    if has_res:
        in_specs.append(pl.BlockSpec((tm, tn), lambda i, j, k: (i, j)))
        args.append(residual)
    if out_shards == 1:
        out_spec = pl.BlockSpec((tm, tn), lambda i, j, k: (i, j))
        out_shape = jax.ShapeDtypeStruct((m, n), out_dtype)
    else:
        out_spec = pl.BlockSpec((None, tm, tn), lambda i, j, k: (j // npb_o, i, j % npb_o))
        out_shape = jax.ShapeDtypeStruct((out_shards, m, ns_o), out_dtype)
    scratch = [pltpu.VMEM((nk, tm, tk), BF16)] if mode == "tn" else []
    if nk > 1:
        scratch.append(pltpu.VMEM((tm, tn), F32))
    return pl.pallas_call(
        body, name=name, grid=(m // tm, n // tn, nk),
        in_specs=in_specs, out_specs=out_spec, out_shape=out_shape, scratch_shapes=scratch,
        compiler_params=_cparams(("parallel", "arbitrary", "arbitrary")),
    )(*args)


def rmsnorm_forward(x, w, *, name):
    t, d = x.shape
    tm = _row_tile(t)

    def body(x_ref, w_ref, o_ref):
        xv = x_ref[...]
        r = lax.rsqrt(jnp.mean(xv * xv, -1, keepdims=True) + NORM_EPS)
        o_ref[...] = (xv * r * w_ref[...]).astype(o_ref.dtype)

    return pl.pallas_call(
        body, name=name, grid=(t // tm,),
        in_specs=[pl.BlockSpec((tm, d), lambda i: (i, 0)), pl.BlockSpec((1, d), lambda i: (0, 0))],
        out_specs=pl.BlockSpec((tm, d), lambda i: (i, 0)),
        out_shape=jax.ShapeDtypeStruct((t, d), BF16),
        compiler_params=_cparams(("parallel",)),
    )(x, w)


def _rmsnorm_grads(xv, wv, dy):
    r = lax.rsqrt(jnp.mean(xv * xv, -1, keepdims=True) + NORM_EPS)
    xh = xv * r
    g = dy * wv
    dx = r * (g - xh * jnp.mean(g * xh, -1, keepdims=True))
    return dx, jnp.sum(dy * xh, axis=0, keepdims=True)


def rmsnorm_backward(x, w, dy, dres, *, n_pad, name):
    t, d = x.shape
    tm = _row_tile(t)

    def body(x_ref, w_ref, dy_ref, dr_ref, dx_ref, dx16_ref, dw_ref):
        i = pl.program_id(0)

        @pl.when(i == 0)
        def _():
            dw_ref[...] = jnp.zeros_like(dw_ref)

        dx, dw = _rmsnorm_grads(x_ref[...], w_ref[...], dy_ref[...])
        rows = i * tm + lax.broadcasted_iota(jnp.int32, (tm, 1), 0)
        dx = jnp.where(rows >= n_pad, dx + dr_ref[...], 0.0)
        dx_ref[...] = dx
        dx16_ref[...] = dx.astype(BF16)
        dw_ref[...] += dw

    row = pl.BlockSpec((tm, d), lambda i: (i, 0))
    vec = pl.BlockSpec((1, d), lambda i: (0, 0))
    return pl.pallas_call(
        body, name=name, grid=(t // tm,),
        in_specs=[row, vec, row, row], out_specs=[row, row, vec],
        out_shape=[jax.ShapeDtypeStruct((t, d), F32), jax.ShapeDtypeStruct((t, d), BF16),
                   jax.ShapeDtypeStruct((1, d), F32)],
        compiler_params=_cparams(("arbitrary",)),
    )(x, w, dy, dres)


def loss_head(h, w, target, *, n_skip):
    t, d = h.shape
    tm = _row_tile(t)

    def body(x_ref, w_ref, y_ref, loss_ref, dx_ref, dx16_ref, dw_ref):
        i = pl.program_id(0)

        @pl.when(i == 0)
        def _():
            dw_ref[...] = jnp.zeros_like(dw_ref)
            loss_ref[...] = jnp.zeros_like(loss_ref)

        xv, wv = x_ref[...], w_ref[...]
        r = lax.rsqrt(jnp.mean(xv * xv, -1, keepdims=True) + NORM_EPS)
        rows = i * tm + lax.broadcasted_iota(jnp.int32, (tm, 1), 0)
        err = jnp.where(rows >= n_skip, xv * r * wv - y_ref[...], 0.0)
        loss_ref[...] += 0.5 * jnp.sum(jnp.mean(err * err, -1, keepdims=True))
        dx, dw = _rmsnorm_grads(xv, wv, err * (1.0 / d))
        dx_ref[...] = dx
        dx16_ref[...] = dx.astype(BF16)
        dw_ref[...] += dw

    row = pl.BlockSpec((tm, d), lambda i: (i, 0))
    vec = pl.BlockSpec((1, d), lambda i: (0, 0))
    return pl.pallas_call(
        body, name="loss_head", grid=(t // tm,),
        in_specs=[row, vec, row],
        out_specs=[pl.BlockSpec((1, LANES), lambda i: (0, 0)), row, row, vec],
        out_shape=[jax.ShapeDtypeStruct((1, LANES), F32), jax.ShapeDtypeStruct((t, d), F32),
                   jax.ShapeDtypeStruct((t, d), BF16), jax.ShapeDtypeStruct((1, d), F32)],
        compiler_params=_cparams(("arbitrary",)),
    )(h, w, target)


HALO = SUBLANES


def _conv_rows(scr_ref, w, first, rows):
    kk = w.shape[0]
    acc = w[0:1, :] * scr_ref[pl.ds(first, rows), :]
    for j in range(1, kk):
        acc = acc + w[j:j + 1, :] * scr_ref[pl.ds(first + j, rows), :]
    return acc


def _dsilu(p):
    s = _sigmoid(p)
    return s * (1.0 + p * (1.0 - s))


def conv_silu(x, w, b, *, x_off, name):
    t = x.shape[0]
    kk, width = w.shape
    tm = _row_tile(t)
    tc = _pick(width, 512, LANES)
    ob, nh = x_off // tc, tm // HALO

    def body(prev_ref, x_ref, w_ref, b_ref, o_ref, scr):
        i = pl.program_id(1)
        scr[0:HALO, :] = jnp.where(i > 0, prev_ref[...], 0.0)
        scr[HALO:HALO + tm, :] = x_ref[...]
        pre = _conv_rows(scr, w_ref[...], HALO - (kk - 1), tm) + b_ref[...]
        o_ref[...] = _silu(pre)

    return pl.pallas_call(
        body, name=name, grid=(width // tc, t // tm),
        in_specs=[pl.BlockSpec((HALO, tc), lambda j, i: (jnp.maximum(i * nh - 1, 0), ob + j)),
                  pl.BlockSpec((tm, tc), lambda j, i: (i, ob + j)),
                  pl.BlockSpec((kk, tc), lambda j, i: (0, j)),
                  pl.BlockSpec((1, tc), lambda j, i: (0, j))],
        out_specs=pl.BlockSpec((tm, tc), lambda j, i: (i, j)),
        out_shape=jax.ShapeDtypeStruct((t, width), F32),
        scratch_shapes=[pltpu.VMEM((tm + HALO, tc), F32)],
        compiler_params=_cparams(("parallel", "arbitrary")),
    )(x, x, w, b)


def conv_silu_backward(x, w, b, dact, dst, *, x_off, name):
    t = x.shape[0]
    kk, width = w.shape
    tm = _row_tile(t)
    tc = _pick(width, 512, LANES)
    ob, nh, nt = x_off // tc, tm // HALO, t // tm
    last_h = t // HALO - 1

    def body(prev_ref, x_ref, next_ref, w_ref, b_ref, d_ref, dnext_ref, _, dx_ref, dw_ref, db_ref, scr_x, scr_d):
        i = pl.program_id(1)

        @pl.when(i == 0)
        def _():
            dw_ref[...] = jnp.zeros_like(dw_ref)
            db_ref[...] = jnp.zeros_like(db_ref)

        wv = w_ref[...]
        scr_x[0:HALO, :] = jnp.where(i > 0, prev_ref[...], 0.0)
        scr_x[HALO:HALO + tm, :] = x_ref[...]
        scr_x[HALO + tm:, :] = next_ref[...]
        ext = tm + HALO
        pre = _conv_rows(scr_x, wv, HALO - (kk - 1), ext) + b_ref[...]
        scr_d[0:tm, :] = d_ref[...]
        scr_d[tm:, :] = jnp.where(i < nt - 1, dnext_ref[...], 0.0)
        scr_d[...] = scr_d[...] * _dsilu(pre)
        dx = wv[0:1, :] * scr_d[pl.ds(kk - 1, tm), :]
        for j in range(1, kk):
            dx = dx + wv[j:j + 1, :] * scr_d[pl.ds(kk - 1 - j, tm), :]
        dx_ref[...] = dx.astype(dx_ref.dtype)
        dpre = scr_d[0:tm, :]
        for j in range(kk):
            dw_ref[j:j + 1, :] += jnp.sum(dpre * scr_x[pl.ds(HALO - (kk - 1) + j, tm), :], axis=0, keepdims=True)
        db_ref[0:1, :] += jnp.sum(dpre, axis=0, keepdims=True)

    nxt = lambda j, i: (jnp.minimum((i + 1) * nh, last_h), j)
    acc = pl.BlockSpec((SUBLANES, tc), lambda j, i: (0, j))
    return pl.pallas_call(
        body, name=name, grid=(width // tc, nt),
        in_specs=[pl.BlockSpec((HALO, tc), lambda j, i: (jnp.maximum(i * nh - 1, 0), ob + j)),
                  pl.BlockSpec((tm, tc), lambda j, i: (i, ob + j)),
                  pl.BlockSpec((HALO, tc), lambda j, i: (jnp.minimum((i + 1) * nh, last_h), ob + j)),
                  pl.BlockSpec((kk, tc), lambda j, i: (0, j)),
                  pl.BlockSpec((1, tc), lambda j, i: (0, j)),
                  pl.BlockSpec((tm, tc), lambda j, i: (i, j)),
                  pl.BlockSpec((HALO, tc), nxt),
                  pl.BlockSpec(memory_space=pl.ANY)],
        out_specs=[pl.BlockSpec((tm, tc), lambda j, i: (i, ob + j)), acc, acc],
        out_shape=[jax.ShapeDtypeStruct(dst.shape, dst.dtype), jax.ShapeDtypeStruct((SUBLANES, width), F32),
                   jax.ShapeDtypeStruct((SUBLANES, width), F32)],
        input_output_aliases={7: 0},
        scratch_shapes=[pltpu.VMEM((tm + 2 * HALO, tc), F32), pltpu.VMEM((tm + HALO, tc), F32)],
        compiler_params=_cparams(("parallel", "arbitrary")),
    )(x, x, x, w, b, dact, dact, dst)


def conv_glu(u, w, *, name):
    _, t, f = u.shape
    kk = w.shape[1]
    tm = _row_tile(t)
    tc = _pick(f, 512, LANES)
    nh = tm // HALO

    def body(prev_ref, x_ref, w_ref, o_ref, scr):
        i = pl.program_id(1)
        scr[:, 0:HALO, :] = jnp.where(i > 0, prev_ref[...], 0.0)
        scr[:, HALO:, :] = x_ref[...]
        gate = _conv_rows(scr.at[0], w_ref[0], HALO - (kk - 1), tm)
        val = _conv_rows(scr.at[1], w_ref[1], HALO - (kk - 1), tm)
        o_ref[...] = (_silu(gate) * val).astype(o_ref.dtype)

    return pl.pallas_call(
        body, name=name, grid=(f // tc, t // tm),
        in_specs=[pl.BlockSpec((2, HALO, tc), lambda j, i: (0, jnp.maximum(i * nh - 1, 0), j)),
                  pl.BlockSpec((2, tm, tc), lambda j, i: (0, i, j)),
                  pl.BlockSpec((2, kk, tc), lambda j, i: (0, 0, j))],
        out_specs=pl.BlockSpec((tm, tc), lambda j, i: (i, j)),
        out_shape=jax.ShapeDtypeStruct((t, f), BF16),
        scratch_shapes=[pltpu.VMEM((2, tm + HALO, tc), F32)],
        compiler_params=_cparams(("parallel", "arbitrary")),
    )(u, u, w)


def conv_glu_backward(u, w, dact, *, name):
    _, t, f = u.shape
    kk = w.shape[1]
    tm = _row_tile(t)
    tc = _pick(f, 512, LANES)
    nh, nt = tm // HALO, t // tm
    last_h = t // HALO - 1
    ext = tm + HALO
    first = HALO - (kk - 1)

    def body(prev_ref, x_ref, next_ref, w_ref, d_ref, dn_ref, du_ref, dw_ref, scr_x, scr_d):
        i = pl.program_id(1)

        @pl.when(i == 0)
        def _():
            dw_ref[...] = jnp.zeros_like(dw_ref)

        scr_x[:, 0:HALO, :] = jnp.where(i > 0, prev_ref[...], 0.0)
        scr_x[:, HALO:HALO + tm, :] = x_ref[...]
        scr_x[:, HALO + tm:, :] = next_ref[...]
        gate = _conv_rows(scr_x.at[0], w_ref[0], first, ext)
        val = _conv_rows(scr_x.at[1], w_ref[1], first, ext)
        scr_d[0, 0:tm, :] = d_ref[...]
        scr_d[0, tm:, :] = jnp.where(i < nt - 1, dn_ref[...], 0.0)
        dact = scr_d[0]
        s = _sigmoid(gate)
        scr_d[0] = dact * val * (s * (1.0 + gate * (1.0 - s)))
        scr_d[1] = dact * (gate * s)
        for h in range(2):
            wv = w_ref[h]
            du = wv[0:1, :] * scr_d[h, pl.ds(kk - 1, tm), :]
            for j in range(1, kk):
                du = du + wv[j:j + 1, :] * scr_d[h, pl.ds(kk - 1 - j, tm), :]
            du_ref[h] = du.astype(du_ref.dtype)
            dc = scr_d[h, 0:tm, :]
            for j in range(kk):
                dw_ref[h, j:j + 1, :] += jnp.sum(dc * scr_x[h, pl.ds(first + j, tm), :], axis=0, keepdims=True)

    return pl.pallas_call(
        body, name=name, grid=(f // tc, nt),
        in_specs=[pl.BlockSpec((2, HALO, tc), lambda j, i: (0, jnp.maximum(i * nh - 1, 0), j)),
                  pl.BlockSpec((2, tm, tc), lambda j, i: (0, i, j)),
                  pl.BlockSpec((2, HALO, tc), lambda j, i: (0, jnp.minimum((i + 1) * nh, last_h), j)),
                  pl.BlockSpec((2, kk, tc), lambda j, i: (0, 0, j)),
                  pl.BlockSpec((tm, tc), lambda j, i: (i, j)),
                  pl.BlockSpec((HALO, tc), lambda j, i: (jnp.minimum((i + 1) * nh, last_h), j))],
        out_specs=[pl.BlockSpec((2, tm, tc), lambda j, i: (0, i, j)),
                   pl.BlockSpec((2, SUBLANES, tc), lambda j, i: (0, 0, j))],
        out_shape=[jax.ShapeDtypeStruct((2, t, f), BF16), jax.ShapeDtypeStruct((2, SUBLANES, f), F32)],
        scratch_shapes=[pltpu.VMEM((2, tm + 2 * HALO, tc), F32), pltpu.VMEM((2, ext, tc), F32)],
        compiler_params=_cparams(("parallel", "arbitrary")),
    )(u, u, u, w, dact, dact)


def adamw(w, g, m, v, *, name):
    r, c = w.shape
    tr = _pick(r, max(SUBLANES, (2 * 1024 * 1024) // (4 * c) // SUBLANES * SUBLANES), SUBLANES)

    def body(w_ref, g_ref, m_ref, v_ref, d_ref, nm_ref, nv_ref):
        gv = g_ref[...]
        nm = ADAM_B1 * m_ref[...] + (1.0 - ADAM_B1) * gv
        nv = ADAM_B2 * v_ref[...] + (1.0 - ADAM_B2) * (gv * gv)
        m_hat = nm / (1.0 - ADAM_B1 ** ADAM_STEP)
        v_hat = nv / (1.0 - ADAM_B2 ** ADAM_STEP)
        d_ref[...] = -ADAM_LR * (m_hat / (jnp.sqrt(v_hat) + ADAM_EPS) + ADAM_WD * w_ref[...])
        nm_ref[...] = nm
        nv_ref[...] = nv

    blk = pl.BlockSpec((tr, c), lambda i: (i, 0))
    shp = jax.ShapeDtypeStruct((r, c), F32)
    return pl.pallas_call(
        body, name=name, grid=(r // tr,),
        in_specs=[blk] * 4, out_specs=[blk] * 3, out_shape=[shp] * 3,
        compiler_params=_cparams(("parallel",)),
    )(w, g, m, v)


def cast_into_slot(x, slot, n_slots, *, name):
    r, c = x.shape
    tr = _pick(r, max(16, (2 * 1024 * 1024) // (4 * c) // 16 * 16), 16)

    def body(s_ref, x_ref, o_ref):
        o_ref[...] = x_ref[...].astype(o_ref.dtype)

    grid_spec = pltpu.PrefetchScalarGridSpec(
        num_scalar_prefetch=1, grid=(r // tr,),
        in_specs=[pl.BlockSpec((tr, c), lambda i, s: (i, 0))],
        out_specs=pl.BlockSpec((None, tr, c), lambda i, s: (s[0], i, 0)))
    return pl.pallas_call(
        body, name=name, grid_spec=grid_spec,
        out_shape=jax.ShapeDtypeStruct((n_slots, r, c), BF16),
        compiler_params=_cparams(("arbitrary",)),
    )(jnp.reshape(slot, (1,)).astype(jnp.int32), x)


class Layout:
    def __init__(self, d):
        self.d = d
        self.h_dn = d // DN_DK
        self.h_m2 = d // M2_P
        self.gn = M2_GROUPS * M2_N
        self.w_xbc = d + 2 * self.gn
        self.z_off = 3 * d
        self.m2z_off = 4 * d
        self.xbc_off = 5 * d
        self.small_off = 5 * d + self.w_xbc
        self.n_small = 2 * self.h_dn + self.h_m2
        self.p = self.small_off + LANES
        self.p_orig = self.small_off + self.n_small

    def permute_w_in(self, w):
        d, s2 = self.d, 2 * self.h_dn
        pad = jnp.zeros((w.shape[0], LANES - self.n_small), w.dtype)
        return jnp.concatenate([w[:, :4 * d], w[:, 4 * d + s2:4 * d + s2 + d + self.w_xbc],
                                w[:, 4 * d:4 * d + s2], w[:, self.p_orig - self.h_m2:], pad], axis=1)

    def unpermute_w_in(self, w):
        d, s2 = self.d, 2 * self.h_dn
        so = self.small_off
        return jnp.concatenate([w[:, :4 * d], w[:, so:so + s2], w[:, 4 * d:so], w[:, so + s2:so + self.n_small]], axis=1)


def _cols(small, lo, hi):
    return jnp.transpose(small[:, lo:hi])[..., None]


def local_step(h0, target, wts, *, n_pad, n_meta):
    t, d = h0.shape
    lay = Layout(d)
    hd, hm = lay.h_dn, lay.h_m2
    zeros_b = jnp.zeros((1, 3 * d), F32)
    r3 = lambda v, n: v.reshape(n, 1, 1)
    dn_al, dn_db = r3(wts["dn_a_log"], hd), r3(wts["dn_dt_bias"], hd)
    dn_nw = wts["dn_norm_w"].reshape(1, 1, DN_DK)
    m2_al, m2_db, m2_dk = r3(wts["m2_a_log"], hm), r3(wts["m2_dt_bias"], hm), r3(wts["m2_d"], hm)

    hn1 = rmsnorm_forward(h0, wts["norm_mix_w"], name="norm_mix")
    proj = matmul(hn1, wts["w_in"], mode="nn", out_dtype=F32, name="in_proj", tn=640)
    act_qkv = conv_silu(proj, wts["dn_conv_w"], zeros_b, x_off=0, name="dn_conv")
    act_xbc = conv_silu(proj, wts["m2_conv_w"], wts["m2_conv_b"], x_off=lay.xbc_off, name="m2_conv")
    small = proj[:, lay.small_off:]
    bcol, acol, dtcol = _cols(small, 0, hd), _cols(small, hd, 2 * hd), _cols(small, 2 * hd, 2 * hd + hm)
    mixed, s_dn = gdn_forward(act_qkv, proj, bcol, acol, dn_al, dn_db, dn_nw, n_pad=n_pad, z_off=lay.z_off,
                              out_cols=2 * d)
    mixed, s_m2 = ssd_forward(act_xbc, proj, dtcol, m2_al, m2_db, m2_dk, wts["m2_norm_w"], mixed,
                              n_pad=n_pad, z_off=lay.m2z_off)
    h1 = matmul(mixed, wts["w_out"], mode="nn", out_dtype=F32, name="out_proj", tk=2 * d, residual=h0)
    hn2 = rmsnorm_forward(h1, wts["norm_ffn_w"], name="norm_ffn")
    up = matmul(hn2, wts["ffn_up"], mode="nn", out_dtype=F32, name="ffn_up", tn=1024, out_shards=2)
    kf, f = wts["ffn_conv_w"].shape[0], wts["ffn_conv_w"].shape[1] // 2
    w_glu = jnp.transpose(wts["ffn_conv_w"].reshape(kf, 2, f), (1, 0, 2))
    act = conv_glu(up, w_glu, name="ffn_conv")
    h2 = matmul(act, wts["ffn_down"], mode="nn", out_dtype=F32, name="ffn_down", tk=f // 2, residual=h1)
    loss, dh2, dh2_16, d_nfw = loss_head(h2, wts["norm_final_w"].reshape(1, d), target, n_skip=n_pad + n_meta)

    g = {}
    d_act = matmul(dh2_16, wts["ffn_down"], mode="nt", out_dtype=F32, name="d_ffn_act", tn=512)
    g["ffn_down"] = matmul(act, dh2_16, mode="tn", out_dtype=F32, name="dw_ffn_down", tm=512, tn=1024, tk=t)
    dup, d_fcw = conv_glu_backward(up, w_glu, d_act, name="d_ffn_conv")
    g["ffn_conv_w"] = jnp.transpose(d_fcw[:, :kf], (1, 0, 2)).reshape(kf, 2 * f)
    dhn2 = matmul(dup, wts["ffn_up"], mode="nt", out_dtype=F32, name="d_norm_ffn_out", tk=f, a_shards=2)
    g["ffn_up"] = matmul(hn2, dup, mode="tn", out_dtype=F32, name="dw_ffn_up", tm=512, tn=1408, tk=t,
                         b_shards=2, out_shards=4)
    dh1, dh1_16, g["norm_ffn_w"] = rmsnorm_backward(h1, wts["norm_ffn_w"], dhn2, dh2, n_pad=n_pad, name="d_norm_ffn")
    d_mixed = matmul(dh1_16, wts["w_out"], mode="nt", out_dtype=F32, name="d_mixed", tn=512)
    g["w_out"] = matmul(mixed, dh1_16, mode="tn", out_dtype=F32, name="dw_out", tm=512, tn=1024, tk=t)

    dq, dk, dv, dproj, dbc, dac, g_al, g_db, g_nw = gdn_backward(
        act_qkv, proj, bcol, acol, dn_al, dn_db, dn_nw, s_dn, d_mixed, n_pad=n_pad, z_off=lay.z_off)
    g["dn_a_log"], g["dn_dt_bias"] = g_al.reshape(1, hd), g_db.reshape(1, hd)
    g["dn_norm_w"] = jnp.sum(g_nw, axis=0)
    dxs, dbm, dcm, dproj, ddt, g_al, g_db, g_dk, g["m2_norm_w"] = ssd_backward(
        act_xbc, proj, dtcol, m2_al, m2_db, m2_dk, wts["m2_norm_w"], s_m2, d_mixed, dproj,
        n_pad=n_pad, z_off=lay.m2z_off)
    g["m2_a_log"], g["m2_dt_bias"], g["m2_d"] = g_al.reshape(1, hm), g_db.reshape(1, hm), g_dk.reshape(1, hm)

    kc = wts["dn_conv_w"].shape[0]
    dw_parts = []
    for idx, dpart in enumerate((dq, dk, dv)):
        dproj, dw, _ = conv_silu_backward(proj, wts["dn_conv_w"][:, idx * d:(idx + 1) * d], zeros_b[:, :d], dpart,
                                          dproj, x_off=idx * d, name=f"d_dn_conv{idx}")
        dw_parts.append(dw[:kc])
    g["dn_conv_w"] = jnp.concatenate(dw_parts, axis=1)
    dw_parts, db_parts = [], []
    off = 0
    for idx, dpart in enumerate((dxs, dbm, dcm)):
        wd = dpart.shape[1]
        dproj, dw, db = conv_silu_backward(proj, wts["m2_conv_w"][:, off:off + wd], wts["m2_conv_b"][:, off:off + wd],
                                           dpart, dproj, x_off=lay.xbc_off + off, name=f"d_m2_conv{idx}")
        dw_parts.append(dw[:kc])
        db_parts.append(db[:1])
        off += wd
    g["m2_conv_w"] = jnp.concatenate(dw_parts, axis=1)
    g["m2_conv_b"] = jnp.concatenate(db_parts, axis=1)
    dsmall = jnp.concatenate([jnp.transpose(dbc[..., 0]), jnp.transpose(dac[..., 0]), jnp.transpose(ddt[..., 0]),
                              jnp.zeros((t, LANES - lay.n_small), F32)], axis=1)
    dproj = lax.dynamic_update_slice(dproj, dsmall.astype(BF16), (0, lay.small_off))
    dhn1 = matmul(dproj, wts["w_in"], mode="nt", out_dtype=F32, name="d_norm_mix_out", tk=lay.p // 3)
    g["w_in"] = matmul(hn1, dproj, mode="tn", out_dtype=F32, name="dw_in", tm=512, tn=640, tk=t)
    dh0, _, g["norm_mix_w"] = rmsnorm_backward(h0, wts["norm_mix_w"], dhn1, dh1, n_pad=n_pad, name="d_norm_mix")
    g["norm_final_w"] = d_nfw
    return loss[0, 0], dh0, g


MESH = pl.DeviceIdType.MESH
ANY = pl.BlockSpec(memory_space=pl.ANY)
N_CHIPS = 4
N_DEV = 8


def _mesh_pos():
    return lax.axis_index("x"), lax.axis_index("y"), lax.axis_index("c")


def _other_chips(x, y):
    return [(1 - x, y), (x, 1 - y), (1 - x, 1 - y)]


def _rcopy(src, dst, send_sems, recv_sems, k, to):
    return pltpu.make_async_remote_copy(src_ref=src, dst_ref=dst, send_sem=send_sems.at[k], recv_sem=recv_sems.at[k],
                                        device_id=to, device_id_type=MESH)


def gather_shards(slabs, *, name):
    n = len(slabs)

    def body(*refs):
        outs = refs[n:2 * n]
        send_sems, recv_sems = refs[2 * n:]
        x, y, c = _mesh_pos()
        sib = (x, y, 1 - c)
        chips = _other_chips(x, y)
        me = 2 * x + y
        halves = [r.shape[1] // 2 for r in outs]
        sent = []
        for a in range(n):
            rows = pl.ds(c * halves[a], halves[a])
            for j, (px, py) in enumerate(chips):
                mine = outs[a].at[me, rows]
                cp = _rcopy(mine, mine, send_sems, recv_sems, 6 * a + j, (px, py, c))
                cp.start()
                sent.append(cp)
        for a in range(n):
            rows = pl.ds(c * halves[a], halves[a])
            for j, (px, py) in enumerate(chips):
                landed = outs[a].at[2 * px + py, rows]
                _rcopy(landed, landed, send_sems, recv_sems, 6 * a + j, (px, py, c)).wait_recv()
                cp = _rcopy(landed, landed, send_sems, recv_sems, 6 * a + 3 + j, sib)
                cp.start()
                sent.append(cp)
        for a in range(n):
            rows = pl.ds((1 - c) * halves[a], halves[a])
            for j, (px, py) in enumerate(chips):
                passed = outs[a].at[2 * px + py, rows]
                _rcopy(passed, passed, send_sems, recv_sems, 6 * a + 3 + j, sib).wait_recv()
        for cp in sent:
            cp.wait_send()

    return pl.pallas_call(
        body, name=name,
        in_specs=[ANY] * n, out_specs=[ANY] * n,
        out_shape=[jax.ShapeDtypeStruct(s.shape, s.dtype) for s in slabs],
        input_output_aliases={a: a for a in range(n)},
        scratch_shapes=[pltpu.SemaphoreType.DMA((6 * n,)), pltpu.SemaphoreType.DMA((6 * n,))],
        compiler_params=pltpu.CompilerParams(has_side_effects=True),
    )(*slabs)


def pair_exchange(grads, *, name):
    n = len(grads)

    def body(*refs):
        ins, outs = refs[:n], refs[n:2 * n]
        send_sems, recv_sems = refs[2 * n:]
        x, y, c = _mesh_pos()
        cps = []
        for a in range(n):
            half = ins[a].shape[1] // 2
            cp = _rcopy(ins[a].at[:, pl.ds((1 - c) * half, half), :], outs[a], send_sems, recv_sems, a, (x, y, 1 - c))
            cp.start()
            cps.append(cp)
        for cp in cps:
            cp.wait()

    return pl.pallas_call(
        body, name=name, in_specs=[ANY] * n, out_specs=[ANY] * n,
        out_shape=[jax.ShapeDtypeStruct((s.shape[0], s.shape[1] // 2, s.shape[2]), s.dtype) for s in grads],
        scratch_shapes=[pltpu.SemaphoreType.DMA((n,)), pltpu.SemaphoreType.DMA((n,))],
        compiler_params=pltpu.CompilerParams(has_side_effects=True),
    )(*grads)


def chip_exchange(parts, *, name):
    n = len(parts)

    def body(*refs):
        ins, outs = refs[:n], refs[n:2 * n]
        send_sems, recv_sems = refs[2 * n:]
        x, y, c = _mesh_pos()
        cps = []
        for a in range(n):
            for j, (px, py) in enumerate(_other_chips(x, y)):
                cp = _rcopy(ins[a].at[2 * px + py], outs[a].at[j], send_sems, recv_sems, 3 * a + j, (px, py, c))
                cp.start()
                cps.append(cp)
        for cp in cps:
            cp.wait()

    return pl.pallas_call(
        body, name=name, in_specs=[ANY] * n, out_specs=[ANY] * n,
        out_shape=[jax.ShapeDtypeStruct((3,) + s.shape[1:], s.dtype) for s in parts],
        scratch_shapes=[pltpu.SemaphoreType.DMA((3 * n,)), pltpu.SemaphoreType.DMA((3 * n,))],
        compiler_params=pltpu.CompilerParams(has_side_effects=True),
    )(*parts)


def pair_join(wholes, *, name):
    n = len(wholes)

    def body(*refs):
        outs = refs[n:2 * n]
        send_sems, recv_sems = refs[2 * n:]
        x, y, c = _mesh_pos()
        cps = []
        for a in range(n):
            half = outs[a].shape[0] // 2
            rows = outs[a].at[pl.ds(c * half, half)]
            cp = _rcopy(rows, rows, send_sems, recv_sems, a, (x, y, 1 - c))
            cp.start()
            cps.append(cp)
        for a, cp in enumerate(cps):
            cp.wait_send()
            half = outs[a].shape[0] // 2
            theirs = outs[a].at[pl.ds((1 - c) * half, half)]
            _rcopy(theirs, theirs, send_sems, recv_sems, a, (x, y, 1 - c)).wait_recv()

    return pl.pallas_call(
        body, name=name, in_specs=[ANY] * n, out_specs=[ANY] * n,
        out_shape=[jax.ShapeDtypeStruct(s.shape, s.dtype) for s in wholes],
        input_output_aliases={a: a for a in range(n)},
        scratch_shapes=[pltpu.SemaphoreType.DMA((n,)), pltpu.SemaphoreType.DMA((n,))],
        compiler_params=pltpu.CompilerParams(has_side_effects=True),
    )(*wholes)


def gather_all(v, *, name):
    def body(in_ref, out_ref, send_sems, recv_sems, local_sem):
        x, y, c = _mesh_pos()
        mine = out_ref.at[4 * x + 2 * y + c]
        lc = pltpu.make_async_copy(in_ref, mine, local_sem)
        lc.start()
        cps = []
        for k in range(1, N_DEV):
            flip = lambda v, bit: 1 - v if (k >> bit) & 1 else v
            cp = _rcopy(in_ref, mine, send_sems, recv_sems, k - 1, (flip(x, 2), flip(y, 1), flip(c, 0)))
            cp.start()
            cps.append(cp)
        for cp in cps:
            cp.wait()
        lc.wait()

    return pl.pallas_call(
        body, name=name, in_specs=[ANY], out_specs=ANY,
        out_shape=jax.ShapeDtypeStruct((N_DEV,) + v.shape, v.dtype),
        scratch_shapes=[pltpu.SemaphoreType.DMA((N_DEV - 1,)), pltpu.SemaphoreType.DMA((N_DEV - 1,)),
                        pltpu.SemaphoreType.DMA(())],
        compiler_params=pltpu.CompilerParams(has_side_effects=True),
    )(v)


def _sum_tile(rows, cols):
    return _pick(rows, max(16, (1024 * 1024) // (4 * cols) // 16 * 16), 16)


def pair_add(g, rb, chip, c, *, name):
    _, r, cols = g.shape
    half = r // 2
    tr = _sum_tile(half, cols)
    nrt = half // tr

    def body(s_ref, g_ref, rb_ref, p16_ref, own_ref):
        v = g_ref[...] + rb_ref[...]
        p16_ref[...] = v.astype(p16_ref.dtype)

        @pl.when(pl.program_id(1) == s_ref[0])
        def _():
            own_ref[...] = v

    grid_spec = pltpu.PrefetchScalarGridSpec(
        num_scalar_prefetch=1, grid=(nrt, N_CHIPS),
        in_specs=[pl.BlockSpec((None, tr, cols), lambda i, k, s: (k, s[1] * nrt + i, 0)),
                  pl.BlockSpec((None, tr, cols), lambda i, k, s: (k, i, 0))],
        out_specs=[pl.BlockSpec((None, tr, cols), lambda i, k, s: (k, i, 0)),
                   pl.BlockSpec((tr, cols), lambda i, k, s: (i, 0))])
    return pl.pallas_call(
        body, name=name, grid_spec=grid_spec,
        out_shape=[jax.ShapeDtypeStruct((N_CHIPS, half, cols), BF16), jax.ShapeDtypeStruct((half, cols), F32)],
        compiler_params=_cparams(("arbitrary", "arbitrary")),
    )(jnp.stack([chip, c]).astype(jnp.int32), g, rb)


def chip_add(own, q, c, *, name):
    r, cols = own.shape
    tr = _sum_tile(r, cols)
    nrt = r // tr

    def body(s_ref, own_ref, q_ref, o_ref):
        o_ref[...] = ((own_ref[...] + q_ref[0].astype(F32)) + q_ref[1].astype(F32)) + q_ref[2].astype(F32)

    grid_spec = pltpu.PrefetchScalarGridSpec(
        num_scalar_prefetch=1, grid=(nrt,),
        in_specs=[pl.BlockSpec((tr, cols), lambda i, s: (i, 0)), pl.BlockSpec((3, tr, cols), lambda i, s: (0, i, 0))],
        out_specs=pl.BlockSpec((tr, cols), lambda i, s: (s[0] * nrt + i, 0)))
    return pl.pallas_call(
        body, name=name, grid_spec=grid_spec,
        out_shape=jax.ShapeDtypeStruct((2 * r, cols), F32),
        compiler_params=_cparams(("arbitrary",)),
    )(jnp.reshape(c, (1,)).astype(jnp.int32), own, q)


def sum_slots(v, *, name):
    n, r, cols = v.shape
    tr = _sum_tile(r, cols)

    def body(v_ref, o_ref):
        acc = v_ref[0]
        for k in range(1, n):
            acc = acc + v_ref[k]
        o_ref[...] = acc

    return pl.pallas_call(
        body, name=name, grid=(r // tr,),
        in_specs=[pl.BlockSpec((n, tr, cols), lambda i: (0, i, 0))],
        out_specs=pl.BlockSpec((tr, cols), lambda i: (i, 0)),
        out_shape=jax.ShapeDtypeStruct((r, cols), F32),
        compiler_params=_cparams(("parallel",)),
    )(v)


PACK_ROWS = 16


def _pack(arrays):
    parts = []
    for a in arrays:
        flat = a.reshape(-1).astype(F32)
        size = PACK_ROWS * LANES
        pad = (-flat.shape[0]) % size
        parts.append(jnp.pad(flat, (0, pad)))
    return jnp.concatenate(parts).reshape(-1, LANES)


def _unpack(slab, shapes):
    out, row = [], 0
    for shp in shapes:
        n = 1
        for s in shp:
            n *= s
        rows = -(-n // (PACK_ROWS * LANES)) * PACK_ROWS
        out.append(slab[row:row + rows].reshape(-1)[:n].reshape(shp))
        row += rows
    return out


WEIGHT_NAMES = ("meta_tokens", "norm_mix_w", "w_in", "dn_conv_w", "dn_a_log", "dn_dt_bias", "dn_norm_w", "m2_conv_w",
                "m2_conv_b", "m2_a_log", "m2_dt_bias", "m2_d", "m2_norm_w", "w_out", "norm_ffn_w", "ffn_up",
                "ffn_conv_w", "ffn_down", "norm_final_w")
BIG = ("w_in", "w_out", "ffn_up", "ffn_down")
BIG_COLUMN_SHARDED = ("w_in", "ffn_up")
SMALL_SHARDED = ("meta_tokens", "dn_conv_w", "m2_conv_w", "ffn_conv_w")
SMALL = tuple(n for n in WEIGHT_NAMES if n not in BIG)


def kernel(x, meta_tokens, norm_mix_w, w_in, dn_conv_w, dn_a_log, dn_dt_bias, dn_norm_w, m2_conv_w, m2_conv_b, m2_a_log, m2_dt_bias, m2_d, m2_norm_w, w_out, norm_ffn_w, ffn_up, ffn_conv_w, ffn_down, norm_final_w, loss_target, m_meta_tokens, m_norm_mix_w, m_w_in, m_dn_conv_w, m_dn_a_log, m_dn_dt_bias, m_dn_norm_w, m_m2_conv_w, m_m2_conv_b, m_m2_a_log, m_m2_dt_bias, m_m2_d, m_m2_norm_w, m_w_out, m_norm_ffn_w, m_ffn_up, m_ffn_conv_w, m_ffn_down, m_norm_final_w, v_meta_tokens, v_norm_mix_w, v_w_in, v_dn_conv_w, v_dn_a_log, v_dn_dt_bias, v_dn_norm_w, v_m2_conv_w, v_m2_conv_b, v_m2_a_log, v_m2_dt_bias, v_m2_d, v_m2_norm_w, v_w_out, v_norm_ffn_w, v_ffn_up, v_ffn_conv_w, v_ffn_down, v_norm_final_w):
    args = tuple(locals().values())
    nw = len(WEIGHT_NAMES)
    wt = dict(zip(WEIGHT_NAMES, args[1:1 + nw]))
    mom = dict(zip(WEIGHT_NAMES, args[2 + nw:2 + 2 * nw]))
    var = dict(zip(WEIGHT_NAMES, args[2 + 2 * nw:2 + 3 * nw]))
    xi, yi, ci = _mesh_pos()
    chip = 2 * xi + yi
    seq, d = x.shape[1], x.shape[2]
    n_meta = wt["meta_tokens"].shape[0]
    n_pad = (-(n_meta + seq)) % ROW_ALIGN
    lay = Layout(d)

    shard2d = {k: wt[k].reshape(wt[k].shape[-2:]) for k in BIG}
    small_local = [wt[k].reshape(wt[k].shape[-2:]) for k in SMALL_SHARDED]
    small_slab = _pack(small_local)
    small_slab = lax.dynamic_update_slice(jnp.zeros((N_CHIPS,) + small_slab.shape, F32), small_slab[None], (chip, 0, 0))
    gathered = gather_shards([cast_into_slot(shard2d[k], chip, N_CHIPS, name="cast_" + k) for k in BIG] + [small_slab],
                             name="gather_weights")
    full = {}
    for k, gth in zip(BIG, gathered):
        if k in BIG_COLUMN_SHARDED:
            full[k] = jnp.transpose(gth, (1, 0, 2)).reshape(gth.shape[1], -1)
        else:
            full[k] = gth.reshape(-1, gth.shape[2])
    full["w_in"] = lay.permute_w_in(full["w_in"])
    per_chip = [_unpack(gathered[-1][s], [a.shape for a in small_local]) for s in range(N_CHIPS)]
    for idx, k in enumerate(SMALL_SHARDED):
        full[k] = jnp.concatenate([per_chip[s][idx] for s in range(N_CHIPS)], axis=-1)
    for k in SMALL:
        if k not in SMALL_SHARDED:
            full[k] = wt[k]

    h0 = jnp.concatenate([jnp.zeros((n_pad, d), F32), full["meta_tokens"], x[0]], axis=0)
    target = jnp.concatenate([jnp.zeros((n_pad + n_meta, d), F32), loss_target[0]], axis=0)
    loss_local, dh0, g = local_step(h0, target, full, n_pad=n_pad, n_meta=n_meta)
    grad_x = dh0[n_pad + n_meta:][None]
    g["meta_tokens"] = dh0[n_pad:n_pad + n_meta]
    loss = lax.psum(loss_local, ("x", "y", "c"))

    g["w_in"] = lay.unpermute_w_in(g["w_in"])
    slots = {}
    for k in BIG:
        if k == "ffn_up":
            slots[k] = g[k]
        elif k in BIG_COLUMN_SHARDED:
            slots[k] = jnp.transpose(g[k].reshape(g[k].shape[0], N_CHIPS, -1), (1, 0, 2))
        else:
            slots[k] = g[k].reshape(N_CHIPS, -1, g[k].shape[1])
    from_sibling = pair_exchange([slots[k] for k in BIG], name="grad_pair_exchange")
    p16, own = [], []
    for k, rb in zip(BIG, from_sibling):
        a, b = pair_add(slots[k], rb, chip, ci, name="grad_pair_add_" + k)
        p16.append(a)
        own.append(b)
    from_chips = chip_exchange(p16, name="grad_chip_exchange")
    wholes = [chip_add(o, q, ci, name="grad_chip_add_" + k) for k, o, q in zip(BIG, own, from_chips)]
    grad = dict(zip(BIG, pair_join(wholes, name="grad_pair_join")))

    small_shapes = [g[k].shape for k in SMALL]
    summed = sum_slots(gather_all(_pack([g[k] for k in SMALL]), name="small_grad_gather"), name="small_grad_sum")
    for k, v in zip(SMALL, _unpack(summed, small_shapes)):
        if k in SMALL_SHARDED:
            width = wt[k].shape[-1]
            v = lax.dynamic_slice_in_dim(v, chip * width, width, axis=v.ndim - 1)
        grad[k] = v

    delta, new_m, new_v = {}, {}, {}
    for k in BIG:
        shp = shard2d[k].shape
        delta[k], new_m[k], new_v[k] = adamw(shard2d[k], grad[k], mom[k].reshape(shp), var[k].reshape(shp),
                                             name="adamw_" + k)
    shapes = [wt[k].shape for k in SMALL]
    packed = adamw(_pack([wt[k] for k in SMALL]), _pack([grad[k] for k in SMALL]), _pack([mom[k] for k in SMALL]),
                   _pack([var[k] for k in SMALL]), name="adamw_small")
    for res, slab in zip((delta, new_m, new_v), packed):
        for k, v in zip(SMALL, _unpack(slab, shapes)):
            res[k] = v
    outs = [loss, grad_x]
    for res in (grad, delta, new_m, new_v):
        outs += [res[k].reshape(wt[k].shape) for k in WEIGHT_NAMES]
    return tuple(outs)
```

```python
import functools
import math

import jax
import jax.numpy as jnp
from jax import lax
from jax.experimental import pallas as pl
from jax.experimental.pallas import tpu as pltpu

F32 = jnp.float32
BF16 = jnp.bfloat16
HI = lax.Precision.HIGHEST

CHUNK = 64
ROW_ALIGN = 128
NORM_EPS = 1e-6
DN_DK = 128
M2_P = 64
M2_N = 128
M2_GROUPS = 4
HEAD_BLOCK = 8
VMEM_LIMIT = 56 * 1024 * 1024

ADAM_LR, ADAM_B1, ADAM_B2, ADAM_EPS, ADAM_WD, ADAM_STEP = 0.001, 0.9, 0.999, 1e-08, 0.01, 10


def _cparams(sem=None):
    return pltpu.CompilerParams(dimension_semantics=sem, vmem_limit_bytes=VMEM_LIMIT)


def _silu(x):
    return x / (1.0 + jnp.exp(-x))


def _sigmoid(x):
    return 1.0 / (1.0 + jnp.exp(-x))


def _softplus(x):
    return jnp.maximum(x, 0.0) + jnp.log(1.0 + jnp.exp(-jnp.abs(x)))


def _tri_masks():
    r = lax.broadcasted_iota(jnp.int32, (CHUNK, CHUNK), 0)
    c = lax.broadcasted_iota(jnp.int32, (CHUNK, CHUNK), 1)
    return (r >= c)[None], (r > c)[None], (r == c)[None]


def _col2row(col, eye):
    return jnp.sum(jnp.where(eye, col, 0.0), axis=1, keepdims=True)


def _cumsum_col(col, causal, eye):
    row = _col2row(col, eye)
    return jnp.sum(jnp.where(causal, row, 0.0), axis=2, keepdims=True)


def _bdot(a, b, dims, precision=None):
    (ca, cb) = dims
    if precision is None:
        a = a.astype(BF16)
        b = b.astype(BF16)
    return lax.dot_general(a, b, (((ca,), (cb,)), ((0,), (0,))), precision=precision,
                           preferred_element_type=F32)


def _dot3(a, b, ca, cb):
    def split(v):
        hi = v.astype(BF16)
        return hi, (v - hi.astype(F32)).astype(BF16)

    def dot(p, q):
        return lax.dot_general(p, q, (((ca,), (cb,)), ((0,), (0,))), preferred_element_type=F32)

    (ah, al), (bh, bl) = split(a), split(b)
    return dot(ah, bh) + (dot(ah, bl) + dot(al, bh))


@jax.custom_vjp
def _bmm3(a, b):
    return _dot3(a, b, 2, 1)


def _bmm3_fwd(a, b):
    return _dot3(a, b, 2, 1), (a, b)


def _bmm3_bwd(res, g):
    a, b = res
    return _dot3(g, b, 2, 2), _dot3(a, g, 1, 1)


_bmm3.defvjp(_bmm3_fwd, _bmm3_bwd)


def _unit_lower_inverse(a_mat, eye):
    inv = jnp.where(eye, 1.0, 0.0) - a_mat
    pw = a_mat
    n = 2
    while n < CHUNK:
        pw = _bmm3(pw, pw)
        inv = inv + _bmm3(inv, pw)
        n *= 2
    return inv


@jax.custom_vjp
def _known_inverse(a_mat, inv):
    return inv


def _known_inverse_fwd(a_mat, inv):
    return inv, inv


def _known_inverse_bwd(inv, g):
    return -_dot3(_dot3(inv, g, 1, 1), inv, 2, 2), jnp.zeros_like(inv)


_known_inverse.defvjp(_known_inverse_fwd, _known_inverse_bwd)


def _gdn_step(state, qa, ka, va, z, braw, araw, a_log, dt_bias, norm_w, vm, inv=None, want_inv=False):
    causal, strict, eye = _tri_masks()
    qa, ka, va = qa * vm, ka * vm, va * vm
    q = qa * lax.rsqrt(jnp.sum(qa * qa, -1, keepdims=True) + NORM_EPS) * (DN_DK ** -0.5)
    k = ka * lax.rsqrt(jnp.sum(ka * ka, -1, keepdims=True) + NORM_EPS)
    beta = _sigmoid(braw) * vm
    g = -jnp.exp(a_log) * _softplus(araw + dt_bias) * vm
    gcum = _cumsum_col(g, causal, eye)
    grow = _col2row(gcum, eye)
    decay = jnp.where(causal, jnp.exp(jnp.where(causal, gcum - grow, 0.0)), 0.0)
    kk = _bdot(k, k, (2, 2))
    a_mat = jnp.where(strict, beta * kk * decay, 0.0)
    tinv = _unit_lower_inverse(a_mat, eye) if inv is None else _known_inverse(a_mat, inv)
    egc = jnp.exp(gcum)
    u = _bmm3(tinv, va * beta)
    w = _bmm3(tinv, k * (beta * egc))
    v_new = u - _bdot(w, state, (2, 1))
    o_inter = _bdot(q * egc, state, (2, 1))
    qk = _bdot(q, k, (2, 2)) * decay
    o = o_inter + _bdot(qk, v_new, (2, 1))
    g_last = jnp.sum(g, axis=1, keepdims=True)
    new_state = state * jnp.exp(g_last) + _bdot(k * jnp.exp(g_last - gcum), v_new, (1, 1))
    o = o * lax.rsqrt(jnp.mean(o * o, -1, keepdims=True) + NORM_EPS) * norm_w * _silu(z)
    return (new_state, o, tinv) if want_inv else (new_state, o)


def _valid_rows(chunk, n_pad):
    r = chunk * CHUNK + lax.broadcasted_iota(jnp.int32, (1, CHUNK, 1), 1)
    return jnp.where(r >= n_pad, 1.0, 0.0).astype(F32)


def _split_heads(x, n, w):
    return jnp.stack([x[:, i * w:(i + 1) * w] for i in range(n)], axis=0)


def _merge_heads(x):
    return jnp.concatenate([x[i] for i in range(x.shape[0])], axis=-1)


def _run_beside(job, step, n_steps):
    @pl.when(step == 0)
    def _():
        job.begin()

    if hasattr(job, "pass_on"):
        @pl.when(step == n_steps // 2)
        def _():
            job.pass_on()

    @pl.when(step == n_steps - 1)
    def _():
        job.end()


def gdn_forward(act_qkv, proj, bcol, acol, a_log3, dt_bias3, norm_w3, *, n_pad, z_off, out_cols, gather=()):
    t, w3 = act_qkv.shape
    wdn = w3 // 3
    heads = wdn // DN_DK
    hb = min(HEAD_BLOCK, heads)
    nhb = heads // hb
    nc = t // CHUNK
    bw = hb * DN_DK
    nqb = wdn // bw
    ng = len(gather)

    def body(q_ref, k_ref, v_ref, z_ref, b_ref, a_ref, al_ref, db_ref, nw_ref, *rest):
        out_ref, sall_ref, inv_ref = rest[ng:ng + 3]
        st_ref = rest[2 * ng + 3]
        n = pl.program_id(1)
        if ng:
            _run_beside(GatherJob(rest[ng + 3:2 * ng + 3], *rest[2 * ng + 4:]), pl.program_id(0) * nc + n, nhb * nc)

        @pl.when(n == 0)
        def _():
            st_ref[...] = jnp.zeros_like(st_ref)

        state = st_ref[...]
        sall_ref[...] = state
        vm = _valid_rows(n, n_pad)
        new_state, o, tinv = _gdn_step(
            state, _split_heads(q_ref[...], hb, DN_DK), _split_heads(k_ref[...], hb, DN_DK),
            _split_heads(v_ref[...], hb, DN_DK), _split_heads(z_ref[...], hb, DN_DK),
            b_ref[...], a_ref[...], al_ref[...], db_ref[...], nw_ref[...], vm, want_inv=True)
        st_ref[...] = new_state
        inv_ref[...] = tinv
        out_ref[...] = _merge_heads(o).astype(out_ref.dtype)

    col = pl.BlockSpec((hb, CHUNK, 1), lambda h, n: (h, n, 0))
    par = pl.BlockSpec((hb, 1, 1), lambda h, n: (h, 0, 0))
    return pl.pallas_call(
        body, name="gdn_fwd",
        grid=(nhb, nc),
        in_specs=[pl.BlockSpec((CHUNK, bw), lambda h, n: (n, h)),
                  pl.BlockSpec((CHUNK, bw), lambda h, n: (n, nqb + h)),
                  pl.BlockSpec((CHUNK, bw), lambda h, n: (n, 2 * nqb + h)),
                  pl.BlockSpec((CHUNK, bw), lambda h, n: (n, z_off // bw + h)),
                  col, col, par, par,
                  pl.BlockSpec((1, 1, DN_DK), lambda h, n: (0, 0, 0))] + [ANY] * ng,
        out_specs=[pl.BlockSpec((CHUNK, bw), lambda h, n: (n, h)),
                   pl.BlockSpec((None, hb, DN_DK, DN_DK), lambda h, n: (n, h, 0, 0)),
                   pl.BlockSpec((None, hb, CHUNK, CHUNK), lambda h, n: (n, h, 0, 0))] + [ANY] * ng,
        out_shape=[jax.ShapeDtypeStruct((t, out_cols), BF16),
                   jax.ShapeDtypeStruct((nc, heads, DN_DK, DN_DK), F32),
                   jax.ShapeDtypeStruct((nc, heads, CHUNK, CHUNK), F32)]
        + [jax.ShapeDtypeStruct(s.shape, s.dtype) for s in gather],
        input_output_aliases={9 + a: 3 + a for a in range(ng)},
        scratch_shapes=[pltpu.VMEM((hb, DN_DK, DN_DK), F32)] + (GatherJob.sems(ng) if ng else []),
        compiler_params=_cparams(("arbitrary", "arbitrary")),
    )(act_qkv, act_qkv, act_qkv, proj, bcol, acol, a_log3, dt_bias3, norm_w3, *gather)


def gdn_backward(act_qkv, proj, bcol, acol, a_log3, dt_bias3, norm_w3, s_all, inv_all, d_mixed, *, n_pad, z_off):
    t, w3 = act_qkv.shape
    wdn = w3 // 3
    heads = wdn // DN_DK
    hb = min(HEAD_BLOCK, heads)
    nhb = heads // hb
    nc = t // CHUNK
    bw = hb * DN_DK
    nqb = wdn // bw

    def body(q_ref, k_ref, v_ref, z_ref, b_ref, a_ref, al_ref, db_ref, nw_ref, s_ref, inv_ref, do_ref,
             dq_ref, dk_ref, dv_ref, dz_ref, dbc_ref, dac_ref, dal_ref, ddb_ref, dnw_ref, ds_ref):
        i = pl.program_id(1)
        n = nc - 1 - i

        @pl.when(i == 0)
        def _():
            ds_ref[...] = jnp.zeros_like(ds_ref)
            dal_ref[...] = jnp.zeros_like(dal_ref)
            ddb_ref[...] = jnp.zeros_like(ddb_ref)
            dnw_ref[...] = jnp.zeros_like(dnw_ref)

        vm = _valid_rows(n, n_pad)
        step = functools.partial(_gdn_step, vm=vm, inv=inv_ref[...])
        _, vjp = jax.vjp(step, s_ref[...], _split_heads(q_ref[...], hb, DN_DK), _split_heads(k_ref[...], hb, DN_DK),
                         _split_heads(v_ref[...], hb, DN_DK), _split_heads(z_ref[...], hb, DN_DK),
                         b_ref[...], a_ref[...], al_ref[...], db_ref[...], nw_ref[...])
        ds, dq, dk, dv, dz, dbr, dar, dal, ddb, dnw = vjp((ds_ref[...], _split_heads(do_ref[...], hb, DN_DK)))
        ds_ref[...] = ds
        dq_ref[...] = _merge_heads(dq)
        dk_ref[...] = _merge_heads(dk)
        dv_ref[...] = _merge_heads(dv)
        dz_ref[...] = _merge_heads(dz).astype(dz_ref.dtype)
        dbc_ref[...] = dbr
        dac_ref[...] = dar
        dal_ref[...] += dal
        ddb_ref[...] += ddb
        dnw_ref[...] += dnw[0]

    rev = lambda n: nc - 1 - n
    col = pl.BlockSpec((hb, CHUNK, 1), lambda h, n: (h, rev(n), 0))
    par = pl.BlockSpec((hb, 1, 1), lambda h, n: (h, 0, 0))
    blk = lambda off: pl.BlockSpec((CHUNK, bw), lambda h, n: (rev(n), off + h))
    return pl.pallas_call(
        body, name="gdn_bwd",
        grid=(nhb, nc),
        in_specs=[blk(0), blk(nqb), blk(2 * nqb), blk(z_off // bw), col, col, par, par,
                  pl.BlockSpec((1, 1, DN_DK), lambda h, n: (0, 0, 0)),
                  pl.BlockSpec((None, hb, DN_DK, DN_DK), lambda h, n: (rev(n), h, 0, 0)),
                  pl.BlockSpec((None, hb, CHUNK, CHUNK), lambda h, n: (rev(n), h, 0, 0)),
                  blk(0)],
        out_specs=[blk(0), blk(0), blk(0), blk(z_off // bw), col, col, par, par,
                   pl.BlockSpec((None, 1, DN_DK), lambda h, n: (h, 0, 0))],
        out_shape=[jax.ShapeDtypeStruct((t, wdn), F32)] * 3
        + [jax.ShapeDtypeStruct((t, proj.shape[1]), BF16),
           jax.ShapeDtypeStruct((heads, t, 1), F32), jax.ShapeDtypeStruct((heads, t, 1), F32),
           jax.ShapeDtypeStruct((heads, 1, 1), F32), jax.ShapeDtypeStruct((heads, 1, 1), F32),
           jax.ShapeDtypeStruct((nhb, 1, DN_DK), F32)],
        scratch_shapes=[pltpu.VMEM((hb, DN_DK, DN_DK), F32)],
        compiler_params=_cparams(("arbitrary", "arbitrary")),
    )(act_qkv, act_qkv, act_qkv, proj, bcol, acol, a_log3, dt_bias3, norm_w3, s_all, inv_all, d_mixed)


def _dot2(a, b, ca, cb):
    return lax.dot_general(a.astype(BF16), b.astype(BF16), (((ca,), (cb,)), ((), ())),
                           preferred_element_type=F32)


def _ssd_step(state, xa, bmat, cmat, z, dtraw, a_log, dt_bias, dskip, norm_w, vm):
    causal, _, eye = _tri_masks()
    r_heads, p, n_state = state.shape
    gw = r_heads * p
    vm2 = vm[0]
    xa, bmat, cmat = xa * vm2, bmat * vm2, cmat * vm2
    dt = _softplus(dtraw + dt_bias) * vm
    a = dt * (-jnp.exp(a_log))
    acs = _cumsum_col(a, causal, eye)
    arow = _col2row(acs, eye)
    lmat = jnp.where(causal, jnp.exp(jnp.where(causal, acs - arow, 0.0)), 0.0)
    hsel = (lax.broadcasted_iota(jnp.int32, (r_heads, 1, gw), 2) // p
            == lax.broadcasted_iota(jnp.int32, (r_heads, 1, gw), 0))

    def spread(col):
        return jnp.sum(jnp.where(hsel, col, 0.0), axis=0)

    xdt = xa * spread(dt)
    cb = _dot2(cmat, bmat, 1, 1)
    m = (cb[None] * lmat).reshape(r_heads * CHUNK, CHUNK)
    yb = _dot2(m, xdt, 1, 0).reshape(r_heads, CHUNK, gw)
    y_diag = jnp.sum(jnp.where(hsel, yb, 0.0), axis=0)
    s2 = state.reshape(gw, n_state)
    y_off = _dot2(cmat, s2, 1, 1) * spread(jnp.exp(acs))
    a_last = jnp.sum(a, axis=1, keepdims=True)
    upd = _dot2(xdt * spread(jnp.exp(a_last - acs)), bmat, 0, 0)
    new_state = state * jnp.exp(a_last) + upd.reshape(r_heads, p, n_state)
    y = y_diag + y_off + xa * spread(dskip)
    y = y * _silu(z)
    y = y * lax.rsqrt(jnp.mean(y * y, -1, keepdims=True) + NORM_EPS) * norm_w
    return new_state, y


def _ssd_dims(act_xbc):
    t, wx = act_xbc.shape
    wm = wx - 2 * M2_GROUPS * M2_N
    gw = wm // M2_GROUPS
    return t, wm, gw, gw // M2_P, wm // M2_P, t // CHUNK


def ssd_forward(act_xbc, proj, dtcol, a_log3, dt_bias3, dskip3, norm_w, mixed, *, n_pad, z_off, gather=()):
    t, wm, gw, rh, heads, nc = _ssd_dims(act_xbc)
    nb = wm // M2_N
    ob = (mixed.shape[1] - wm) // gw
    ng = len(gather)

    def body(x_ref, b_ref, c_ref, z_ref, dt_ref, al_ref, db_ref, dk_ref, nw_ref, _, *rest):
        out_ref, sall_ref = rest[ng:ng + 2]
        st_ref = rest[2 * ng + 2]
        n = pl.program_id(1)
        if ng:
            _run_beside(GatherJob(rest[ng + 2:2 * ng + 2], *rest[2 * ng + 3:]), pl.program_id(0) * nc + n,
                        M2_GROUPS * nc)

        @pl.when(n == 0)
        def _():
            st_ref[...] = jnp.zeros_like(st_ref)

        state = st_ref[...]
        sall_ref[...] = state
        new_state, y = _ssd_step(state, x_ref[...], b_ref[...], c_ref[...], z_ref[...], dt_ref[...],
                                 al_ref[...], db_ref[...], dk_ref[...], nw_ref[...], _valid_rows(n, n_pad))
        st_ref[...] = new_state
        out_ref[...] = y.astype(out_ref.dtype)

    par = pl.BlockSpec((rh, 1, 1), lambda g, n: (g, 0, 0))
    return pl.pallas_call(
        body, name="ssd_fwd",
        grid=(M2_GROUPS, nc),
        in_specs=[pl.BlockSpec((CHUNK, gw), lambda g, n: (n, g)),
                  pl.BlockSpec((CHUNK, M2_N), lambda g, n: (n, nb + g)),
                  pl.BlockSpec((CHUNK, M2_N), lambda g, n: (n, nb + M2_GROUPS + g)),
                  pl.BlockSpec((CHUNK, gw), lambda g, n: (n, z_off // gw + g)),
                  pl.BlockSpec((rh, CHUNK, 1), lambda g, n: (g, n, 0)),
                  par, par, par,
                  pl.BlockSpec((1, gw), lambda g, n: (0, g)),
                  pl.BlockSpec(memory_space=pl.ANY)] + [ANY] * ng,
        out_specs=[pl.BlockSpec((CHUNK, gw), lambda g, n: (n, ob + g)),
                   pl.BlockSpec((None, rh, M2_P, M2_N), lambda g, n: (n, g, 0, 0))] + [ANY] * ng,
        out_shape=[jax.ShapeDtypeStruct(mixed.shape, mixed.dtype),
                   jax.ShapeDtypeStruct((nc, heads, M2_P, M2_N), F32)]
        + [jax.ShapeDtypeStruct(s.shape, s.dtype) for s in gather],
        input_output_aliases={9: 0, **{10 + a: 2 + a for a in range(ng)}},
        scratch_shapes=[pltpu.VMEM((rh, M2_P, M2_N), F32)] + (GatherJob.sems(ng) if ng else []),
        compiler_params=_cparams(("arbitrary", "arbitrary")),
    )(act_xbc, act_xbc, act_xbc, proj, dtcol, a_log3, dt_bias3, dskip3, norm_w, mixed, *gather)


def ssd_backward(act_xbc, proj, dtcol, a_log3, dt_bias3, dskip3, norm_w, s_all, d_mixed, dproj, *, n_pad, z_off,
                 exchange=()):
    t, wm, gw, rh, heads, nc = _ssd_dims(act_xbc)
    nb = wm // M2_N
    ne = len(exchange)

    def body(x_ref, b_ref, c_ref, z_ref, dt_ref, al_ref, db_ref, dk_ref, nw_ref, s_ref, dy_ref, _, *rest):
        dx_ref, dbm_ref, dcm_ref, dz_ref, ddt_ref, dal_ref, ddb_ref, ddk_ref, dnw_ref = rest[ne:ne + 9]
        ds_ref = rest[2 * ne + 9]
        i = pl.program_id(1)
        n = nc - 1 - i
        if ne:
            _run_beside(ChipExchangeJob(rest[:ne], rest[ne + 9:2 * ne + 9], *rest[2 * ne + 10:]),
                        pl.program_id(0) * nc + i, M2_GROUPS * nc)

        @pl.when(i == 0)
        def _():
            ds_ref[...] = jnp.zeros_like(ds_ref)
            dal_ref[...] = jnp.zeros_like(dal_ref)
            ddb_ref[...] = jnp.zeros_like(ddb_ref)
            ddk_ref[...] = jnp.zeros_like(ddk_ref)
            dnw_ref[...] = jnp.zeros_like(dnw_ref)

        step = functools.partial(_ssd_step, vm=_valid_rows(n, n_pad))
        _, vjp = jax.vjp(step, s_ref[...], x_ref[...], b_ref[...], c_ref[...], z_ref[...], dt_ref[...],
                         al_ref[...], db_ref[...], dk_ref[...], nw_ref[...])
        ds, dx, dbm, dcm, dz, ddt, dal, ddb, ddk, dnw = vjp((ds_ref[...], dy_ref[...]))
        ds_ref[...] = ds
        dx_ref[...] = dx
        dbm_ref[...] = dbm
        dcm_ref[...] = dcm
        dz_ref[...] = dz.astype(dz_ref.dtype)
        ddt_ref[...] = ddt
        dal_ref[...] += dal
        ddb_ref[...] += ddb
        ddk_ref[...] += ddk
        dnw_ref[...] += dnw

    rev = lambda n: nc - 1 - n
    par = pl.BlockSpec((rh, 1, 1), lambda g, n: (g, 0, 0))
    wide = lambda off: pl.BlockSpec((CHUNK, gw), lambda g, n: (rev(n), off + g))
    narrow = lambda off: pl.BlockSpec((CHUNK, M2_N), lambda g, n: (rev(n), off + g))
    col = pl.BlockSpec((rh, CHUNK, 1), lambda g, n: (g, rev(n), 0))
    gn = M2_GROUPS * M2_N
    return pl.pallas_call(
        body, name="ssd_bwd",
        grid=(M2_GROUPS, nc),
        in_specs=[wide(0), narrow(nb), narrow(nb + M2_GROUPS), wide(z_off // gw), col, par, par, par,
                  pl.BlockSpec((1, gw), lambda g, n: (0, g)),
                  pl.BlockSpec((None, rh, M2_P, M2_N), lambda g, n: (rev(n), g, 0, 0)),
                  wide(M2_GROUPS), pl.BlockSpec(memory_space=pl.ANY)] + [ANY] * ne,
        out_specs=[wide(0), narrow(0), narrow(0), wide(z_off // gw), col, par, par, par,
                   pl.BlockSpec((1, gw), lambda g, n: (0, g))] + [ANY] * ne,
        out_shape=[jax.ShapeDtypeStruct((t, wm), F32), jax.ShapeDtypeStruct((t, gn), F32),
                   jax.ShapeDtypeStruct((t, gn), F32), jax.ShapeDtypeStruct(dproj.shape, dproj.dtype),
                   jax.ShapeDtypeStruct((heads, t, 1), F32),
                   jax.ShapeDtypeStruct((heads, 1, 1), F32), jax.ShapeDtypeStruct((heads, 1, 1), F32),
                   jax.ShapeDtypeStruct((heads, 1, 1), F32), jax.ShapeDtypeStruct((1, wm), F32)]
        + [jax.ShapeDtypeStruct((3,) + s.shape[1:], s.dtype) for s in exchange],
        input_output_aliases={11: 3},
        scratch_shapes=[pltpu.VMEM((rh, M2_P, M2_N), F32)] + (ChipExchangeJob.sems(ne) if ne else []),
        compiler_params=_cparams(("arbitrary", "arbitrary")),
    )(act_xbc, act_xbc, act_xbc, proj, dtcol, a_log3, dt_bias3, dskip3, norm_w, s_all, d_mixed, dproj, *exchange)


SUBLANES = 8
LANES = 128


def _pick(dim, target, align):
    best = None
    for d in range(align, min(dim, target) + 1, align):
        if dim % d == 0:
            best = d
    return dim if best is None else best


def _row_tile(t):
    return _pick(t, 512, 16)


def matmul(a, b, *, mode, out_dtype, name, tm=1056, tn=512, tk=2048, residual=None, out_shards=1,
           a_shards=1, b_shards=1):
    if mode == "tn":
        kd, m = a.shape
        n = b.shape[-1] * b_shards
    else:
        m, kd = a.shape[-2], a.shape[-1] * a_shards
        n = b.shape[1] if mode == "nn" else b.shape[0]
    tm = _pick(m, tm, LANES if mode == "tn" else 16)
    ks = kd // a_shards
    tk = _pick(ks, tk, LANES)
    nkb = ks // tk
    nk = kd // tk
    ns_o, ns_b = n // out_shards, n // b_shards
    tn = _pick(math.gcd(ns_o, ns_b), tn, LANES)
    npb_o, npb_b = ns_o // tn, ns_b // tn
    if mode == "tn":
        a_spec = pl.BlockSpec((tk, tm), lambda i, j, k: (k, i))
    elif a_shards == 1:
        a_spec = pl.BlockSpec((tm, tk), lambda i, j, k: (i, k))
    else:
        a_spec = pl.BlockSpec((None, tm, tk), lambda i, j, k: (k // nkb, i, k % nkb))
    contract = ((1,), (1,)) if mode == "nt" else ((1,), (0,))
    if mode == "nt":
        b_spec = pl.BlockSpec((tn, tk), lambda i, j, k: (j, k))
    elif b_shards == 1:
        b_spec = pl.BlockSpec((tk, tn), lambda i, j, k: (k, j))
    else:
        b_spec = pl.BlockSpec((None, tk, tn), lambda i, j, k: (j // npb_b, k, j % npb_b))
    has_res = residual is not None

    def body(*refs):
        a_ref, b_ref = refs[0], refs[1]
        r_ref = refs[2] if has_res else None
        o_ref = refs[3] if has_res else refs[2]
        acc_ref = refs[-1]
        k = pl.program_id(2)
        if mode == "tn":
            at_ref = refs[4] if has_res else refs[3]

            @pl.when(pl.program_id(1) == 0)
            def _():
                at_ref[k] = jnp.transpose(a_ref[...].astype(F32)).astype(BF16)

            lhs = at_ref[k]
        else:
            lhs = a_ref[...].astype(BF16)
        part = lax.dot_general(lhs, b_ref[...].astype(BF16), (contract, ((), ())), preferred_element_type=F32)

        def finish(total):
            if has_res:
                total = total + r_ref[...]
            o_ref[...] = total.astype(o_ref.dtype)

        if nk == 1:
            finish(part)
        else:
            @pl.when(k == 0)
            def _():
                acc_ref[...] = part

            @pl.when((k > 0) & (k < nk - 1))
            def _():
                acc_ref[...] += part

            @pl.when(k == nk - 1)
            def _():
                finish(acc_ref[...] + part)

    in_specs = [a_spec, b_spec]
    args = [a, b]
    if has_res:
        in_specs.append(pl.BlockSpec((tm, tn), lambda i, j, k: (i, j)))
        args.append(residual)
    if out_shards == 1:
        out_spec = pl.BlockSpec((tm, tn), lambda i, j, k: (i, j))
        out_shape = jax.ShapeDtypeStruct((m, n), out_dtype)
    else:
        out_spec = pl.BlockSpec((None, tm, tn), lambda i, j, k: (j // npb_o, i, j % npb_o))
        out_shape = jax.ShapeDtypeStruct((out_shards, m, ns_o), out_dtype)
    scratch = [pltpu.VMEM((nk, tm, tk), BF16)] if mode == "tn" else []
    if nk > 1:
        scratch.append(pltpu.VMEM((tm, tn), F32))
    return pl.pallas_call(
        body, name=name, grid=(m // tm, n // tn, nk),
        in_specs=in_specs, out_specs=out_spec, out_shape=out_shape, scratch_shapes=scratch,
        compiler_params=_cparams(("parallel", "arbitrary", "arbitrary")),
    )(*args)


def rmsnorm_forward(x, w, *, name):
    t, d = x.shape
    tm = _row_tile(t)

    def body(x_ref, w_ref, o_ref):
        xv = x_ref[...]
        r = lax.rsqrt(jnp.mean(xv * xv, -1, keepdims=True) + NORM_EPS)
        o_ref[...] = (xv * r * w_ref[...]).astype(o_ref.dtype)

    return pl.pallas_call(
        body, name=name, grid=(t // tm,),
        in_specs=[pl.BlockSpec((tm, d), lambda i: (i, 0)), pl.BlockSpec((1, d), lambda i: (0, 0))],
        out_specs=pl.BlockSpec((tm, d), lambda i: (i, 0)),
        out_shape=jax.ShapeDtypeStruct((t, d), BF16),
        compiler_params=_cparams(("parallel",)),
    )(x, w)


def _rmsnorm_grads(xv, wv, dy):
    r = lax.rsqrt(jnp.mean(xv * xv, -1, keepdims=True) + NORM_EPS)
    xh = xv * r
    g = dy * wv
    dx = r * (g - xh * jnp.mean(g * xh, -1, keepdims=True))
    return dx, jnp.sum(dy * xh, axis=0, keepdims=True)


def rmsnorm_backward(x, w, dy, dres, *, n_pad, name):
    t, d = x.shape
    tm = _row_tile(t)

    def body(x_ref, w_ref, dy_ref, dr_ref, dx_ref, dx16_ref, dw_ref):
        i = pl.program_id(0)

        @pl.when(i == 0)
        def _():
            dw_ref[...] = jnp.zeros_like(dw_ref)

        dx, dw = _rmsnorm_grads(x_ref[...], w_ref[...], dy_ref[...])
        rows = i * tm + lax.broadcasted_iota(jnp.int32, (tm, 1), 0)
        dx = jnp.where(rows >= n_pad, dx + dr_ref[...], 0.0)
        dx_ref[...] = dx
        dx16_ref[...] = dx.astype(BF16)
        dw_ref[...] += dw

    row = pl.BlockSpec((tm, d), lambda i: (i, 0))
    vec = pl.BlockSpec((1, d), lambda i: (0, 0))
    return pl.pallas_call(
        body, name=name, grid=(t // tm,),
        in_specs=[row, vec, row, row], out_specs=[row, row, vec],
        out_shape=[jax.ShapeDtypeStruct((t, d), F32), jax.ShapeDtypeStruct((t, d), BF16),
                   jax.ShapeDtypeStruct((1, d), F32)],
        compiler_params=_cparams(("arbitrary",)),
    )(x, w, dy, dres)


def loss_head(h, w, target, *, n_skip):
    t, d = h.shape
    tm = _row_tile(t)

    def body(x_ref, w_ref, y_ref, loss_ref, dx_ref, dx16_ref, dw_ref):
        i = pl.program_id(0)

        @pl.when(i == 0)
        def _():
            dw_ref[...] = jnp.zeros_like(dw_ref)
            loss_ref[...] = jnp.zeros_like(loss_ref)

        xv, wv = x_ref[...], w_ref[...]
        r = lax.rsqrt(jnp.mean(xv * xv, -1, keepdims=True) + NORM_EPS)
        rows = i * tm + lax.broadcasted_iota(jnp.int32, (tm, 1), 0)
        err = jnp.where(rows >= n_skip, xv * r * wv - y_ref[...], 0.0)
        loss_ref[...] += 0.5 * jnp.sum(jnp.mean(err * err, -1, keepdims=True))
        dx, dw = _rmsnorm_grads(xv, wv, err * (1.0 / d))
        dx_ref[...] = dx
        dx16_ref[...] = dx.astype(BF16)
        dw_ref[...] += dw

    row = pl.BlockSpec((tm, d), lambda i: (i, 0))
    vec = pl.BlockSpec((1, d), lambda i: (0, 0))
    return pl.pallas_call(
        body, name="loss_head", grid=(t // tm,),
        in_specs=[row, vec, row],
        out_specs=[pl.BlockSpec((1, LANES), lambda i: (0, 0)), row, row, vec],
        out_shape=[jax.ShapeDtypeStruct((1, LANES), F32), jax.ShapeDtypeStruct((t, d), F32),
                   jax.ShapeDtypeStruct((t, d), BF16), jax.ShapeDtypeStruct((1, d), F32)],
        compiler_params=_cparams(("arbitrary",)),
    )(h, w, target)


HALO = SUBLANES


def _conv_rows(scr_ref, w, first, rows):
    kk = w.shape[0]
    acc = w[0:1, :] * scr_ref[pl.ds(first, rows), :]
    for j in range(1, kk):
        acc = acc + w[j:j + 1, :] * scr_ref[pl.ds(first + j, rows), :]
    return acc


def _dsilu(p):
    s = _sigmoid(p)
    return s * (1.0 + p * (1.0 - s))


def conv_silu(x, w, b, *, x_off, name):
    t = x.shape[0]
    kk, width = w.shape
    tm = _row_tile(t)
    tc = _pick(width, 512, LANES)
    ob, nh = x_off // tc, tm // HALO

    def body(prev_ref, x_ref, w_ref, b_ref, o_ref, scr):
        i = pl.program_id(1)
        scr[0:HALO, :] = jnp.where(i > 0, prev_ref[...], 0.0)
        scr[HALO:HALO + tm, :] = x_ref[...]
        pre = _conv_rows(scr, w_ref[...], HALO - (kk - 1), tm) + b_ref[...]
        o_ref[...] = _silu(pre)

    return pl.pallas_call(
        body, name=name, grid=(width // tc, t // tm),
        in_specs=[pl.BlockSpec((HALO, tc), lambda j, i: (jnp.maximum(i * nh - 1, 0), ob + j)),
                  pl.BlockSpec((tm, tc), lambda j, i: (i, ob + j)),
                  pl.BlockSpec((kk, tc), lambda j, i: (0, j)),
                  pl.BlockSpec((1, tc), lambda j, i: (0, j))],
        out_specs=pl.BlockSpec((tm, tc), lambda j, i: (i, j)),
        out_shape=jax.ShapeDtypeStruct((t, width), F32),
        scratch_shapes=[pltpu.VMEM((tm + HALO, tc), F32)],
        compiler_params=_cparams(("parallel", "arbitrary")),
    )(x, x, w, b)


def conv_silu_backward(x, w, b, dact, dst, *, x_off, name):
    t = x.shape[0]
    kk, width = w.shape
    tm = _row_tile(t)
    tc = _pick(width, 512, LANES)
    ob, nh, nt = x_off // tc, tm // HALO, t // tm
    last_h = t // HALO - 1

    def body(prev_ref, x_ref, next_ref, w_ref, b_ref, d_ref, dnext_ref, _, dx_ref, dw_ref, db_ref, scr_x, scr_d):
        i = pl.program_id(1)

        @pl.when(i == 0)
        def _():
            dw_ref[...] = jnp.zeros_like(dw_ref)
            db_ref[...] = jnp.zeros_like(db_ref)

        wv = w_ref[...]
        scr_x[0:HALO, :] = jnp.where(i > 0, prev_ref[...], 0.0)
        scr_x[HALO:HALO + tm, :] = x_ref[...]
        scr_x[HALO + tm:, :] = next_ref[...]
        ext = tm + HALO
        pre = _conv_rows(scr_x, wv, HALO - (kk - 1), ext) + b_ref[...]
        scr_d[0:tm, :] = d_ref[...]
        scr_d[tm:, :] = jnp.where(i < nt - 1, dnext_ref[...], 0.0)
        scr_d[...] = scr_d[...] * _dsilu(pre)
        dx = wv[0:1, :] * scr_d[pl.ds(kk - 1, tm), :]
        for j in range(1, kk):
            dx = dx + wv[j:j + 1, :] * scr_d[pl.ds(kk - 1 - j, tm), :]
        dx_ref[...] = dx.astype(dx_ref.dtype)
        dpre = scr_d[0:tm, :]
        for j in range(kk):
            dw_ref[j:j + 1, :] += jnp.sum(dpre * scr_x[pl.ds(HALO - (kk - 1) + j, tm), :], axis=0, keepdims=True)
        db_ref[0:1, :] += jnp.sum(dpre, axis=0, keepdims=True)

    nxt = lambda j, i: (jnp.minimum((i + 1) * nh, last_h), j)
    acc = pl.BlockSpec((SUBLANES, tc), lambda j, i: (0, j))
    return pl.pallas_call(
        body, name=name, grid=(width // tc, nt),
        in_specs=[pl.BlockSpec((HALO, tc), lambda j, i: (jnp.maximum(i * nh - 1, 0), ob + j)),
                  pl.BlockSpec((tm, tc), lambda j, i: (i, ob + j)),
                  pl.BlockSpec((HALO, tc), lambda j, i: (jnp.minimum((i + 1) * nh, last_h), ob + j)),
                  pl.BlockSpec((kk, tc), lambda j, i: (0, j)),
                  pl.BlockSpec((1, tc), lambda j, i: (0, j)),
                  pl.BlockSpec((tm, tc), lambda j, i: (i, j)),
                  pl.BlockSpec((HALO, tc), nxt),
                  pl.BlockSpec(memory_space=pl.ANY)],
        out_specs=[pl.BlockSpec((tm, tc), lambda j, i: (i, ob + j)), acc, acc],
        out_shape=[jax.ShapeDtypeStruct(dst.shape, dst.dtype), jax.ShapeDtypeStruct((SUBLANES, width), F32),
                   jax.ShapeDtypeStruct((SUBLANES, width), F32)],
        input_output_aliases={7: 0},
        scratch_shapes=[pltpu.VMEM((tm + 2 * HALO, tc), F32), pltpu.VMEM((tm + HALO, tc), F32)],
        compiler_params=_cparams(("parallel", "arbitrary")),
    )(x, x, x, w, b, dact, dact, dst)


def conv_glu(u, w, *, name):
    _, t, f = u.shape
    kk = w.shape[1]
    tm = _row_tile(t)
    tc = _pick(f, 512, LANES)
    nh = tm // HALO

    def body(prev_ref, x_ref, w_ref, o_ref, scr):
        i = pl.program_id(1)
        scr[:, 0:HALO, :] = jnp.where(i > 0, prev_ref[...], 0.0)
        scr[:, HALO:, :] = x_ref[...]
        gate = _conv_rows(scr.at[0], w_ref[0], HALO - (kk - 1), tm)
        val = _conv_rows(scr.at[1], w_ref[1], HALO - (kk - 1), tm)
        o_ref[...] = (_silu(gate) * val).astype(o_ref.dtype)

    return pl.pallas_call(
        body, name=name, grid=(f // tc, t // tm),
        in_specs=[pl.BlockSpec((2, HALO, tc), lambda j, i: (0, jnp.maximum(i * nh - 1, 0), j)),
                  pl.BlockSpec((2, tm, tc), lambda j, i: (0, i, j)),
                  pl.BlockSpec((2, kk, tc), lambda j, i: (0, 0, j))],
        out_specs=pl.BlockSpec((tm, tc), lambda j, i: (i, j)),
        out_shape=jax.ShapeDtypeStruct((t, f), BF16),
        scratch_shapes=[pltpu.VMEM((2, tm + HALO, tc), F32)],
        compiler_params=_cparams(("parallel", "arbitrary")),
    )(u, u, w)


def conv_glu_backward(u, w, dact, *, name):
    _, t, f = u.shape
    kk = w.shape[1]
    tm = _row_tile(t)
    tc = _pick(f, 512, LANES)
    nh, nt = tm // HALO, t // tm
    last_h = t // HALO - 1
    ext = tm + HALO
    first = HALO - (kk - 1)

    def body(prev_ref, x_ref, next_ref, w_ref, d_ref, dn_ref, du_ref, dw_ref, scr_x, scr_d):
        i = pl.program_id(1)

        @pl.when(i == 0)
        def _():
            dw_ref[...] = jnp.zeros_like(dw_ref)

        scr_x[:, 0:HALO, :] = jnp.where(i > 0, prev_ref[...], 0.0)
        scr_x[:, HALO:HALO + tm, :] = x_ref[...]
        scr_x[:, HALO + tm:, :] = next_ref[...]
        gate = _conv_rows(scr_x.at[0], w_ref[0], first, ext)
        val = _conv_rows(scr_x.at[1], w_ref[1], first, ext)
        scr_d[0, 0:tm, :] = d_ref[...]
        scr_d[0, tm:, :] = jnp.where(i < nt - 1, dn_ref[...], 0.0)
        dact = scr_d[0]
        s = _sigmoid(gate)
        scr_d[0] = dact * val * (s * (1.0 + gate * (1.0 - s)))
        scr_d[1] = dact * (gate * s)
        for h in range(2):
            wv = w_ref[h]
            du = wv[0:1, :] * scr_d[h, pl.ds(kk - 1, tm), :]
            for j in range(1, kk):
                du = du + wv[j:j + 1, :] * scr_d[h, pl.ds(kk - 1 - j, tm), :]
            du_ref[h] = du.astype(du_ref.dtype)
            dc = scr_d[h, 0:tm, :]
            for j in range(kk):
                dw_ref[h, j:j + 1, :] += jnp.sum(dc * scr_x[h, pl.ds(first + j, tm), :], axis=0, keepdims=True)

    return pl.pallas_call(
        body, name=name, grid=(f // tc, nt),
        in_specs=[pl.BlockSpec((2, HALO, tc), lambda j, i: (0, jnp.maximum(i * nh - 1, 0), j)),
                  pl.BlockSpec((2, tm, tc), lambda j, i: (0, i, j)),
                  pl.BlockSpec((2, HALO, tc), lambda j, i: (0, jnp.minimum((i + 1) * nh, last_h), j)),
                  pl.BlockSpec((2, kk, tc), lambda j, i: (0, 0, j)),
                  pl.BlockSpec((tm, tc), lambda j, i: (i, j)),
                  pl.BlockSpec((HALO, tc), lambda j, i: (jnp.minimum((i + 1) * nh, last_h), j))],
        out_specs=[pl.BlockSpec((2, tm, tc), lambda j, i: (0, i, j)),
                   pl.BlockSpec((2, SUBLANES, tc), lambda j, i: (0, 0, j))],
        out_shape=[jax.ShapeDtypeStruct((2, t, f), BF16), jax.ShapeDtypeStruct((2, SUBLANES, f), F32)],
        scratch_shapes=[pltpu.VMEM((2, tm + 2 * HALO, tc), F32), pltpu.VMEM((2, ext, tc), F32)],
        compiler_params=_cparams(("parallel", "arbitrary")),
    )(u, u, u, w, dact, dact)


def adamw(w, g, m, v, *, name):
    r, c = w.shape
    tr = _pick(r, max(SUBLANES, (2 * 1024 * 1024) // (4 * c) // SUBLANES * SUBLANES), SUBLANES)

    def body(w_ref, g_ref, m_ref, v_ref, d_ref, nm_ref, nv_ref):
        gv = g_ref[...]
        nm = ADAM_B1 * m_ref[...] + (1.0 - ADAM_B1) * gv
        nv = ADAM_B2 * v_ref[...] + (1.0 - ADAM_B2) * (gv * gv)
        m_hat = nm / (1.0 - ADAM_B1 ** ADAM_STEP)
        v_hat = nv / (1.0 - ADAM_B2 ** ADAM_STEP)
        d_ref[...] = -ADAM_LR * (m_hat / (jnp.sqrt(v_hat) + ADAM_EPS) + ADAM_WD * w_ref[...])
        nm_ref[...] = nm
        nv_ref[...] = nv

    blk = pl.BlockSpec((tr, c), lambda i: (i, 0))
    shp = jax.ShapeDtypeStruct((r, c), F32)
    return pl.pallas_call(
        body, name=name, grid=(r // tr,),
        in_specs=[blk] * 4, out_specs=[blk] * 3, out_shape=[shp] * 3,
        compiler_params=_cparams(("parallel",)),
    )(w, g, m, v)


def cast_into_slot(x, slot, n_slots, *, name):
    r, c = x.shape
    tr = _pick(r, max(16, (2 * 1024 * 1024) // (4 * c) // 16 * 16), 16)

    def body(s_ref, x_ref, o_ref):
        o_ref[...] = x_ref[...].astype(o_ref.dtype)

    grid_spec = pltpu.PrefetchScalarGridSpec(
        num_scalar_prefetch=1, grid=(r // tr,),
        in_specs=[pl.BlockSpec((tr, c), lambda i, s: (i, 0))],
        out_specs=pl.BlockSpec((None, tr, c), lambda i, s: (s[0], i, 0)))
    return pl.pallas_call(
        body, name=name, grid_spec=grid_spec,
        out_shape=jax.ShapeDtypeStruct((n_slots, r, c), BF16),
        compiler_params=_cparams(("arbitrary",)),
    )(jnp.reshape(slot, (1,)).astype(jnp.int32), x)


class Layout:
    def __init__(self, d):
        self.d = d
        self.h_dn = d // DN_DK
        self.h_m2 = d // M2_P
        self.gn = M2_GROUPS * M2_N
        self.w_xbc = d + 2 * self.gn
        self.z_off = 3 * d
        self.m2z_off = 4 * d
        self.xbc_off = 5 * d
        self.small_off = 5 * d + self.w_xbc
        self.n_small = 2 * self.h_dn + self.h_m2
        self.p = self.small_off + LANES
        self.p_orig = self.small_off + self.n_small

    def permute_w_in(self, w):
        d, s2 = self.d, 2 * self.h_dn
        pad = jnp.zeros((w.shape[0], LANES - self.n_small), w.dtype)
        return jnp.concatenate([w[:, :4 * d], w[:, 4 * d + s2:4 * d + s2 + d + self.w_xbc],
                                w[:, 4 * d:4 * d + s2], w[:, self.p_orig - self.h_m2:], pad], axis=1)

    def unpermute_w_in(self, w):
        d, s2 = self.d, 2 * self.h_dn
        so = self.small_off
        return jnp.concatenate([w[:, :4 * d], w[:, so:so + s2], w[:, 4 * d:so], w[:, so + s2:so + self.n_small]], axis=1)


def _cols(small, lo, hi):
    return jnp.transpose(small[:, lo:hi])[..., None]


def local_step(h0, target, wts, slabs, chip, core, *, n_pad, n_meta):
    t, d = h0.shape
    lay = Layout(d)
    hd, hm = lay.h_dn, lay.h_m2
    zeros_b = jnp.zeros((1, 3 * d), F32)
    r3 = lambda v, n: v.reshape(n, 1, 1)
    dn_al, dn_db = r3(wts["dn_a_log"], hd), r3(wts["dn_dt_bias"], hd)
    dn_nw = wts["dn_norm_w"].reshape(1, 1, DN_DK)
    m2_al, m2_db, m2_dk = r3(wts["m2_a_log"], hm), r3(wts["m2_dt_bias"], hm), r3(wts["m2_d"], hm)

    hn1 = rmsnorm_forward(h0, wts["norm_mix_w"], name="norm_mix")
    proj = matmul(hn1, wts["w_in"], mode="nn", out_dtype=F32, name="in_proj", tn=640)
    act_qkv = conv_silu(proj, wts["dn_conv_w"], zeros_b, x_off=0, name="dn_conv")
    act_xbc = conv_silu(proj, wts["m2_conv_w"], wts["m2_conv_b"], x_off=lay.xbc_off, name="m2_conv")
    small = proj[:, lay.small_off:]
    bcol, acol, dtcol = _cols(small, 0, hd), _cols(small, hd, 2 * hd), _cols(small, 2 * hd, 2 * hd + hm)
    mixed, s_dn, inv_dn, w_out_all, ffn_down_all = gdn_forward(
        act_qkv, proj, bcol, acol, dn_al, dn_db, dn_nw, n_pad=n_pad, z_off=lay.z_off, out_cols=2 * d,
        gather=[slabs["w_out"], slabs["ffn_down"]])
    mixed, s_m2, ffn_up_all = ssd_forward(act_xbc, proj, dtcol, m2_al, m2_db, m2_dk, wts["m2_norm_w"], mixed,
                                          n_pad=n_pad, z_off=lay.m2z_off, gather=[slabs["ffn_up"]])
    wts = dict(wts, w_out=w_out_all.reshape(-1, d), ffn_down=ffn_down_all.reshape(-1, d),
               ffn_up=jnp.transpose(ffn_up_all, (1, 0, 2)).reshape(d, -1))
    h1 = matmul(mixed, wts["w_out"], mode="nn", out_dtype=F32, name="out_proj", tk=2 * d, residual=h0)
    hn2 = rmsnorm_forward(h1, wts["norm_ffn_w"], name="norm_ffn")
    up = matmul(hn2, wts["ffn_up"], mode="nn", out_dtype=F32, name="ffn_up", tn=1024, out_shards=2)
    kf, f = wts["ffn_conv_w"].shape[0], wts["ffn_conv_w"].shape[1] // 2
    w_glu = jnp.transpose(wts["ffn_conv_w"].reshape(kf, 2, f), (1, 0, 2))
    act = conv_glu(up, w_glu, name="ffn_conv")
    h2 = matmul(act, wts["ffn_down"], mode="nn", out_dtype=F32, name="ffn_down", tk=f // 2, residual=h1)
    loss, dh2, dh2_16, d_nfw = loss_head(h2, wts["norm_final_w"].reshape(1, d), target, n_skip=n_pad + n_meta)

    g = {}
    d_act = matmul(dh2_16, wts["ffn_down"], mode="nt", out_dtype=F32, name="d_ffn_act", tn=512)
    g["ffn_down"] = matmul(act, dh2_16, mode="tn", out_dtype=F32, name="dw_ffn_down", tm=512, tn=1024, tk=t)
    dup, d_fcw = conv_glu_backward(up, w_glu, d_act, name="d_ffn_conv")
    g["ffn_conv_w"] = jnp.transpose(d_fcw[:, :kf], (1, 0, 2)).reshape(kf, 2 * f)
    dhn2 = matmul(dup, wts["ffn_up"], mode="nt", out_dtype=F32, name="d_norm_ffn_out", tk=f, a_shards=2)
    g["ffn_up"] = matmul(hn2, dup, mode="tn", out_dtype=F32, name="dw_ffn_up", tm=512, tn=1408, tk=t,
                         b_shards=2, out_shards=4)
    dh1, dh1_16, g["norm_ffn_w"] = rmsnorm_backward(h1, wts["norm_ffn_w"], dhn2, dh2, n_pad=n_pad, name="d_norm_ffn")
    d_mixed = matmul(dh1_16, wts["w_out"], mode="nt", out_dtype=F32, name="d_mixed", tn=512)
    g["w_out"] = matmul(mixed, dh1_16, mode="tn", out_dtype=F32, name="dw_out", tm=512, tn=1024, tk=t)

    early = ("w_out", "ffn_up", "ffn_down")
    slots = [g[k] if g[k].ndim == 3 else g[k].reshape(N_CHIPS, -1, g[k].shape[1]) for k in early]
    p16, own = [], []
    for k, slot, rb in zip(early, slots, pair_exchange(slots, name="grad_pair_exchange_early")):
        a, b = pair_add(slot, rb, chip, core, name="grad_pair_add_" + k)
        p16.append(a)
        own.append(b)

    dq, dk, dv, dproj, dbc, dac, g_al, g_db, g_nw = gdn_backward(
        act_qkv, proj, bcol, acol, dn_al, dn_db, dn_nw, s_dn, inv_dn, d_mixed, n_pad=n_pad, z_off=lay.z_off)
    g["dn_a_log"], g["dn_dt_bias"] = g_al.reshape(1, hd), g_db.reshape(1, hd)
    g["dn_norm_w"] = jnp.sum(g_nw, axis=0)
    dxs, dbm, dcm, dproj, ddt, g_al, g_db, g_dk, g["m2_norm_w"], *from_chips = ssd_backward(
        act_xbc, proj, dtcol, m2_al, m2_db, m2_dk, wts["m2_norm_w"], s_m2, d_mixed, dproj,
        n_pad=n_pad, z_off=lay.m2z_off, exchange=p16)
    for k, o, q in zip(early, own, from_chips):
        g[k] = (o, q)
    g["m2_a_log"], g["m2_dt_bias"], g["m2_d"] = g_al.reshape(1, hm), g_db.reshape(1, hm), g_dk.reshape(1, hm)

    kc = wts["dn_conv_w"].shape[0]
    dw_parts = []
    for idx, dpart in enumerate((dq, dk, dv)):
        dproj, dw, _ = conv_silu_backward(proj, wts["dn_conv_w"][:, idx * d:(idx + 1) * d], zeros_b[:, :d], dpart,
                                          dproj, x_off=idx * d, name=f"d_dn_conv{idx}")
        dw_parts.append(dw[:kc])
    g["dn_conv_w"] = jnp.concatenate(dw_parts, axis=1)
    dw_parts, db_parts = [], []
    off = 0
    for idx, dpart in enumerate((dxs, dbm, dcm)):
        wd = dpart.shape[1]
        dproj, dw, db = conv_silu_backward(proj, wts["m2_conv_w"][:, off:off + wd], wts["m2_conv_b"][:, off:off + wd],
                                           dpart, dproj, x_off=lay.xbc_off + off, name=f"d_m2_conv{idx}")
        dw_parts.append(dw[:kc])
        db_parts.append(db[:1])
        off += wd
    g["m2_conv_w"] = jnp.concatenate(dw_parts, axis=1)
    g["m2_conv_b"] = jnp.concatenate(db_parts, axis=1)
    dsmall = jnp.concatenate([jnp.transpose(dbc[..., 0]), jnp.transpose(dac[..., 0]), jnp.transpose(ddt[..., 0]),
                              jnp.zeros((t, LANES - lay.n_small), F32)], axis=1)
    dproj = lax.dynamic_update_slice(dproj, dsmall.astype(BF16), (0, lay.small_off))
    dhn1 = matmul(dproj, wts["w_in"], mode="nt", out_dtype=F32, name="d_norm_mix_out", tk=lay.p // 3)
    g["w_in"] = matmul(hn1, dproj, mode="tn", out_dtype=F32, name="dw_in", tm=512, tn=640, tk=t)
    dh0, _, g["norm_mix_w"] = rmsnorm_backward(h0, wts["norm_mix_w"], dhn1, dh1, n_pad=n_pad, name="d_norm_mix")
    g["norm_final_w"] = d_nfw
    return loss[0, 0], dh0, g


MESH = pl.DeviceIdType.MESH
ANY = pl.BlockSpec(memory_space=pl.ANY)
N_CHIPS = 4
N_DEV = 8


def _mesh_pos():
    return lax.axis_index("x"), lax.axis_index("y"), lax.axis_index("c")


def _other_chips(x, y):
    return [(1 - x, y), (x, 1 - y), (1 - x, 1 - y)]


def _rcopy(src, dst, send_sems, recv_sems, k, to):
    return pltpu.make_async_remote_copy(src_ref=src, dst_ref=dst, send_sem=send_sems.at[k], recv_sem=recv_sems.at[k],
                                        device_id=to, device_id_type=MESH)


class GatherJob:
    def __init__(self, slabs, send_sems, recv_sems):
        self.slabs, self.send, self.recv = slabs, send_sems, recv_sems
        self.x, self.y, self.c = _mesh_pos()

    @staticmethod
    def sems(n):
        return [pltpu.SemaphoreType.DMA((6 * n,)), pltpu.SemaphoreType.DMA((6 * n,))]

    def _pieces(self):
        x, y, c = self.x, self.y, self.c
        for a, slab in enumerate(self.slabs):
            half = slab.shape[1] // 2
            for j, (px, py) in enumerate(_other_chips(x, y)):
                yield a, j, (px, py), slab, pl.ds(c * half, half), pl.ds((1 - c) * half, half)

    def _ici(self, a, j, chip, ref):
        return _rcopy(ref, ref, self.send, self.recv, 6 * a + j, (chip[0], chip[1], self.c))

    def _d2d(self, a, j, ref):
        return _rcopy(ref, ref, self.send, self.recv, 6 * a + 3 + j, (self.x, self.y, 1 - self.c))

    def begin(self):
        for a, j, chip, slab, mine, _ in self._pieces():
            self._ici(a, j, chip, slab.at[2 * self.x + self.y, mine]).start()

    def pass_on(self):
        for a, j, chip, slab, mine, _ in self._pieces():
            landed = slab.at[2 * chip[0] + chip[1], mine]
            self._ici(a, j, chip, landed).wait_recv()
            self._d2d(a, j, landed).start()

    def end(self):
        for a, j, chip, slab, mine, theirs in self._pieces():
            self._d2d(a, j, slab.at[2 * chip[0] + chip[1], theirs]).wait_recv()
        for a, j, chip, slab, mine, _ in self._pieces():
            self._ici(a, j, chip, slab.at[2 * self.x + self.y, mine]).wait_send()
            self._d2d(a, j, slab.at[2 * chip[0] + chip[1], mine]).wait_send()


class ChipExchangeJob:
    def __init__(self, parts, outs, send_sems, recv_sems):
        self.parts, self.outs, self.send, self.recv = parts, outs, send_sems, recv_sems
        self.x, self.y, self.c = _mesh_pos()

    @staticmethod
    def sems(n):
        return [pltpu.SemaphoreType.DMA((3 * n,)), pltpu.SemaphoreType.DMA((3 * n,))]

    def _copies(self):
        for a, (part, out) in enumerate(zip(self.parts, self.outs)):
            for j, (px, py) in enumerate(_other_chips(self.x, self.y)):
                yield _rcopy(part.at[2 * px + py], out.at[j], self.send, self.recv, 3 * a + j, (px, py, self.c))

    def begin(self):
        for cp in self._copies():
            cp.start()

    def end(self):
        for cp in self._copies():
            cp.wait()


def gather_shards(slabs, *, name):
    n = len(slabs)

    def body(*refs):
        job = GatherJob(refs[n:2 * n], *refs[2 * n:])
        job.begin()
        job.pass_on()
        job.end()

    return pl.pallas_call(
        body, name=name,
        in_specs=[ANY] * n, out_specs=[ANY] * n,
        out_shape=[jax.ShapeDtypeStruct(s.shape, s.dtype) for s in slabs],
        input_output_aliases={a: a for a in range(n)},
        scratch_shapes=GatherJob.sems(n),
        compiler_params=pltpu.CompilerParams(has_side_effects=True),
    )(*slabs)


def pair_exchange(grads, *, name):
    n = len(grads)

    def body(*refs):
        ins, outs = refs[:n], refs[n:2 * n]
        send_sems, recv_sems = refs[2 * n:]
        x, y, c = _mesh_pos()
        cps = []
        for a in range(n):
            half = ins[a].shape[1] // 2
            cp = _rcopy(ins[a].at[:, pl.ds((1 - c) * half, half), :], outs[a], send_sems, recv_sems, a, (x, y, 1 - c))
            cp.start()
            cps.append(cp)
        for cp in cps:
            cp.wait()

    return pl.pallas_call(
        body, name=name, in_specs=[ANY] * n, out_specs=[ANY] * n,
        out_shape=[jax.ShapeDtypeStruct((s.shape[0], s.shape[1] // 2, s.shape[2]), s.dtype) for s in grads],
        scratch_shapes=[pltpu.SemaphoreType.DMA((n,)), pltpu.SemaphoreType.DMA((n,))],
        compiler_params=pltpu.CompilerParams(has_side_effects=True),
    )(*grads)


def chip_exchange(parts, *, name):
    n = len(parts)

    def body(*refs):
        job = ChipExchangeJob(refs[:n], refs[n:2 * n], *refs[2 * n:])
        job.begin()
        job.end()

    return pl.pallas_call(
        body, name=name, in_specs=[ANY] * n, out_specs=[ANY] * n,
        out_shape=[jax.ShapeDtypeStruct((3,) + s.shape[1:], s.dtype) for s in parts],
        scratch_shapes=ChipExchangeJob.sems(n),
        compiler_params=pltpu.CompilerParams(has_side_effects=True),
    )(*parts)


def pair_join(wholes, *, name):
    n = len(wholes)

    def body(*refs):
        outs = refs[n:2 * n]
        send_sems, recv_sems = refs[2 * n:]
        x, y, c = _mesh_pos()
        cps = []
        for a in range(n):
            half = outs[a].shape[0] // 2
            rows = outs[a].at[pl.ds(c * half, half)]
            cp = _rcopy(rows, rows, send_sems, recv_sems, a, (x, y, 1 - c))
            cp.start()
            cps.append(cp)
        for a, cp in enumerate(cps):
            cp.wait_send()
            half = outs[a].shape[0] // 2
            theirs = outs[a].at[pl.ds((1 - c) * half, half)]
            _rcopy(theirs, theirs, send_sems, recv_sems, a, (x, y, 1 - c)).wait_recv()

    return pl.pallas_call(
        body, name=name, in_specs=[ANY] * n, out_specs=[ANY] * n,
        out_shape=[jax.ShapeDtypeStruct(s.shape, s.dtype) for s in wholes],
        input_output_aliases={a: a for a in range(n)},
        scratch_shapes=[pltpu.SemaphoreType.DMA((n,)), pltpu.SemaphoreType.DMA((n,))],
        compiler_params=pltpu.CompilerParams(has_side_effects=True),
    )(*wholes)


def gather_all(v, *, name):
    def body(in_ref, out_ref, send_sems, recv_sems, local_sem):
        x, y, c = _mesh_pos()
        mine = out_ref.at[4 * x + 2 * y + c]
        lc = pltpu.make_async_copy(in_ref, mine, local_sem)
        lc.start()
        cps = []
        for k in range(1, N_DEV):
            flip = lambda v, bit: 1 - v if (k >> bit) & 1 else v
            cp = _rcopy(in_ref, mine, send_sems, recv_sems, k - 1, (flip(x, 2), flip(y, 1), flip(c, 0)))
            cp.start()
            cps.append(cp)
        for cp in cps:
            cp.wait()
        lc.wait()

    return pl.pallas_call(
        body, name=name, in_specs=[ANY], out_specs=ANY,
        out_shape=jax.ShapeDtypeStruct((N_DEV,) + v.shape, v.dtype),
        scratch_shapes=[pltpu.SemaphoreType.DMA((N_DEV - 1,)), pltpu.SemaphoreType.DMA((N_DEV - 1,)),
                        pltpu.SemaphoreType.DMA(())],
        compiler_params=pltpu.CompilerParams(has_side_effects=True),
    )(v)


def _sum_tile(rows, cols):
    return _pick(rows, max(16, (1024 * 1024) // (4 * cols) // 16 * 16), 16)


def pair_add(g, rb, chip, c, *, name):
    _, r, cols = g.shape
    half = r // 2
    tr = _sum_tile(half, cols)
    nrt = half // tr

    def body(s_ref, g_ref, rb_ref, p16_ref, own_ref):
        v = g_ref[...] + rb_ref[...]
        p16_ref[...] = v.astype(p16_ref.dtype)

        @pl.when(pl.program_id(1) == s_ref[0])
        def _():
            own_ref[...] = v

    grid_spec = pltpu.PrefetchScalarGridSpec(
        num_scalar_prefetch=1, grid=(nrt, N_CHIPS),
        in_specs=[pl.BlockSpec((None, tr, cols), lambda i, k, s: (k, s[1] * nrt + i, 0)),
                  pl.BlockSpec((None, tr, cols), lambda i, k, s: (k, i, 0))],
        out_specs=[pl.BlockSpec((None, tr, cols), lambda i, k, s: (k, i, 0)),
                   pl.BlockSpec((tr, cols), lambda i, k, s: (i, 0))])
    return pl.pallas_call(
        body, name=name, grid_spec=grid_spec,
        out_shape=[jax.ShapeDtypeStruct((N_CHIPS, half, cols), BF16), jax.ShapeDtypeStruct((half, cols), F32)],
        compiler_params=_cparams(("arbitrary", "arbitrary")),
    )(jnp.stack([chip, c]).astype(jnp.int32), g, rb)


def chip_add(own, q, c, *, name):
    r, cols = own.shape
    tr = _sum_tile(r, cols)
    nrt = r // tr

    def body(s_ref, own_ref, q_ref, o_ref):
        o_ref[...] = ((own_ref[...] + q_ref[0].astype(F32)) + q_ref[1].astype(F32)) + q_ref[2].astype(F32)

    grid_spec = pltpu.PrefetchScalarGridSpec(
        num_scalar_prefetch=1, grid=(nrt,),
        in_specs=[pl.BlockSpec((tr, cols), lambda i, s: (i, 0)), pl.BlockSpec((3, tr, cols), lambda i, s: (0, i, 0))],
        out_specs=pl.BlockSpec((tr, cols), lambda i, s: (s[0] * nrt + i, 0)))
    return pl.pallas_call(
        body, name=name, grid_spec=grid_spec,
        out_shape=jax.ShapeDtypeStruct((2 * r, cols), F32),
        compiler_params=_cparams(("arbitrary",)),
    )(jnp.reshape(c, (1,)).astype(jnp.int32), own, q)


def sum_slots(v, *, name):
    n, r, cols = v.shape
    tr = _sum_tile(r, cols)

    def body(v_ref, o_ref):
        acc = v_ref[0]
        for k in range(1, n):
            acc = acc + v_ref[k]
        o_ref[...] = acc

    return pl.pallas_call(
        body, name=name, grid=(r // tr,),
        in_specs=[pl.BlockSpec((n, tr, cols), lambda i: (0, i, 0))],
        out_specs=pl.BlockSpec((tr, cols), lambda i: (i, 0)),
        out_shape=jax.ShapeDtypeStruct((r, cols), F32),
        compiler_params=_cparams(("parallel",)),
    )(v)


PACK_ROWS = 16


def _pack(arrays):
    parts = []
    for a in arrays:
        flat = a.reshape(-1).astype(F32)
        size = PACK_ROWS * LANES
        pad = (-flat.shape[0]) % size
        parts.append(jnp.pad(flat, (0, pad)))
    return jnp.concatenate(parts).reshape(-1, LANES)


def _unpack(slab, shapes):
    out, row = [], 0
    for shp in shapes:
        n = 1
        for s in shp:
            n *= s
        rows = -(-n // (PACK_ROWS * LANES)) * PACK_ROWS
        out.append(slab[row:row + rows].reshape(-1)[:n].reshape(shp))
        row += rows
    return out


WEIGHT_NAMES = ("meta_tokens", "norm_mix_w", "w_in", "dn_conv_w", "dn_a_log", "dn_dt_bias", "dn_norm_w", "m2_conv_w",
                "m2_conv_b", "m2_a_log", "m2_dt_bias", "m2_d", "m2_norm_w", "w_out", "norm_ffn_w", "ffn_up",
                "ffn_conv_w", "ffn_down", "norm_final_w")
BIG = ("w_in", "w_out", "ffn_up", "ffn_down")
BIG_COLUMN_SHARDED = ("w_in", "ffn_up")
SMALL_SHARDED = ("meta_tokens", "dn_conv_w", "m2_conv_w", "ffn_conv_w")
SMALL = tuple(n for n in WEIGHT_NAMES if n not in BIG)


def kernel(x, meta_tokens, norm_mix_w, w_in, dn_conv_w, dn_a_log, dn_dt_bias, dn_norm_w, m2_conv_w, m2_conv_b, m2_a_log, m2_dt_bias, m2_d, m2_norm_w, w_out, norm_ffn_w, ffn_up, ffn_conv_w, ffn_down, norm_final_w, loss_target, m_meta_tokens, m_norm_mix_w, m_w_in, m_dn_conv_w, m_dn_a_log, m_dn_dt_bias, m_dn_norm_w, m_m2_conv_w, m_m2_conv_b, m_m2_a_log, m_m2_dt_bias, m_m2_d, m_m2_norm_w, m_w_out, m_norm_ffn_w, m_ffn_up, m_ffn_conv_w, m_ffn_down, m_norm_final_w, v_meta_tokens, v_norm_mix_w, v_w_in, v_dn_conv_w, v_dn_a_log, v_dn_dt_bias, v_dn_norm_w, v_m2_conv_w, v_m2_conv_b, v_m2_a_log, v_m2_dt_bias, v_m2_d, v_m2_norm_w, v_w_out, v_norm_ffn_w, v_ffn_up, v_ffn_conv_w, v_ffn_down, v_norm_final_w):
    args = tuple(locals().values())
    nw = len(WEIGHT_NAMES)
    wt = dict(zip(WEIGHT_NAMES, args[1:1 + nw]))
    mom = dict(zip(WEIGHT_NAMES, args[2 + nw:2 + 2 * nw]))
    var = dict(zip(WEIGHT_NAMES, args[2 + 2 * nw:2 + 3 * nw]))
    xi, yi, ci = _mesh_pos()
    chip = 2 * xi + yi
    seq, d = x.shape[1], x.shape[2]
    n_meta = wt["meta_tokens"].shape[0]
    n_pad = (-(n_meta + seq)) % ROW_ALIGN
    lay = Layout(d)

    shard2d = {k: wt[k].reshape(wt[k].shape[-2:]) for k in BIG}
    small_local = [wt[k].reshape(wt[k].shape[-2:]) for k in SMALL_SHARDED]
    small_slab = _pack(small_local)
    small_slab = lax.dynamic_update_slice(jnp.zeros((N_CHIPS,) + small_slab.shape, F32), small_slab[None], (chip, 0, 0))
    slabs = {k: cast_into_slot(shard2d[k], chip, N_CHIPS, name="cast_" + k) for k in BIG}
    w_in_all, small_all = gather_shards([slabs.pop("w_in"), small_slab], name="gather_w_in")
    full = {"w_in": lay.permute_w_in(jnp.transpose(w_in_all, (1, 0, 2)).reshape(d, -1))}
    per_chip = [_unpack(small_all[s], [a.shape for a in small_local]) for s in range(N_CHIPS)]
    for idx, k in enumerate(SMALL_SHARDED):
        full[k] = jnp.concatenate([per_chip[s][idx] for s in range(N_CHIPS)], axis=-1)
    for k in SMALL:
        if k not in SMALL_SHARDED:
            full[k] = wt[k]

    h0 = jnp.concatenate([jnp.zeros((n_pad, d), F32), full["meta_tokens"], x[0]], axis=0)
    target = jnp.concatenate([jnp.zeros((n_pad + n_meta, d), F32), loss_target[0]], axis=0)
    loss_local, dh0, g = local_step(h0, target, full, slabs, chip, ci, n_pad=n_pad, n_meta=n_meta)
    grad_x = dh0[n_pad + n_meta:][None]
    g["meta_tokens"] = dh0[n_pad:n_pad + n_meta]
    loss = lax.psum(loss_local, ("x", "y", "c"))

    w_in_slots = jnp.transpose(lay.unpermute_w_in(g["w_in"]).reshape(d, N_CHIPS, -1), (1, 0, 2))
    (from_sibling,) = pair_exchange([w_in_slots], name="grad_pair_exchange_w_in")
    p16, own = pair_add(w_in_slots, from_sibling, chip, ci, name="grad_pair_add_w_in")
    (from_chips,) = chip_exchange([p16], name="grad_chip_exchange_w_in")
    g["w_in"] = (own, from_chips)
    wholes = [chip_add(*g[k], ci, name="grad_chip_add_" + k) for k in BIG]
    grad = dict(zip(BIG, pair_join(wholes, name="grad_pair_join")))

    small_shapes = [g[k].shape for k in SMALL]
    summed = sum_slots(gather_all(_pack([g[k] for k in SMALL]), name="small_grad_gather"), name="small_grad_sum")
    for k, v in zip(SMALL, _unpack(summed, small_shapes)):
        if k in SMALL_SHARDED:
            width = wt[k].shape[-1]
            v = lax.dynamic_slice_in_dim(v, chip * width, width, axis=v.ndim - 1)
        grad[k] = v

    delta, new_m, new_v = {}, {}, {}
    for k in BIG:
        shp = shard2d[k].shape
        delta[k], new_m[k], new_v[k] = adamw(shard2d[k], grad[k], mom[k].reshape(shp), var[k].reshape(shp),
                                             name="adamw_" + k)
    shapes = [wt[k].shape for k in SMALL]
    packed = adamw(_pack([wt[k] for k in SMALL]), _pack([grad[k] for k in SMALL]), _pack([mom[k] for k in SMALL]),
                   _pack([var[k] for k in SMALL]), name="adamw_small")
    for res, slab in zip((delta, new_m, new_v), packed):
        for k, v in zip(SMALL, _unpack(slab, shapes)):
            res[k] = v
    outs = [loss, grad_x]
    for res in (grad, delta, new_m, new_v):
        outs += [res[k].reshape(wt[k].shape) for k in WEIGHT_NAMES]
    return tuple(outs)
```

```python
import functools
import math

import jax
import jax.numpy as jnp
from jax import lax
from jax.experimental import pallas as pl
from jax.experimental.pallas import tpu as pltpu

F32 = jnp.float32
BF16 = jnp.bfloat16
HI = lax.Precision.HIGHEST

CHUNK = 64
ROW_ALIGN = 128
NORM_EPS = 1e-6
DN_DK = 128
M2_P = 64
M2_N = 128
M2_GROUPS = 4
HEAD_BLOCK = 8
VMEM_LIMIT = 56 * 1024 * 1024

ADAM_LR, ADAM_B1, ADAM_B2, ADAM_EPS, ADAM_WD, ADAM_STEP = 0.001, 0.9, 0.999, 1e-08, 0.01, 10


def _cparams(sem=None):
    return pltpu.CompilerParams(dimension_semantics=sem, vmem_limit_bytes=VMEM_LIMIT)


def _silu(x):
    return x / (1.0 + jnp.exp(-x))


def _sigmoid(x):
    return 1.0 / (1.0 + jnp.exp(-x))


def _softplus(x):
    return jnp.maximum(x, 0.0) + jnp.log(1.0 + jnp.exp(-jnp.abs(x)))


def _tri_masks():
    r = lax.broadcasted_iota(jnp.int32, (CHUNK, CHUNK), 0)
    c = lax.broadcasted_iota(jnp.int32, (CHUNK, CHUNK), 1)
    return (r >= c)[None], (r > c)[None], (r == c)[None]


def _col2row(col, eye):
    return jnp.sum(jnp.where(eye, col, 0.0), axis=1, keepdims=True)


def _cumsum_col(col, causal, eye):
    row = _col2row(col, eye)
    return jnp.sum(jnp.where(causal, row, 0.0), axis=2, keepdims=True)


def _bdot(a, b, dims, precision=None):
    (ca, cb) = dims
    if precision is None:
        a = a.astype(BF16)
        b = b.astype(BF16)
    return lax.dot_general(a, b, (((ca,), (cb,)), ((0,), (0,))), precision=precision,
                           preferred_element_type=F32)


def _dot3(a, b, ca, cb):
    def split(v):
        hi = v.astype(BF16)
        return hi, (v - hi.astype(F32)).astype(BF16)

    def dot(p, q):
        return lax.dot_general(p, q, (((ca,), (cb,)), ((0,), (0,))), preferred_element_type=F32)

    (ah, al), (bh, bl) = split(a), split(b)
    return dot(ah, bh) + (dot(ah, bl) + dot(al, bh))


@jax.custom_vjp
def _bmm3(a, b):
    return _dot3(a, b, 2, 1)


def _bmm3_fwd(a, b):
    return _dot3(a, b, 2, 1), (a, b)


def _bmm3_bwd(res, g):
    a, b = res
    return _dot3(g, b, 2, 2), _dot3(a, g, 1, 1)


_bmm3.defvjp(_bmm3_fwd, _bmm3_bwd)


def _unit_lower_inverse(a_mat, eye):
    inv = jnp.where(eye, 1.0, 0.0) - a_mat
    pw = a_mat
    n = 2
    while n < CHUNK:
        pw = _bmm3(pw, pw)
        inv = inv + _bmm3(inv, pw)
        n *= 2
    return inv


@jax.custom_vjp
def _known_inverse(a_mat, inv):
    return inv


def _known_inverse_fwd(a_mat, inv):
    return inv, inv


def _known_inverse_bwd(inv, g):
    return -_dot3(_dot3(inv, g, 1, 1), inv, 2, 2), jnp.zeros_like(inv)


_known_inverse.defvjp(_known_inverse_fwd, _known_inverse_bwd)


def _gdn_step(state, qa, ka, va, z, braw, araw, a_log, dt_bias, norm_w, vm, inv=None, want_inv=False):
    causal, strict, eye = _tri_masks()
    qa, ka, va = qa * vm, ka * vm, va * vm
    q = qa * lax.rsqrt(jnp.sum(qa * qa, -1, keepdims=True) + NORM_EPS) * (DN_DK ** -0.5)
    k = ka * lax.rsqrt(jnp.sum(ka * ka, -1, keepdims=True) + NORM_EPS)
    beta = _sigmoid(braw) * vm
    g = -jnp.exp(a_log) * _softplus(araw + dt_bias) * vm
    gcum = _cumsum_col(g, causal, eye)
    grow = _col2row(gcum, eye)
    decay = jnp.where(causal, jnp.exp(jnp.where(causal, gcum - grow, 0.0)), 0.0)
    kk = _bdot(k, k, (2, 2))
    a_mat = jnp.where(strict, beta * kk * decay, 0.0)
    tinv = _unit_lower_inverse(a_mat, eye) if inv is None else _known_inverse(a_mat, inv)
    egc = jnp.exp(gcum)
    u = _bmm3(tinv, va * beta)
    w = _bmm3(tinv, k * (beta * egc))
    v_new = u - _bdot(w, state, (2, 1))
    o_inter = _bdot(q * egc, state, (2, 1))
    qk = _bdot(q, k, (2, 2)) * decay
    o = o_inter + _bdot(qk, v_new, (2, 1))
    g_last = jnp.sum(g, axis=1, keepdims=True)
    new_state = state * jnp.exp(g_last) + _bdot(k * jnp.exp(g_last - gcum), v_new, (1, 1))
    o = o * lax.rsqrt(jnp.mean(o * o, -1, keepdims=True) + NORM_EPS) * norm_w * _silu(z)
    return (new_state, o, tinv) if want_inv else (new_state, o)


def _valid_rows(chunk, n_pad):
    r = chunk * CHUNK + lax.broadcasted_iota(jnp.int32, (1, CHUNK, 1), 1)
    return jnp.where(r >= n_pad, 1.0, 0.0).astype(F32)


def _split_heads(x, n, w):
    return jnp.stack([x[:, i * w:(i + 1) * w] for i in range(n)], axis=0)


def _merge_heads(x):
    return jnp.concatenate([x[i] for i in range(x.shape[0])], axis=-1)


def _run_beside(job, step, n_steps):
    @pl.when(step == 0)
    def _():
        job.begin()

    if hasattr(job, "pass_on"):
        @pl.when(step == n_steps // 2)
        def _():
            job.pass_on()

    @pl.when(step == n_steps - 1)
    def _():
        job.end()


def _take_cols(slab, first, n):
    lane = lax.broadcasted_iota(jnp.int32, slab.shape, 1)
    return jnp.stack([jnp.sum(jnp.where(lane == first + j, slab, 0.0), axis=1, keepdims=True) for j in range(n)],
                     axis=0)


def _put_cols(cols, first):
    lane = lax.broadcasted_iota(jnp.int32, (cols.shape[1], LANES), 1)
    out = jnp.zeros((cols.shape[1], LANES), F32)
    for j in range(cols.shape[0]):
        out = out + jnp.where(lane == first + j, cols[j], 0.0)
    return out


def gdn_forward(act_qkv, proj, a_log3, dt_bias3, norm_w3, *, n_pad, z_off, small_off, out_cols, gather=()):
    t, w3 = act_qkv.shape
    wdn = w3 // 3
    heads = wdn // DN_DK
    hb = min(HEAD_BLOCK, heads)
    nhb = heads // hb
    nc = t // CHUNK
    bw = hb * DN_DK
    nqb = wdn // bw
    ng = len(gather)

    def body(q_ref, k_ref, v_ref, z_ref, sm_ref, al_ref, db_ref, nw_ref, *rest):
        out_ref, sall_ref, inv_ref = rest[ng:ng + 3]
        st_ref = rest[2 * ng + 3]
        n = pl.program_id(1)
        if ng:
            _run_beside(GatherJob(rest[ng + 3:2 * ng + 3], *rest[2 * ng + 4:]), pl.program_id(0) * nc + n, nhb * nc)

        @pl.when(n == 0)
        def _():
            st_ref[...] = jnp.zeros_like(st_ref)

        state = st_ref[...]
        sall_ref[...] = state
        vm = _valid_rows(n, n_pad)
        head0 = pl.program_id(0) * hb
        new_state, o, tinv = _gdn_step(
            state, _split_heads(q_ref[...], hb, DN_DK), _split_heads(k_ref[...], hb, DN_DK),
            _split_heads(v_ref[...], hb, DN_DK), _split_heads(z_ref[...], hb, DN_DK),
            _take_cols(sm_ref[...], head0, hb), _take_cols(sm_ref[...], heads + head0, hb),
            al_ref[...], db_ref[...], nw_ref[...], vm, want_inv=True)
        st_ref[...] = new_state
        inv_ref[...] = tinv
        out_ref[...] = _merge_heads(o).astype(out_ref.dtype)

    par = pl.BlockSpec((hb, 1, 1), lambda h, n: (h, 0, 0))
    return pl.pallas_call(
        body, name="gdn_fwd",
        grid=(nhb, nc),
        in_specs=[pl.BlockSpec((CHUNK, bw), lambda h, n: (n, h)),
                  pl.BlockSpec((CHUNK, bw), lambda h, n: (n, nqb + h)),
                  pl.BlockSpec((CHUNK, bw), lambda h, n: (n, 2 * nqb + h)),
                  pl.BlockSpec((CHUNK, bw), lambda h, n: (n, z_off // bw + h)),
                  pl.BlockSpec((CHUNK, LANES), lambda h, n: (n, small_off // LANES)),
                  par, par,
                  pl.BlockSpec((1, 1, DN_DK), lambda h, n: (0, 0, 0))] + [ANY] * ng,
        out_specs=[pl.BlockSpec((CHUNK, bw), lambda h, n: (n, h)),
                   pl.BlockSpec((None, hb, DN_DK, DN_DK), lambda h, n: (n, h, 0, 0)),
                   pl.BlockSpec((None, hb, CHUNK, CHUNK), lambda h, n: (n, h, 0, 0))] + [ANY] * ng,
        out_shape=[jax.ShapeDtypeStruct((t, out_cols), BF16),
                   jax.ShapeDtypeStruct((nc, heads, DN_DK, DN_DK), F32),
                   jax.ShapeDtypeStruct((nc, heads, CHUNK, CHUNK), F32)]
        + [jax.ShapeDtypeStruct(s.shape, s.dtype) for s in gather],
        input_output_aliases={8 + a: 3 + a for a in range(ng)},
        scratch_shapes=[pltpu.VMEM((hb, DN_DK, DN_DK), F32)] + (GatherJob.sems(ng) if ng else []),
        compiler_params=_cparams(("arbitrary", "arbitrary")),
    )(act_qkv, act_qkv, act_qkv, proj, proj, a_log3, dt_bias3, norm_w3, *gather)


def gdn_backward(act_qkv, proj, a_log3, dt_bias3, norm_w3, s_all, inv_all, d_mixed, *, n_pad, z_off, small_off):
    t, w3 = act_qkv.shape
    wdn = w3 // 3
    heads = wdn // DN_DK
    hb = min(HEAD_BLOCK, heads)
    nhb = heads // hb
    nc = t // CHUNK
    bw = hb * DN_DK
    nqb = wdn // bw

    def body(q_ref, k_ref, v_ref, z_ref, sm_ref, al_ref, db_ref, nw_ref, s_ref, inv_ref, do_ref,
             dq_ref, dk_ref, dv_ref, dz_ref, dsm_ref, dal_ref, ddb_ref, dnw_ref, ds_ref):
        i = pl.program_id(1)
        n = nc - 1 - i
        head0 = pl.program_id(0) * hb

        @pl.when(i == 0)
        def _():
            ds_ref[...] = jnp.zeros_like(ds_ref)
            dal_ref[...] = jnp.zeros_like(dal_ref)
            ddb_ref[...] = jnp.zeros_like(ddb_ref)
            dnw_ref[...] = jnp.zeros_like(dnw_ref)

        vm = _valid_rows(n, n_pad)
        step = functools.partial(_gdn_step, vm=vm, inv=inv_ref[...])
        _, vjp = jax.vjp(step, s_ref[...], _split_heads(q_ref[...], hb, DN_DK), _split_heads(k_ref[...], hb, DN_DK),
                         _split_heads(v_ref[...], hb, DN_DK), _split_heads(z_ref[...], hb, DN_DK),
                         _take_cols(sm_ref[...], head0, hb), _take_cols(sm_ref[...], heads + head0, hb),
                         al_ref[...], db_ref[...], nw_ref[...])
        ds, dq, dk, dv, dz, dbr, dar, dal, ddb, dnw = vjp((ds_ref[...], _split_heads(do_ref[...], hb, DN_DK)))
        ds_ref[...] = ds
        dq_ref[...] = _merge_heads(dq)
        dk_ref[...] = _merge_heads(dk)
        dv_ref[...] = _merge_heads(dv)
        dz_ref[...] = _merge_heads(dz).astype(dz_ref.dtype)
        dsm_ref[...] = _put_cols(dbr, head0) + _put_cols(dar, heads + head0)
        dal_ref[...] += dal
        ddb_ref[...] += ddb
        dnw_ref[...] += dnw[0]

    rev = lambda n: nc - 1 - n
    par = pl.BlockSpec((hb, 1, 1), lambda h, n: (h, 0, 0))
    blk = lambda off: pl.BlockSpec((CHUNK, bw), lambda h, n: (rev(n), off + h))
    return pl.pallas_call(
        body, name="gdn_bwd",
        grid=(nhb, nc),
        in_specs=[blk(0), blk(nqb), blk(2 * nqb), blk(z_off // bw),
                  pl.BlockSpec((CHUNK, LANES), lambda h, n: (rev(n), small_off // LANES)), par, par,
                  pl.BlockSpec((1, 1, DN_DK), lambda h, n: (0, 0, 0)),
                  pl.BlockSpec((None, hb, DN_DK, DN_DK), lambda h, n: (rev(n), h, 0, 0)),
                  pl.BlockSpec((None, hb, CHUNK, CHUNK), lambda h, n: (rev(n), h, 0, 0)),
                  blk(0)],
        out_specs=[blk(0), blk(0), blk(0), blk(z_off // bw),
                   pl.BlockSpec((None, CHUNK, LANES), lambda h, n: (h, rev(n), 0)), par, par,
                   pl.BlockSpec((None, 1, DN_DK), lambda h, n: (h, 0, 0))],
        out_shape=[jax.ShapeDtypeStruct((t, wdn), F32)] * 3
        + [jax.ShapeDtypeStruct((t, proj.shape[1]), BF16),
           jax.ShapeDtypeStruct((nhb, t, LANES), F32),
           jax.ShapeDtypeStruct((heads, 1, 1), F32), jax.ShapeDtypeStruct((heads, 1, 1), F32),
           jax.ShapeDtypeStruct((nhb, 1, DN_DK), F32)],
        scratch_shapes=[pltpu.VMEM((hb, DN_DK, DN_DK), F32)],
        compiler_params=_cparams(("arbitrary", "arbitrary")),
    )(act_qkv, act_qkv, act_qkv, proj, proj, a_log3, dt_bias3, norm_w3, s_all, inv_all, d_mixed)


def _dot2(a, b, ca, cb):
    return lax.dot_general(a.astype(BF16), b.astype(BF16), (((ca,), (cb,)), ((), ())),
                           preferred_element_type=F32)


def _ssd_step(state, xa, bmat, cmat, z, dtraw, a_log, dt_bias, dskip, norm_w, vm):
    causal, _, eye = _tri_masks()
    r_heads, p, n_state = state.shape
    gw = r_heads * p
    vm2 = vm[0]
    xa, bmat, cmat = xa * vm2, bmat * vm2, cmat * vm2
    dt = _softplus(dtraw + dt_bias) * vm
    a = dt * (-jnp.exp(a_log))
    acs = _cumsum_col(a, causal, eye)
    arow = _col2row(acs, eye)
    lmat = jnp.where(causal, jnp.exp(jnp.where(causal, acs - arow, 0.0)), 0.0)
    hsel = (lax.broadcasted_iota(jnp.int32, (r_heads, 1, gw), 2) // p
            == lax.broadcasted_iota(jnp.int32, (r_heads, 1, gw), 0))

    def spread(col):
        return jnp.sum(jnp.where(hsel, col, 0.0), axis=0)

    xdt = xa * spread(dt)
    cb = _dot2(cmat, bmat, 1, 1)
    m = (cb[None] * lmat).reshape(r_heads * CHUNK, CHUNK)
    yb = _dot2(m, xdt, 1, 0).reshape(r_heads, CHUNK, gw)
    y_diag = jnp.sum(jnp.where(hsel, yb, 0.0), axis=0)
    s2 = state.reshape(gw, n_state)
    y_off = _dot2(cmat, s2, 1, 1) * spread(jnp.exp(acs))
    a_last = jnp.sum(a, axis=1, keepdims=True)
    upd = _dot2(xdt * spread(jnp.exp(a_last - acs)), bmat, 0, 0)
    new_state = state * jnp.exp(a_last) + upd.reshape(r_heads, p, n_state)
    y = y_diag + y_off + xa * spread(dskip)
    y = y * _silu(z)
    y = y * lax.rsqrt(jnp.mean(y * y, -1, keepdims=True) + NORM_EPS) * norm_w
    return new_state, y


def _ssd_dims(act_xbc):
    t, wx = act_xbc.shape
    wm = wx - 2 * M2_GROUPS * M2_N
    gw = wm // M2_GROUPS
    return t, wm, gw, gw // M2_P, wm // M2_P, t // CHUNK


def ssd_forward(act_xbc, proj, a_log3, dt_bias3, dskip3, norm_w, mixed, *, n_pad, z_off, small_off, dt_lane,
                gather=()):
    t, wm, gw, rh, heads, nc = _ssd_dims(act_xbc)
    nb = wm // M2_N
    ob = (mixed.shape[1] - wm) // gw
    ng = len(gather)

    def body(x_ref, b_ref, c_ref, z_ref, dt_ref, al_ref, db_ref, dk_ref, nw_ref, _, *rest):
        out_ref, sall_ref = rest[ng:ng + 2]
        st_ref = rest[2 * ng + 2]
        n = pl.program_id(1)
        if ng:
            _run_beside(GatherJob(rest[ng + 2:2 * ng + 2], *rest[2 * ng + 3:]), pl.program_id(0) * nc + n,
                        M2_GROUPS * nc)

        @pl.when(n == 0)
        def _():
            st_ref[...] = jnp.zeros_like(st_ref)

        state = st_ref[...]
        sall_ref[...] = state
        dtraw = _take_cols(dt_ref[...], dt_lane + pl.program_id(0) * rh, rh)
        new_state, y = _ssd_step(state, x_ref[...], b_ref[...], c_ref[...], z_ref[...], dtraw,
                                 al_ref[...], db_ref[...], dk_ref[...], nw_ref[...], _valid_rows(n, n_pad))
        st_ref[...] = new_state
        out_ref[...] = y.astype(out_ref.dtype)

    par = pl.BlockSpec((rh, 1, 1), lambda g, n: (g, 0, 0))
    return pl.pallas_call(
        body, name="ssd_fwd",
        grid=(M2_GROUPS, nc),
        in_specs=[pl.BlockSpec((CHUNK, gw), lambda g, n: (n, g)),
                  pl.BlockSpec((CHUNK, M2_N), lambda g, n: (n, nb + g)),
                  pl.BlockSpec((CHUNK, M2_N), lambda g, n: (n, nb + M2_GROUPS + g)),
                  pl.BlockSpec((CHUNK, gw), lambda g, n: (n, z_off // gw + g)),
                  pl.BlockSpec((CHUNK, LANES), lambda g, n: (n, small_off // LANES)),
                  par, par, par,
                  pl.BlockSpec((1, gw), lambda g, n: (0, g)),
                  pl.BlockSpec(memory_space=pl.ANY)] + [ANY] * ng,
        out_specs=[pl.BlockSpec((CHUNK, gw), lambda g, n: (n, ob + g)),
                   pl.BlockSpec((None, rh, M2_P, M2_N), lambda g, n: (n, g, 0, 0))] + [ANY] * ng,
        out_shape=[jax.ShapeDtypeStruct(mixed.shape, mixed.dtype),
                   jax.ShapeDtypeStruct((nc, heads, M2_P, M2_N), F32)]
        + [jax.ShapeDtypeStruct(s.shape, s.dtype) for s in gather],
        input_output_aliases={9: 0, **{10 + a: 2 + a for a in range(ng)}},
        scratch_shapes=[pltpu.VMEM((rh, M2_P, M2_N), F32)] + (GatherJob.sems(ng) if ng else []),
        compiler_params=_cparams(("arbitrary", "arbitrary")),
    )(act_xbc, act_xbc, act_xbc, proj, proj, a_log3, dt_bias3, dskip3, norm_w, mixed, *gather)


def ssd_backward(act_xbc, proj, a_log3, dt_bias3, dskip3, norm_w, s_all, d_mixed, dproj, *, n_pad, z_off,
                 small_off, dt_lane, exchange=()):
    t, wm, gw, rh, heads, nc = _ssd_dims(act_xbc)
    nb = wm // M2_N
    ne = len(exchange)

    def body(x_ref, b_ref, c_ref, z_ref, dt_ref, al_ref, db_ref, dk_ref, nw_ref, s_ref, dy_ref, _, *rest):
        dx_ref, dbm_ref, dcm_ref, dz_ref, ddt_ref, dal_ref, ddb_ref, ddk_ref, dnw_ref = rest[ne:ne + 9]
        ds_ref = rest[2 * ne + 9]
        i = pl.program_id(1)
        n = nc - 1 - i
        if ne:
            _run_beside(ChipExchangeJob(rest[:ne], rest[ne + 9:2 * ne + 9], *rest[2 * ne + 10:]),
                        pl.program_id(0) * nc + i, M2_GROUPS * nc)

        @pl.when(i == 0)
        def _():
            ds_ref[...] = jnp.zeros_like(ds_ref)
            dal_ref[...] = jnp.zeros_like(dal_ref)
            ddb_ref[...] = jnp.zeros_like(ddb_ref)
            ddk_ref[...] = jnp.zeros_like(ddk_ref)
            dnw_ref[...] = jnp.zeros_like(dnw_ref)

        step = functools.partial(_ssd_step, vm=_valid_rows(n, n_pad))
        lane0 = dt_lane + pl.program_id(0) * rh
        _, vjp = jax.vjp(step, s_ref[...], x_ref[...], b_ref[...], c_ref[...], z_ref[...],
                         _take_cols(dt_ref[...], lane0, rh), al_ref[...], db_ref[...], dk_ref[...], nw_ref[...])
        ds, dx, dbm, dcm, dz, ddt, dal, ddb, ddk, dnw = vjp((ds_ref[...], dy_ref[...]))
        ds_ref[...] = ds
        dx_ref[...] = dx
        dbm_ref[...] = dbm
        dcm_ref[...] = dcm
        dz_ref[...] = dz.astype(dz_ref.dtype)
        ddt_ref[...] = _put_cols(ddt, lane0)
        dal_ref[...] += dal
        ddb_ref[...] += ddb
        ddk_ref[...] += ddk
        dnw_ref[...] += dnw

    rev = lambda n: nc - 1 - n
    par = pl.BlockSpec((rh, 1, 1), lambda g, n: (g, 0, 0))
    wide = lambda off: pl.BlockSpec((CHUNK, gw), lambda g, n: (rev(n), off + g))
    narrow = lambda off: pl.BlockSpec((CHUNK, M2_N), lambda g, n: (rev(n), off + g))
    col = pl.BlockSpec((None, CHUNK, LANES), lambda g, n: (g, rev(n), 0))
    gn = M2_GROUPS * M2_N
    return pl.pallas_call(
        body, name="ssd_bwd",
        grid=(M2_GROUPS, nc),
        in_specs=[wide(0), narrow(nb), narrow(nb + M2_GROUPS), wide(z_off // gw),
                  pl.BlockSpec((CHUNK, LANES), lambda g, n: (rev(n), small_off // LANES)), par, par, par,
                  pl.BlockSpec((1, gw), lambda g, n: (0, g)),
                  pl.BlockSpec((None, rh, M2_P, M2_N), lambda g, n: (rev(n), g, 0, 0)),
                  wide(M2_GROUPS), pl.BlockSpec(memory_space=pl.ANY)] + [ANY] * ne,
        out_specs=[wide(0), narrow(0), narrow(0), wide(z_off // gw), col, par, par, par,
                   pl.BlockSpec((1, gw), lambda g, n: (0, g))] + [ANY] * ne,
        out_shape=[jax.ShapeDtypeStruct((t, wm), F32), jax.ShapeDtypeStruct((t, gn), F32),
                   jax.ShapeDtypeStruct((t, gn), F32), jax.ShapeDtypeStruct(dproj.shape, dproj.dtype),
                   jax.ShapeDtypeStruct((M2_GROUPS, t, LANES), F32),
                   jax.ShapeDtypeStruct((heads, 1, 1), F32), jax.ShapeDtypeStruct((heads, 1, 1), F32),
                   jax.ShapeDtypeStruct((heads, 1, 1), F32), jax.ShapeDtypeStruct((1, wm), F32)]
        + [jax.ShapeDtypeStruct((3,) + s.shape[1:], s.dtype) for s in exchange],
        input_output_aliases={11: 3},
        scratch_shapes=[pltpu.VMEM((rh, M2_P, M2_N), F32)] + (ChipExchangeJob.sems(ne) if ne else []),
        compiler_params=_cparams(("arbitrary", "arbitrary")),
    )(act_xbc, act_xbc, act_xbc, proj, proj, a_log3, dt_bias3, dskip3, norm_w, s_all, d_mixed, dproj, *exchange)


SUBLANES = 8
LANES = 128


def _pick(dim, target, align):
    best = None
    for d in range(align, min(dim, target) + 1, align):
        if dim % d == 0:
            best = d
    return dim if best is None else best


def _row_tile(t):
    return _pick(t, 512, 16)


def matmul(a, b, *, mode, out_dtype, name, tm=1056, tn=512, tk=2048, residual=None, out_shards=1,
           a_shards=1, b_shards=1, exchange=()):
    if mode == "tn":
        kd, m = a.shape
        n = b.shape[-1] * b_shards
    else:
        m, kd = a.shape[-2], a.shape[-1] * a_shards
        n = b.shape[1] if mode == "nn" else b.shape[0]
    tm = _pick(m, tm, LANES if mode == "tn" else 16)
    ks = kd // a_shards
    tk = _pick(ks, tk, LANES)
    nkb = ks // tk
    nk = kd // tk
    ns_o, ns_b = n // out_shards, n // b_shards
    tn = _pick(math.gcd(ns_o, ns_b), tn, LANES)
    npb_o, npb_b = ns_o // tn, ns_b // tn
    if mode == "tn":
        a_spec = pl.BlockSpec((tk, tm), lambda i, j, k: (k, i))
    elif a_shards == 1:
        a_spec = pl.BlockSpec((tm, tk), lambda i, j, k: (i, k))
    else:
        a_spec = pl.BlockSpec((None, tm, tk), lambda i, j, k: (k // nkb, i, k % nkb))
    contract = ((1,), (1,)) if mode == "nt" else ((1,), (0,))
    if mode == "nt":
        b_spec = pl.BlockSpec((tn, tk), lambda i, j, k: (j, k))
    elif b_shards == 1:
        b_spec = pl.BlockSpec((tk, tn), lambda i, j, k: (k, j))
    else:
        b_spec = pl.BlockSpec((None, tk, tn), lambda i, j, k: (j // npb_b, k, j % npb_b))
    has_res = residual is not None
    ne = len(exchange)
    grid = (m // tm, n // tn, nk)

    def body(*refs):
        refs = list(refs)
        a_ref, b_ref = refs[:2]
        del refs[:2]
        r_ref = refs.pop(0) if has_res else None
        ex_in = [refs.pop(0) for _ in range(ne)]
        o_ref = refs.pop(0)
        ex_out = [refs.pop(0) for _ in range(ne)]
        at_ref = refs.pop(0) if mode == "tn" else None
        acc_ref = refs.pop(0) if nk > 1 else None
        k = pl.program_id(2)
        if ne:
            step = (pl.program_id(0) * grid[1] + pl.program_id(1)) * nk + k
            _run_beside(ChipExchangeJob(ex_in, ex_out, *refs), step, grid[0] * grid[1] * nk)
        if mode == "tn":
            @pl.when(pl.program_id(1) == 0)
            def _():
                at_ref[k] = jnp.transpose(a_ref[...].astype(F32)).astype(BF16)

            lhs = at_ref[k]
        else:
            lhs = a_ref[...].astype(BF16)
        part = lax.dot_general(lhs, b_ref[...].astype(BF16), (contract, ((), ())), preferred_element_type=F32)

        def finish(total):
            if has_res:
                total = total + r_ref[...]
            o_ref[...] = total.astype(o_ref.dtype)

        if nk == 1:
            finish(part)
        else:
            @pl.when(k == 0)
            def _():
                acc_ref[...] = part

            @pl.when((k > 0) & (k < nk - 1))
            def _():
                acc_ref[...] += part

            @pl.when(k == nk - 1)
            def _():
                finish(acc_ref[...] + part)

    in_specs = [a_spec, b_spec]
    args = [a, b]
    if has_res:
        in_specs.append(pl.BlockSpec((tm, tn), lambda i, j, k: (i, j)))
        args.append(residual)
    if out_shards == 1:
        out_spec = pl.BlockSpec((tm, tn), lambda i, j, k: (i, j))
        out_shape = jax.ShapeDtypeStruct((m, n), out_dtype)
    else:
        out_spec = pl.BlockSpec((None, tm, tn), lambda i, j, k: (j // npb_o, i, j % npb_o))
        out_shape = jax.ShapeDtypeStruct((out_shards, m, ns_o), out_dtype)
    scratch = [pltpu.VMEM((nk, tm, tk), BF16)] if mode == "tn" else []
    if nk > 1:
        scratch.append(pltpu.VMEM((tm, tn), F32))
    if not ne:
        return pl.pallas_call(
            body, name=name, grid=grid,
            in_specs=in_specs, out_specs=out_spec, out_shape=out_shape, scratch_shapes=scratch,
            compiler_params=_cparams(("parallel", "arbitrary", "arbitrary")),
        )(*args)
    return pl.pallas_call(
        body, name=name, grid=grid,
        in_specs=in_specs + [ANY] * ne, out_specs=[out_spec] + [ANY] * ne,
        out_shape=[out_shape] + [jax.ShapeDtypeStruct((3,) + s.shape[1:], s.dtype) for s in exchange],
        scratch_shapes=scratch + ChipExchangeJob.sems(ne),
        compiler_params=_cparams(("arbitrary", "arbitrary", "arbitrary")),
    )(*args, *exchange)


def rmsnorm_forward(x, w, *, name):
    t, d = x.shape
    tm = _row_tile(t)

    def body(x_ref, w_ref, o_ref):
        xv = x_ref[...]
        r = lax.rsqrt(jnp.mean(xv * xv, -1, keepdims=True) + NORM_EPS)
        o_ref[...] = (xv * r * w_ref[...]).astype(o_ref.dtype)

    return pl.pallas_call(
        body, name=name, grid=(t // tm,),
        in_specs=[pl.BlockSpec((tm, d), lambda i: (i, 0)), pl.BlockSpec((1, d), lambda i: (0, 0))],
        out_specs=pl.BlockSpec((tm, d), lambda i: (i, 0)),
        out_shape=jax.ShapeDtypeStruct((t, d), BF16),
        compiler_params=_cparams(("parallel",)),
    )(x, w)


def _rmsnorm_grads(xv, wv, dy):
    r = lax.rsqrt(jnp.mean(xv * xv, -1, keepdims=True) + NORM_EPS)
    xh = xv * r
    g = dy * wv
    dx = r * (g - xh * jnp.mean(g * xh, -1, keepdims=True))
    return dx, jnp.sum(dy * xh, axis=0, keepdims=True)


def rmsnorm_backward(x, w, dy, dres, *, n_pad, name):
    t, d = x.shape
    tm = _row_tile(t)

    def body(x_ref, w_ref, dy_ref, dr_ref, dx_ref, dx16_ref, dw_ref):
        i = pl.program_id(0)

        @pl.when(i == 0)
        def _():
            dw_ref[...] = jnp.zeros_like(dw_ref)

        dx, dw = _rmsnorm_grads(x_ref[...], w_ref[...], dy_ref[...])
        rows = i * tm + lax.broadcasted_iota(jnp.int32, (tm, 1), 0)
        dx = jnp.where(rows >= n_pad, dx + dr_ref[...], 0.0)
        dx_ref[...] = dx
        dx16_ref[...] = dx.astype(BF16)
        dw_ref[...] += dw

    row = pl.BlockSpec((tm, d), lambda i: (i, 0))
    vec = pl.BlockSpec((1, d), lambda i: (0, 0))
    return pl.pallas_call(
        body, name=name, grid=(t // tm,),
        in_specs=[row, vec, row, row], out_specs=[row, row, vec],
        out_shape=[jax.ShapeDtypeStruct((t, d), F32), jax.ShapeDtypeStruct((t, d), BF16),
                   jax.ShapeDtypeStruct((1, d), F32)],
        compiler_params=_cparams(("arbitrary",)),
    )(x, w, dy, dres)


def loss_head(h, w, target, *, n_skip):
    t, d = h.shape
    tm = _row_tile(t)

    def body(x_ref, w_ref, y_ref, loss_ref, dx_ref, dx16_ref, dw_ref):
        i = pl.program_id(0)

        @pl.when(i == 0)
        def _():
            dw_ref[...] = jnp.zeros_like(dw_ref)
            loss_ref[...] = jnp.zeros_like(loss_ref)

        xv, wv = x_ref[...], w_ref[...]
        r = lax.rsqrt(jnp.mean(xv * xv, -1, keepdims=True) + NORM_EPS)
        rows = i * tm + lax.broadcasted_iota(jnp.int32, (tm, 1), 0)
        err = jnp.where(rows >= n_skip, xv * r * wv - y_ref[...], 0.0)
        loss_ref[...] += 0.5 * jnp.sum(jnp.mean(err * err, -1, keepdims=True))
        dx, dw = _rmsnorm_grads(xv, wv, err * (1.0 / d))
        dx_ref[...] = dx
        dx16_ref[...] = dx.astype(BF16)
        dw_ref[...] += dw

    row = pl.BlockSpec((tm, d), lambda i: (i, 0))
    vec = pl.BlockSpec((1, d), lambda i: (0, 0))
    return pl.pallas_call(
        body, name="loss_head", grid=(t // tm,),
        in_specs=[row, vec, row],
        out_specs=[pl.BlockSpec((1, LANES), lambda i: (0, 0)), row, row, vec],
        out_shape=[jax.ShapeDtypeStruct((1, LANES), F32), jax.ShapeDtypeStruct((t, d), F32),
                   jax.ShapeDtypeStruct((t, d), BF16), jax.ShapeDtypeStruct((1, d), F32)],
        compiler_params=_cparams(("arbitrary",)),
    )(h, w, target)


HALO = SUBLANES


def _conv_rows(scr_ref, w, first, rows):
    kk = w.shape[0]
    acc = w[0:1, :] * scr_ref[pl.ds(first, rows), :]
    for j in range(1, kk):
        acc = acc + w[j:j + 1, :] * scr_ref[pl.ds(first + j, rows), :]
    return acc


def _dsilu(p):
    s = _sigmoid(p)
    return s * (1.0 + p * (1.0 - s))


def conv_silu(x, w, b, *, x_off, name):
    t = x.shape[0]
    kk, width = w.shape
    tm = _row_tile(t)
    tc = _pick(width, 512, LANES)
    ob, nh = x_off // tc, tm // HALO

    def body(prev_ref, x_ref, w_ref, b_ref, o_ref, scr):
        i = pl.program_id(1)
        scr[0:HALO, :] = jnp.where(i > 0, prev_ref[...], 0.0)
        scr[HALO:HALO + tm, :] = x_ref[...]
        pre = _conv_rows(scr, w_ref[...], HALO - (kk - 1), tm) + b_ref[...]
        o_ref[...] = _silu(pre)

    return pl.pallas_call(
        body, name=name, grid=(width // tc, t // tm),
        in_specs=[pl.BlockSpec((HALO, tc), lambda j, i: (jnp.maximum(i * nh - 1, 0), ob + j)),
                  pl.BlockSpec((tm, tc), lambda j, i: (i, ob + j)),
                  pl.BlockSpec((kk, tc), lambda j, i: (0, j)),
                  pl.BlockSpec((1, tc), lambda j, i: (0, j))],
        out_specs=pl.BlockSpec((tm, tc), lambda j, i: (i, j)),
        out_shape=jax.ShapeDtypeStruct((t, width), F32),
        scratch_shapes=[pltpu.VMEM((tm + HALO, tc), F32)],
        compiler_params=_cparams(("parallel", "arbitrary")),
    )(x, x, w, b)


def conv_silu_backward(x, w, b, dact, dst, *, x_off, name):
    t = x.shape[0]
    kk, width = w.shape
    tm = _row_tile(t)
    tc = _pick(width, 512, LANES)
    ob, nh, nt = x_off // tc, tm // HALO, t // tm
    last_h = t // HALO - 1

    def body(prev_ref, x_ref, next_ref, w_ref, b_ref, d_ref, dnext_ref, _, dx_ref, dw_ref, db_ref, scr_x, scr_d):
        i = pl.program_id(1)

        @pl.when(i == 0)
        def _():
            dw_ref[...] = jnp.zeros_like(dw_ref)
            db_ref[...] = jnp.zeros_like(db_ref)

        wv = w_ref[...]
        scr_x[0:HALO, :] = jnp.where(i > 0, prev_ref[...], 0.0)
        scr_x[HALO:HALO + tm, :] = x_ref[...]
        scr_x[HALO + tm:, :] = next_ref[...]
        ext = tm + HALO
        pre = _conv_rows(scr_x, wv, HALO - (kk - 1), ext) + b_ref[...]
        scr_d[0:tm, :] = d_ref[...]
        scr_d[tm:, :] = jnp.where(i < nt - 1, dnext_ref[...], 0.0)
        scr_d[...] = scr_d[...] * _dsilu(pre)
        dx = wv[0:1, :] * scr_d[pl.ds(kk - 1, tm), :]
        for j in range(1, kk):
            dx = dx + wv[j:j + 1, :] * scr_d[pl.ds(kk - 1 - j, tm), :]
        dx_ref[...] = dx.astype(dx_ref.dtype)
        dpre = scr_d[0:tm, :]
        for j in range(kk):
            dw_ref[j:j + 1, :] += jnp.sum(dpre * scr_x[pl.ds(HALO - (kk - 1) + j, tm), :], axis=0, keepdims=True)
        db_ref[0:1, :] += jnp.sum(dpre, axis=0, keepdims=True)

    nxt = lambda j, i: (jnp.minimum((i + 1) * nh, last_h), j)
    acc = pl.BlockSpec((SUBLANES, tc), lambda j, i: (0, j))
    return pl.pallas_call(
        body, name=name, grid=(width // tc, nt),
        in_specs=[pl.BlockSpec((HALO, tc), lambda j, i: (jnp.maximum(i * nh - 1, 0), ob + j)),
                  pl.BlockSpec((tm, tc), lambda j, i: (i, ob + j)),
                  pl.BlockSpec((HALO, tc), lambda j, i: (jnp.minimum((i + 1) * nh, last_h), ob + j)),
                  pl.BlockSpec((kk, tc), lambda j, i: (0, j)),
                  pl.BlockSpec((1, tc), lambda j, i: (0, j)),
                  pl.BlockSpec((tm, tc), lambda j, i: (i, j)),
                  pl.BlockSpec((HALO, tc), nxt),
                  pl.BlockSpec(memory_space=pl.ANY)],
        out_specs=[pl.BlockSpec((tm, tc), lambda j, i: (i, ob + j)), acc, acc],
        out_shape=[jax.ShapeDtypeStruct(dst.shape, dst.dtype), jax.ShapeDtypeStruct((SUBLANES, width), F32),
                   jax.ShapeDtypeStruct((SUBLANES, width), F32)],
        input_output_aliases={7: 0},
        scratch_shapes=[pltpu.VMEM((tm + 2 * HALO, tc), F32), pltpu.VMEM((tm + HALO, tc), F32)],
        compiler_params=_cparams(("parallel", "arbitrary")),
    )(x, x, x, w, b, dact, dact, dst)


def conv_glu(u, w, *, name):
    _, t, f = u.shape
    kk = w.shape[1]
    tm = _row_tile(t)
    tc = _pick(f, 512, LANES)
    nh = tm // HALO

    def body(prev_ref, x_ref, w_ref, o_ref, scr):
        i = pl.program_id(1)
        scr[:, 0:HALO, :] = jnp.where(i > 0, prev_ref[...], 0.0)
        scr[:, HALO:, :] = x_ref[...]
        gate = _conv_rows(scr.at[0], w_ref[0], HALO - (kk - 1), tm)
        val = _conv_rows(scr.at[1], w_ref[1], HALO - (kk - 1), tm)
        o_ref[...] = (_silu(gate) * val).astype(o_ref.dtype)

    return pl.pallas_call(
        body, name=name, grid=(f // tc, t // tm),
        in_specs=[pl.BlockSpec((2, HALO, tc), lambda j, i: (0, jnp.maximum(i * nh - 1, 0), j)),
                  pl.BlockSpec((2, tm, tc), lambda j, i: (0, i, j)),
                  pl.BlockSpec((2, kk, tc), lambda j, i: (0, 0, j))],
        out_specs=pl.BlockSpec((tm, tc), lambda j, i: (i, j)),
        out_shape=jax.ShapeDtypeStruct((t, f), BF16),
        scratch_shapes=[pltpu.VMEM((2, tm + HALO, tc), F32)],
        compiler_params=_cparams(("parallel", "arbitrary")),
    )(u, u, w)


def conv_glu_backward(u, w, dact, *, name):
    _, t, f = u.shape
    kk = w.shape[1]
    tm = _row_tile(t)
    tc = _pick(f, 512, LANES)
    nh, nt = tm // HALO, t // tm
    last_h = t // HALO - 1
    ext = tm + HALO
    first = HALO - (kk - 1)

    def body(prev_ref, x_ref, next_ref, w_ref, d_ref, dn_ref, du_ref, dw_ref, scr_x, scr_d):
        i = pl.program_id(1)

        @pl.when(i == 0)
        def _():
            dw_ref[...] = jnp.zeros_like(dw_ref)

        scr_x[:, 0:HALO, :] = jnp.where(i > 0, prev_ref[...], 0.0)
        scr_x[:, HALO:HALO + tm, :] = x_ref[...]
        scr_x[:, HALO + tm:, :] = next_ref[...]
        gate = _conv_rows(scr_x.at[0], w_ref[0], first, ext)
        val = _conv_rows(scr_x.at[1], w_ref[1], first, ext)
        scr_d[0, 0:tm, :] = d_ref[...]
        scr_d[0, tm:, :] = jnp.where(i < nt - 1, dn_ref[...], 0.0)
        dact = scr_d[0]
        s = _sigmoid(gate)
        scr_d[0] = dact * val * (s * (1.0 + gate * (1.0 - s)))
        scr_d[1] = dact * (gate * s)
        for h in range(2):
            wv = w_ref[h]
            du = wv[0:1, :] * scr_d[h, pl.ds(kk - 1, tm), :]
            for j in range(1, kk):
                du = du + wv[j:j + 1, :] * scr_d[h, pl.ds(kk - 1 - j, tm), :]
            du_ref[h] = du.astype(du_ref.dtype)
            dc = scr_d[h, 0:tm, :]
            for j in range(kk):
                dw_ref[h, j:j + 1, :] += jnp.sum(dc * scr_x[h, pl.ds(first + j, tm), :], axis=0, keepdims=True)

    return pl.pallas_call(
        body, name=name, grid=(f // tc, nt),
        in_specs=[pl.BlockSpec((2, HALO, tc), lambda j, i: (0, jnp.maximum(i * nh - 1, 0), j)),
                  pl.BlockSpec((2, tm, tc), lambda j, i: (0, i, j)),
                  pl.BlockSpec((2, HALO, tc), lambda j, i: (0, jnp.minimum((i + 1) * nh, last_h), j)),
                  pl.BlockSpec((2, kk, tc), lambda j, i: (0, 0, j)),
                  pl.BlockSpec((tm, tc), lambda j, i: (i, j)),
                  pl.BlockSpec((HALO, tc), lambda j, i: (jnp.minimum((i + 1) * nh, last_h), j))],
        out_specs=[pl.BlockSpec((2, tm, tc), lambda j, i: (0, i, j)),
                   pl.BlockSpec((2, SUBLANES, tc), lambda j, i: (0, 0, j))],
        out_shape=[jax.ShapeDtypeStruct((2, t, f), BF16), jax.ShapeDtypeStruct((2, SUBLANES, f), F32)],
        scratch_shapes=[pltpu.VMEM((2, tm + 2 * HALO, tc), F32), pltpu.VMEM((2, ext, tc), F32)],
        compiler_params=_cparams(("parallel", "arbitrary")),
    )(u, u, u, w, dact, dact)


def adamw(w, g, m, v, *, name):
    r, c = w.shape
    tr = _pick(r, max(SUBLANES, (2 * 1024 * 1024) // (4 * c) // SUBLANES * SUBLANES), SUBLANES)

    def body(w_ref, g_ref, m_ref, v_ref, d_ref, nm_ref, nv_ref):
        gv = g_ref[...]
        nm = ADAM_B1 * m_ref[...] + (1.0 - ADAM_B1) * gv
        nv = ADAM_B2 * v_ref[...] + (1.0 - ADAM_B2) * (gv * gv)
        m_hat = nm / (1.0 - ADAM_B1 ** ADAM_STEP)
        v_hat = nv / (1.0 - ADAM_B2 ** ADAM_STEP)
        d_ref[...] = -ADAM_LR * (m_hat / (jnp.sqrt(v_hat) + ADAM_EPS) + ADAM_WD * w_ref[...])
        nm_ref[...] = nm
        nv_ref[...] = nv

    blk = pl.BlockSpec((tr, c), lambda i: (i, 0))
    shp = jax.ShapeDtypeStruct((r, c), F32)
    return pl.pallas_call(
        body, name=name, grid=(r // tr,),
        in_specs=[blk] * 4, out_specs=[blk] * 3, out_shape=[shp] * 3,
        compiler_params=_cparams(("parallel",)),
    )(w, g, m, v)


def cast_into_slot(x, slot, n_slots, *, name):
    r, c = x.shape
    tr = _pick(r, max(16, (2 * 1024 * 1024) // (4 * c) // 16 * 16), 16)

    def body(s_ref, x_ref, o_ref):
        o_ref[...] = x_ref[...].astype(o_ref.dtype)

    grid_spec = pltpu.PrefetchScalarGridSpec(
        num_scalar_prefetch=1, grid=(r // tr,),
        in_specs=[pl.BlockSpec((tr, c), lambda i, s: (i, 0))],
        out_specs=pl.BlockSpec((None, tr, c), lambda i, s: (s[0], i, 0)))
    return pl.pallas_call(
        body, name=name, grid_spec=grid_spec,
        out_shape=jax.ShapeDtypeStruct((n_slots, r, c), BF16),
        compiler_params=_cparams(("arbitrary",)),
    )(jnp.reshape(slot, (1,)).astype(jnp.int32), x)


class Layout:
    def __init__(self, d):
        self.d = d
        self.h_dn = d // DN_DK
        self.h_m2 = d // M2_P
        self.gn = M2_GROUPS * M2_N
        self.w_xbc = d + 2 * self.gn
        self.z_off = 3 * d
        self.m2z_off = 4 * d
        self.xbc_off = 5 * d
        self.small_off = 5 * d + self.w_xbc
        self.n_small = 2 * self.h_dn + self.h_m2
        self.p = self.small_off + LANES
        self.p_orig = self.small_off + self.n_small

    def permute_w_in(self, w):
        d, s2 = self.d, 2 * self.h_dn
        pad = jnp.zeros((w.shape[0], LANES - self.n_small), w.dtype)
        return jnp.concatenate([w[:, :4 * d], w[:, 4 * d + s2:4 * d + s2 + d + self.w_xbc],
                                w[:, 4 * d:4 * d + s2], w[:, self.p_orig - self.h_m2:], pad], axis=1)

    def unpermute_w_in(self, w):
        d, s2 = self.d, 2 * self.h_dn
        so = self.small_off
        return jnp.concatenate([w[:, :4 * d], w[:, so:so + s2], w[:, 4 * d:so], w[:, so + s2:so + self.n_small]], axis=1)


def _cols(small, lo, hi):
    return jnp.transpose(small[:, lo:hi])[..., None]


def local_step(h0, target, wts, slabs, chip, core, *, n_pad, n_meta):
    t, d = h0.shape
    lay = Layout(d)
    hd, hm = lay.h_dn, lay.h_m2
    zeros_b = jnp.zeros((1, 3 * d), F32)
    r3 = lambda v, n: v.reshape(n, 1, 1)
    dn_al, dn_db = r3(wts["dn_a_log"], hd), r3(wts["dn_dt_bias"], hd)
    dn_nw = wts["dn_norm_w"].reshape(1, 1, DN_DK)
    m2_al, m2_db, m2_dk = r3(wts["m2_a_log"], hm), r3(wts["m2_dt_bias"], hm), r3(wts["m2_d"], hm)

    hn1 = rmsnorm_forward(h0, wts["norm_mix_w"], name="norm_mix")
    proj = matmul(hn1, wts["w_in"], mode="nn", out_dtype=F32, name="in_proj", tn=640)
    act_qkv = conv_silu(proj, wts["dn_conv_w"], zeros_b, x_off=0, name="dn_conv")
    act_xbc = conv_silu(proj, wts["m2_conv_w"], wts["m2_conv_b"], x_off=lay.xbc_off, name="m2_conv")
    small_at = dict(small_off=lay.small_off)
    dt_at = dict(small_off=lay.small_off, dt_lane=2 * hd)
    mixed, s_dn, inv_dn, ffn_up_all, ffn_down_all = gdn_forward(
        act_qkv, proj, dn_al, dn_db, dn_nw, n_pad=n_pad, z_off=lay.z_off, out_cols=2 * d,
        gather=[slabs["ffn_up"], slabs["ffn_down"]], **small_at)
    mixed, s_m2, w_out_all = ssd_forward(act_xbc, proj, m2_al, m2_db, m2_dk, wts["m2_norm_w"], mixed,
                                         n_pad=n_pad, z_off=lay.m2z_off, gather=[slabs["w_out"]], **dt_at)
    wts = dict(wts, w_out=w_out_all.reshape(-1, d), ffn_down=ffn_down_all.reshape(-1, d),
               ffn_up=jnp.transpose(ffn_up_all, (1, 0, 2)).reshape(d, -1))
    h1 = matmul(mixed, wts["w_out"], mode="nn", out_dtype=F32, name="out_proj", tk=2 * d, residual=h0)
    hn2 = rmsnorm_forward(h1, wts["norm_ffn_w"], name="norm_ffn")
    up = matmul(hn2, wts["ffn_up"], mode="nn", out_dtype=F32, name="ffn_up", tn=1024, out_shards=2)
    kf, f = wts["ffn_conv_w"].shape[0], wts["ffn_conv_w"].shape[1] // 2
    w_glu = jnp.transpose(wts["ffn_conv_w"].reshape(kf, 2, f), (1, 0, 2))
    act = conv_glu(up, w_glu, name="ffn_conv")
    h2 = matmul(act, wts["ffn_down"], mode="nn", out_dtype=F32, name="ffn_down", tk=f // 2, residual=h1)
    loss, dh2, dh2_16, d_nfw = loss_head(h2, wts["norm_final_w"].reshape(1, d), target, n_skip=n_pad + n_meta)

    g = {}
    d_act = matmul(dh2_16, wts["ffn_down"], mode="nt", out_dtype=F32, name="d_ffn_act", tn=512)
    g["ffn_down"] = matmul(act, dh2_16, mode="tn", out_dtype=F32, name="dw_ffn_down", tm=512, tn=1024, tk=t)
    dup, d_fcw = conv_glu_backward(up, w_glu, d_act, name="d_ffn_conv")
    g["ffn_conv_w"] = jnp.transpose(d_fcw[:, :kf], (1, 0, 2)).reshape(kf, 2 * f)
    dhn2 = matmul(dup, wts["ffn_up"], mode="nt", out_dtype=F32, name="d_norm_ffn_out", tk=f, a_shards=2)
    g["ffn_up"] = matmul(hn2, dup, mode="tn", out_dtype=F32, name="dw_ffn_up", tm=512, tn=1408, tk=t,
                         b_shards=2, out_shards=4)
    dh1, dh1_16, g["norm_ffn_w"] = rmsnorm_backward(h1, wts["norm_ffn_w"], dhn2, dh2, n_pad=n_pad, name="d_norm_ffn")
    d_mixed = matmul(dh1_16, wts["w_out"], mode="nt", out_dtype=F32, name="d_mixed", tn=512)
    g["w_out"] = matmul(mixed, dh1_16, mode="tn", out_dtype=F32, name="dw_out", tm=512, tn=1024, tk=t)

    early = ("w_out", "ffn_up", "ffn_down")
    slots = [g[k] if g[k].ndim == 3 else g[k].reshape(N_CHIPS, -1, g[k].shape[1]) for k in early]
    p16, own = [], []
    for k, slot, rb in zip(early, slots, pair_exchange(slots, name="grad_pair_exchange_early")):
        a, b = pair_add(slot, rb, chip, core, name="grad_pair_add_" + k)
        p16.append(a)
        own.append(b)

    dq, dk, dv, dproj, dsm_dn, g_al, g_db, g_nw = gdn_backward(
        act_qkv, proj, dn_al, dn_db, dn_nw, s_dn, inv_dn, d_mixed, n_pad=n_pad, z_off=lay.z_off, **small_at)
    g["dn_a_log"], g["dn_dt_bias"] = g_al.reshape(1, hd), g_db.reshape(1, hd)
    g["dn_norm_w"] = jnp.sum(g_nw, axis=0)
    dxs, dbm, dcm, dproj, dsm_m2, g_al, g_db, g_dk, g["m2_norm_w"], *from_chips = ssd_backward(
        act_xbc, proj, m2_al, m2_db, m2_dk, wts["m2_norm_w"], s_m2, d_mixed, dproj,
        n_pad=n_pad, z_off=lay.m2z_off, exchange=p16, **dt_at)
    for k, o, q in zip(early, own, from_chips):
        g[k] = (o, q)
    g["m2_a_log"], g["m2_dt_bias"], g["m2_d"] = g_al.reshape(1, hm), g_db.reshape(1, hm), g_dk.reshape(1, hm)

    kc = wts["dn_conv_w"].shape[0]
    dw_parts = []
    for idx, dpart in enumerate((dq, dk, dv)):
        dproj, dw, _ = conv_silu_backward(proj, wts["dn_conv_w"][:, idx * d:(idx + 1) * d], zeros_b[:, :d], dpart,
                                          dproj, x_off=idx * d, name=f"d_dn_conv{idx}")
        dw_parts.append(dw[:kc])
    g["dn_conv_w"] = jnp.concatenate(dw_parts, axis=1)
    dw_parts, db_parts = [], []
    off = 0
    for idx, dpart in enumerate((dxs, dbm, dcm)):
        wd = dpart.shape[1]
        dproj, dw, db = conv_silu_backward(proj, wts["m2_conv_w"][:, off:off + wd], wts["m2_conv_b"][:, off:off + wd],
                                           dpart, dproj, x_off=lay.xbc_off + off, name=f"d_m2_conv{idx}")
        dw_parts.append(dw[:kc])
        db_parts.append(db[:1])
        off += wd
    g["m2_conv_w"] = jnp.concatenate(dw_parts, axis=1)
    g["m2_conv_b"] = jnp.concatenate(db_parts, axis=1)
    dsmall = jnp.sum(dsm_dn, axis=0) + jnp.sum(dsm_m2, axis=0)
    dproj = lax.dynamic_update_slice(dproj, dsmall.astype(BF16), (0, lay.small_off))

    dw_in = matmul(hn1, dproj, mode="tn", out_dtype=F32, name="dw_in", tm=512, tn=640, tk=t)
    w_in_slots = jnp.transpose(lay.unpermute_w_in(dw_in).reshape(d, N_CHIPS, -1), (1, 0, 2))
    (from_sibling,) = pair_exchange([w_in_slots], name="grad_pair_exchange_w_in")
    p16_w_in, own_w_in = pair_add(w_in_slots, from_sibling, chip, core, name="grad_pair_add_w_in")
    dhn1, from_chips = matmul(dproj, wts["w_in"], mode="nt", out_dtype=F32, name="d_norm_mix_out", tk=lay.p // 3,
                              exchange=[p16_w_in])
    g["w_in"] = (own_w_in, from_chips)
    dh0, _, g["norm_mix_w"] = rmsnorm_backward(h0, wts["norm_mix_w"], dhn1, dh1, n_pad=n_pad, name="d_norm_mix")
    g["norm_final_w"] = d_nfw
    return loss[0, 0], dh0, g


MESH = pl.DeviceIdType.MESH
ANY = pl.BlockSpec(memory_space=pl.ANY)
N_CHIPS = 4
N_DEV = 8


def _mesh_pos():
    return lax.axis_index("x"), lax.axis_index("y"), lax.axis_index("c")


def _other_chips(x, y):
    return [(1 - x, y), (x, 1 - y), (1 - x, 1 - y)]


def _rcopy(src, dst, send_sems, recv_sems, k, to):
    return pltpu.make_async_remote_copy(src_ref=src, dst_ref=dst, send_sem=send_sems.at[k], recv_sem=recv_sems.at[k],
                                        device_id=to, device_id_type=MESH)


class GatherJob:
    def __init__(self, slabs, send_sems, recv_sems):
        self.slabs, self.send, self.recv = slabs, send_sems, recv_sems
        self.x, self.y, self.c = _mesh_pos()

    @staticmethod
    def sems(n):
        return [pltpu.SemaphoreType.DMA((6 * n,)), pltpu.SemaphoreType.DMA((6 * n,))]

    def _pieces(self):
        x, y, c = self.x, self.y, self.c
        for a, slab in enumerate(self.slabs):
            half = slab.shape[1] // 2
            for j, (px, py) in enumerate(_other_chips(x, y)):
                yield a, j, (px, py), slab, pl.ds(c * half, half), pl.ds((1 - c) * half, half)

    def _ici(self, a, j, chip, ref):
        return _rcopy(ref, ref, self.send, self.recv, 6 * a + j, (chip[0], chip[1], self.c))

    def _d2d(self, a, j, ref):
        return _rcopy(ref, ref, self.send, self.recv, 6 * a + 3 + j, (self.x, self.y, 1 - self.c))

    def begin(self):
        for a, j, chip, slab, mine, _ in self._pieces():
            self._ici(a, j, chip, slab.at[2 * self.x + self.y, mine]).start()

    def pass_on(self):
        for a, j, chip, slab, mine, _ in self._pieces():
            landed = slab.at[2 * chip[0] + chip[1], mine]
            self._ici(a, j, chip, landed).wait_recv()
            self._d2d(a, j, landed).start()

    def end(self):
        for a, j, chip, slab, mine, theirs in self._pieces():
            self._d2d(a, j, slab.at[2 * chip[0] + chip[1], theirs]).wait_recv()
        for a, j, chip, slab, mine, _ in self._pieces():
            self._ici(a, j, chip, slab.at[2 * self.x + self.y, mine]).wait_send()
            self._d2d(a, j, slab.at[2 * chip[0] + chip[1], mine]).wait_send()


class ChipExchangeJob:
    def __init__(self, parts, outs, send_sems, recv_sems):
        self.parts, self.outs, self.send, self.recv = parts, outs, send_sems, recv_sems
        self.x, self.y, self.c = _mesh_pos()

    @staticmethod
    def sems(n):
        return [pltpu.SemaphoreType.DMA((3 * n,)), pltpu.SemaphoreType.DMA((3 * n,))]

    def _copies(self):
        for a, (part, out) in enumerate(zip(self.parts, self.outs)):
            for j, (px, py) in enumerate(_other_chips(self.x, self.y)):
                yield _rcopy(part.at[2 * px + py], out.at[j], self.send, self.recv, 3 * a + j, (px, py, self.c))

    def begin(self):
        for cp in self._copies():
            cp.start()

    def end(self):
        for cp in self._copies():
            cp.wait()


def gather_shards(slabs, *, name):
    n = len(slabs)

    def body(*refs):
        job = GatherJob(refs[n:2 * n], *refs[2 * n:])
        job.begin()
        job.pass_on()
        job.end()

    return pl.pallas_call(
        body, name=name,
        in_specs=[ANY] * n, out_specs=[ANY] * n,
        out_shape=[jax.ShapeDtypeStruct(s.shape, s.dtype) for s in slabs],
        input_output_aliases={a: a for a in range(n)},
        scratch_shapes=GatherJob.sems(n),
        compiler_params=pltpu.CompilerParams(has_side_effects=True),
    )(*slabs)


def pair_exchange(grads, *, name):
    n = len(grads)

    def body(*refs):
        ins, outs = refs[:n], refs[n:2 * n]
        send_sems, recv_sems = refs[2 * n:]
        x, y, c = _mesh_pos()
        cps = []
        for a in range(n):
            half = ins[a].shape[1] // 2
            cp = _rcopy(ins[a].at[:, pl.ds((1 - c) * half, half), :], outs[a], send_sems, recv_sems, a, (x, y, 1 - c))
            cp.start()
            cps.append(cp)
        for cp in cps:
            cp.wait()

    return pl.pallas_call(
        body, name=name, in_specs=[ANY] * n, out_specs=[ANY] * n,
        out_shape=[jax.ShapeDtypeStruct((s.shape[0], s.shape[1] // 2, s.shape[2]), s.dtype) for s in grads],
        scratch_shapes=[pltpu.SemaphoreType.DMA((n,)), pltpu.SemaphoreType.DMA((n,))],
        compiler_params=pltpu.CompilerParams(has_side_effects=True),
    )(*grads)


def chip_exchange(parts, *, name):
    n = len(parts)

    def body(*refs):
        job = ChipExchangeJob(refs[:n], refs[n:2 * n], *refs[2 * n:])
        job.begin()
        job.end()

    return pl.pallas_call(
        body, name=name, in_specs=[ANY] * n, out_specs=[ANY] * n,
        out_shape=[jax.ShapeDtypeStruct((3,) + s.shape[1:], s.dtype) for s in parts],
        scratch_shapes=ChipExchangeJob.sems(n),
        compiler_params=pltpu.CompilerParams(has_side_effects=True),
    )(*parts)


def pair_join(wholes, *, name):
    n = len(wholes)

    def body(*refs):
        outs = refs[n:2 * n]
        send_sems, recv_sems = refs[2 * n:]
        x, y, c = _mesh_pos()
        cps = []
        for a in range(n):
            half = outs[a].shape[0] // 2
            rows = outs[a].at[pl.ds(c * half, half)]
            cp = _rcopy(rows, rows, send_sems, recv_sems, a, (x, y, 1 - c))
            cp.start()
            cps.append(cp)
        for a, cp in enumerate(cps):
            cp.wait_send()
            half = outs[a].shape[0] // 2
            theirs = outs[a].at[pl.ds((1 - c) * half, half)]
            _rcopy(theirs, theirs, send_sems, recv_sems, a, (x, y, 1 - c)).wait_recv()

    return pl.pallas_call(
        body, name=name, in_specs=[ANY] * n, out_specs=[ANY] * n,
        out_shape=[jax.ShapeDtypeStruct(s.shape, s.dtype) for s in wholes],
        input_output_aliases={a: a for a in range(n)},
        scratch_shapes=[pltpu.SemaphoreType.DMA((n,)), pltpu.SemaphoreType.DMA((n,))],
        compiler_params=pltpu.CompilerParams(has_side_effects=True),
    )(*wholes)


def gather_all(v, *, name):
    def body(in_ref, out_ref, send_sems, recv_sems, local_sem):
        x, y, c = _mesh_pos()
        mine = out_ref.at[4 * x + 2 * y + c]
        lc = pltpu.make_async_copy(in_ref, mine, local_sem)
        lc.start()
        cps = []
        for k in range(1, N_DEV):
            flip = lambda v, bit: 1 - v if (k >> bit) & 1 else v
            cp = _rcopy(in_ref, mine, send_sems, recv_sems, k - 1, (flip(x, 2), flip(y, 1), flip(c, 0)))
            cp.start()
            cps.append(cp)
        for cp in cps:
            cp.wait()
        lc.wait()

    return pl.pallas_call(
        body, name=name, in_specs=[ANY], out_specs=ANY,
        out_shape=jax.ShapeDtypeStruct((N_DEV,) + v.shape, v.dtype),
        scratch_shapes=[pltpu.SemaphoreType.DMA((N_DEV - 1,)), pltpu.SemaphoreType.DMA((N_DEV - 1,)),
                        pltpu.SemaphoreType.DMA(())],
        compiler_params=pltpu.CompilerParams(has_side_effects=True),
    )(v)


def _sum_tile(rows, cols):
    return _pick(rows, max(16, (1024 * 1024) // (4 * cols) // 16 * 16), 16)


def pair_add(g, rb, chip, c, *, name):
    _, r, cols = g.shape
    half = r // 2
    tr = _sum_tile(half, cols)
    nrt = half // tr

    def body(s_ref, g_ref, rb_ref, p16_ref, own_ref):
        v = g_ref[...] + rb_ref[...]
        p16_ref[...] = v.astype(p16_ref.dtype)

        @pl.when(pl.program_id(1) == s_ref[0])
        def _():
            own_ref[...] = v

    grid_spec = pltpu.PrefetchScalarGridSpec(
        num_scalar_prefetch=1, grid=(nrt, N_CHIPS),
        in_specs=[pl.BlockSpec((None, tr, cols), lambda i, k, s: (k, s[1] * nrt + i, 0)),
                  pl.BlockSpec((None, tr, cols), lambda i, k, s: (k, i, 0))],
        out_specs=[pl.BlockSpec((None, tr, cols), lambda i, k, s: (k, i, 0)),
                   pl.BlockSpec((tr, cols), lambda i, k, s: (i, 0))])
    return pl.pallas_call(
        body, name=name, grid_spec=grid_spec,
        out_shape=[jax.ShapeDtypeStruct((N_CHIPS, half, cols), BF16), jax.ShapeDtypeStruct((half, cols), F32)],
        compiler_params=_cparams(("arbitrary", "arbitrary")),
    )(jnp.stack([chip, c]).astype(jnp.int32), g, rb)


def chip_add(own, q, c, *, name):
    r, cols = own.shape
    tr = _sum_tile(r, cols)
    nrt = r // tr

    def body(s_ref, own_ref, q_ref, o_ref):
        o_ref[...] = ((own_ref[...] + q_ref[0].astype(F32)) + q_ref[1].astype(F32)) + q_ref[2].astype(F32)

    grid_spec = pltpu.PrefetchScalarGridSpec(
        num_scalar_prefetch=1, grid=(nrt,),
        in_specs=[pl.BlockSpec((tr, cols), lambda i, s: (i, 0)), pl.BlockSpec((3, tr, cols), lambda i, s: (0, i, 0))],
        out_specs=pl.BlockSpec((tr, cols), lambda i, s: (s[0] * nrt + i, 0)))
    return pl.pallas_call(
        body, name=name, grid_spec=grid_spec,
        out_shape=jax.ShapeDtypeStruct((2 * r, cols), F32),
        compiler_params=_cparams(("arbitrary",)),
    )(jnp.reshape(c, (1,)).astype(jnp.int32), own, q)


def sum_slots(v, *, name):
    n, r, cols = v.shape
    tr = _sum_tile(r, cols)

    def body(v_ref, o_ref):
        acc = v_ref[0]
        for k in range(1, n):
            acc = acc + v_ref[k]
        o_ref[...] = acc

    return pl.pallas_call(
        body, name=name, grid=(r // tr,),
        in_specs=[pl.BlockSpec((n, tr, cols), lambda i: (0, i, 0))],
        out_specs=pl.BlockSpec((tr, cols), lambda i: (i, 0)),
        out_shape=jax.ShapeDtypeStruct((r, cols), F32),
        compiler_params=_cparams(("parallel",)),
    )(v)


PACK_ROWS = 16


def _pack(arrays):
    parts = []
    for a in arrays:
        flat = a.reshape(-1).astype(F32)
        size = PACK_ROWS * LANES
        pad = (-flat.shape[0]) % size
        parts.append(jnp.pad(flat, (0, pad)))
    return jnp.concatenate(parts).reshape(-1, LANES)


def _unpack(slab, shapes):
    out, row = [], 0
    for shp in shapes:
        n = 1
        for s in shp:
            n *= s
        rows = -(-n // (PACK_ROWS * LANES)) * PACK_ROWS
        out.append(slab[row:row + rows].reshape(-1)[:n].reshape(shp))
        row += rows
    return out


WEIGHT_NAMES = ("meta_tokens", "norm_mix_w", "w_in", "dn_conv_w", "dn_a_log", "dn_dt_bias", "dn_norm_w", "m2_conv_w",
                "m2_conv_b", "m2_a_log", "m2_dt_bias", "m2_d", "m2_norm_w", "w_out", "norm_ffn_w", "ffn_up",
                "ffn_conv_w", "ffn_down", "norm_final_w")
BIG = ("w_in", "w_out", "ffn_up", "ffn_down")
BIG_COLUMN_SHARDED = ("w_in", "ffn_up")
SMALL_SHARDED = ("meta_tokens", "dn_conv_w", "m2_conv_w", "ffn_conv_w")
SMALL = tuple(n for n in WEIGHT_NAMES if n not in BIG)


def kernel(x, meta_tokens, norm_mix_w, w_in, dn_conv_w, dn_a_log, dn_dt_bias, dn_norm_w, m2_conv_w, m2_conv_b, m2_a_log, m2_dt_bias, m2_d, m2_norm_w, w_out, norm_ffn_w, ffn_up, ffn_conv_w, ffn_down, norm_final_w, loss_target, m_meta_tokens, m_norm_mix_w, m_w_in, m_dn_conv_w, m_dn_a_log, m_dn_dt_bias, m_dn_norm_w, m_m2_conv_w, m_m2_conv_b, m_m2_a_log, m_m2_dt_bias, m_m2_d, m_m2_norm_w, m_w_out, m_norm_ffn_w, m_ffn_up, m_ffn_conv_w, m_ffn_down, m_norm_final_w, v_meta_tokens, v_norm_mix_w, v_w_in, v_dn_conv_w, v_dn_a_log, v_dn_dt_bias, v_dn_norm_w, v_m2_conv_w, v_m2_conv_b, v_m2_a_log, v_m2_dt_bias, v_m2_d, v_m2_norm_w, v_w_out, v_norm_ffn_w, v_ffn_up, v_ffn_conv_w, v_ffn_down, v_norm_final_w):
    args = tuple(locals().values())
    nw = len(WEIGHT_NAMES)
    wt = dict(zip(WEIGHT_NAMES, args[1:1 + nw]))
    mom = dict(zip(WEIGHT_NAMES, args[2 + nw:2 + 2 * nw]))
    var = dict(zip(WEIGHT_NAMES, args[2 + 2 * nw:2 + 3 * nw]))
    xi, yi, ci = _mesh_pos()
    chip = 2 * xi + yi
    seq, d = x.shape[1], x.shape[2]
    n_meta = wt["meta_tokens"].shape[0]
    n_pad = (-(n_meta + seq)) % ROW_ALIGN
    lay = Layout(d)

    shard2d = {k: wt[k].reshape(wt[k].shape[-2:]) for k in BIG}
    small_local = [wt[k].reshape(wt[k].shape[-2:]) for k in SMALL_SHARDED]
    small_slab = _pack(small_local)
    small_slab = lax.dynamic_update_slice(jnp.zeros((N_CHIPS,) + small_slab.shape, F32), small_slab[None], (chip, 0, 0))
    slabs = {k: cast_into_slot(shard2d[k], chip, N_CHIPS, name="cast_" + k) for k in BIG}
    w_in_all, small_all = gather_shards([slabs.pop("w_in"), small_slab], name="gather_w_in")
    full = {"w_in": lay.permute_w_in(jnp.transpose(w_in_all, (1, 0, 2)).reshape(d, -1))}
    per_chip = [_unpack(small_all[s], [a.shape for a in small_local]) for s in range(N_CHIPS)]
    for idx, k in enumerate(SMALL_SHARDED):
        full[k] = jnp.concatenate([per_chip[s][idx] for s in range(N_CHIPS)], axis=-1)
    for k in SMALL:
        if k not in SMALL_SHARDED:
            full[k] = wt[k]

    h0 = jnp.concatenate([jnp.zeros((n_pad, d), F32), full["meta_tokens"], x[0]], axis=0)
    target = jnp.concatenate([jnp.zeros((n_pad + n_meta, d), F32), loss_target[0]], axis=0)
    loss_local, dh0, g = local_step(h0, target, full, slabs, chip, ci, n_pad=n_pad, n_meta=n_meta)
    grad_x = dh0[n_pad + n_meta:][None]
    g["meta_tokens"] = dh0[n_pad:n_pad + n_meta]
    loss = lax.psum(loss_local, ("x", "y", "c"))

    wholes = [chip_add(*g[k], ci, name="grad_chip_add_" + k) for k in BIG]
    grad = dict(zip(BIG, pair_join(wholes, name="grad_pair_join")))

    small_shapes = [g[k].shape for k in SMALL]
    summed = sum_slots(gather_all(_pack([g[k] for k in SMALL]), name="small_grad_gather"), name="small_grad_sum")
    for k, v in zip(SMALL, _unpack(summed, small_shapes)):
        if k in SMALL_SHARDED:
            width = wt[k].shape[-1]
            v = lax.dynamic_slice_in_dim(v, chip * width, width, axis=v.ndim - 1)
        grad[k] = v

    delta, new_m, new_v = {}, {}, {}
    for k in BIG:
        shp = shard2d[k].shape
        delta[k], new_m[k], new_v[k] = adamw(shard2d[k], grad[k], mom[k].reshape(shp), var[k].reshape(shp),
                                             name="adamw_" + k)
    shapes = [wt[k].shape for k in SMALL]
    packed = adamw(_pack([wt[k] for k in SMALL]), _pack([grad[k] for k in SMALL]), _pack([mom[k] for k in SMALL]),
                   _pack([var[k] for k in SMALL]), name="adamw_small")
    for res, slab in zip((delta, new_m, new_v), packed):
        for k, v in zip(SMALL, _unpack(slab, shapes)):
            res[k] = v
    outs = [loss, grad_x]
    for res in (grad, delta, new_m, new_v):
        outs += [res[k].reshape(wt[k].shape) for k in WEIGHT_NAMES]
    return tuple(outs)
```

```python
import functools
import math

import jax
import jax.numpy as jnp
from jax import lax
from jax.experimental import pallas as pl
from jax.experimental.pallas import tpu as pltpu

F32 = jnp.float32
BF16 = jnp.bfloat16
HI = lax.Precision.HIGHEST

CHUNK = 64
ROW_ALIGN = 128
NORM_EPS = 1e-6
DN_DK = 128
M2_P = 64
M2_N = 128
M2_GROUPS = 4
HEAD_BLOCK = 8
VMEM_LIMIT = 56 * 1024 * 1024

ADAM_LR, ADAM_B1, ADAM_B2, ADAM_EPS, ADAM_WD, ADAM_STEP = 0.001, 0.9, 0.999, 1e-08, 0.01, 10


def _cparams(sem=None):
    return pltpu.CompilerParams(dimension_semantics=sem, vmem_limit_bytes=VMEM_LIMIT)


def _silu(x):
    return x / (1.0 + jnp.exp(-x))


def _sigmoid(x):
    return 1.0 / (1.0 + jnp.exp(-x))


def _softplus(x):
    return jnp.maximum(x, 0.0) + jnp.log(1.0 + jnp.exp(-jnp.abs(x)))


def _tri_masks():
    r = lax.broadcasted_iota(jnp.int32, (CHUNK, CHUNK), 0)
    c = lax.broadcasted_iota(jnp.int32, (CHUNK, CHUNK), 1)
    return (r >= c)[None], (r > c)[None], (r == c)[None]


def _col2row(col, eye):
    return jnp.sum(jnp.where(eye, col, 0.0), axis=1, keepdims=True)


def _cumsum_col(col, causal, eye):
    row = _col2row(col, eye)
    return jnp.sum(jnp.where(causal, row, 0.0), axis=2, keepdims=True)


def _bdot(a, b, dims, precision=None):
    (ca, cb) = dims
    if precision is None:
        a = a.astype(BF16)
        b = b.astype(BF16)
    return lax.dot_general(a, b, (((ca,), (cb,)), ((0,), (0,))), precision=precision,
                           preferred_element_type=F32)


def _dot3(a, b, ca, cb):
    def split(v):
        hi = v.astype(BF16)
        return hi, (v - hi.astype(F32)).astype(BF16)

    def dot(p, q):
        return lax.dot_general(p, q, (((ca,), (cb,)), ((0,), (0,))), preferred_element_type=F32)

    (ah, al), (bh, bl) = split(a), split(b)
    return dot(ah, bh) + (dot(ah, bl) + dot(al, bh))


@jax.custom_vjp
def _bmm3(a, b):
    return _dot3(a, b, 2, 1)


def _bmm3_fwd(a, b):
    return _dot3(a, b, 2, 1), (a, b)


def _bmm3_bwd(res, g):
    a, b = res
    return _dot3(g, b, 2, 2), _dot3(a, g, 1, 1)


_bmm3.defvjp(_bmm3_fwd, _bmm3_bwd)


def _unit_lower_inverse(a_mat, eye):
    inv = jnp.where(eye, 1.0, 0.0) - a_mat
    pw = a_mat
    n = 2
    while n < CHUNK:
        pw = _bmm3(pw, pw)
        inv = inv + _bmm3(inv, pw)
        n *= 2
    return inv


@jax.custom_vjp
def _known_inverse(a_mat, inv):
    return inv


def _known_inverse_fwd(a_mat, inv):
    return inv, inv


def _known_inverse_bwd(inv, g):
    return -_dot3(_dot3(inv, g, 1, 1), inv, 2, 2), jnp.zeros_like(inv)


_known_inverse.defvjp(_known_inverse_fwd, _known_inverse_bwd)


def _gdn_step(state, qa, ka, va, z, braw, araw, a_log, dt_bias, norm_w, vm, inv=None, want_inv=False):
    causal, strict, eye = _tri_masks()
    qa, ka, va = qa * vm, ka * vm, va * vm
    q = qa * lax.rsqrt(jnp.sum(qa * qa, -1, keepdims=True) + NORM_EPS) * (DN_DK ** -0.5)
    k = ka * lax.rsqrt(jnp.sum(ka * ka, -1, keepdims=True) + NORM_EPS)
    beta = _sigmoid(braw) * vm
    g = -jnp.exp(a_log) * _softplus(araw + dt_bias) * vm
    gcum = _cumsum_col(g, causal, eye)
    grow = _col2row(gcum, eye)
    decay = jnp.where(causal, jnp.exp(jnp.where(causal, gcum - grow, 0.0)), 0.0)
    kk = _bdot(k, k, (2, 2))
    a_mat = jnp.where(strict, beta * kk * decay, 0.0)
    tinv = _unit_lower_inverse(a_mat, eye) if inv is None else _known_inverse(a_mat, inv)
    egc = jnp.exp(gcum)
    u = _bmm3(tinv, va * beta)
    w = _bmm3(tinv, k * (beta * egc))
    v_new = u - _bdot(w, state, (2, 1))
    o_inter = _bdot(q * egc, state, (2, 1))
    qk = _bdot(q, k, (2, 2)) * decay
    o = o_inter + _bdot(qk, v_new, (2, 1))
    g_last = jnp.sum(g, axis=1, keepdims=True)
    new_state = state * jnp.exp(g_last) + _bdot(k * jnp.exp(g_last - gcum), v_new, (1, 1))
    o = o * lax.rsqrt(jnp.mean(o * o, -1, keepdims=True) + NORM_EPS) * norm_w * _silu(z)
    return (new_state, o, tinv) if want_inv else (new_state, o)


def _valid_rows(chunk, n_pad):
    r = chunk * CHUNK + lax.broadcasted_iota(jnp.int32, (1, CHUNK, 1), 1)
    return jnp.where(r >= n_pad, 1.0, 0.0).astype(F32)


def _split_heads(x, n, w):
    return jnp.stack([x[:, i * w:(i + 1) * w] for i in range(n)], axis=0)


def _merge_heads(x):
    return jnp.concatenate([x[i] for i in range(x.shape[0])], axis=-1)


def _run_beside(job, step, n_steps):
    @pl.when(step == 0)
    def _():
        job.begin()

    if hasattr(job, "pass_on"):
        @pl.when(step == (7 * n_steps) // 8)
        def _():
            job.pass_on()

    @pl.when(step == n_steps - 1)
    def _():
        job.end()


def _take_cols(slab, first, n):
    lane = lax.broadcasted_iota(jnp.int32, slab.shape, 1)
    return jnp.stack([jnp.sum(jnp.where(lane == first + j, slab, 0.0), axis=1, keepdims=True) for j in range(n)],
                     axis=0)


def _put_cols(cols, first):
    lane = lax.broadcasted_iota(jnp.int32, (cols.shape[1], LANES), 1)
    out = jnp.zeros((cols.shape[1], LANES), F32)
    for j in range(cols.shape[0]):
        out = out + jnp.where(lane == first + j, cols[j], 0.0)
    return out


def gdn_forward(act_qkv, proj, a_log3, dt_bias3, norm_w3, *, n_pad, z_off, small_off, out_cols, gather=()):
    t, w3 = act_qkv.shape
    wdn = w3 // 3
    heads = wdn // DN_DK
    hb = min(HEAD_BLOCK, heads)
    nhb = heads // hb
    nc = t // CHUNK
    bw = hb * DN_DK
    nqb = wdn // bw
    ng = len(gather)

    def body(q_ref, k_ref, v_ref, z_ref, sm_ref, al_ref, db_ref, nw_ref, *rest):
        out_ref, sall_ref, inv_ref = rest[ng:ng + 3]
        st_ref = rest[2 * ng + 3]
        n = pl.program_id(1)
        if ng:
            _run_beside(GatherJob(rest[ng + 3:2 * ng + 3], *rest[2 * ng + 4:]), pl.program_id(0) * nc + n, nhb * nc)

        @pl.when(n == 0)
        def _():
            st_ref[...] = jnp.zeros_like(st_ref)

        state = st_ref[...]
        sall_ref[...] = state
        vm = _valid_rows(n, n_pad)
        head0 = pl.program_id(0) * hb
        new_state, o, tinv = _gdn_step(
            state, _split_heads(q_ref[...], hb, DN_DK), _split_heads(k_ref[...], hb, DN_DK),
            _split_heads(v_ref[...], hb, DN_DK), _split_heads(z_ref[...], hb, DN_DK),
            _take_cols(sm_ref[...], head0, hb), _take_cols(sm_ref[...], heads + head0, hb),
            al_ref[...], db_ref[...], nw_ref[...], vm, want_inv=True)
        st_ref[...] = new_state
        inv_ref[...] = tinv
        out_ref[...] = _merge_heads(o).astype(out_ref.dtype)

    par = pl.BlockSpec((hb, 1, 1), lambda h, n: (h, 0, 0))
    return pl.pallas_call(
        body, name="gdn_fwd",
        grid=(nhb, nc),
        in_specs=[pl.BlockSpec((CHUNK, bw), lambda h, n: (n, h)),
                  pl.BlockSpec((CHUNK, bw), lambda h, n: (n, nqb + h)),
                  pl.BlockSpec((CHUNK, bw), lambda h, n: (n, 2 * nqb + h)),
                  pl.BlockSpec((CHUNK, bw), lambda h, n: (n, z_off // bw + h)),
                  pl.BlockSpec((CHUNK, LANES), lambda h, n: (n, small_off // LANES)),
                  par, par,
                  pl.BlockSpec((1, 1, DN_DK), lambda h, n: (0, 0, 0))] + [ANY] * ng,
        out_specs=[pl.BlockSpec((CHUNK, bw), lambda h, n: (n, h)),
                   pl.BlockSpec((None, hb, DN_DK, DN_DK), lambda h, n: (n, h, 0, 0)),
                   pl.BlockSpec((None, hb, CHUNK, CHUNK), lambda h, n: (n, h, 0, 0))] + [ANY] * ng,
        out_shape=[jax.ShapeDtypeStruct((t, out_cols), BF16),
                   jax.ShapeDtypeStruct((nc, heads, DN_DK, DN_DK), F32),
                   jax.ShapeDtypeStruct((nc, heads, CHUNK, CHUNK), F32)]
        + [jax.ShapeDtypeStruct(s.shape, s.dtype) for s in gather],
        input_output_aliases={8 + a: 3 + a for a in range(ng)},
        scratch_shapes=[pltpu.VMEM((hb, DN_DK, DN_DK), F32)] + (GatherJob.sems(ng) if ng else []),
        compiler_params=_cparams(("arbitrary", "arbitrary")),
    )(act_qkv, act_qkv, act_qkv, proj, proj, a_log3, dt_bias3, norm_w3, *gather)


def gdn_backward(act_qkv, proj, a_log3, dt_bias3, norm_w3, s_all, inv_all, d_mixed, *, n_pad, z_off, small_off,
                 exchange=()):
    t, w3 = act_qkv.shape
    wdn = w3 // 3
    heads = wdn // DN_DK
    hb = min(HEAD_BLOCK, heads)
    nhb = heads // hb
    nc = t // CHUNK
    bw = hb * DN_DK
    nqb = wdn // bw
    ne = len(exchange)

    def body(q_ref, k_ref, v_ref, z_ref, sm_ref, al_ref, db_ref, nw_ref, s_ref, inv_ref, do_ref, *rest):
        dq_ref, dk_ref, dv_ref, dz_ref, dsm_ref, dal_ref, ddb_ref, dnw_ref = rest[ne:ne + 8]
        ds_ref = rest[2 * ne + 8]
        i = pl.program_id(1)
        n = nc - 1 - i
        head0 = pl.program_id(0) * hb
        if ne:
            _run_beside(PairExchangeJob(rest[:ne], rest[ne + 8:2 * ne + 8], *rest[2 * ne + 9:]),
                        pl.program_id(0) * nc + i, nhb * nc)

        @pl.when(i == 0)
        def _():
            ds_ref[...] = jnp.zeros_like(ds_ref)
            dal_ref[...] = jnp.zeros_like(dal_ref)
            ddb_ref[...] = jnp.zeros_like(ddb_ref)
            dnw_ref[...] = jnp.zeros_like(dnw_ref)

        vm = _valid_rows(n, n_pad)
        step = functools.partial(_gdn_step, vm=vm, inv=inv_ref[...])
        _, vjp = jax.vjp(step, s_ref[...], _split_heads(q_ref[...], hb, DN_DK), _split_heads(k_ref[...], hb, DN_DK),
                         _split_heads(v_ref[...], hb, DN_DK), _split_heads(z_ref[...], hb, DN_DK),
                         _take_cols(sm_ref[...], head0, hb), _take_cols(sm_ref[...], heads + head0, hb),
                         al_ref[...], db_ref[...], nw_ref[...])
        ds, dq, dk, dv, dz, dbr, dar, dal, ddb, dnw = vjp((ds_ref[...], _split_heads(do_ref[...], hb, DN_DK)))
        ds_ref[...] = ds
        dq_ref[...] = _merge_heads(dq)
        dk_ref[...] = _merge_heads(dk)
        dv_ref[...] = _merge_heads(dv)
        dz_ref[...] = _merge_heads(dz).astype(dz_ref.dtype)
        dsm_ref[...] = _put_cols(dbr, head0) + _put_cols(dar, heads + head0)
        dal_ref[...] += dal
        ddb_ref[...] += ddb
        dnw_ref[...] += dnw[0]

    rev = lambda n: nc - 1 - n
    par = pl.BlockSpec((hb, 1, 1), lambda h, n: (h, 0, 0))
    blk = lambda off: pl.BlockSpec((CHUNK, bw), lambda h, n: (rev(n), off + h))
    return pl.pallas_call(
        body, name="gdn_bwd",
        grid=(nhb, nc),
        in_specs=[blk(0), blk(nqb), blk(2 * nqb), blk(z_off // bw),
                  pl.BlockSpec((CHUNK, LANES), lambda h, n: (rev(n), small_off // LANES)), par, par,
                  pl.BlockSpec((1, 1, DN_DK), lambda h, n: (0, 0, 0)),
                  pl.BlockSpec((None, hb, DN_DK, DN_DK), lambda h, n: (rev(n), h, 0, 0)),
                  pl.BlockSpec((None, hb, CHUNK, CHUNK), lambda h, n: (rev(n), h, 0, 0)),
                  blk(0)] + [ANY] * ne,
        out_specs=[blk(0), blk(0), blk(0), blk(z_off // bw),
                   pl.BlockSpec((None, CHUNK, LANES), lambda h, n: (h, rev(n), 0)), par, par,
                   pl.BlockSpec((None, 1, DN_DK), lambda h, n: (h, 0, 0))] + [ANY] * ne,
        out_shape=[jax.ShapeDtypeStruct((t, wdn), F32)] * 3
        + [jax.ShapeDtypeStruct((t, proj.shape[1]), BF16),
           jax.ShapeDtypeStruct((nhb, t, LANES), F32),
           jax.ShapeDtypeStruct((heads, 1, 1), F32), jax.ShapeDtypeStruct((heads, 1, 1), F32),
           jax.ShapeDtypeStruct((nhb, 1, DN_DK), F32)] + [_half_rows(s) for s in exchange],
        scratch_shapes=[pltpu.VMEM((hb, DN_DK, DN_DK), F32)] + (PairExchangeJob.sems(ne) if ne else []),
        compiler_params=_cparams(("arbitrary", "arbitrary")),
    )(act_qkv, act_qkv, act_qkv, proj, proj, a_log3, dt_bias3, norm_w3, s_all, inv_all, d_mixed, *exchange)


def _dot2(a, b, ca, cb):
    return lax.dot_general(a.astype(BF16), b.astype(BF16), (((ca,), (cb,)), ((), ())),
                           preferred_element_type=F32)


def _ssd_step(state, xa, bmat, cmat, z, dtraw, a_log, dt_bias, dskip, norm_w, vm):
    causal, _, eye = _tri_masks()
    r_heads, p, n_state = state.shape
    gw = r_heads * p
    vm2 = vm[0]
    xa, bmat, cmat = xa * vm2, bmat * vm2, cmat * vm2
    dt = _softplus(dtraw + dt_bias) * vm
    a = dt * (-jnp.exp(a_log))
    acs = _cumsum_col(a, causal, eye)
    arow = _col2row(acs, eye)
    lmat = jnp.where(causal, jnp.exp(jnp.where(causal, acs - arow, 0.0)), 0.0)
    hsel = (lax.broadcasted_iota(jnp.int32, (r_heads, 1, gw), 2) // p
            == lax.broadcasted_iota(jnp.int32, (r_heads, 1, gw), 0))

    def spread(col):
        return jnp.sum(jnp.where(hsel, col, 0.0), axis=0)

    xdt = xa * spread(dt)
    cb = _dot2(cmat, bmat, 1, 1)
    m = (cb[None] * lmat).reshape(r_heads * CHUNK, CHUNK)
    yb = _dot2(m, xdt, 1, 0).reshape(r_heads, CHUNK, gw)
    y_diag = jnp.sum(jnp.where(hsel, yb, 0.0), axis=0)
    s2 = state.reshape(gw, n_state)
    y_off = _dot2(cmat, s2, 1, 1) * spread(jnp.exp(acs))
    a_last = jnp.sum(a, axis=1, keepdims=True)
    upd = _dot2(xdt * spread(jnp.exp(a_last - acs)), bmat, 0, 0)
    new_state = state * jnp.exp(a_last) + upd.reshape(r_heads, p, n_state)
    y = y_diag + y_off + xa * spread(dskip)
    y = y * _silu(z)
    y = y * lax.rsqrt(jnp.mean(y * y, -1, keepdims=True) + NORM_EPS) * norm_w
    return new_state, y


def _ssd_dims(act_xbc):
    t, wx = act_xbc.shape
    wm = wx - 2 * M2_GROUPS * M2_N
    gw = wm // M2_GROUPS
    return t, wm, gw, gw // M2_P, wm // M2_P, t // CHUNK


def ssd_forward(act_xbc, proj, a_log3, dt_bias3, dskip3, norm_w, mixed, *, n_pad, z_off, small_off, dt_lane,
                gather=()):
    t, wm, gw, rh, heads, nc = _ssd_dims(act_xbc)
    nb = wm // M2_N
    ob = (mixed.shape[1] - wm) // gw
    ng = len(gather)

    def body(x_ref, b_ref, c_ref, z_ref, dt_ref, al_ref, db_ref, dk_ref, nw_ref, _, *rest):
        out_ref, sall_ref = rest[ng:ng + 2]
        st_ref = rest[2 * ng + 2]
        n = pl.program_id(1)
        if ng:
            _run_beside(GatherJob(rest[ng + 2:2 * ng + 2], *rest[2 * ng + 3:]), pl.program_id(0) * nc + n,
                        M2_GROUPS * nc)

        @pl.when(n == 0)
        def _():
            st_ref[...] = jnp.zeros_like(st_ref)

        state = st_ref[...]
        sall_ref[...] = state
        dtraw = _take_cols(dt_ref[...], dt_lane + pl.program_id(0) * rh, rh)
        new_state, y = _ssd_step(state, x_ref[...], b_ref[...], c_ref[...], z_ref[...], dtraw,
                                 al_ref[...], db_ref[...], dk_ref[...], nw_ref[...], _valid_rows(n, n_pad))
        st_ref[...] = new_state
        out_ref[...] = y.astype(out_ref.dtype)

    par = pl.BlockSpec((rh, 1, 1), lambda g, n: (g, 0, 0))
    return pl.pallas_call(
        body, name="ssd_fwd",
        grid=(M2_GROUPS, nc),
        in_specs=[pl.BlockSpec((CHUNK, gw), lambda g, n: (n, g)),
                  pl.BlockSpec((CHUNK, M2_N), lambda g, n: (n, nb + g)),
                  pl.BlockSpec((CHUNK, M2_N), lambda g, n: (n, nb + M2_GROUPS + g)),
                  pl.BlockSpec((CHUNK, gw), lambda g, n: (n, z_off // gw + g)),
                  pl.BlockSpec((CHUNK, LANES), lambda g, n: (n, small_off // LANES)),
                  par, par, par,
                  pl.BlockSpec((1, gw), lambda g, n: (0, g)),
                  pl.BlockSpec(memory_space=pl.ANY)] + [ANY] * ng,
        out_specs=[pl.BlockSpec((CHUNK, gw), lambda g, n: (n, ob + g)),
                   pl.BlockSpec((None, rh, M2_P, M2_N), lambda g, n: (n, g, 0, 0))] + [ANY] * ng,
        out_shape=[jax.ShapeDtypeStruct(mixed.shape, mixed.dtype),
                   jax.ShapeDtypeStruct((nc, heads, M2_P, M2_N), F32)]
        + [jax.ShapeDtypeStruct(s.shape, s.dtype) for s in gather],
        input_output_aliases={9: 0, **{10 + a: 2 + a for a in range(ng)}},
        scratch_shapes=[pltpu.VMEM((rh, M2_P, M2_N), F32)] + (GatherJob.sems(ng) if ng else []),
        compiler_params=_cparams(("arbitrary", "arbitrary")),
    )(act_xbc, act_xbc, act_xbc, proj, proj, a_log3, dt_bias3, dskip3, norm_w, mixed, *gather)


def ssd_backward(act_xbc, proj, a_log3, dt_bias3, dskip3, norm_w, s_all, d_mixed, dproj, *, n_pad, z_off,
                 small_off, dt_lane, exchange=()):
    t, wm, gw, rh, heads, nc = _ssd_dims(act_xbc)
    nb = wm // M2_N
    ne = len(exchange)

    def body(x_ref, b_ref, c_ref, z_ref, dt_ref, al_ref, db_ref, dk_ref, nw_ref, s_ref, dy_ref, _, *rest):
        dx_ref, dbm_ref, dcm_ref, dz_ref, ddt_ref, dal_ref, ddb_ref, ddk_ref, dnw_ref = rest[ne:ne + 9]
        ds_ref = rest[2 * ne + 9]
        i = pl.program_id(1)
        n = nc - 1 - i
        if ne:
            _run_beside(ChipExchangeJob(rest[:ne], rest[ne + 9:2 * ne + 9], *rest[2 * ne + 10:]),
                        pl.program_id(0) * nc + i, M2_GROUPS * nc)

        @pl.when(i == 0)
        def _():
            ds_ref[...] = jnp.zeros_like(ds_ref)
            dal_ref[...] = jnp.zeros_like(dal_ref)
            ddb_ref[...] = jnp.zeros_like(ddb_ref)
            ddk_ref[...] = jnp.zeros_like(ddk_ref)
            dnw_ref[...] = jnp.zeros_like(dnw_ref)

        step = functools.partial(_ssd_step, vm=_valid_rows(n, n_pad))
        lane0 = dt_lane + pl.program_id(0) * rh
        _, vjp = jax.vjp(step, s_ref[...], x_ref[...], b_ref[...], c_ref[...], z_ref[...],
                         _take_cols(dt_ref[...], lane0, rh), al_ref[...], db_ref[...], dk_ref[...], nw_ref[...])
        ds, dx, dbm, dcm, dz, ddt, dal, ddb, ddk, dnw = vjp((ds_ref[...], dy_ref[...]))
        ds_ref[...] = ds
        dx_ref[...] = dx
        dbm_ref[...] = dbm
        dcm_ref[...] = dcm
        dz_ref[...] = dz.astype(dz_ref.dtype)
        ddt_ref[...] = _put_cols(ddt, lane0)
        dal_ref[...] += dal
        ddb_ref[...] += ddb
        ddk_ref[...] += ddk
        dnw_ref[...] += dnw

    rev = lambda n: nc - 1 - n
    par = pl.BlockSpec((rh, 1, 1), lambda g, n: (g, 0, 0))
    wide = lambda off: pl.BlockSpec((CHUNK, gw), lambda g, n: (rev(n), off + g))
    narrow = lambda off: pl.BlockSpec((CHUNK, M2_N), lambda g, n: (rev(n), off + g))
    col = pl.BlockSpec((None, CHUNK, LANES), lambda g, n: (g, rev(n), 0))
    gn = M2_GROUPS * M2_N
    return pl.pallas_call(
        body, name="ssd_bwd",
        grid=(M2_GROUPS, nc),
        in_specs=[wide(0), narrow(nb), narrow(nb + M2_GROUPS), wide(z_off // gw),
                  pl.BlockSpec((CHUNK, LANES), lambda g, n: (rev(n), small_off // LANES)), par, par, par,
                  pl.BlockSpec((1, gw), lambda g, n: (0, g)),
                  pl.BlockSpec((None, rh, M2_P, M2_N), lambda g, n: (rev(n), g, 0, 0)),
                  wide(M2_GROUPS), pl.BlockSpec(memory_space=pl.ANY)] + [ANY] * ne,
        out_specs=[wide(0), narrow(0), narrow(0), wide(z_off // gw), col, par, par, par,
                   pl.BlockSpec((1, gw), lambda g, n: (0, g))] + [ANY] * ne,
        out_shape=[jax.ShapeDtypeStruct((t, wm), F32), jax.ShapeDtypeStruct((t, gn), F32),
                   jax.ShapeDtypeStruct((t, gn), F32), jax.ShapeDtypeStruct(dproj.shape, dproj.dtype),
                   jax.ShapeDtypeStruct((M2_GROUPS, t, LANES), F32),
                   jax.ShapeDtypeStruct((heads, 1, 1), F32), jax.ShapeDtypeStruct((heads, 1, 1), F32),
                   jax.ShapeDtypeStruct((heads, 1, 1), F32), jax.ShapeDtypeStruct((1, wm), F32)]
        + [jax.ShapeDtypeStruct((3,) + s.shape[1:], s.dtype) for s in exchange],
        input_output_aliases={11: 3},
        scratch_shapes=[pltpu.VMEM((rh, M2_P, M2_N), F32)] + (ChipExchangeJob.sems(ne) if ne else []),
        compiler_params=_cparams(("arbitrary", "arbitrary")),
    )(act_xbc, act_xbc, act_xbc, proj, proj, a_log3, dt_bias3, dskip3, norm_w, s_all, d_mixed, dproj, *exchange)


SUBLANES = 8
LANES = 128


def _pick(dim, target, align):
    best = None
    for d in range(align, min(dim, target) + 1, align):
        if dim % d == 0:
            best = d
    return dim if best is None else best


def _row_tile(t):
    return _pick(t, 512, 16)


def matmul(a, b, *, mode, out_dtype, name, tm=1056, tn=512, tk=2048, residual=None, out_shards=1,
           a_shards=1, b_shards=1, exchange=()):
    if mode == "tn":
        kd, m = a.shape
        n = b.shape[-1] * b_shards
    else:
        m, kd = a.shape[-2], a.shape[-1] * a_shards
        n = b.shape[1] if mode == "nn" else b.shape[0]
    tm = _pick(m, tm, LANES if mode == "tn" else 16)
    ks = kd // a_shards
    tk = _pick(ks, tk, LANES)
    nkb = ks // tk
    nk = kd // tk
    ns_o, ns_b = n // out_shards, n // b_shards
    tn = _pick(math.gcd(ns_o, ns_b), tn, LANES)
    npb_o, npb_b = ns_o // tn, ns_b // tn
    if mode == "tn":
        a_spec = pl.BlockSpec((tk, tm), lambda i, j, k: (k, i))
    elif a_shards == 1:
        a_spec = pl.BlockSpec((tm, tk), lambda i, j, k: (i, k))
    else:
        a_spec = pl.BlockSpec((None, tm, tk), lambda i, j, k: (k // nkb, i, k % nkb))
    contract = ((1,), (1,)) if mode == "nt" else ((1,), (0,))
    if mode == "nt":
        b_spec = pl.BlockSpec((tn, tk), lambda i, j, k: (j, k))
    elif b_shards == 1:
        b_spec = pl.BlockSpec((tk, tn), lambda i, j, k: (k, j))
    else:
        b_spec = pl.BlockSpec((None, tk, tn), lambda i, j, k: (j // npb_b, k, j % npb_b))
    has_res = residual is not None
    ne = len(exchange)
    grid = (m // tm, n // tn, nk)

    def body(*refs):
        refs = list(refs)
        a_ref, b_ref = refs[:2]
        del refs[:2]
        r_ref = refs.pop(0) if has_res else None
        ex_in = [refs.pop(0) for _ in range(ne)]
        o_ref = refs.pop(0)
        ex_out = [refs.pop(0) for _ in range(ne)]
        at_ref = refs.pop(0) if mode == "tn" else None
        acc_ref = refs.pop(0) if nk > 1 else None
        k = pl.program_id(2)
        if ne:
            step = (pl.program_id(0) * grid[1] + pl.program_id(1)) * nk + k
            _run_beside(ChipExchangeJob(ex_in, ex_out, *refs), step, grid[0] * grid[1] * nk)
        if mode == "tn":
            @pl.when(pl.program_id(1) == 0)
            def _():
                at_ref[k] = jnp.transpose(a_ref[...].astype(F32)).astype(BF16)

            lhs = at_ref[k]
        else:
            lhs = a_ref[...].astype(BF16)
        part = lax.dot_general(lhs, b_ref[...].astype(BF16), (contract, ((), ())), preferred_element_type=F32)

        def finish(total):
            if has_res:
                total = total + r_ref[...]
            o_ref[...] = total.astype(o_ref.dtype)

        if nk == 1:
            finish(part)
        else:
            @pl.when(k == 0)
            def _():
                acc_ref[...] = part

            @pl.when((k > 0) & (k < nk - 1))
            def _():
                acc_ref[...] += part

            @pl.when(k == nk - 1)
            def _():
                finish(acc_ref[...] + part)

    in_specs = [a_spec, b_spec]
    args = [a, b]
    if has_res:
        in_specs.append(pl.BlockSpec((tm, tn), lambda i, j, k: (i, j)))
        args.append(residual)
    if out_shards == 1:
        out_spec = pl.BlockSpec((tm, tn), lambda i, j, k: (i, j))
        out_shape = jax.ShapeDtypeStruct((m, n), out_dtype)
    else:
        out_spec = pl.BlockSpec((None, tm, tn), lambda i, j, k: (j // npb_o, i, j % npb_o))
        out_shape = jax.ShapeDtypeStruct((out_shards, m, ns_o), out_dtype)
    scratch = [pltpu.VMEM((nk, tm, tk), BF16)] if mode == "tn" else []
    if nk > 1:
        scratch.append(pltpu.VMEM((tm, tn), F32))
    if not ne:
        return pl.pallas_call(
            body, name=name, grid=grid,
            in_specs=in_specs, out_specs=out_spec, out_shape=out_shape, scratch_shapes=scratch,
            compiler_params=_cparams(("parallel", "arbitrary", "arbitrary")),
        )(*args)
    return pl.pallas_call(
        body, name=name, grid=grid,
        in_specs=in_specs + [ANY] * ne, out_specs=[out_spec] + [ANY] * ne,
        out_shape=[out_shape] + [jax.ShapeDtypeStruct((3,) + s.shape[1:], s.dtype) for s in exchange],
        scratch_shapes=scratch + ChipExchangeJob.sems(ne),
        compiler_params=_cparams(("arbitrary", "arbitrary", "arbitrary")),
    )(*args, *exchange)


def rmsnorm_forward(x, w, *, name):
    t, d = x.shape
    tm = _row_tile(t)

    def body(x_ref, w_ref, o_ref):
        xv = x_ref[...]
        r = lax.rsqrt(jnp.mean(xv * xv, -1, keepdims=True) + NORM_EPS)
        o_ref[...] = (xv * r * w_ref[...]).astype(o_ref.dtype)

    return pl.pallas_call(
        body, name=name, grid=(t // tm,),
        in_specs=[pl.BlockSpec((tm, d), lambda i: (i, 0)), pl.BlockSpec((1, d), lambda i: (0, 0))],
        out_specs=pl.BlockSpec((tm, d), lambda i: (i, 0)),
        out_shape=jax.ShapeDtypeStruct((t, d), BF16),
        compiler_params=_cparams(("parallel",)),
    )(x, w)


def _rmsnorm_grads(xv, wv, dy):
    r = lax.rsqrt(jnp.mean(xv * xv, -1, keepdims=True) + NORM_EPS)
    xh = xv * r
    g = dy * wv
    dx = r * (g - xh * jnp.mean(g * xh, -1, keepdims=True))
    return dx, jnp.sum(dy * xh, axis=0, keepdims=True)


def rmsnorm_backward(x, w, dy, dres, *, n_pad, name):
    t, d = x.shape
    tm = _row_tile(t)

    def body(x_ref, w_ref, dy_ref, dr_ref, dx_ref, dx16_ref, dw_ref):
        i = pl.program_id(0)

        @pl.when(i == 0)
        def _():
            dw_ref[...] = jnp.zeros_like(dw_ref)

        dx, dw = _rmsnorm_grads(x_ref[...], w_ref[...], dy_ref[...])
        rows = i * tm + lax.broadcasted_iota(jnp.int32, (tm, 1), 0)
        dx = jnp.where(rows >= n_pad, dx + dr_ref[...], 0.0)
        dx_ref[...] = dx
        dx16_ref[...] = dx.astype(BF16)
        dw_ref[...] += dw

    row = pl.BlockSpec((tm, d), lambda i: (i, 0))
    vec = pl.BlockSpec((1, d), lambda i: (0, 0))
    return pl.pallas_call(
        body, name=name, grid=(t // tm,),
        in_specs=[row, vec, row, row], out_specs=[row, row, vec],
        out_shape=[jax.ShapeDtypeStruct((t, d), F32), jax.ShapeDtypeStruct((t, d), BF16),
                   jax.ShapeDtypeStruct((1, d), F32)],
        compiler_params=_cparams(("arbitrary",)),
    )(x, w, dy, dres)


def loss_head(h, w, target, *, n_skip):
    t, d = h.shape
    tm = _row_tile(t)

    def body(x_ref, w_ref, y_ref, loss_ref, dx_ref, dx16_ref, dw_ref):
        i = pl.program_id(0)

        @pl.when(i == 0)
        def _():
            dw_ref[...] = jnp.zeros_like(dw_ref)
            loss_ref[...] = jnp.zeros_like(loss_ref)

        xv, wv = x_ref[...], w_ref[...]
        r = lax.rsqrt(jnp.mean(xv * xv, -1, keepdims=True) + NORM_EPS)
        rows = i * tm + lax.broadcasted_iota(jnp.int32, (tm, 1), 0)
        err = jnp.where(rows >= n_skip, xv * r * wv - y_ref[...], 0.0)
        loss_ref[...] += 0.5 * jnp.sum(jnp.mean(err * err, -1, keepdims=True))
        dx, dw = _rmsnorm_grads(xv, wv, err * (1.0 / d))
        dx_ref[...] = dx
        dx16_ref[...] = dx.astype(BF16)
        dw_ref[...] += dw

    row = pl.BlockSpec((tm, d), lambda i: (i, 0))
    vec = pl.BlockSpec((1, d), lambda i: (0, 0))
    return pl.pallas_call(
        body, name="loss_head", grid=(t // tm,),
        in_specs=[row, vec, row],
        out_specs=[pl.BlockSpec((1, LANES), lambda i: (0, 0)), row, row, vec],
        out_shape=[jax.ShapeDtypeStruct((1, LANES), F32), jax.ShapeDtypeStruct((t, d), F32),
                   jax.ShapeDtypeStruct((t, d), BF16), jax.ShapeDtypeStruct((1, d), F32)],
        compiler_params=_cparams(("arbitrary",)),
    )(h, w, target)


HALO = SUBLANES


STRIP = 32


def _taps(blk, kk, rows):
    return [blk[HALO - (kk - 1) + j:HALO - (kk - 1) + j + rows, :] for j in range(kk)]


def _fir(taps, w):
    acc = w[0:1, :] * taps[0]
    for j in range(1, len(taps)):
        acc = acc + w[j:j + 1, :] * taps[j]
    return acc


def _fir_transposed(dpre, w, rows):
    kk = w.shape[0]
    acc = w[0:1, :] * dpre[kk - 1:kk - 1 + rows, :]
    for j in range(1, kk):
        acc = acc + w[j:j + 1, :] * dpre[kk - 1 - j:kk - 1 - j + rows, :]
    return acc


def _fold8(v):
    return jnp.sum(v.reshape(v.shape[0] // SUBLANES, SUBLANES, v.shape[1]), axis=0)


def _strips(tm, body, init):
    def step(r, carry):
        return body(pl.multiple_of(r * STRIP, STRIP), carry)
    return lax.fori_loop(0, tm // STRIP, step, init)


def _dsilu(p):
    s = _sigmoid(p)
    return s * (1.0 + p * (1.0 - s))


def conv_silu(x, w, b, *, x_off, name):
    t = x.shape[0]
    kk, width = w.shape
    tm = _row_tile(t)
    tc = _pick(width, 512, LANES)
    ob, nh = x_off // tc, tm // HALO

    def body(prev_ref, x_ref, w_ref, b_ref, o_ref, scr):
        i = pl.program_id(1)
        scr[0:HALO, :] = jnp.where(i > 0, prev_ref[...], 0.0)
        scr[HALO:HALO + tm, :] = x_ref[...]
        wv, bv = w_ref[...], b_ref[...]

        def strip(base, carry):
            blk = scr[pl.ds(base, STRIP + HALO), :]
            o_ref[pl.ds(base, STRIP), :] = _silu(_fir(_taps(blk, kk, STRIP), wv) + bv)
            return carry

        _strips(tm, strip, 0)

    return pl.pallas_call(
        body, name=name, grid=(width // tc, t // tm),
        in_specs=[pl.BlockSpec((HALO, tc), lambda j, i: (jnp.maximum(i * nh - 1, 0), ob + j)),
                  pl.BlockSpec((tm, tc), lambda j, i: (i, ob + j)),
                  pl.BlockSpec((kk, tc), lambda j, i: (0, j)),
                  pl.BlockSpec((1, tc), lambda j, i: (0, j))],
        out_specs=pl.BlockSpec((tm, tc), lambda j, i: (i, j)),
        out_shape=jax.ShapeDtypeStruct((t, width), F32),
        scratch_shapes=[pltpu.VMEM((tm + HALO, tc), F32)],
        compiler_params=_cparams(("parallel", "arbitrary")),
    )(x, x, w, b)


def conv_silu_backward(x, w, b, dact, dst, *, x_off, name):
    t = x.shape[0]
    kk, width = w.shape
    tm = _row_tile(t)
    tc = _pick(width, 512, LANES)
    ob, nh, nt = x_off // tc, tm // HALO, t // tm
    last_h = t // HALO - 1

    def body(prev_ref, x_ref, next_ref, w_ref, b_ref, d_ref, dnext_ref, _, dx_ref, dw_ref, db_ref, scr_x, scr_d):
        i = pl.program_id(1)

        @pl.when(i == 0)
        def _():
            dw_ref[...] = jnp.zeros_like(dw_ref)
            db_ref[...] = jnp.zeros_like(db_ref)

        wv, bv = w_ref[...], b_ref[...]
        scr_x[0:HALO, :] = jnp.where(i > 0, prev_ref[...], 0.0)
        scr_x[HALO:HALO + tm, :] = x_ref[...]
        scr_x[HALO + tm:, :] = next_ref[...]
        scr_d[0:tm, :] = d_ref[...]
        scr_d[tm:, :] = jnp.where(i < nt - 1, dnext_ref[...], 0.0)

        def strip(base, carry):
            taps = _taps(scr_x[pl.ds(base, STRIP + 2 * HALO), :], kk, STRIP + HALO)
            dpre = scr_d[pl.ds(base, STRIP + HALO), :] * _dsilu(_fir(taps, wv) + bv)
            dx_ref[pl.ds(base, STRIP), :] = _fir_transposed(dpre, wv, STRIP).astype(dx_ref.dtype)
            d0 = dpre[0:STRIP, :]
            return tuple(c + _fold8(d0 * tap[0:STRIP, :]) for c, tap in zip(carry, taps)) + (carry[kk] + _fold8(d0),)

        sums = _strips(tm, strip, tuple(jnp.zeros((SUBLANES, tc), F32) for _ in range(kk + 1)))
        for j in range(kk):
            dw_ref[j:j + 1, :] += jnp.sum(sums[j], axis=0, keepdims=True)
        db_ref[0:1, :] += jnp.sum(sums[kk], axis=0, keepdims=True)

    nxt = lambda j, i: (jnp.minimum((i + 1) * nh, last_h), j)
    acc = pl.BlockSpec((SUBLANES, tc), lambda j, i: (0, j))
    return pl.pallas_call(
        body, name=name, grid=(width // tc, nt),
        in_specs=[pl.BlockSpec((HALO, tc), lambda j, i: (jnp.maximum(i * nh - 1, 0), ob + j)),
                  pl.BlockSpec((tm, tc), lambda j, i: (i, ob + j)),
                  pl.BlockSpec((HALO, tc), lambda j, i: (jnp.minimum((i + 1) * nh, last_h), ob + j)),
                  pl.BlockSpec((kk, tc), lambda j, i: (0, j)),
                  pl.BlockSpec((1, tc), lambda j, i: (0, j)),
                  pl.BlockSpec((tm, tc), lambda j, i: (i, j)),
                  pl.BlockSpec((HALO, tc), nxt),
                  pl.BlockSpec(memory_space=pl.ANY)],
        out_specs=[pl.BlockSpec((tm, tc), lambda j, i: (i, ob + j)), acc, acc],
        out_shape=[jax.ShapeDtypeStruct(dst.shape, dst.dtype), jax.ShapeDtypeStruct((SUBLANES, width), F32),
                   jax.ShapeDtypeStruct((SUBLANES, width), F32)],
        input_output_aliases={7: 0},
        scratch_shapes=[pltpu.VMEM((tm + 2 * HALO, tc), F32), pltpu.VMEM((tm + HALO, tc), F32)],
        compiler_params=_cparams(("parallel", "arbitrary")),
    )(x, x, x, w, b, dact, dact, dst)


def conv_glu(u, w, *, name):
    _, t, f = u.shape
    kk = w.shape[1]
    tm = _row_tile(t)
    tc = _pick(f, 512, LANES)
    nh = tm // HALO

    def body(prev_ref, x_ref, w_ref, o_ref, scr):
        i = pl.program_id(1)
        scr[:, 0:HALO, :] = jnp.where(i > 0, prev_ref[...], 0.0)
        scr[:, HALO:, :] = x_ref[...]
        wg, wv = w_ref[0], w_ref[1]

        def strip(base, carry):
            gate = _fir(_taps(scr[0, pl.ds(base, STRIP + HALO), :], kk, STRIP), wg)
            val = _fir(_taps(scr[1, pl.ds(base, STRIP + HALO), :], kk, STRIP), wv)
            o_ref[pl.ds(base, STRIP), :] = (_silu(gate) * val).astype(o_ref.dtype)
            return carry

        _strips(tm, strip, 0)

    return pl.pallas_call(
        body, name=name, grid=(f // tc, t // tm),
        in_specs=[pl.BlockSpec((2, HALO, tc), lambda j, i: (0, jnp.maximum(i * nh - 1, 0), j)),
                  pl.BlockSpec((2, tm, tc), lambda j, i: (0, i, j)),
                  pl.BlockSpec((2, kk, tc), lambda j, i: (0, 0, j))],
        out_specs=pl.BlockSpec((tm, tc), lambda j, i: (i, j)),
        out_shape=jax.ShapeDtypeStruct((t, f), BF16),
        scratch_shapes=[pltpu.VMEM((2, tm + HALO, tc), F32)],
        compiler_params=_cparams(("parallel", "arbitrary")),
    )(u, u, w)


def conv_glu_backward(u, w, dact, *, name):
    _, t, f = u.shape
    kk = w.shape[1]
    tm = _row_tile(t)
    tc = _pick(f, 512, LANES)
    nh, nt = tm // HALO, t // tm
    last_h = t // HALO - 1
    ext = tm + HALO
    first = HALO - (kk - 1)

    def body(prev_ref, x_ref, next_ref, w_ref, d_ref, dn_ref, du_ref, dw_ref, scr_x, scr_d):
        i = pl.program_id(1)

        @pl.when(i == 0)
        def _():
            dw_ref[...] = jnp.zeros_like(dw_ref)

        scr_x[:, 0:HALO, :] = jnp.where(i > 0, prev_ref[...], 0.0)
        scr_x[:, HALO:HALO + tm, :] = x_ref[...]
        scr_x[:, HALO + tm:, :] = next_ref[...]
        scr_d[0:tm, :] = d_ref[...]
        scr_d[tm:, :] = jnp.where(i < nt - 1, dn_ref[...], 0.0)
        ws = (w_ref[0], w_ref[1])

        def strip(base, carry):
            taps = [_taps(scr_x[h, pl.ds(base, STRIP + 2 * HALO), :], kk, STRIP + HALO) for h in range(2)]
            gate, val = _fir(taps[0], ws[0]), _fir(taps[1], ws[1])
            dact = scr_d[pl.ds(base, STRIP + HALO), :]
            s = _sigmoid(gate)
            dconv = (dact * val * (s * (1.0 + gate * (1.0 - s))), dact * (gate * s))
            out = []
            for h in range(2):
                du_ref[h, pl.ds(base, STRIP), :] = _fir_transposed(dconv[h], ws[h], STRIP).astype(du_ref.dtype)
                d0 = dconv[h][0:STRIP, :]
                out += [c + _fold8(d0 * tap[0:STRIP, :]) for c, tap in zip(carry[h * kk:(h + 1) * kk], taps[h])]
            return tuple(out)

        sums = _strips(tm, strip, tuple(jnp.zeros((SUBLANES, tc), F32) for _ in range(2 * kk)))
        for h in range(2):
            for j in range(kk):
                dw_ref[h, j:j + 1, :] += jnp.sum(sums[h * kk + j], axis=0, keepdims=True)

    return pl.pallas_call(
        body, name=name, grid=(f // tc, nt),
        in_specs=[pl.BlockSpec((2, HALO, tc), lambda j, i: (0, jnp.maximum(i * nh - 1, 0), j)),
                  pl.BlockSpec((2, tm, tc), lambda j, i: (0, i, j)),
                  pl.BlockSpec((2, HALO, tc), lambda j, i: (0, jnp.minimum((i + 1) * nh, last_h), j)),
                  pl.BlockSpec((2, kk, tc), lambda j, i: (0, 0, j)),
                  pl.BlockSpec((tm, tc), lambda j, i: (i, j)),
                  pl.BlockSpec((HALO, tc), lambda j, i: (jnp.minimum((i + 1) * nh, last_h), j))],
        out_specs=[pl.BlockSpec((2, tm, tc), lambda j, i: (0, i, j)),
                   pl.BlockSpec((2, SUBLANES, tc), lambda j, i: (0, 0, j))],
        out_shape=[jax.ShapeDtypeStruct((2, t, f), BF16), jax.ShapeDtypeStruct((2, SUBLANES, f), F32)],
        scratch_shapes=[pltpu.VMEM((2, tm + 2 * HALO, tc), F32), pltpu.VMEM((ext, tc), F32)],
        compiler_params=_cparams(("parallel", "arbitrary")),
    )(u, u, u, w, dact, dact)


def adamw(w, g, m, v, *, name):
    r, c = w.shape
    tr = _pick(r, max(SUBLANES, (2 * 1024 * 1024) // (4 * c) // SUBLANES * SUBLANES), SUBLANES)

    def body(w_ref, g_ref, m_ref, v_ref, d_ref, nm_ref, nv_ref):
        gv = g_ref[...]
        nm = ADAM_B1 * m_ref[...] + (1.0 - ADAM_B1) * gv
        nv = ADAM_B2 * v_ref[...] + (1.0 - ADAM_B2) * (gv * gv)
        m_hat = nm / (1.0 - ADAM_B1 ** ADAM_STEP)
        v_hat = nv / (1.0 - ADAM_B2 ** ADAM_STEP)
        d_ref[...] = -ADAM_LR * (m_hat / (jnp.sqrt(v_hat) + ADAM_EPS) + ADAM_WD * w_ref[...])
        nm_ref[...] = nm
        nv_ref[...] = nv

    blk = pl.BlockSpec((tr, c), lambda i: (i, 0))
    shp = jax.ShapeDtypeStruct((r, c), F32)
    return pl.pallas_call(
        body, name=name, grid=(r // tr,),
        in_specs=[blk] * 4, out_specs=[blk] * 3, out_shape=[shp] * 3,
        compiler_params=_cparams(("parallel",)),
    )(w, g, m, v)


def cast_into_slot(x, slot, n_slots, *, name):
    r, c = x.shape
    tr = _pick(r, max(16, (2 * 1024 * 1024) // (4 * c) // 16 * 16), 16)

    def body(s_ref, x_ref, o_ref):
        o_ref[...] = x_ref[...].astype(o_ref.dtype)

    grid_spec = pltpu.PrefetchScalarGridSpec(
        num_scalar_prefetch=1, grid=(r // tr,),
        in_specs=[pl.BlockSpec((tr, c), lambda i, s: (i, 0))],
        out_specs=pl.BlockSpec((None, tr, c), lambda i, s: (s[0], i, 0)))
    return pl.pallas_call(
        body, name=name, grid_spec=grid_spec,
        out_shape=jax.ShapeDtypeStruct((n_slots, r, c), BF16),
        compiler_params=_cparams(("arbitrary",)),
    )(jnp.reshape(slot, (1,)).astype(jnp.int32), x)


class Layout:
    def __init__(self, d):
        self.d = d
        self.h_dn = d // DN_DK
        self.h_m2 = d // M2_P
        self.gn = M2_GROUPS * M2_N
        self.w_xbc = d + 2 * self.gn
        self.z_off = 3 * d
        self.m2z_off = 4 * d
        self.xbc_off = 5 * d
        self.small_off = 5 * d + self.w_xbc
        self.n_small = 2 * self.h_dn + self.h_m2
        self.p = self.small_off + LANES
        self.p_orig = self.small_off + self.n_small

    def w_in_of_slots(self, slab):
        n_slots, rows, cs = slab.shape
        pieces = []
        for o0, _, ln in sorted(self._segments(), key=lambda seg: seg[1]):
            for s in range(n_slots):
                lo, hi = max(o0, s * cs), min(o0 + ln, (s + 1) * cs)
                if lo < hi:
                    pieces.append(slab[s, :, lo - s * cs:hi - s * cs])
        pieces.append(jnp.zeros((rows, LANES - self.n_small), slab.dtype))
        return jnp.concatenate(pieces, axis=1)

    def _segments(self):
        d, s2, so = self.d, 2 * self.h_dn, self.small_off
        return [(0, 0, 4 * d), (4 * d, so, s2), (4 * d + s2, 4 * d, so - 4 * d),
                (self.p_orig - self.h_m2, so + s2, self.h_m2)]

    def slots_of_w_in(self, w, n_slots):
        cs = self.p_orig // n_slots
        slots = []
        for s in range(n_slots):
            pieces = []
            for o0, k0, ln in self._segments():
                lo, hi = max(o0, s * cs), min(o0 + ln, (s + 1) * cs)
                if lo < hi:
                    pieces.append(w[:, k0 + lo - o0:k0 + hi - o0])
            slots.append(jnp.concatenate(pieces, axis=1))
        return jnp.stack(slots, axis=0)


def _cols(small, lo, hi):
    return jnp.transpose(small[:, lo:hi])[..., None]


def local_step(h0, target, wts, slabs, chip, core, *, n_pad, n_meta):
    t, d = h0.shape
    lay = Layout(d)
    hd, hm = lay.h_dn, lay.h_m2
    zeros_b = jnp.zeros((1, 3 * d), F32)
    r3 = lambda v, n: v.reshape(n, 1, 1)
    dn_al, dn_db = r3(wts["dn_a_log"], hd), r3(wts["dn_dt_bias"], hd)
    dn_nw = wts["dn_norm_w"].reshape(1, 1, DN_DK)
    m2_al, m2_db, m2_dk = r3(wts["m2_a_log"], hm), r3(wts["m2_dt_bias"], hm), r3(wts["m2_d"], hm)

    hn1 = rmsnorm_forward(h0, wts["norm_mix_w"], name="norm_mix")
    proj = matmul(hn1, wts["w_in"], mode="nn", out_dtype=F32, name="in_proj", tn=640)
    act_qkv = conv_silu(proj, wts["dn_conv_w"], zeros_b, x_off=0, name="dn_conv")
    act_xbc = conv_silu(proj, wts["m2_conv_w"], wts["m2_conv_b"], x_off=lay.xbc_off, name="m2_conv")
    small_at = dict(small_off=lay.small_off)
    dt_at = dict(small_off=lay.small_off, dt_lane=2 * hd)
    mixed, s_dn, inv_dn, ffn_up_all, ffn_down_all = gdn_forward(
        act_qkv, proj, dn_al, dn_db, dn_nw, n_pad=n_pad, z_off=lay.z_off, out_cols=2 * d,
        gather=[slabs["ffn_up"], slabs["ffn_down"]], **small_at)
    mixed, s_m2, w_out_all = ssd_forward(act_xbc, proj, m2_al, m2_db, m2_dk, wts["m2_norm_w"], mixed,
                                         n_pad=n_pad, z_off=lay.m2z_off, gather=[slabs["w_out"]], **dt_at)
    wts = dict(wts, w_out=w_out_all.reshape(-1, d), ffn_down=ffn_down_all.reshape(-1, d),
               ffn_up=jnp.transpose(ffn_up_all, (1, 0, 2)).reshape(d, -1))
    h1 = matmul(mixed, wts["w_out"], mode="nn", out_dtype=F32, name="out_proj", tk=2 * d, residual=h0)
    hn2 = rmsnorm_forward(h1, wts["norm_ffn_w"], name="norm_ffn")
    up = matmul(hn2, wts["ffn_up"], mode="nn", out_dtype=F32, name="ffn_up", tn=1024, out_shards=2)
    kf, f = wts["ffn_conv_w"].shape[0], wts["ffn_conv_w"].shape[1] // 2
    w_glu = jnp.transpose(wts["ffn_conv_w"].reshape(kf, 2, f), (1, 0, 2))
    act = conv_glu(up, w_glu, name="ffn_conv")
    h2 = matmul(act, wts["ffn_down"], mode="nn", out_dtype=F32, name="ffn_down", tk=f // 2, residual=h1)
    loss, dh2, dh2_16, d_nfw = loss_head(h2, wts["norm_final_w"].reshape(1, d), target, n_skip=n_pad + n_meta)

    g = {}
    d_act = matmul(dh2_16, wts["ffn_down"], mode="nt", out_dtype=F32, name="d_ffn_act", tn=512)
    g["ffn_down"] = matmul(act, dh2_16, mode="tn", out_dtype=F32, name="dw_ffn_down", tm=512, tn=1024, tk=t)
    dup, d_fcw = conv_glu_backward(up, w_glu, d_act, name="d_ffn_conv")
    g["ffn_conv_w"] = jnp.transpose(d_fcw[:, :kf], (1, 0, 2)).reshape(kf, 2 * f)
    dhn2 = matmul(dup, wts["ffn_up"], mode="nt", out_dtype=F32, name="d_norm_ffn_out", tk=f, a_shards=2)
    g["ffn_up"] = matmul(hn2, dup, mode="tn", out_dtype=F32, name="dw_ffn_up", tm=512, tn=1408, tk=t,
                         b_shards=2, out_shards=4)
    dh1, dh1_16, g["norm_ffn_w"] = rmsnorm_backward(h1, wts["norm_ffn_w"], dhn2, dh2, n_pad=n_pad, name="d_norm_ffn")
    d_mixed = matmul(dh1_16, wts["w_out"], mode="nt", out_dtype=F32, name="d_mixed", tn=512)
    g["w_out"] = matmul(mixed, dh1_16, mode="tn", out_dtype=F32, name="dw_out", tm=512, tn=1024, tk=t)

    early = ("w_out", "ffn_up", "ffn_down")
    slots = [g[k] if g[k].ndim == 3 else g[k].reshape(N_CHIPS, -1, g[k].shape[1]) for k in early]
    dq, dk, dv, dproj, dsm_dn, g_al, g_db, g_nw, *from_sibling = gdn_backward(
        act_qkv, proj, dn_al, dn_db, dn_nw, s_dn, inv_dn, d_mixed, n_pad=n_pad, z_off=lay.z_off, exchange=slots,
        **small_at)
    g["dn_a_log"], g["dn_dt_bias"] = g_al.reshape(1, hd), g_db.reshape(1, hd)
    g["dn_norm_w"] = jnp.sum(g_nw, axis=0)
    p16, own = [], []
    for k, slot, rb in zip(early, slots, from_sibling):
        a, b = pair_add(slot, rb, chip, core, name="grad_pair_add_" + k)
        p16.append(a)
        own.append(b)
    dxs, dbm, dcm, dproj, dsm_m2, g_al, g_db, g_dk, g["m2_norm_w"], *from_chips = ssd_backward(
        act_xbc, proj, m2_al, m2_db, m2_dk, wts["m2_norm_w"], s_m2, d_mixed, dproj,
        n_pad=n_pad, z_off=lay.m2z_off, exchange=p16, **dt_at)
    for k, o, q in zip(early, own, from_chips):
        g[k] = (o, q)
    g["m2_a_log"], g["m2_dt_bias"], g["m2_d"] = g_al.reshape(1, hm), g_db.reshape(1, hm), g_dk.reshape(1, hm)

    kc = wts["dn_conv_w"].shape[0]
    dw_parts = []
    for idx, dpart in enumerate((dq, dk, dv)):
        dproj, dw, _ = conv_silu_backward(proj, wts["dn_conv_w"][:, idx * d:(idx + 1) * d], zeros_b[:, :d], dpart,
                                          dproj, x_off=idx * d, name=f"d_dn_conv{idx}")
        dw_parts.append(dw[:kc])
    g["dn_conv_w"] = jnp.concatenate(dw_parts, axis=1)
    dw_parts, db_parts = [], []
    off = 0
    for idx, dpart in enumerate((dxs, dbm, dcm)):
        wd = dpart.shape[1]
        dproj, dw, db = conv_silu_backward(proj, wts["m2_conv_w"][:, off:off + wd], wts["m2_conv_b"][:, off:off + wd],
                                           dpart, dproj, x_off=lay.xbc_off + off, name=f"d_m2_conv{idx}")
        dw_parts.append(dw[:kc])
        db_parts.append(db[:1])
        off += wd
    g["m2_conv_w"] = jnp.concatenate(dw_parts, axis=1)
    g["m2_conv_b"] = jnp.concatenate(db_parts, axis=1)
    dsmall = jnp.sum(dsm_dn, axis=0) + jnp.sum(dsm_m2, axis=0)
    dproj = lax.dynamic_update_slice(dproj, dsmall.astype(BF16), (0, lay.small_off))

    dw_in = matmul(hn1, dproj, mode="tn", out_dtype=F32, name="dw_in", tm=512, tn=640, tk=t)
    w_in_slots = lay.slots_of_w_in(dw_in, N_CHIPS)
    (from_sibling,) = pair_exchange([w_in_slots], name="grad_pair_exchange_w_in")
    p16_w_in, own_w_in = pair_add(w_in_slots, from_sibling, chip, core, name="grad_pair_add_w_in")
    dhn1, from_chips = matmul(dproj, wts["w_in"], mode="nt", out_dtype=F32, name="d_norm_mix_out", tk=lay.p // 3,
                              exchange=[p16_w_in])
    g["w_in"] = (own_w_in, from_chips)
    dh0, _, g["norm_mix_w"] = rmsnorm_backward(h0, wts["norm_mix_w"], dhn1, dh1, n_pad=n_pad, name="d_norm_mix")
    g["norm_final_w"] = d_nfw
    return loss[0, 0], dh0, g


MESH = pl.DeviceIdType.MESH
ANY = pl.BlockSpec(memory_space=pl.ANY)
N_CHIPS = 4
N_DEV = 8


def _mesh_pos():
    return lax.axis_index("x"), lax.axis_index("y"), lax.axis_index("c")


def _other_chips(x, y):
    return [(1 - x, y), (x, 1 - y), (1 - x, 1 - y)]


def _rcopy(src, dst, send_sems, recv_sems, k, to):
    return pltpu.make_async_remote_copy(src_ref=src, dst_ref=dst, send_sem=send_sems.at[k], recv_sem=recv_sems.at[k],
                                        device_id=to, device_id_type=MESH)


class GatherJob:
    def __init__(self, slabs, send_sems, recv_sems):
        self.slabs, self.send, self.recv = slabs, send_sems, recv_sems
        self.x, self.y, self.c = _mesh_pos()

    @staticmethod
    def sems(n):
        return [pltpu.SemaphoreType.DMA((6 * n,)), pltpu.SemaphoreType.DMA((6 * n,))]

    def _pieces(self):
        x, y, c = self.x, self.y, self.c
        for a, slab in enumerate(self.slabs):
            half = slab.shape[1] // 2
            for j, (px, py) in enumerate(_other_chips(x, y)):
                yield a, j, (px, py), slab, pl.ds(c * half, half), pl.ds((1 - c) * half, half)

    def _ici(self, a, j, chip, ref):
        return _rcopy(ref, ref, self.send, self.recv, 6 * a + j, (chip[0], chip[1], self.c))

    def _d2d(self, a, j, ref):
        return _rcopy(ref, ref, self.send, self.recv, 6 * a + 3 + j, (self.x, self.y, 1 - self.c))

    def begin(self):
        for a, j, chip, slab, mine, _ in self._pieces():
            self._ici(a, j, chip, slab.at[2 * self.x + self.y, mine]).start()

    def pass_on(self):
        for a, j, chip, slab, mine, _ in self._pieces():
            landed = slab.at[2 * chip[0] + chip[1], mine]
            self._ici(a, j, chip, landed).wait_recv()
            self._d2d(a, j, landed).start()

    def end(self):
        for a, j, chip, slab, mine, theirs in self._pieces():
            self._d2d(a, j, slab.at[2 * chip[0] + chip[1], theirs]).wait_recv()
        for a, j, chip, slab, mine, _ in self._pieces():
            self._ici(a, j, chip, slab.at[2 * self.x + self.y, mine]).wait_send()
            self._d2d(a, j, slab.at[2 * chip[0] + chip[1], mine]).wait_send()


class ChipExchangeJob:
    def __init__(self, parts, outs, send_sems, recv_sems):
        self.parts, self.outs, self.send, self.recv = parts, outs, send_sems, recv_sems
        self.x, self.y, self.c = _mesh_pos()

    @staticmethod
    def sems(n):
        return [pltpu.SemaphoreType.DMA((3 * n,)), pltpu.SemaphoreType.DMA((3 * n,))]

    def _copies(self):
        for a, (part, out) in enumerate(zip(self.parts, self.outs)):
            for j, (px, py) in enumerate(_other_chips(self.x, self.y)):
                yield _rcopy(part.at[2 * px + py], out.at[j], self.send, self.recv, 3 * a + j, (px, py, self.c))

    def begin(self):
        for cp in self._copies():
            cp.start()

    def end(self):
        for cp in self._copies():
            cp.wait()


class PairExchangeJob:
    def __init__(self, grads, outs, send_sems, recv_sems):
        self.grads, self.outs, self.send, self.recv = grads, outs, send_sems, recv_sems
        self.x, self.y, self.c = _mesh_pos()

    @staticmethod
    def sems(n):
        return [pltpu.SemaphoreType.DMA((n,)), pltpu.SemaphoreType.DMA((n,))]

    def _copies(self):
        for a, (grad, out) in enumerate(zip(self.grads, self.outs)):
            half = grad.shape[1] // 2
            yield _rcopy(grad.at[:, pl.ds((1 - self.c) * half, half), :], out, self.send, self.recv, a,
                         (self.x, self.y, 1 - self.c))

    def begin(self):
        for cp in self._copies():
            cp.start()

    def end(self):
        for cp in self._copies():
            cp.wait()


def gather_shards(slabs, *, name):
    n = len(slabs)

    def body(*refs):
        job = GatherJob(refs[n:2 * n], *refs[2 * n:])
        job.begin()
        job.pass_on()
        job.end()

    return pl.pallas_call(
        body, name=name,
        in_specs=[ANY] * n, out_specs=[ANY] * n,
        out_shape=[jax.ShapeDtypeStruct(s.shape, s.dtype) for s in slabs],
        input_output_aliases={a: a for a in range(n)},
        scratch_shapes=GatherJob.sems(n),
        compiler_params=pltpu.CompilerParams(has_side_effects=True),
    )(*slabs)


def _half_rows(s):
    return jax.ShapeDtypeStruct((s.shape[0], s.shape[1] // 2, s.shape[2]), s.dtype)


def pair_exchange(grads, *, name):
    n = len(grads)

    def body(*refs):
        job = PairExchangeJob(refs[:n], refs[n:2 * n], *refs[2 * n:])
        job.begin()
        job.end()

    return pl.pallas_call(
        body, name=name, in_specs=[ANY] * n, out_specs=[ANY] * n,
        out_shape=[_half_rows(s) for s in grads],
        scratch_shapes=PairExchangeJob.sems(n),
        compiler_params=pltpu.CompilerParams(has_side_effects=True),
    )(*grads)


def chip_exchange(parts, *, name):
    n = len(parts)

    def body(*refs):
        job = ChipExchangeJob(refs[:n], refs[n:2 * n], *refs[2 * n:])
        job.begin()
        job.end()

    return pl.pallas_call(
        body, name=name, in_specs=[ANY] * n, out_specs=[ANY] * n,
        out_shape=[jax.ShapeDtypeStruct((3,) + s.shape[1:], s.dtype) for s in parts],
        scratch_shapes=ChipExchangeJob.sems(n),
        compiler_params=pltpu.CompilerParams(has_side_effects=True),
    )(*parts)


def pair_join(wholes, *, name):
    n = len(wholes)

    def body(*refs):
        outs = refs[n:2 * n]
        send_sems, recv_sems = refs[2 * n:]
        x, y, c = _mesh_pos()
        cps = []
        for a in range(n):
            half = outs[a].shape[0] // 2
            rows = outs[a].at[pl.ds(c * half, half)]
            cp = _rcopy(rows, rows, send_sems, recv_sems, a, (x, y, 1 - c))
            cp.start()
            cps.append(cp)
        for a, cp in enumerate(cps):
            cp.wait_send()
            half = outs[a].shape[0] // 2
            theirs = outs[a].at[pl.ds((1 - c) * half, half)]
            _rcopy(theirs, theirs, send_sems, recv_sems, a, (x, y, 1 - c)).wait_recv()

    return pl.pallas_call(
        body, name=name, in_specs=[ANY] * n, out_specs=[ANY] * n,
        out_shape=[jax.ShapeDtypeStruct(s.shape, s.dtype) for s in wholes],
        input_output_aliases={a: a for a in range(n)},
        scratch_shapes=[pltpu.SemaphoreType.DMA((n,)), pltpu.SemaphoreType.DMA((n,))],
        compiler_params=pltpu.CompilerParams(has_side_effects=True),
    )(*wholes)


def gather_all(v, *, name):
    def body(in_ref, out_ref, send_sems, recv_sems, local_sem):
        x, y, c = _mesh_pos()
        mine = out_ref.at[4 * x + 2 * y + c]
        lc = pltpu.make_async_copy(in_ref, mine, local_sem)
        lc.start()
        cps = []
        for k in range(1, N_DEV):
            flip = lambda v, bit: 1 - v if (k >> bit) & 1 else v
            cp = _rcopy(in_ref, mine, send_sems, recv_sems, k - 1, (flip(x, 2), flip(y, 1), flip(c, 0)))
            cp.start()
            cps.append(cp)
        for cp in cps:
            cp.wait()
        lc.wait()

    return pl.pallas_call(
        body, name=name, in_specs=[ANY], out_specs=ANY,
        out_shape=jax.ShapeDtypeStruct((N_DEV,) + v.shape, v.dtype),
        scratch_shapes=[pltpu.SemaphoreType.DMA((N_DEV - 1,)), pltpu.SemaphoreType.DMA((N_DEV - 1,)),
                        pltpu.SemaphoreType.DMA(())],
        compiler_params=pltpu.CompilerParams(has_side_effects=True),
    )(v)


def _sum_tile(rows, cols):
    return _pick(rows, max(16, (1024 * 1024) // (4 * cols) // 16 * 16), 16)


def pair_add(g, rb, chip, c, *, name):
    _, r, cols = g.shape
    half = r // 2
    tr = _sum_tile(half, cols)
    nrt = half // tr

    def body(s_ref, g_ref, rb_ref, p16_ref, own_ref):
        v = g_ref[...] + rb_ref[...]
        p16_ref[...] = v.astype(p16_ref.dtype)

        @pl.when(pl.program_id(1) == s_ref[0])
        def _():
            own_ref[...] = v

    grid_spec = pltpu.PrefetchScalarGridSpec(
        num_scalar_prefetch=1, grid=(nrt, N_CHIPS),
        in_specs=[pl.BlockSpec((None, tr, cols), lambda i, k, s: (k, s[1] * nrt + i, 0)),
                  pl.BlockSpec((None, tr, cols), lambda i, k, s: (k, i, 0))],
        out_specs=[pl.BlockSpec((None, tr, cols), lambda i, k, s: (k, i, 0)),
                   pl.BlockSpec((tr, cols), lambda i, k, s: (i, 0))])
    return pl.pallas_call(
        body, name=name, grid_spec=grid_spec,
        out_shape=[jax.ShapeDtypeStruct((N_CHIPS, half, cols), BF16), jax.ShapeDtypeStruct((half, cols), F32)],
        compiler_params=_cparams(("arbitrary", "arbitrary")),
    )(jnp.stack([chip, c]).astype(jnp.int32), g, rb)


def chip_add(own, q, c, *, name):
    r, cols = own.shape
    tr = _sum_tile(r, cols)
    nrt = r // tr

    def body(s_ref, own_ref, q_ref, o_ref):
        o_ref[...] = ((own_ref[...] + q_ref[0].astype(F32)) + q_ref[1].astype(F32)) + q_ref[2].astype(F32)

    grid_spec = pltpu.PrefetchScalarGridSpec(
        num_scalar_prefetch=1, grid=(nrt,),
        in_specs=[pl.BlockSpec((tr, cols), lambda i, s: (i, 0)), pl.BlockSpec((3, tr, cols), lambda i, s: (0, i, 0))],
        out_specs=pl.BlockSpec((tr, cols), lambda i, s: (s[0] * nrt + i, 0)))
    return pl.pallas_call(
        body, name=name, grid_spec=grid_spec,
        out_shape=jax.ShapeDtypeStruct((2 * r, cols), F32),
        compiler_params=_cparams(("arbitrary",)),
    )(jnp.reshape(c, (1,)).astype(jnp.int32), own, q)


def sum_slots(v, *, name):
    n, r, cols = v.shape
    tr = _sum_tile(r, cols)

    def body(v_ref, o_ref):
        acc = v_ref[0]
        for k in range(1, n):
            acc = acc + v_ref[k]
        o_ref[...] = acc

    return pl.pallas_call(
        body, name=name, grid=(r // tr,),
        in_specs=[pl.BlockSpec((n, tr, cols), lambda i: (0, i, 0))],
        out_specs=pl.BlockSpec((tr, cols), lambda i: (i, 0)),
        out_shape=jax.ShapeDtypeStruct((r, cols), F32),
        compiler_params=_cparams(("parallel",)),
    )(v)


PACK_ROWS = 16


def _pack(arrays):
    parts = []
    for a in arrays:
        flat = a.reshape(-1).astype(F32)
        size = PACK_ROWS * LANES
        pad = (-flat.shape[0]) % size
        parts.append(jnp.pad(flat, (0, pad)))
    return jnp.concatenate(parts).reshape(-1, LANES)


def _unpack(slab, shapes):
    out, row = [], 0
    for shp in shapes:
        n = 1
        for s in shp:
            n *= s
        rows = -(-n // (PACK_ROWS * LANES)) * PACK_ROWS
        out.append(slab[row:row + rows].reshape(-1)[:n].reshape(shp))
        row += rows
    return out


WEIGHT_NAMES = ("meta_tokens", "norm_mix_w", "w_in", "dn_conv_w", "dn_a_log", "dn_dt_bias", "dn_norm_w", "m2_conv_w",
                "m2_conv_b", "m2_a_log", "m2_dt_bias", "m2_d", "m2_norm_w", "w_out", "norm_ffn_w", "ffn_up",
                "ffn_conv_w", "ffn_down", "norm_final_w")
BIG = ("w_in", "w_out", "ffn_up", "ffn_down")
BIG_COLUMN_SHARDED = ("w_in", "ffn_up")
SMALL_SHARDED = ("meta_tokens", "dn_conv_w", "m2_conv_w", "ffn_conv_w")
SMALL = tuple(n for n in WEIGHT_NAMES if n not in BIG)


def kernel(x, meta_tokens, norm_mix_w, w_in, dn_conv_w, dn_a_log, dn_dt_bias, dn_norm_w, m2_conv_w, m2_conv_b, m2_a_log, m2_dt_bias, m2_d, m2_norm_w, w_out, norm_ffn_w, ffn_up, ffn_conv_w, ffn_down, norm_final_w, loss_target, m_meta_tokens, m_norm_mix_w, m_w_in, m_dn_conv_w, m_dn_a_log, m_dn_dt_bias, m_dn_norm_w, m_m2_conv_w, m_m2_conv_b, m_m2_a_log, m_m2_dt_bias, m_m2_d, m_m2_norm_w, m_w_out, m_norm_ffn_w, m_ffn_up, m_ffn_conv_w, m_ffn_down, m_norm_final_w, v_meta_tokens, v_norm_mix_w, v_w_in, v_dn_conv_w, v_dn_a_log, v_dn_dt_bias, v_dn_norm_w, v_m2_conv_w, v_m2_conv_b, v_m2_a_log, v_m2_dt_bias, v_m2_d, v_m2_norm_w, v_w_out, v_norm_ffn_w, v_ffn_up, v_ffn_conv_w, v_ffn_down, v_norm_final_w):
    args = tuple(locals().values())
    nw = len(WEIGHT_NAMES)
    wt = dict(zip(WEIGHT_NAMES, args[1:1 + nw]))
    mom = dict(zip(WEIGHT_NAMES, args[2 + nw:2 + 2 * nw]))
    var = dict(zip(WEIGHT_NAMES, args[2 + 2 * nw:2 + 3 * nw]))
    xi, yi, ci = _mesh_pos()
    chip = 2 * xi + yi
    seq, d = x.shape[1], x.shape[2]
    n_meta = wt["meta_tokens"].shape[0]
    n_pad = (-(n_meta + seq)) % ROW_ALIGN
    lay = Layout(d)

    shard2d = {k: wt[k].reshape(wt[k].shape[-2:]) for k in BIG}
    small_local = [wt[k].reshape(wt[k].shape[-2:]) for k in SMALL_SHARDED]
    small_slab = _pack(small_local)
    small_slab = lax.dynamic_update_slice(jnp.zeros((N_CHIPS,) + small_slab.shape, F32), small_slab[None], (chip, 0, 0))
    slabs = {k: cast_into_slot(shard2d[k], chip, N_CHIPS, name="cast_" + k) for k in BIG}
    w_in_all, small_all = gather_shards([slabs.pop("w_in"), small_slab], name="gather_w_in")
    full = {"w_in": lay.w_in_of_slots(w_in_all)}
    per_chip = [_unpack(small_all[s], [a.shape for a in small_local]) for s in range(N_CHIPS)]
    for idx, k in enumerate(SMALL_SHARDED):
        full[k] = jnp.concatenate([per_chip[s][idx] for s in range(N_CHIPS)], axis=-1)
    for k in SMALL:
        if k not in SMALL_SHARDED:
            full[k] = wt[k]

    h0 = jnp.concatenate([jnp.zeros((n_pad, d), F32), full["meta_tokens"], x[0]], axis=0)
    target = jnp.concatenate([jnp.zeros((n_pad + n_meta, d), F32), loss_target[0]], axis=0)
    loss_local, dh0, g = local_step(h0, target, full, slabs, chip, ci, n_pad=n_pad, n_meta=n_meta)
    grad_x = dh0[n_pad + n_meta:][None]
    g["meta_tokens"] = dh0[n_pad:n_pad + n_meta]
    loss = lax.psum(loss_local, ("x", "y", "c"))

    wholes = [chip_add(*g[k], ci, name="grad_chip_add_" + k) for k in BIG]
    grad = dict(zip(BIG, pair_join(wholes, name="grad_pair_join")))

    small_shapes = [g[k].shape for k in SMALL]
    summed = sum_slots(gather_all(_pack([g[k] for k in SMALL]), name="small_grad_gather"), name="small_grad_sum")
    for k, v in zip(SMALL, _unpack(summed, small_shapes)):
        if k in SMALL_SHARDED:
            width = wt[k].shape[-1]
            v = lax.dynamic_slice_in_dim(v, chip * width, width, axis=v.ndim - 1)
        grad[k] = v

    delta, new_m, new_v = {}, {}, {}
    for k in BIG:
        shp = shard2d[k].shape
        delta[k], new_m[k], new_v[k] = adamw(shard2d[k], grad[k], mom[k].reshape(shp), var[k].reshape(shp),
                                             name="adamw_" + k)
    shapes = [wt[k].shape for k in SMALL]
    packed = adamw(_pack([wt[k] for k in SMALL]), _pack([grad[k] for k in SMALL]), _pack([mom[k] for k in SMALL]),
                   _pack([var[k] for k in SMALL]), name="adamw_small")
    for res, slab in zip((delta, new_m, new_v), packed):
        for k, v in zip(SMALL, _unpack(slab, shapes)):
            res[k] = v
    outs = [loss, grad_x]
    for res in (grad, delta, new_m, new_v):
        outs += [res[k].reshape(wt[k].shape) for k in WEIGHT_NAMES]
    return tuple(outs)
```

```python
import functools
import math

import jax
import jax.numpy as jnp
from jax import lax
from jax.experimental import pallas as pl
from jax.experimental.pallas import tpu as pltpu

F32 = jnp.float32
BF16 = jnp.bfloat16
HI = lax.Precision.HIGHEST

CHUNK = 64
ROW_ALIGN = 128
NORM_EPS = 1e-6
DN_DK = 128
M2_P = 64
M2_N = 128
M2_GROUPS = 4
HEAD_BLOCK = 8
VMEM_LIMIT = 56 * 1024 * 1024

ADAM_LR, ADAM_B1, ADAM_B2, ADAM_EPS, ADAM_WD, ADAM_STEP = 0.001, 0.9, 0.999, 1e-08, 0.01, 10


def _cparams(sem=None):
    return pltpu.CompilerParams(dimension_semantics=sem, vmem_limit_bytes=VMEM_LIMIT)


def _silu(x):
    return x / (1.0 + jnp.exp(-x))


def _sigmoid(x):
    return 1.0 / (1.0 + jnp.exp(-x))


def _softplus(x):
    return jnp.maximum(x, 0.0) + jnp.log(1.0 + jnp.exp(-jnp.abs(x)))


def _tri_masks():
    r = lax.broadcasted_iota(jnp.int32, (CHUNK, CHUNK), 0)
    c = lax.broadcasted_iota(jnp.int32, (CHUNK, CHUNK), 1)
    return (r >= c)[None], (r > c)[None], (r == c)[None]


def _col2row(col, eye):
    return jnp.sum(jnp.where(eye, col, 0.0), axis=1, keepdims=True)


def _cumsum_col(col, causal, eye):
    row = _col2row(col, eye)
    return jnp.sum(jnp.where(causal, row, 0.0), axis=2, keepdims=True)


def _bdot(a, b, dims, precision=None):
    (ca, cb) = dims
    if precision is None:
        a = a.astype(BF16)
        b = b.astype(BF16)
    return lax.dot_general(a, b, (((ca,), (cb,)), ((0,), (0,))), precision=precision,
                           preferred_element_type=F32)


def _dot3(a, b, ca, cb):
    def split(v):
        hi = v.astype(BF16)
        return hi, (v - hi.astype(F32)).astype(BF16)

    def dot(p, q):
        return lax.dot_general(p, q, (((ca,), (cb,)), ((0,), (0,))), preferred_element_type=F32)

    (ah, al), (bh, bl) = split(a), split(b)
    return dot(ah, bh) + (dot(ah, bl) + dot(al, bh))


@jax.custom_vjp
def _bmm3(a, b):
    return _dot3(a, b, 2, 1)


def _bmm3_fwd(a, b):
    return _dot3(a, b, 2, 1), (a, b)


def _bmm3_bwd(res, g):
    a, b = res
    return _dot3(g, b, 2, 2), _dot3(a, g, 1, 1)


_bmm3.defvjp(_bmm3_fwd, _bmm3_bwd)


def _unit_lower_inverse(a_mat, eye):
    inv = jnp.where(eye, 1.0, 0.0) - a_mat
    pw = a_mat
    n = 2
    while n < CHUNK:
        pw = _bmm3(pw, pw)
        inv = inv + _bmm3(inv, pw)
        n *= 2
    return inv


@jax.custom_vjp
def _known_inverse(a_mat, inv):
    return inv


def _known_inverse_fwd(a_mat, inv):
    return inv, inv


def _known_inverse_bwd(inv, g):
    return -_dot3(_dot3(inv, g, 1, 1), inv, 2, 2), jnp.zeros_like(inv)


_known_inverse.defvjp(_known_inverse_fwd, _known_inverse_bwd)


def _gdn_step(state, qa, ka, va, z, braw, araw, a_log, dt_bias, norm_w, vm, inv=None, want_inv=False):
    causal, strict, eye = _tri_masks()
    qa, ka, va = qa * vm, ka * vm, va * vm
    q = qa * lax.rsqrt(jnp.sum(qa * qa, -1, keepdims=True) + NORM_EPS) * (DN_DK ** -0.5)
    k = ka * lax.rsqrt(jnp.sum(ka * ka, -1, keepdims=True) + NORM_EPS)
    beta = _sigmoid(braw) * vm
    g = -jnp.exp(a_log) * _softplus(araw + dt_bias) * vm
    gcum = _cumsum_col(g, causal, eye)
    grow = _col2row(gcum, eye)
    decay = jnp.where(causal, jnp.exp(jnp.where(causal, gcum - grow, 0.0)), 0.0)
    kk = _bdot(k, k, (2, 2))
    a_mat = jnp.where(strict, beta * kk * decay, 0.0)
    tinv = _unit_lower_inverse(a_mat, eye) if inv is None else _known_inverse(a_mat, inv)
    egc = jnp.exp(gcum)
    u = _bmm3(tinv, va * beta)
    w = _bmm3(tinv, k * (beta * egc))
    v_new = u - _bdot(w, state, (2, 1))
    o_inter = _bdot(q * egc, state, (2, 1))
    qk = _bdot(q, k, (2, 2)) * decay
    o = o_inter + _bdot(qk, v_new, (2, 1))
    g_last = jnp.sum(g, axis=1, keepdims=True)
    new_state = state * jnp.exp(g_last) + _bdot(k * jnp.exp(g_last - gcum), v_new, (1, 1))
    o = o * lax.rsqrt(jnp.mean(o * o, -1, keepdims=True) + NORM_EPS) * norm_w * _silu(z)
    return (new_state, o, tinv) if want_inv else (new_state, o)


def _valid_rows(chunk, n_pad):
    r = chunk * CHUNK + lax.broadcasted_iota(jnp.int32, (1, CHUNK, 1), 1)
    return jnp.where(r >= n_pad, 1.0, 0.0).astype(F32)


def _split_heads(x, n, w):
    return jnp.stack([x[:, i * w:(i + 1) * w] for i in range(n)], axis=0)


def _merge_heads(x):
    return jnp.concatenate([x[i] for i in range(x.shape[0])], axis=-1)


def _run_beside(job, step, n_steps):
    @pl.when(step == 0)
    def _():
        job.begin()

    if hasattr(job, "pass_on"):
        @pl.when(step == (7 * n_steps) // 8)
        def _():
            job.pass_on()

    @pl.when(step == n_steps - 1)
    def _():
        job.end()


def _take_cols(slab, first, n):
    lane = lax.broadcasted_iota(jnp.int32, slab.shape, 1)
    return jnp.stack([jnp.sum(jnp.where(lane == first + j, slab, 0.0), axis=1, keepdims=True) for j in range(n)],
                     axis=0)


def _put_cols(cols, first):
    lane = lax.broadcasted_iota(jnp.int32, (cols.shape[1], LANES), 1)
    out = jnp.zeros((cols.shape[1], LANES), F32)
    for j in range(cols.shape[0]):
        out = out + jnp.where(lane == first + j, cols[j], 0.0)
    return out


def gdn_forward(act_qkv, proj, a_log3, dt_bias3, norm_w3, *, n_pad, z_off, small_off, out_cols, gather=()):
    t, w3 = act_qkv.shape
    wdn = w3 // 3
    heads = wdn // DN_DK
    hb = min(HEAD_BLOCK, heads)
    nhb = heads // hb
    nc = t // CHUNK
    bw = hb * DN_DK
    nqb = wdn // bw
    ng = len(gather)

    def body(q_ref, k_ref, v_ref, z_ref, sm_ref, al_ref, db_ref, nw_ref, *rest):
        out_ref, sall_ref, inv_ref = rest[ng:ng + 3]
        st_ref = rest[2 * ng + 3]
        n = pl.program_id(1)
        if ng:
            _run_beside(GatherJob(rest[ng + 3:2 * ng + 3], *rest[2 * ng + 4:]), pl.program_id(0) * nc + n, nhb * nc)

        @pl.when(n == 0)
        def _():
            st_ref[...] = jnp.zeros_like(st_ref)

        state = st_ref[...]
        sall_ref[...] = state
        vm = _valid_rows(n, n_pad)
        head0 = pl.program_id(0) * hb
        new_state, o, tinv = _gdn_step(
            state, _split_heads(q_ref[...], hb, DN_DK), _split_heads(k_ref[...], hb, DN_DK),
            _split_heads(v_ref[...], hb, DN_DK), _split_heads(z_ref[...], hb, DN_DK),
            _take_cols(sm_ref[...], head0, hb), _take_cols(sm_ref[...], heads + head0, hb),
            al_ref[...], db_ref[...], nw_ref[...], vm, want_inv=True)
        st_ref[...] = new_state
        inv_ref[...] = tinv
        out_ref[...] = _merge_heads(o).astype(out_ref.dtype)

    par = pl.BlockSpec((hb, 1, 1), lambda h, n: (h, 0, 0))
    return pl.pallas_call(
        body, name="gdn_fwd",
        grid=(nhb, nc),
        in_specs=[pl.BlockSpec((CHUNK, bw), lambda h, n: (n, h)),
                  pl.BlockSpec((CHUNK, bw), lambda h, n: (n, nqb + h)),
                  pl.BlockSpec((CHUNK, bw), lambda h, n: (n, 2 * nqb + h)),
                  pl.BlockSpec((CHUNK, bw), lambda h, n: (n, z_off // bw + h)),
                  pl.BlockSpec((CHUNK, LANES), lambda h, n: (n, small_off // LANES)),
                  par, par,
                  pl.BlockSpec((1, 1, DN_DK), lambda h, n: (0, 0, 0))] + [ANY] * ng,
        out_specs=[pl.BlockSpec((CHUNK, bw), lambda h, n: (n, h)),
                   pl.BlockSpec((None, hb, DN_DK, DN_DK), lambda h, n: (n, h, 0, 0)),
                   pl.BlockSpec((None, hb, CHUNK, CHUNK), lambda h, n: (n, h, 0, 0))] + [ANY] * ng,
        out_shape=[jax.ShapeDtypeStruct((t, out_cols), BF16),
                   jax.ShapeDtypeStruct((nc, heads, DN_DK, DN_DK), F32),
                   jax.ShapeDtypeStruct((nc, heads, CHUNK, CHUNK), F32)]
        + [jax.ShapeDtypeStruct(s.shape, s.dtype) for s in gather],
        input_output_aliases={8 + a: 3 + a for a in range(ng)},
        scratch_shapes=[pltpu.VMEM((hb, DN_DK, DN_DK), F32)] + (GatherJob.sems(ng) if ng else []),
        compiler_params=_cparams(("arbitrary", "arbitrary")),
    )(act_qkv, act_qkv, act_qkv, proj, proj, a_log3, dt_bias3, norm_w3, *gather)


def gdn_backward(act_qkv, proj, a_log3, dt_bias3, norm_w3, s_all, inv_all, d_mixed, *, n_pad, z_off, small_off,
                 exchange=()):
    t, w3 = act_qkv.shape
    wdn = w3 // 3
    heads = wdn // DN_DK
    hb = min(HEAD_BLOCK, heads)
    nhb = heads // hb
    nc = t // CHUNK
    bw = hb * DN_DK
    nqb = wdn // bw
    ne = len(exchange)

    def body(q_ref, k_ref, v_ref, z_ref, sm_ref, al_ref, db_ref, nw_ref, s_ref, inv_ref, do_ref, *rest):
        dq_ref, dk_ref, dv_ref, dz_ref, dsm_ref, dal_ref, ddb_ref, dnw_ref = rest[ne:ne + 8]
        ds_ref = rest[2 * ne + 8]
        i = pl.program_id(1)
        n = nc - 1 - i
        head0 = pl.program_id(0) * hb
        if ne:
            _run_beside(PairExchangeJob(rest[:ne], rest[ne + 8:2 * ne + 8], *rest[2 * ne + 9:]),
                        pl.program_id(0) * nc + i, nhb * nc)

        @pl.when(i == 0)
        def _():
            ds_ref[...] = jnp.zeros_like(ds_ref)
            dal_ref[...] = jnp.zeros_like(dal_ref)
            ddb_ref[...] = jnp.zeros_like(ddb_ref)
            dnw_ref[...] = jnp.zeros_like(dnw_ref)

        vm = _valid_rows(n, n_pad)
        step = functools.partial(_gdn_step, vm=vm, inv=inv_ref[...])
        _, vjp = jax.vjp(step, s_ref[...], _split_heads(q_ref[...], hb, DN_DK), _split_heads(k_ref[...], hb, DN_DK),
                         _split_heads(v_ref[...], hb, DN_DK), _split_heads(z_ref[...], hb, DN_DK),
                         _take_cols(sm_ref[...], head0, hb), _take_cols(sm_ref[...], heads + head0, hb),
                         al_ref[...], db_ref[...], nw_ref[...])
        ds, dq, dk, dv, dz, dbr, dar, dal, ddb, dnw = vjp((ds_ref[...], _split_heads(do_ref[...], hb, DN_DK)))
        ds_ref[...] = ds
        dq_ref[...] = _merge_heads(dq)
        dk_ref[...] = _merge_heads(dk)
        dv_ref[...] = _merge_heads(dv)
        dz_ref[...] = _merge_heads(dz).astype(dz_ref.dtype)
        dsm_ref[...] = _put_cols(dbr, head0) + _put_cols(dar, heads + head0)
        dal_ref[...] += dal
        ddb_ref[...] += ddb
        dnw_ref[...] += dnw[0]

    rev = lambda n: nc - 1 - n
    par = pl.BlockSpec((hb, 1, 1), lambda h, n: (h, 0, 0))
    blk = lambda off: pl.BlockSpec((CHUNK, bw), lambda h, n: (rev(n), off + h))
    return pl.pallas_call(
        body, name="gdn_bwd",
        grid=(nhb, nc),
        in_specs=[blk(0), blk(nqb), blk(2 * nqb), blk(z_off // bw),
                  pl.BlockSpec((CHUNK, LANES), lambda h, n: (rev(n), small_off // LANES)), par, par,
                  pl.BlockSpec((1, 1, DN_DK), lambda h, n: (0, 0, 0)),
                  pl.BlockSpec((None, hb, DN_DK, DN_DK), lambda h, n: (rev(n), h, 0, 0)),
                  pl.BlockSpec((None, hb, CHUNK, CHUNK), lambda h, n: (rev(n), h, 0, 0)),
                  blk(0)] + [ANY] * ne,
        out_specs=[blk(0), blk(0), blk(0), blk(z_off // bw),
                   pl.BlockSpec((None, CHUNK, LANES), lambda h, n: (h, rev(n), 0)), par, par,
                   pl.BlockSpec((None, 1, DN_DK), lambda h, n: (h, 0, 0))] + [ANY] * ne,
        out_shape=[jax.ShapeDtypeStruct((t, wdn), F32)] * 3
        + [jax.ShapeDtypeStruct((t, proj.shape[1]), BF16),
           jax.ShapeDtypeStruct((nhb, t, LANES), F32),
           jax.ShapeDtypeStruct((heads, 1, 1), F32), jax.ShapeDtypeStruct((heads, 1, 1), F32),
           jax.ShapeDtypeStruct((nhb, 1, DN_DK), F32)] + [_half_rows(s) for s in exchange],
        scratch_shapes=[pltpu.VMEM((hb, DN_DK, DN_DK), F32)] + (PairExchangeJob.sems(ne) if ne else []),
        compiler_params=_cparams(("arbitrary", "arbitrary")),
    )(act_qkv, act_qkv, act_qkv, proj, proj, a_log3, dt_bias3, norm_w3, s_all, inv_all, d_mixed, *exchange)


def _dot2(a, b, ca, cb):
    return lax.dot_general(a.astype(BF16), b.astype(BF16), (((ca,), (cb,)), ((), ())),
                           preferred_element_type=F32)


def _ssd_step(state, xa, bmat, cmat, z, dtraw, a_log, dt_bias, dskip, norm_w, vm):
    causal, _, eye = _tri_masks()
    r_heads, p, n_state = state.shape
    gw = r_heads * p
    vm2 = vm[0]
    xa, bmat, cmat = xa * vm2, bmat * vm2, cmat * vm2
    dt = _softplus(dtraw + dt_bias) * vm
    a = dt * (-jnp.exp(a_log))
    acs = _cumsum_col(a, causal, eye)
    arow = _col2row(acs, eye)
    lmat = jnp.where(causal, jnp.exp(jnp.where(causal, acs - arow, 0.0)), 0.0)
    hsel = (lax.broadcasted_iota(jnp.int32, (r_heads, 1, gw), 2) // p
            == lax.broadcasted_iota(jnp.int32, (r_heads, 1, gw), 0))

    def spread(col):
        return jnp.sum(jnp.where(hsel, col, 0.0), axis=0)

    xdt = xa * spread(dt)
    cb = _dot2(cmat, bmat, 1, 1)
    m = (cb[None] * lmat).reshape(r_heads * CHUNK, CHUNK)
    yb = _dot2(m, xdt, 1, 0).reshape(r_heads, CHUNK, gw)
    y_diag = jnp.sum(jnp.where(hsel, yb, 0.0), axis=0)
    s2 = state.reshape(gw, n_state)
    y_off = _dot2(cmat, s2, 1, 1) * spread(jnp.exp(acs))
    a_last = jnp.sum(a, axis=1, keepdims=True)
    upd = _dot2(xdt * spread(jnp.exp(a_last - acs)), bmat, 0, 0)
    new_state = state * jnp.exp(a_last) + upd.reshape(r_heads, p, n_state)
    y = y_diag + y_off + xa * spread(dskip)
    y = y * _silu(z)
    y = y * lax.rsqrt(jnp.mean(y * y, -1, keepdims=True) + NORM_EPS) * norm_w
    return new_state, y


def _ssd_dims(act_xbc):
    t, wx = act_xbc.shape
    wm = wx - 2 * M2_GROUPS * M2_N
    gw = wm // M2_GROUPS
    return t, wm, gw, gw // M2_P, wm // M2_P, t // CHUNK


def ssd_forward(act_xbc, proj, a_log3, dt_bias3, dskip3, norm_w, mixed, *, n_pad, z_off, small_off, dt_lane,
                gather=()):
    t, wm, gw, rh, heads, nc = _ssd_dims(act_xbc)
    nb = wm // M2_N
    ob = (mixed.shape[1] - wm) // gw
    ng = len(gather)

    def body(x_ref, b_ref, c_ref, z_ref, dt_ref, al_ref, db_ref, dk_ref, nw_ref, _, *rest):
        out_ref, sall_ref = rest[ng:ng + 2]
        st_ref = rest[2 * ng + 2]
        n = pl.program_id(1)
        if ng:
            _run_beside(GatherJob(rest[ng + 2:2 * ng + 2], *rest[2 * ng + 3:]), pl.program_id(0) * nc + n,
                        M2_GROUPS * nc)

        @pl.when(n == 0)
        def _():
            st_ref[...] = jnp.zeros_like(st_ref)

        state = st_ref[...]
        sall_ref[...] = state
        dtraw = _take_cols(dt_ref[...], dt_lane + pl.program_id(0) * rh, rh)
        new_state, y = _ssd_step(state, x_ref[...], b_ref[...], c_ref[...], z_ref[...], dtraw,
                                 al_ref[...], db_ref[...], dk_ref[...], nw_ref[...], _valid_rows(n, n_pad))
        st_ref[...] = new_state
        out_ref[...] = y.astype(out_ref.dtype)

    par = pl.BlockSpec((rh, 1, 1), lambda g, n: (g, 0, 0))
    return pl.pallas_call(
        body, name="ssd_fwd",
        grid=(M2_GROUPS, nc),
        in_specs=[pl.BlockSpec((CHUNK, gw), lambda g, n: (n, g)),
                  pl.BlockSpec((CHUNK, M2_N), lambda g, n: (n, nb + g)),
                  pl.BlockSpec((CHUNK, M2_N), lambda g, n: (n, nb + M2_GROUPS + g)),
                  pl.BlockSpec((CHUNK, gw), lambda g, n: (n, z_off // gw + g)),
                  pl.BlockSpec((CHUNK, LANES), lambda g, n: (n, small_off // LANES)),
                  par, par, par,
                  pl.BlockSpec((1, gw), lambda g, n: (0, g)),
                  pl.BlockSpec(memory_space=pl.ANY)] + [ANY] * ng,
        out_specs=[pl.BlockSpec((CHUNK, gw), lambda g, n: (n, ob + g)),
                   pl.BlockSpec((None, rh, M2_P, M2_N), lambda g, n: (n, g, 0, 0))] + [ANY] * ng,
        out_shape=[jax.ShapeDtypeStruct(mixed.shape, mixed.dtype),
                   jax.ShapeDtypeStruct((nc, heads, M2_P, M2_N), F32)]
        + [jax.ShapeDtypeStruct(s.shape, s.dtype) for s in gather],
        input_output_aliases={9: 0, **{10 + a: 2 + a for a in range(ng)}},
        scratch_shapes=[pltpu.VMEM((rh, M2_P, M2_N), F32)] + (GatherJob.sems(ng) if ng else []),
        compiler_params=_cparams(("arbitrary", "arbitrary")),
    )(act_xbc, act_xbc, act_xbc, proj, proj, a_log3, dt_bias3, dskip3, norm_w, mixed, *gather)


def ssd_backward(act_xbc, proj, a_log3, dt_bias3, dskip3, norm_w, s_all, d_mixed, dproj, *, n_pad, z_off,
                 small_off, dt_lane, exchange=()):
    t, wm, gw, rh, heads, nc = _ssd_dims(act_xbc)
    nb = wm // M2_N
    ne = len(exchange)

    def body(x_ref, b_ref, c_ref, z_ref, dt_ref, al_ref, db_ref, dk_ref, nw_ref, s_ref, dy_ref, _, *rest):
        dx_ref, dbm_ref, dcm_ref, dz_ref, ddt_ref, dal_ref, ddb_ref, ddk_ref, dnw_ref = rest[ne:ne + 9]
        ds_ref = rest[2 * ne + 9]
        i = pl.program_id(1)
        n = nc - 1 - i
        if ne:
            _run_beside(ChipExchangeJob(rest[:ne], rest[ne + 9:2 * ne + 9], *rest[2 * ne + 10:]),
                        pl.program_id(0) * nc + i, M2_GROUPS * nc)

        @pl.when(i == 0)
        def _():
            ds_ref[...] = jnp.zeros_like(ds_ref)
            dal_ref[...] = jnp.zeros_like(dal_ref)
            ddb_ref[...] = jnp.zeros_like(ddb_ref)
            ddk_ref[...] = jnp.zeros_like(ddk_ref)
            dnw_ref[...] = jnp.zeros_like(dnw_ref)

        step = functools.partial(_ssd_step, vm=_valid_rows(n, n_pad))
        lane0 = dt_lane + pl.program_id(0) * rh
        _, vjp = jax.vjp(step, s_ref[...], x_ref[...], b_ref[...], c_ref[...], z_ref[...],
                         _take_cols(dt_ref[...], lane0, rh), al_ref[...], db_ref[...], dk_ref[...], nw_ref[...])
        ds, dx, dbm, dcm, dz, ddt, dal, ddb, ddk, dnw = vjp((ds_ref[...], dy_ref[...]))
        ds_ref[...] = ds
        dx_ref[...] = dx
        dbm_ref[...] = dbm
        dcm_ref[...] = dcm
        dz_ref[...] = dz.astype(dz_ref.dtype)
        ddt_ref[...] = _put_cols(ddt, lane0)
        dal_ref[...] += dal
        ddb_ref[...] += ddb
        ddk_ref[...] += ddk
        dnw_ref[...] += dnw

    rev = lambda n: nc - 1 - n
    par = pl.BlockSpec((rh, 1, 1), lambda g, n: (g, 0, 0))
    wide = lambda off: pl.BlockSpec((CHUNK, gw), lambda g, n: (rev(n), off + g))
    narrow = lambda off: pl.BlockSpec((CHUNK, M2_N), lambda g, n: (rev(n), off + g))
    col = pl.BlockSpec((None, CHUNK, LANES), lambda g, n: (g, rev(n), 0))
    gn = M2_GROUPS * M2_N
    return pl.pallas_call(
        body, name="ssd_bwd",
        grid=(M2_GROUPS, nc),
        in_specs=[wide(0), narrow(nb), narrow(nb + M2_GROUPS), wide(z_off // gw),
                  pl.BlockSpec((CHUNK, LANES), lambda g, n: (rev(n), small_off // LANES)), par, par, par,
                  pl.BlockSpec((1, gw), lambda g, n: (0, g)),
                  pl.BlockSpec((None, rh, M2_P, M2_N), lambda g, n: (rev(n), g, 0, 0)),
                  wide(M2_GROUPS), pl.BlockSpec(memory_space=pl.ANY)] + [ANY] * ne,
        out_specs=[wide(0), narrow(0), narrow(0), wide(z_off // gw), col, par, par, par,
                   pl.BlockSpec((1, gw), lambda g, n: (0, g))] + [ANY] * ne,
        out_shape=[jax.ShapeDtypeStruct((t, wm), F32), jax.ShapeDtypeStruct((t, gn), F32),
                   jax.ShapeDtypeStruct((t, gn), F32), jax.ShapeDtypeStruct(dproj.shape, dproj.dtype),
                   jax.ShapeDtypeStruct((M2_GROUPS, t, LANES), F32),
                   jax.ShapeDtypeStruct((heads, 1, 1), F32), jax.ShapeDtypeStruct((heads, 1, 1), F32),
                   jax.ShapeDtypeStruct((heads, 1, 1), F32), jax.ShapeDtypeStruct((1, wm), F32)]
        + [jax.ShapeDtypeStruct((3,) + s.shape[1:], s.dtype) for s in exchange],
        input_output_aliases={11: 3},
        scratch_shapes=[pltpu.VMEM((rh, M2_P, M2_N), F32)] + (ChipExchangeJob.sems(ne) if ne else []),
        compiler_params=_cparams(("arbitrary", "arbitrary")),
    )(act_xbc, act_xbc, act_xbc, proj, proj, a_log3, dt_bias3, dskip3, norm_w, s_all, d_mixed, dproj, *exchange)


SUBLANES = 8
LANES = 128


def _pick(dim, target, align):
    best = None
    for d in range(align, min(dim, target) + 1, align):
        if dim % d == 0:
            best = d
    return dim if best is None else best


def _row_tile(t):
    return _pick(t, 512, 16)


def matmul(a, b, *, mode, out_dtype, name, tm=1056, tn=512, tk=2048, residual=None, out_shards=1,
           a_shards=1, b_shards=1, exchange=()):
    if mode == "tn":
        kd, m = a.shape
        n = b.shape[-1] * b_shards
    else:
        m, kd = a.shape[-2], a.shape[-1] * a_shards
        n = b.shape[1] if mode == "nn" else b.shape[0]
    tm = _pick(m, tm, LANES if mode == "tn" else 16)
    ks = kd // a_shards
    tk = _pick(ks, tk, LANES)
    nkb = ks // tk
    nk = kd // tk
    ns_o, ns_b = n // out_shards, n // b_shards
    tn = _pick(math.gcd(ns_o, ns_b), tn, LANES)
    npb_o, npb_b = ns_o // tn, ns_b // tn
    if mode == "tn":
        a_spec = pl.BlockSpec((tk, tm), lambda i, j, k: (k, i))
    elif a_shards == 1:
        a_spec = pl.BlockSpec((tm, tk), lambda i, j, k: (i, k))
    else:
        a_spec = pl.BlockSpec((None, tm, tk), lambda i, j, k: (k // nkb, i, k % nkb))
    contract = ((1,), (1,)) if mode == "nt" else ((1,), (0,))
    if mode == "nt":
        b_spec = pl.BlockSpec((tn, tk), lambda i, j, k: (j, k))
    elif b_shards == 1:
        b_spec = pl.BlockSpec((tk, tn), lambda i, j, k: (k, j))
    else:
        b_spec = pl.BlockSpec((None, tk, tn), lambda i, j, k: (j // npb_b, k, j % npb_b))
    has_res = residual is not None
    ne = len(exchange)
    grid = (m // tm, n // tn, nk)

    def body(*refs):
        refs = list(refs)
        a_ref, b_ref = refs[:2]
        del refs[:2]
        r_ref = refs.pop(0) if has_res else None
        ex_in = [refs.pop(0) for _ in range(ne)]
        o_ref = refs.pop(0)
        ex_out = [refs.pop(0) for _ in range(ne)]
        at_ref = refs.pop(0) if mode == "tn" else None
        acc_ref = refs.pop(0) if nk > 1 else None
        k = pl.program_id(2)
        if ne:
            step = (pl.program_id(0) * grid[1] + pl.program_id(1)) * nk + k
            _run_beside(ChipExchangeJob(ex_in, ex_out, *refs), step, grid[0] * grid[1] * nk)
        if mode == "tn":
            @pl.when(pl.program_id(1) == 0)
            def _():
                at_ref[k] = jnp.transpose(a_ref[...].astype(F32)).astype(BF16)

            lhs = at_ref[k]
        else:
            lhs = a_ref[...].astype(BF16)
        part = lax.dot_general(lhs, b_ref[...].astype(BF16), (contract, ((), ())), preferred_element_type=F32)

        def finish(total):
            if has_res:
                total = total + r_ref[...]
            o_ref[...] = total.astype(o_ref.dtype)

        if nk == 1:
            finish(part)
        else:
            @pl.when(k == 0)
            def _():
                acc_ref[...] = part

            @pl.when((k > 0) & (k < nk - 1))
            def _():
                acc_ref[...] += part

            @pl.when(k == nk - 1)
            def _():
                finish(acc_ref[...] + part)

    in_specs = [a_spec, b_spec]
    args = [a, b]
    if has_res:
        in_specs.append(pl.BlockSpec((tm, tn), lambda i, j, k: (i, j)))
        args.append(residual)
    if out_shards == 1:
        out_spec = pl.BlockSpec((tm, tn), lambda i, j, k: (i, j))
        out_shape = jax.ShapeDtypeStruct((m, n), out_dtype)
    else:
        out_spec = pl.BlockSpec((None, tm, tn), lambda i, j, k: (j // npb_o, i, j % npb_o))
        out_shape = jax.ShapeDtypeStruct((out_shards, m, ns_o), out_dtype)
    scratch = [pltpu.VMEM((nk, tm, tk), BF16)] if mode == "tn" else []
    if nk > 1:
        scratch.append(pltpu.VMEM((tm, tn), F32))
    if not ne:
        return pl.pallas_call(
            body, name=name, grid=grid,
            in_specs=in_specs, out_specs=out_spec, out_shape=out_shape, scratch_shapes=scratch,
            compiler_params=_cparams(("parallel", "arbitrary", "arbitrary")),
        )(*args)
    return pl.pallas_call(
        body, name=name, grid=grid,
        in_specs=in_specs + [ANY] * ne, out_specs=[out_spec] + [ANY] * ne,
        out_shape=[out_shape] + [jax.ShapeDtypeStruct((3,) + s.shape[1:], s.dtype) for s in exchange],
        scratch_shapes=scratch + ChipExchangeJob.sems(ne),
        compiler_params=_cparams(("arbitrary", "arbitrary", "arbitrary")),
    )(*args, *exchange)


def rmsnorm_forward(x, w, *, name):
    t, d = x.shape
    tm = _row_tile(t)

    def body(x_ref, w_ref, o_ref):
        xv = x_ref[...]
        r = lax.rsqrt(jnp.mean(xv * xv, -1, keepdims=True) + NORM_EPS)
        o_ref[...] = (xv * r * w_ref[...]).astype(o_ref.dtype)

    return pl.pallas_call(
        body, name=name, grid=(t // tm,),
        in_specs=[pl.BlockSpec((tm, d), lambda i: (i, 0)), pl.BlockSpec((1, d), lambda i: (0, 0))],
        out_specs=pl.BlockSpec((tm, d), lambda i: (i, 0)),
        out_shape=jax.ShapeDtypeStruct((t, d), BF16),
        compiler_params=_cparams(("parallel",)),
    )(x, w)


def _rmsnorm_grads(xv, wv, dy):
    r = lax.rsqrt(jnp.mean(xv * xv, -1, keepdims=True) + NORM_EPS)
    xh = xv * r
    g = dy * wv
    dx = r * (g - xh * jnp.mean(g * xh, -1, keepdims=True))
    return dx, jnp.sum(dy * xh, axis=0, keepdims=True)


def rmsnorm_backward(x, w, dy, dres, *, n_pad, name):
    t, d = x.shape
    tm = _row_tile(t)

    def body(x_ref, w_ref, dy_ref, dr_ref, dx_ref, dx16_ref, dw_ref):
        i = pl.program_id(0)

        @pl.when(i == 0)
        def _():
            dw_ref[...] = jnp.zeros_like(dw_ref)

        dx, dw = _rmsnorm_grads(x_ref[...], w_ref[...], dy_ref[...])
        rows = i * tm + lax.broadcasted_iota(jnp.int32, (tm, 1), 0)
        dx = jnp.where(rows >= n_pad, dx + dr_ref[...], 0.0)
        dx_ref[...] = dx
        dx16_ref[...] = dx.astype(BF16)
        dw_ref[...] += dw

    row = pl.BlockSpec((tm, d), lambda i: (i, 0))
    vec = pl.BlockSpec((1, d), lambda i: (0, 0))
    return pl.pallas_call(
        body, name=name, grid=(t // tm,),
        in_specs=[row, vec, row, row], out_specs=[row, row, vec],
        out_shape=[jax.ShapeDtypeStruct((t, d), F32), jax.ShapeDtypeStruct((t, d), BF16),
                   jax.ShapeDtypeStruct((1, d), F32)],
        compiler_params=_cparams(("arbitrary",)),
    )(x, w, dy, dres)


def loss_head(h, w, target, *, n_skip):
    t, d = h.shape
    tm = _row_tile(t)

    def body(x_ref, w_ref, y_ref, loss_ref, dx_ref, dx16_ref, dw_ref):
        i = pl.program_id(0)

        @pl.when(i == 0)
        def _():
            dw_ref[...] = jnp.zeros_like(dw_ref)
            loss_ref[...] = jnp.zeros_like(loss_ref)

        xv, wv = x_ref[...], w_ref[...]
        r = lax.rsqrt(jnp.mean(xv * xv, -1, keepdims=True) + NORM_EPS)
        rows = i * tm + lax.broadcasted_iota(jnp.int32, (tm, 1), 0)
        err = jnp.where(rows >= n_skip, xv * r * wv - y_ref[...], 0.0)
        loss_ref[...] += 0.5 * jnp.sum(jnp.mean(err * err, -1, keepdims=True))
        dx, dw = _rmsnorm_grads(xv, wv, err * (1.0 / d))
        dx_ref[...] = dx
        dx16_ref[...] = dx.astype(BF16)
        dw_ref[...] += dw

    row = pl.BlockSpec((tm, d), lambda i: (i, 0))
    vec = pl.BlockSpec((1, d), lambda i: (0, 0))
    return pl.pallas_call(
        body, name="loss_head", grid=(t // tm,),
        in_specs=[row, vec, row],
        out_specs=[pl.BlockSpec((1, LANES), lambda i: (0, 0)), row, row, vec],
        out_shape=[jax.ShapeDtypeStruct((1, LANES), F32), jax.ShapeDtypeStruct((t, d), F32),
                   jax.ShapeDtypeStruct((t, d), BF16), jax.ShapeDtypeStruct((1, d), F32)],
        compiler_params=_cparams(("arbitrary",)),
    )(h, w, target)


HALO = SUBLANES


STRIP = 32


def _taps(blk, kk, rows):
    return [blk[HALO - (kk - 1) + j:HALO - (kk - 1) + j + rows, :] for j in range(kk)]


def _fir(taps, w):
    acc = w[0:1, :] * taps[0]
    for j in range(1, len(taps)):
        acc = acc + w[j:j + 1, :] * taps[j]
    return acc


def _fir_transposed(dpre, w, rows):
    kk = w.shape[0]
    acc = w[0:1, :] * dpre[kk - 1:kk - 1 + rows, :]
    for j in range(1, kk):
        acc = acc + w[j:j + 1, :] * dpre[kk - 1 - j:kk - 1 - j + rows, :]
    return acc


def _fold8(v):
    return jnp.sum(v.reshape(v.shape[0] // SUBLANES, SUBLANES, v.shape[1]), axis=0)


def _strips(tm, body, init):
    def step(r, carry):
        return body(pl.multiple_of(r * STRIP, STRIP), carry)
    return lax.fori_loop(0, tm // STRIP, step, init)


def _dsilu(p):
    s = _sigmoid(p)
    return s * (1.0 + p * (1.0 - s))


def conv_silu(x, w, b, *, x_off, name):
    t = x.shape[0]
    kk, width = w.shape
    tm = _row_tile(t)
    tc = _pick(width, 512, LANES)
    ob, nh = x_off // tc, tm // HALO

    def body(prev_ref, x_ref, w_ref, b_ref, o_ref, scr):
        i = pl.program_id(1)
        scr[0:HALO, :] = jnp.where(i > 0, prev_ref[...], 0.0)
        scr[HALO:HALO + tm, :] = x_ref[...]
        wv, bv = w_ref[...], b_ref[...]

        def strip(base, carry):
            blk = scr[pl.ds(base, STRIP + HALO), :]
            o_ref[pl.ds(base, STRIP), :] = _silu(_fir(_taps(blk, kk, STRIP), wv) + bv)
            return carry

        _strips(tm, strip, 0)

    return pl.pallas_call(
        body, name=name, grid=(width // tc, t // tm),
        in_specs=[pl.BlockSpec((HALO, tc), lambda j, i: (jnp.maximum(i * nh - 1, 0), ob + j)),
                  pl.BlockSpec((tm, tc), lambda j, i: (i, ob + j)),
                  pl.BlockSpec((kk, tc), lambda j, i: (0, j)),
                  pl.BlockSpec((1, tc), lambda j, i: (0, j))],
        out_specs=pl.BlockSpec((tm, tc), lambda j, i: (i, j)),
        out_shape=jax.ShapeDtypeStruct((t, width), F32),
        scratch_shapes=[pltpu.VMEM((tm + HALO, tc), F32)],
        compiler_params=_cparams(("parallel", "arbitrary")),
    )(x, x, w, b)


def conv_silu_backward(x, w, b, dact, dst, *, x_off, name):
    t = x.shape[0]
    kk, width = w.shape
    tm = _row_tile(t)
    tc = _pick(width, 512, LANES)
    ob, nh, nt = x_off // tc, tm // HALO, t // tm
    last_h = t // HALO - 1

    def body(prev_ref, x_ref, next_ref, w_ref, b_ref, d_ref, dnext_ref, _, dx_ref, dw_ref, db_ref, scr_x, scr_d):
        i = pl.program_id(1)

        @pl.when(i == 0)
        def _():
            dw_ref[...] = jnp.zeros_like(dw_ref)
            db_ref[...] = jnp.zeros_like(db_ref)

        wv, bv = w_ref[...], b_ref[...]
        scr_x[0:HALO, :] = jnp.where(i > 0, prev_ref[...], 0.0)
        scr_x[HALO:HALO + tm, :] = x_ref[...]
        scr_x[HALO + tm:, :] = next_ref[...]
        scr_d[0:tm, :] = d_ref[...]
        scr_d[tm:, :] = jnp.where(i < nt - 1, dnext_ref[...], 0.0)

        def strip(base, carry):
            taps = _taps(scr_x[pl.ds(base, STRIP + 2 * HALO), :], kk, STRIP + HALO)
            dpre = scr_d[pl.ds(base, STRIP + HALO), :] * _dsilu(_fir(taps, wv) + bv)
            dx_ref[pl.ds(base, STRIP), :] = _fir_transposed(dpre, wv, STRIP).astype(dx_ref.dtype)
            d0 = dpre[0:STRIP, :]
            return tuple(c + _fold8(d0 * tap[0:STRIP, :]) for c, tap in zip(carry, taps)) + (carry[kk] + _fold8(d0),)

        sums = _strips(tm, strip, tuple(jnp.zeros((SUBLANES, tc), F32) for _ in range(kk + 1)))
        for j in range(kk):
            dw_ref[j:j + 1, :] += jnp.sum(sums[j], axis=0, keepdims=True)
        db_ref[0:1, :] += jnp.sum(sums[kk], axis=0, keepdims=True)

    nxt = lambda j, i: (jnp.minimum((i + 1) * nh, last_h), j)
    acc = pl.BlockSpec((SUBLANES, tc), lambda j, i: (0, j))
    return pl.pallas_call(
        body, name=name, grid=(width // tc, nt),
        in_specs=[pl.BlockSpec((HALO, tc), lambda j, i: (jnp.maximum(i * nh - 1, 0), ob + j)),
                  pl.BlockSpec((tm, tc), lambda j, i: (i, ob + j)),
                  pl.BlockSpec((HALO, tc), lambda j, i: (jnp.minimum((i + 1) * nh, last_h), ob + j)),
                  pl.BlockSpec((kk, tc), lambda j, i: (0, j)),
                  pl.BlockSpec((1, tc), lambda j, i: (0, j)),
                  pl.BlockSpec((tm, tc), lambda j, i: (i, j)),
                  pl.BlockSpec((HALO, tc), nxt),
                  pl.BlockSpec(memory_space=pl.ANY)],
        out_specs=[pl.BlockSpec((tm, tc), lambda j, i: (i, ob + j)), acc, acc],
        out_shape=[jax.ShapeDtypeStruct(dst.shape, dst.dtype), jax.ShapeDtypeStruct((SUBLANES, width), F32),
                   jax.ShapeDtypeStruct((SUBLANES, width), F32)],
        input_output_aliases={7: 0},
        scratch_shapes=[pltpu.VMEM((tm + 2 * HALO, tc), F32), pltpu.VMEM((tm + HALO, tc), F32)],
        compiler_params=_cparams(("parallel", "arbitrary")),
    )(x, x, x, w, b, dact, dact, dst)


def conv_glu(u, w, *, name):
    _, t, f = u.shape
    kk = w.shape[1]
    tm = _row_tile(t)
    tc = _pick(f, 512, LANES)
    nh = tm // HALO

    def body(prev_ref, x_ref, w_ref, o_ref, scr):
        i = pl.program_id(1)
        scr[:, 0:HALO, :] = jnp.where(i > 0, prev_ref[...], 0.0)
        scr[:, HALO:, :] = x_ref[...]
        wg, wv = w_ref[0], w_ref[1]

        def strip(base, carry):
            gate = _fir(_taps(scr[0, pl.ds(base, STRIP + HALO), :], kk, STRIP), wg)
            val = _fir(_taps(scr[1, pl.ds(base, STRIP + HALO), :], kk, STRIP), wv)
            o_ref[pl.ds(base, STRIP), :] = (_silu(gate) * val).astype(o_ref.dtype)
            return carry

        _strips(tm, strip, 0)

    return pl.pallas_call(
        body, name=name, grid=(f // tc, t // tm),
        in_specs=[pl.BlockSpec((2, HALO, tc), lambda j, i: (0, jnp.maximum(i * nh - 1, 0), j)),
                  pl.BlockSpec((2, tm, tc), lambda j, i: (0, i, j)),
                  pl.BlockSpec((2, kk, tc), lambda j, i: (0, 0, j))],
        out_specs=pl.BlockSpec((tm, tc), lambda j, i: (i, j)),
        out_shape=jax.ShapeDtypeStruct((t, f), BF16),
        scratch_shapes=[pltpu.VMEM((2, tm + HALO, tc), F32)],
        compiler_params=_cparams(("parallel", "arbitrary")),
    )(u, u, w)


def conv_glu_backward(u, w, dact, *, name):
    _, t, f = u.shape
    kk = w.shape[1]
    tm = _row_tile(t)
    tc = _pick(f, 512, LANES)
    nh, nt = tm // HALO, t // tm
    last_h = t // HALO - 1
    ext = tm + HALO
    first = HALO - (kk - 1)

    def body(prev_ref, x_ref, next_ref, w_ref, d_ref, dn_ref, du_ref, dw_ref, scr_x, scr_d):
        i = pl.program_id(1)

        @pl.when(i == 0)
        def _():
            dw_ref[...] = jnp.zeros_like(dw_ref)

        scr_x[:, 0:HALO, :] = jnp.where(i > 0, prev_ref[...], 0.0)
        scr_x[:, HALO:HALO + tm, :] = x_ref[...]
        scr_x[:, HALO + tm:, :] = next_ref[...]
        scr_d[0:tm, :] = d_ref[...]
        scr_d[tm:, :] = jnp.where(i < nt - 1, dn_ref[...], 0.0)
        ws = (w_ref[0], w_ref[1])

        def strip(base, carry):
            taps = [_taps(scr_x[h, pl.ds(base, STRIP + 2 * HALO), :], kk, STRIP + HALO) for h in range(2)]
            gate, val = _fir(taps[0], ws[0]), _fir(taps[1], ws[1])
            dact = scr_d[pl.ds(base, STRIP + HALO), :]
            s = _sigmoid(gate)
            dconv = (dact * val * (s * (1.0 + gate * (1.0 - s))), dact * (gate * s))
            out = []
            for h in range(2):
                du_ref[h, pl.ds(base, STRIP), :] = _fir_transposed(dconv[h], ws[h], STRIP).astype(du_ref.dtype)
                d0 = dconv[h][0:STRIP, :]
                out += [c + _fold8(d0 * tap[0:STRIP, :]) for c, tap in zip(carry[h * kk:(h + 1) * kk], taps[h])]
            return tuple(out)

        sums = _strips(tm, strip, tuple(jnp.zeros((SUBLANES, tc), F32) for _ in range(2 * kk)))
        for h in range(2):
            for j in range(kk):
                dw_ref[h, j:j + 1, :] += jnp.sum(sums[h * kk + j], axis=0, keepdims=True)

    return pl.pallas_call(
        body, name=name, grid=(f // tc, nt),
        in_specs=[pl.BlockSpec((2, HALO, tc), lambda j, i: (0, jnp.maximum(i * nh - 1, 0), j)),
                  pl.BlockSpec((2, tm, tc), lambda j, i: (0, i, j)),
                  pl.BlockSpec((2, HALO, tc), lambda j, i: (0, jnp.minimum((i + 1) * nh, last_h), j)),
                  pl.BlockSpec((2, kk, tc), lambda j, i: (0, 0, j)),
                  pl.BlockSpec((tm, tc), lambda j, i: (i, j)),
                  pl.BlockSpec((HALO, tc), lambda j, i: (jnp.minimum((i + 1) * nh, last_h), j))],
        out_specs=[pl.BlockSpec((2, tm, tc), lambda j, i: (0, i, j)),
                   pl.BlockSpec((2, SUBLANES, tc), lambda j, i: (0, 0, j))],
        out_shape=[jax.ShapeDtypeStruct((2, t, f), BF16), jax.ShapeDtypeStruct((2, SUBLANES, f), F32)],
        scratch_shapes=[pltpu.VMEM((2, tm + 2 * HALO, tc), F32), pltpu.VMEM((ext, tc), F32)],
        compiler_params=_cparams(("parallel", "arbitrary")),
    )(u, u, u, w, dact, dact)


ELEMENTWISE_BLOCK_BYTES = 3 * 1024 * 1024


def _rows_spec(a, tr):
    c = a.shape[-1]
    if a.ndim == 2:
        return pl.BlockSpec((tr, c), lambda i, *_: (i, 0))
    return pl.BlockSpec((None, tr, c), lambda i, *_: (0, i, 0))


def adamw(w, g, m, v, *, name):
    r, c = w.shape[-2:]
    tr = _pick(r, max(SUBLANES, ELEMENTWISE_BLOCK_BYTES // (4 * c) // SUBLANES * SUBLANES), SUBLANES)

    def body(w_ref, g_ref, m_ref, v_ref, d_ref, nm_ref, nv_ref):
        gv = g_ref[...]
        nm = ADAM_B1 * m_ref[...] + (1.0 - ADAM_B1) * gv
        nv = ADAM_B2 * v_ref[...] + (1.0 - ADAM_B2) * (gv * gv)
        m_hat = nm / (1.0 - ADAM_B1 ** ADAM_STEP)
        v_hat = nv / (1.0 - ADAM_B2 ** ADAM_STEP)
        d_ref[...] = -ADAM_LR * (m_hat / (jnp.sqrt(v_hat) + ADAM_EPS) + ADAM_WD * w_ref[...])
        nm_ref[...] = nm
        nv_ref[...] = nv

    shp = jax.ShapeDtypeStruct(w.shape, F32)
    return pl.pallas_call(
        body, name=name, grid=(r // tr,),
        in_specs=[_rows_spec(a, tr) for a in (w, g, m, v)], out_specs=[_rows_spec(w, tr)] * 3, out_shape=[shp] * 3,
        compiler_params=_cparams(("parallel",)),
    )(w, g, m, v)


def cast_into_slot(x, slot, n_slots, *, name):
    r, c = x.shape[-2:]
    tr = _pick(r, max(16, ELEMENTWISE_BLOCK_BYTES // (4 * c) // 16 * 16), 16)

    def body(s_ref, x_ref, o_ref):
        o_ref[...] = x_ref[...].astype(o_ref.dtype)

    grid_spec = pltpu.PrefetchScalarGridSpec(
        num_scalar_prefetch=1, grid=(r // tr,),
        in_specs=[_rows_spec(x, tr)],
        out_specs=pl.BlockSpec((None, tr, c), lambda i, s: (s[0], i, 0)))
    return pl.pallas_call(
        body, name=name, grid_spec=grid_spec,
        out_shape=jax.ShapeDtypeStruct((n_slots, r, c), BF16),
        compiler_params=_cparams(("arbitrary",)),
    )(jnp.reshape(slot, (1,)).astype(jnp.int32), x)


class Layout:
    def __init__(self, d):
        self.d = d
        self.h_dn = d // DN_DK
        self.h_m2 = d // M2_P
        self.gn = M2_GROUPS * M2_N
        self.w_xbc = d + 2 * self.gn
        self.z_off = 3 * d
        self.m2z_off = 4 * d
        self.xbc_off = 5 * d
        self.small_off = 5 * d + self.w_xbc
        self.n_small = 2 * self.h_dn + self.h_m2
        self.p = self.small_off + LANES
        self.p_orig = self.small_off + self.n_small

    def w_in_of_slots(self, slab):
        n_slots, rows, cs = slab.shape
        pieces = []
        for o0, _, ln in sorted(self._segments(), key=lambda seg: seg[1]):
            for s in range(n_slots):
                lo, hi = max(o0, s * cs), min(o0 + ln, (s + 1) * cs)
                if lo < hi:
                    pieces.append(slab[s, :, lo - s * cs:hi - s * cs])
        pieces.append(jnp.zeros((rows, LANES - self.n_small), slab.dtype))
        return jnp.concatenate(pieces, axis=1)

    def _segments(self):
        d, s2, so = self.d, 2 * self.h_dn, self.small_off
        return [(0, 0, 4 * d), (4 * d, so, s2), (4 * d + s2, 4 * d, so - 4 * d),
                (self.p_orig - self.h_m2, so + s2, self.h_m2)]

    def slots_of_w_in(self, w, n_slots):
        cs = self.p_orig // n_slots
        slots = []
        for s in range(n_slots):
            pieces = []
            for o0, k0, ln in self._segments():
                lo, hi = max(o0, s * cs), min(o0 + ln, (s + 1) * cs)
                if lo < hi:
                    pieces.append(w[:, k0 + lo - o0:k0 + hi - o0])
            slots.append(jnp.concatenate(pieces, axis=1))
        return jnp.stack(slots, axis=0)


def _cols(small, lo, hi):
    return jnp.transpose(small[:, lo:hi])[..., None]


def local_step(h0, target, wts, slabs, chip, core, *, n_pad, n_meta):
    t, d = h0.shape
    lay = Layout(d)
    hd, hm = lay.h_dn, lay.h_m2
    zeros_b = jnp.zeros((1, 3 * d), F32)
    r3 = lambda v, n: v.reshape(n, 1, 1)
    dn_al, dn_db = r3(wts["dn_a_log"], hd), r3(wts["dn_dt_bias"], hd)
    dn_nw = wts["dn_norm_w"].reshape(1, 1, DN_DK)
    m2_al, m2_db, m2_dk = r3(wts["m2_a_log"], hm), r3(wts["m2_dt_bias"], hm), r3(wts["m2_d"], hm)

    hn1 = rmsnorm_forward(h0, wts["norm_mix_w"], name="norm_mix")
    proj = matmul(hn1, wts["w_in"], mode="nn", out_dtype=F32, name="in_proj", tn=1920)
    act_qkv = conv_silu(proj, wts["dn_conv_w"], zeros_b, x_off=0, name="dn_conv")
    act_xbc = conv_silu(proj, wts["m2_conv_w"], wts["m2_conv_b"], x_off=lay.xbc_off, name="m2_conv")
    small_at = dict(small_off=lay.small_off)
    dt_at = dict(small_off=lay.small_off, dt_lane=2 * hd)
    mixed, s_dn, inv_dn, ffn_up_all, ffn_down_all = gdn_forward(
        act_qkv, proj, dn_al, dn_db, dn_nw, n_pad=n_pad, z_off=lay.z_off, out_cols=2 * d,
        gather=[slabs["ffn_up"], slabs["ffn_down"]], **small_at)
    mixed, s_m2, w_out_all = ssd_forward(act_xbc, proj, m2_al, m2_db, m2_dk, wts["m2_norm_w"], mixed,
                                         n_pad=n_pad, z_off=lay.m2z_off, gather=[slabs["w_out"]], **dt_at)
    wts = dict(wts, w_out=w_out_all.reshape(-1, d), ffn_down=ffn_down_all.reshape(-1, d),
               ffn_up=jnp.transpose(ffn_up_all, (1, 0, 2)).reshape(d, -1))
    h1 = matmul(mixed, wts["w_out"], mode="nn", out_dtype=F32, name="out_proj", tk=2 * d, residual=h0)
    hn2 = rmsnorm_forward(h1, wts["norm_ffn_w"], name="norm_ffn")
    up = matmul(hn2, wts["ffn_up"], mode="nn", out_dtype=F32, name="ffn_up", tn=1408, out_shards=2)
    kf, f = wts["ffn_conv_w"].shape[0], wts["ffn_conv_w"].shape[1] // 2
    w_glu = jnp.transpose(wts["ffn_conv_w"].reshape(kf, 2, f), (1, 0, 2))
    act = conv_glu(up, w_glu, name="ffn_conv")
    h2 = matmul(act, wts["ffn_down"], mode="nn", out_dtype=F32, name="ffn_down", tk=f // 2, residual=h1)
    loss, dh2, dh2_16, d_nfw = loss_head(h2, wts["norm_final_w"].reshape(1, d), target, n_skip=n_pad + n_meta)

    g = {}
    d_act = matmul(dh2_16, wts["ffn_down"], mode="nt", out_dtype=F32, name="d_ffn_act", tn=1408)
    g["ffn_down"] = matmul(act, dh2_16, mode="tn", out_dtype=F32, name="dw_ffn_down", tm=512, tn=1024, tk=t)
    dup, d_fcw = conv_glu_backward(up, w_glu, d_act, name="d_ffn_conv")
    g["ffn_conv_w"] = jnp.transpose(d_fcw[:, :kf], (1, 0, 2)).reshape(kf, 2 * f)
    dhn2 = matmul(dup, wts["ffn_up"], mode="nt", out_dtype=F32, name="d_norm_ffn_out", tk=f, a_shards=2)
    g["ffn_up"] = matmul(hn2, dup, mode="tn", out_dtype=F32, name="dw_ffn_up", tm=512, tn=1408, tk=t,
                         b_shards=2, out_shards=4)
    dh1, dh1_16, g["norm_ffn_w"] = rmsnorm_backward(h1, wts["norm_ffn_w"], dhn2, dh2, n_pad=n_pad, name="d_norm_ffn")
    d_mixed = matmul(dh1_16, wts["w_out"], mode="nt", out_dtype=F32, name="d_mixed", tn=512)
    g["w_out"] = matmul(mixed, dh1_16, mode="tn", out_dtype=F32, name="dw_out", tm=512, tn=1024, tk=t)

    early = ("w_out", "ffn_up", "ffn_down")
    slots = [g[k] if g[k].ndim == 3 else g[k].reshape(N_CHIPS, -1, g[k].shape[1]) for k in early]
    dq, dk, dv, dproj, dsm_dn, g_al, g_db, g_nw, *from_sibling = gdn_backward(
        act_qkv, proj, dn_al, dn_db, dn_nw, s_dn, inv_dn, d_mixed, n_pad=n_pad, z_off=lay.z_off, exchange=slots,
        **small_at)
    g["dn_a_log"], g["dn_dt_bias"] = g_al.reshape(1, hd), g_db.reshape(1, hd)
    g["dn_norm_w"] = jnp.sum(g_nw, axis=0)
    p16, own = [], []
    for k, slot, rb in zip(early, slots, from_sibling):
        a, b = pair_add(slot, rb, chip, core, name="grad_pair_add_" + k)
        p16.append(a)
        own.append(b)
    dxs, dbm, dcm, dproj, dsm_m2, g_al, g_db, g_dk, g["m2_norm_w"], *from_chips = ssd_backward(
        act_xbc, proj, m2_al, m2_db, m2_dk, wts["m2_norm_w"], s_m2, d_mixed, dproj,
        n_pad=n_pad, z_off=lay.m2z_off, exchange=p16, **dt_at)
    for k, o, q in zip(early, own, from_chips):
        g[k] = (o, q)
    g["m2_a_log"], g["m2_dt_bias"], g["m2_d"] = g_al.reshape(1, hm), g_db.reshape(1, hm), g_dk.reshape(1, hm)

    kc = wts["dn_conv_w"].shape[0]
    dw_parts = []
    for idx, dpart in enumerate((dq, dk, dv)):
        dproj, dw, _ = conv_silu_backward(proj, wts["dn_conv_w"][:, idx * d:(idx + 1) * d], zeros_b[:, :d], dpart,
                                          dproj, x_off=idx * d, name=f"d_dn_conv{idx}")
        dw_parts.append(dw[:kc])
    g["dn_conv_w"] = jnp.concatenate(dw_parts, axis=1)
    dw_parts, db_parts = [], []
    off = 0
    for idx, dpart in enumerate((dxs, dbm, dcm)):
        wd = dpart.shape[1]
        dproj, dw, db = conv_silu_backward(proj, wts["m2_conv_w"][:, off:off + wd], wts["m2_conv_b"][:, off:off + wd],
                                           dpart, dproj, x_off=lay.xbc_off + off, name=f"d_m2_conv{idx}")
        dw_parts.append(dw[:kc])
        db_parts.append(db[:1])
        off += wd
    g["m2_conv_w"] = jnp.concatenate(dw_parts, axis=1)
    g["m2_conv_b"] = jnp.concatenate(db_parts, axis=1)
    dsmall = jnp.sum(dsm_dn, axis=0) + jnp.sum(dsm_m2, axis=0)
    dproj = lax.dynamic_update_slice(dproj, dsmall.astype(BF16), (0, lay.small_off))

    dw_in = matmul(hn1, dproj, mode="tn", out_dtype=F32, name="dw_in", tm=512, tn=896, tk=t)
    w_in_slots = lay.slots_of_w_in(dw_in, N_CHIPS)
    (from_sibling,) = pair_exchange([w_in_slots], name="grad_pair_exchange_w_in")
    p16_w_in, own_w_in = pair_add(w_in_slots, from_sibling, chip, core, name="grad_pair_add_w_in")
    dhn1, from_chips = matmul(dproj, wts["w_in"], mode="nt", out_dtype=F32, name="d_norm_mix_out", tk=lay.p // 3,
                              exchange=[p16_w_in])
    g["w_in"] = (own_w_in, from_chips)
    dh0, _, g["norm_mix_w"] = rmsnorm_backward(h0, wts["norm_mix_w"], dhn1, dh1, n_pad=n_pad, name="d_norm_mix")
    g["norm_final_w"] = d_nfw
    return loss[0, 0], dh0, g


MESH = pl.DeviceIdType.MESH
ANY = pl.BlockSpec(memory_space=pl.ANY)
N_CHIPS = 4
N_DEV = 8


def _mesh_pos():
    return lax.axis_index("x"), lax.axis_index("y"), lax.axis_index("c")


def _other_chips(x, y):
    return [(1 - x, y), (x, 1 - y), (1 - x, 1 - y)]


def _rcopy(src, dst, send_sems, recv_sems, k, to):
    return pltpu.make_async_remote_copy(src_ref=src, dst_ref=dst, send_sem=send_sems.at[k], recv_sem=recv_sems.at[k],
                                        device_id=to, device_id_type=MESH)


class GatherJob:
    def __init__(self, slabs, send_sems, recv_sems):
        self.slabs, self.send, self.recv = slabs, send_sems, recv_sems
        self.x, self.y, self.c = _mesh_pos()

    @staticmethod
    def sems(n):
        return [pltpu.SemaphoreType.DMA((6 * n,)), pltpu.SemaphoreType.DMA((6 * n,))]

    def _pieces(self):
        x, y, c = self.x, self.y, self.c
        for a, slab in enumerate(self.slabs):
            half = slab.shape[1] // 2
            for j, (px, py) in enumerate(_other_chips(x, y)):
                yield a, j, (px, py), slab, pl.ds(c * half, half), pl.ds((1 - c) * half, half)

    def _ici(self, a, j, chip, ref):
        return _rcopy(ref, ref, self.send, self.recv, 6 * a + j, (chip[0], chip[1], self.c))

    def _d2d(self, a, j, ref):
        return _rcopy(ref, ref, self.send, self.recv, 6 * a + 3 + j, (self.x, self.y, 1 - self.c))

    def begin(self):
        for a, j, chip, slab, mine, _ in self._pieces():
            self._ici(a, j, chip, slab.at[2 * self.x + self.y, mine]).start()

    def pass_on(self):
        for a, j, chip, slab, mine, _ in self._pieces():
            landed = slab.at[2 * chip[0] + chip[1], mine]
            self._ici(a, j, chip, landed).wait_recv()
            self._d2d(a, j, landed).start()

    def end(self):
        for a, j, chip, slab, mine, theirs in self._pieces():
            self._d2d(a, j, slab.at[2 * chip[0] + chip[1], theirs]).wait_recv()
        for a, j, chip, slab, mine, _ in self._pieces():
            self._ici(a, j, chip, slab.at[2 * self.x + self.y, mine]).wait_send()
            self._d2d(a, j, slab.at[2 * chip[0] + chip[1], mine]).wait_send()


class ChipExchangeJob:
    def __init__(self, parts, outs, send_sems, recv_sems):
        self.parts, self.outs, self.send, self.recv = parts, outs, send_sems, recv_sems
        self.x, self.y, self.c = _mesh_pos()

    @staticmethod
    def sems(n):
        return [pltpu.SemaphoreType.DMA((3 * n,)), pltpu.SemaphoreType.DMA((3 * n,))]

    def _copies(self):
        for a, (part, out) in enumerate(zip(self.parts, self.outs)):
            for j, (px, py) in enumerate(_other_chips(self.x, self.y)):
                yield _rcopy(part.at[2 * px + py], out.at[j], self.send, self.recv, 3 * a + j, (px, py, self.c))

    def begin(self):
        for cp in self._copies():
            cp.start()

    def end(self):
        for cp in self._copies():
            cp.wait()


class PairExchangeJob:
    def __init__(self, grads, outs, send_sems, recv_sems):
        self.grads, self.outs, self.send, self.recv = grads, outs, send_sems, recv_sems
        self.x, self.y, self.c = _mesh_pos()

    @staticmethod
    def sems(n):
        return [pltpu.SemaphoreType.DMA((n,)), pltpu.SemaphoreType.DMA((n,))]

    def _copies(self):
        for a, (grad, out) in enumerate(zip(self.grads, self.outs)):
            half = grad.shape[1] // 2
            yield _rcopy(grad.at[:, pl.ds((1 - self.c) * half, half), :], out, self.send, self.recv, a,
                         (self.x, self.y, 1 - self.c))

    def begin(self):
        for cp in self._copies():
            cp.start()

    def end(self):
        for cp in self._copies():
            cp.wait()


def gather_shards(slabs, *, name):
    n = len(slabs)

    def body(*refs):
        job = GatherJob(refs[n:2 * n], *refs[2 * n:])
        job.begin()
        job.pass_on()
        job.end()

    return pl.pallas_call(
        body, name=name,
        in_specs=[ANY] * n, out_specs=[ANY] * n,
        out_shape=[jax.ShapeDtypeStruct(s.shape, s.dtype) for s in slabs],
        input_output_aliases={a: a for a in range(n)},
        scratch_shapes=GatherJob.sems(n),
        compiler_params=pltpu.CompilerParams(has_side_effects=True),
    )(*slabs)


def _half_rows(s):
    return jax.ShapeDtypeStruct((s.shape[0], s.shape[1] // 2, s.shape[2]), s.dtype)


def pair_exchange(grads, *, name):
    n = len(grads)

    def body(*refs):
        job = PairExchangeJob(refs[:n], refs[n:2 * n], *refs[2 * n:])
        job.begin()
        job.end()

    return pl.pallas_call(
        body, name=name, in_specs=[ANY] * n, out_specs=[ANY] * n,
        out_shape=[_half_rows(s) for s in grads],
        scratch_shapes=PairExchangeJob.sems(n),
        compiler_params=pltpu.CompilerParams(has_side_effects=True),
    )(*grads)


def chip_exchange(parts, *, name):
    n = len(parts)

    def body(*refs):
        job = ChipExchangeJob(refs[:n], refs[n:2 * n], *refs[2 * n:])
        job.begin()
        job.end()

    return pl.pallas_call(
        body, name=name, in_specs=[ANY] * n, out_specs=[ANY] * n,
        out_shape=[jax.ShapeDtypeStruct((3,) + s.shape[1:], s.dtype) for s in parts],
        scratch_shapes=ChipExchangeJob.sems(n),
        compiler_params=pltpu.CompilerParams(has_side_effects=True),
    )(*parts)


def pair_join(wholes, *, name):
    n = len(wholes)

    def body(*refs):
        outs = refs[n:2 * n]
        send_sems, recv_sems = refs[2 * n:]
        x, y, c = _mesh_pos()
        cps = []
        for a in range(n):
            half = outs[a].shape[0] // 2
            rows = outs[a].at[pl.ds(c * half, half)]
            cp = _rcopy(rows, rows, send_sems, recv_sems, a, (x, y, 1 - c))
            cp.start()
            cps.append(cp)
        for a, cp in enumerate(cps):
            cp.wait_send()
            half = outs[a].shape[0] // 2
            theirs = outs[a].at[pl.ds((1 - c) * half, half)]
            _rcopy(theirs, theirs, send_sems, recv_sems, a, (x, y, 1 - c)).wait_recv()

    return pl.pallas_call(
        body, name=name, in_specs=[ANY] * n, out_specs=[ANY] * n,
        out_shape=[jax.ShapeDtypeStruct(s.shape, s.dtype) for s in wholes],
        input_output_aliases={a: a for a in range(n)},
        scratch_shapes=[pltpu.SemaphoreType.DMA((n,)), pltpu.SemaphoreType.DMA((n,))],
        compiler_params=pltpu.CompilerParams(has_side_effects=True),
    )(*wholes)


def gather_all(v, *, name):
    def body(in_ref, out_ref, send_sems, recv_sems, local_sem):
        x, y, c = _mesh_pos()
        mine = out_ref.at[4 * x + 2 * y + c]
        lc = pltpu.make_async_copy(in_ref, mine, local_sem)
        lc.start()
        cps = []
        for k in range(1, N_DEV):
            flip = lambda v, bit: 1 - v if (k >> bit) & 1 else v
            cp = _rcopy(in_ref, mine, send_sems, recv_sems, k - 1, (flip(x, 2), flip(y, 1), flip(c, 0)))
            cp.start()
            cps.append(cp)
        for cp in cps:
            cp.wait()
        lc.wait()

    return pl.pallas_call(
        body, name=name, in_specs=[ANY], out_specs=ANY,
        out_shape=jax.ShapeDtypeStruct((N_DEV,) + v.shape, v.dtype),
        scratch_shapes=[pltpu.SemaphoreType.DMA((N_DEV - 1,)), pltpu.SemaphoreType.DMA((N_DEV - 1,)),
                        pltpu.SemaphoreType.DMA(())],
        compiler_params=pltpu.CompilerParams(has_side_effects=True),
    )(v)


def _sum_tile(rows, cols):
    return _pick(rows, max(16, ELEMENTWISE_BLOCK_BYTES // (4 * cols) // 16 * 16), 16)


def pair_add(g, rb, chip, c, *, name):
    _, r, cols = g.shape
    half = r // 2
    tr = _sum_tile(half, cols)
    nrt = half // tr

    def body(s_ref, g_ref, rb_ref, p16_ref, own_ref):
        v = g_ref[...] + rb_ref[...]
        p16_ref[...] = v.astype(p16_ref.dtype)

        @pl.when(pl.program_id(1) == s_ref[0])
        def _():
            own_ref[...] = v

    grid_spec = pltpu.PrefetchScalarGridSpec(
        num_scalar_prefetch=1, grid=(nrt, N_CHIPS),
        in_specs=[pl.BlockSpec((None, tr, cols), lambda i, k, s: (k, s[1] * nrt + i, 0)),
                  pl.BlockSpec((None, tr, cols), lambda i, k, s: (k, i, 0))],
        out_specs=[pl.BlockSpec((None, tr, cols), lambda i, k, s: (k, i, 0)),
                   pl.BlockSpec((tr, cols), lambda i, k, s: (i, 0))])
    return pl.pallas_call(
        body, name=name, grid_spec=grid_spec,
        out_shape=[jax.ShapeDtypeStruct((N_CHIPS, half, cols), BF16), jax.ShapeDtypeStruct((half, cols), F32)],
        compiler_params=_cparams(("arbitrary", "arbitrary")),
    )(jnp.stack([chip, c]).astype(jnp.int32), g, rb)


def chip_add(own, q, c, *, name):
    r, cols = own.shape
    tr = _sum_tile(r, cols)
    nrt = r // tr

    def body(s_ref, own_ref, q_ref, o_ref):
        o_ref[...] = ((own_ref[...] + q_ref[0].astype(F32)) + q_ref[1].astype(F32)) + q_ref[2].astype(F32)

    grid_spec = pltpu.PrefetchScalarGridSpec(
        num_scalar_prefetch=1, grid=(nrt,),
        in_specs=[pl.BlockSpec((tr, cols), lambda i, s: (i, 0)), pl.BlockSpec((3, tr, cols), lambda i, s: (0, i, 0))],
        out_specs=pl.BlockSpec((tr, cols), lambda i, s: (s[0] * nrt + i, 0)))
    return pl.pallas_call(
        body, name=name, grid_spec=grid_spec,
        out_shape=jax.ShapeDtypeStruct((2 * r, cols), F32),
        compiler_params=_cparams(("arbitrary",)),
    )(jnp.reshape(c, (1,)).astype(jnp.int32), own, q)


def sum_slots(v, *, name):
    n, r, cols = v.shape
    tr = _sum_tile(r, cols)

    def body(v_ref, o_ref):
        acc = v_ref[0]
        for k in range(1, n):
            acc = acc + v_ref[k]
        o_ref[...] = acc

    return pl.pallas_call(
        body, name=name, grid=(r // tr,),
        in_specs=[pl.BlockSpec((n, tr, cols), lambda i: (0, i, 0))],
        out_specs=pl.BlockSpec((tr, cols), lambda i: (i, 0)),
        out_shape=jax.ShapeDtypeStruct((r, cols), F32),
        compiler_params=_cparams(("parallel",)),
    )(v)


PACK_ROWS = 16


def _pack(arrays):
    parts = []
    for a in arrays:
        flat = a.reshape(-1).astype(F32)
        size = PACK_ROWS * LANES
        pad = (-flat.shape[0]) % size
        parts.append(jnp.pad(flat, (0, pad)))
    return jnp.concatenate(parts).reshape(-1, LANES)


def _unpack(slab, shapes):
    out, row = [], 0
    for shp in shapes:
        n = 1
        for s in shp:
            n *= s
        rows = -(-n // (PACK_ROWS * LANES)) * PACK_ROWS
        out.append(slab[row:row + rows].reshape(-1)[:n].reshape(shp))
        row += rows
    return out


WEIGHT_NAMES = ("meta_tokens", "norm_mix_w", "w_in", "dn_conv_w", "dn_a_log", "dn_dt_bias", "dn_norm_w", "m2_conv_w",
                "m2_conv_b", "m2_a_log", "m2_dt_bias", "m2_d", "m2_norm_w", "w_out", "norm_ffn_w", "ffn_up",
                "ffn_conv_w", "ffn_down", "norm_final_w")
BIG = ("w_in", "w_out", "ffn_up", "ffn_down")
BIG_COLUMN_SHARDED = ("w_in", "ffn_up")
SMALL_SHARDED = ("meta_tokens", "dn_conv_w", "m2_conv_w", "ffn_conv_w")
SMALL = tuple(n for n in WEIGHT_NAMES if n not in BIG)


def kernel(x, meta_tokens, norm_mix_w, w_in, dn_conv_w, dn_a_log, dn_dt_bias, dn_norm_w, m2_conv_w, m2_conv_b, m2_a_log, m2_dt_bias, m2_d, m2_norm_w, w_out, norm_ffn_w, ffn_up, ffn_conv_w, ffn_down, norm_final_w, loss_target, m_meta_tokens, m_norm_mix_w, m_w_in, m_dn_conv_w, m_dn_a_log, m_dn_dt_bias, m_dn_norm_w, m_m2_conv_w, m_m2_conv_b, m_m2_a_log, m_m2_dt_bias, m_m2_d, m_m2_norm_w, m_w_out, m_norm_ffn_w, m_ffn_up, m_ffn_conv_w, m_ffn_down, m_norm_final_w, v_meta_tokens, v_norm_mix_w, v_w_in, v_dn_conv_w, v_dn_a_log, v_dn_dt_bias, v_dn_norm_w, v_m2_conv_w, v_m2_conv_b, v_m2_a_log, v_m2_dt_bias, v_m2_d, v_m2_norm_w, v_w_out, v_norm_ffn_w, v_ffn_up, v_ffn_conv_w, v_ffn_down, v_norm_final_w):
    args = tuple(locals().values())
    nw = len(WEIGHT_NAMES)
    wt = dict(zip(WEIGHT_NAMES, args[1:1 + nw]))
    mom = dict(zip(WEIGHT_NAMES, args[2 + nw:2 + 2 * nw]))
    var = dict(zip(WEIGHT_NAMES, args[2 + 2 * nw:2 + 3 * nw]))
    xi, yi, ci = _mesh_pos()
    chip = 2 * xi + yi
    seq, d = x.shape[1], x.shape[2]
    n_meta = wt["meta_tokens"].shape[0]
    n_pad = (-(n_meta + seq)) % ROW_ALIGN
    lay = Layout(d)

    small_local = [wt[k].reshape(wt[k].shape[-2:]) for k in SMALL_SHARDED]
    small_slab = _pack(small_local)
    small_slab = lax.dynamic_update_slice(jnp.zeros((N_CHIPS,) + small_slab.shape, F32), small_slab[None], (chip, 0, 0))
    slabs = {k: cast_into_slot(wt[k], chip, N_CHIPS, name="cast_" + k) for k in BIG}
    w_in_all, small_all = gather_shards([slabs.pop("w_in"), small_slab], name="gather_w_in")
    full = {"w_in": lay.w_in_of_slots(w_in_all)}
    per_chip = [_unpack(small_all[s], [a.shape for a in small_local]) for s in range(N_CHIPS)]
    for idx, k in enumerate(SMALL_SHARDED):
        full[k] = jnp.concatenate([per_chip[s][idx] for s in range(N_CHIPS)], axis=-1)
    for k in SMALL:
        if k not in SMALL_SHARDED:
            full[k] = wt[k]

    h0 = jnp.concatenate([jnp.zeros((n_pad, d), F32), full["meta_tokens"], x[0]], axis=0)
    target = jnp.concatenate([jnp.zeros((n_pad + n_meta, d), F32), loss_target[0]], axis=0)
    loss_local, dh0, g = local_step(h0, target, full, slabs, chip, ci, n_pad=n_pad, n_meta=n_meta)
    grad_x = dh0[n_pad + n_meta:][None]
    g["meta_tokens"] = dh0[n_pad:n_pad + n_meta]
    loss = lax.psum(loss_local, ("x", "y", "c"))

    wholes = [chip_add(*g[k], ci, name="grad_chip_add_" + k) for k in BIG]
    grad = dict(zip(BIG, pair_join(wholes, name="grad_pair_join")))

    small_shapes = [g[k].shape for k in SMALL]
    summed = sum_slots(gather_all(_pack([g[k] for k in SMALL]), name="small_grad_gather"), name="small_grad_sum")
    for k, v in zip(SMALL, _unpack(summed, small_shapes)):
        if k in SMALL_SHARDED:
            width = wt[k].shape[-1]
            v = lax.dynamic_slice_in_dim(v, chip * width, width, axis=v.ndim - 1)
        grad[k] = v

    delta, new_m, new_v = {}, {}, {}
    for k in BIG:
        delta[k], new_m[k], new_v[k] = adamw(wt[k], grad[k], mom[k], var[k], name="adamw_" + k)
    shapes = [wt[k].shape for k in SMALL]
    packed = adamw(_pack([wt[k] for k in SMALL]), _pack([grad[k] for k in SMALL]), _pack([mom[k] for k in SMALL]),
                   _pack([var[k] for k in SMALL]), name="adamw_small")
    for res, slab in zip((delta, new_m, new_v), packed):
        for k, v in zip(SMALL, _unpack(slab, shapes)):
            res[k] = v
    outs = [loss, grad_x]
    for res in (grad, delta, new_m, new_v):
        outs += [res[k].reshape(wt[k].shape) for k in WEIGHT_NAMES]
    return tuple(outs)
```

```python
import functools
import math

import jax
import jax.numpy as jnp
from jax import lax
from jax.experimental import pallas as pl
from jax.experimental.pallas import tpu as pltpu

F32 = jnp.float32
BF16 = jnp.bfloat16

CHUNK = 64
ROW_ALIGN = 128
NORM_EPS = 1e-6
DN_DK = 128
M2_P = 64
M2_N = 128
M2_GROUPS = 4
HEAD_BLOCK = 8
VMEM_LIMIT = 56 * 1024 * 1024

ADAM_LR, ADAM_B1, ADAM_B2, ADAM_EPS, ADAM_WD, ADAM_STEP = 0.001, 0.9, 0.999, 1e-08, 0.01, 10


def _cparams(sem=None):
    return pltpu.CompilerParams(dimension_semantics=sem, vmem_limit_bytes=VMEM_LIMIT)


def _silu(x):
    return x / (1.0 + jnp.exp(-x))


def _sigmoid(x):
    return 1.0 / (1.0 + jnp.exp(-x))


def _softplus(x):
    return jnp.maximum(x, 0.0) + jnp.log(1.0 + jnp.exp(-jnp.abs(x)))


def _tri_masks():
    r = lax.broadcasted_iota(jnp.int32, (CHUNK, CHUNK), 0)
    c = lax.broadcasted_iota(jnp.int32, (CHUNK, CHUNK), 1)
    return (r >= c)[None], (r > c)[None], (r == c)[None]


def _col2row(col, eye):
    return jnp.sum(jnp.where(eye, col, 0.0), axis=1, keepdims=True)


def _tri_sum(v, suffix):
    r = lax.broadcasted_iota(jnp.int32, (v.shape[0], v.shape[0]), 0)
    c = lax.broadcasted_iota(jnp.int32, (v.shape[0], v.shape[0]), 1)
    tri = jnp.where((r <= c) if suffix else (r >= c), 1.0, 0.0).astype(BF16)
    hi = v.astype(BF16)
    rest = v - hi.astype(F32)
    mid = rest.astype(BF16)
    lo = (rest - mid.astype(F32)).astype(BF16)
    dot = lambda part: jnp.dot(tri, part, preferred_element_type=F32)
    return dot(hi) + (dot(mid) + dot(lo))


@jax.custom_vjp
def _running_sum(v):
    return _tri_sum(v, False)


_running_sum.defvjp(lambda v: (_tri_sum(v, False), None), lambda _, g: (_tri_sum(g, True),))


def _bdot(a, b, dims):
    return lax.dot_general(a.astype(BF16), b.astype(BF16), (((dims[0],), (dims[1],)), ((0,), (0,))),
                           preferred_element_type=F32)


def _dot3(a, b, ca, cb):
    def split(v):
        hi = v.astype(BF16)
        return hi, (v - hi.astype(F32)).astype(BF16)

    def dot(p, q):
        return lax.dot_general(p, q, (((ca,), (cb,)), ((0,), (0,))), preferred_element_type=F32)

    (ah, al), (bh, bl) = split(a), split(b)
    return dot(ah, bh) + (dot(ah, bl) + dot(al, bh))


@jax.custom_vjp
def _bmm3(a, b):
    return _dot3(a, b, 2, 1)


def _bmm3_fwd(a, b):
    return _dot3(a, b, 2, 1), (a, b)


def _bmm3_bwd(res, g):
    a, b = res
    return _dot3(g, b, 2, 2), _dot3(a, g, 1, 1)


_bmm3.defvjp(_bmm3_fwd, _bmm3_bwd)


def _unit_lower_inverse(a_mat, eye):
    inv = jnp.where(eye, 1.0, 0.0) - a_mat
    pw = a_mat
    n = 2
    while n < CHUNK:
        pw = _bmm3(pw, pw)
        inv = inv + _bmm3(inv, pw)
        n *= 2
    return inv


@jax.custom_vjp
def _known_inverse(a_mat, inv):
    return inv


def _known_inverse_fwd(a_mat, inv):
    return inv, inv


def _known_inverse_bwd(inv, g):
    return -_dot3(_dot3(inv, g, 1, 1), inv, 2, 2), jnp.zeros_like(inv)


_known_inverse.defvjp(_known_inverse_fwd, _known_inverse_bwd)


def _gdn_step(state, qa, ka, va, z, small, a_log, dt_bias, norm_w, vm, head0, n_heads, inv=None, want_inv=False):
    causal, strict, eye = _tri_masks()
    nh = state.shape[0]
    qa, ka, va = qa * vm, ka * vm, va * vm
    q = qa * lax.rsqrt(jnp.sum(qa * qa, -1, keepdims=True) + NORM_EPS) * (DN_DK ** -0.5)
    k = ka * lax.rsqrt(jnp.sum(ka * ka, -1, keepdims=True) + NORM_EPS)
    g_all = -jnp.exp(a_log) * _softplus(small + dt_bias) * vm[0]
    gcum_all = _running_sum(g_all)
    glast_all = jnp.sum(g_all, axis=0, keepdims=True)
    beta = _take_cols(_sigmoid(small) * vm[0], head0, nh)
    gcum = _take_cols(gcum_all, n_heads + head0, nh)
    egc = _take_cols(jnp.exp(gcum_all), n_heads + head0, nh)
    etail = _take_cols(jnp.exp(glast_all - gcum_all), n_heads + head0, nh)
    elast = _take_cols(jnp.exp(glast_all), n_heads + head0, nh)
    grow = _col2row(gcum, eye)
    decay = jnp.where(causal, jnp.exp(jnp.where(causal, gcum - grow, 0.0)), 0.0)
    kk = _bdot(k, k, (2, 2))
    a_mat = jnp.where(strict, beta * kk * decay, 0.0)
    tinv = _unit_lower_inverse(a_mat, eye) if inv is None else _known_inverse(a_mat, inv)
    u = _bmm3(tinv, va * beta)
    w = _bmm3(tinv, k * (beta * egc))
    v_new = u - _bdot(w, state, (2, 1))
    o_inter = _bdot(q * egc, state, (2, 1))
    qk = _bdot(q, k, (2, 2)) * decay
    o = o_inter + _bdot(qk, v_new, (2, 1))
    new_state = state * elast + _bdot(k * etail, v_new, (1, 1))
    o = o * lax.rsqrt(jnp.mean(o * o, -1, keepdims=True) + NORM_EPS) * norm_w * _silu(z)
    return (new_state, o, tinv) if want_inv else (new_state, o)


def _valid_rows(chunk, n_pad):
    r = chunk * CHUNK + lax.broadcasted_iota(jnp.int32, (1, CHUNK, 1), 1)
    return jnp.where(r >= n_pad, 1.0, 0.0).astype(F32)


def _split_heads(x, n, w):
    return jnp.stack([x[:, i * w:(i + 1) * w] for i in range(n)], axis=0)


def _merge_heads(x):
    return jnp.concatenate([x[i] for i in range(x.shape[0])], axis=-1)


def _run_beside(job, step, n_steps):
    @pl.when(step == 0)
    def _():
        job.begin()

    if hasattr(job, "pass_on"):
        @pl.when(step == (7 * n_steps) // 8)
        def _():
            job.pass_on()

    @pl.when(step == n_steps - 1)
    def _():
        job.end()


def _take_cols(slab, first, n):
    lane = lax.broadcasted_iota(jnp.int32, (n, 1, slab.shape[1]), 2)
    col = lax.broadcasted_iota(jnp.int32, (n, 1, slab.shape[1]), 0)
    return jnp.sum(jnp.where(lane == first + col, slab[None], 0.0), axis=2, keepdims=True)


def gdn_forward(act_qkv, proj, a_log3, dt_bias3, norm_w3, *, n_pad, z_off, small_off, out_cols, gather=()):
    t, w3 = act_qkv.shape
    wdn = w3 // 3
    heads = wdn // DN_DK
    hb = min(HEAD_BLOCK, heads)
    nhb = heads // hb
    nc = t // CHUNK
    bw = hb * DN_DK
    nqb = wdn // bw
    ng = len(gather)

    def body(q_ref, k_ref, v_ref, z_ref, sm_ref, al_ref, db_ref, nw_ref, *rest):
        out_ref, sall_ref, inv_ref = rest[ng:ng + 3]
        st_ref = rest[2 * ng + 3]
        n = pl.program_id(1)
        if ng:
            _run_beside(GatherJob(rest[ng + 3:2 * ng + 3], *rest[2 * ng + 4:]), pl.program_id(0) * nc + n, nhb * nc)

        @pl.when(n == 0)
        def _():
            st_ref[...] = jnp.zeros_like(st_ref)

        state = st_ref[...]
        sall_ref[...] = state
        vm = _valid_rows(n, n_pad)
        head0 = pl.program_id(0) * hb
        new_state, o, tinv = _gdn_step(
            state, _split_heads(q_ref[...], hb, DN_DK), _split_heads(k_ref[...], hb, DN_DK),
            _split_heads(v_ref[...], hb, DN_DK), _split_heads(z_ref[...], hb, DN_DK),
            sm_ref[...], al_ref[...], db_ref[...], nw_ref[...], vm, head0, heads, want_inv=True)
        st_ref[...] = new_state
        inv_ref[...] = tinv
        out_ref[...] = _merge_heads(o).astype(out_ref.dtype)

    par = pl.BlockSpec((1, LANES), lambda h, n: (0, 0))
    return pl.pallas_call(
        body, name="gdn_fwd",
        grid=(nhb, nc),
        in_specs=[pl.BlockSpec((CHUNK, bw), lambda h, n: (n, h)),
                  pl.BlockSpec((CHUNK, bw), lambda h, n: (n, nqb + h)),
                  pl.BlockSpec((CHUNK, bw), lambda h, n: (n, 2 * nqb + h)),
                  pl.BlockSpec((CHUNK, bw), lambda h, n: (n, z_off // bw + h)),
                  pl.BlockSpec((CHUNK, LANES), lambda h, n: (n, small_off // LANES)),
                  par, par,
                  pl.BlockSpec((1, 1, DN_DK), lambda h, n: (0, 0, 0))] + [ANY] * ng,
        out_specs=[pl.BlockSpec((CHUNK, bw), lambda h, n: (n, h)),
                   pl.BlockSpec((None, hb, DN_DK, DN_DK), lambda h, n: (n, h, 0, 0)),
                   pl.BlockSpec((None, hb, CHUNK, CHUNK), lambda h, n: (n, h, 0, 0))] + [ANY] * ng,
        out_shape=[jax.ShapeDtypeStruct((t, out_cols), BF16),
                   jax.ShapeDtypeStruct((nc, heads, DN_DK, DN_DK), F32),
                   jax.ShapeDtypeStruct((nc, heads, CHUNK, CHUNK), F32)]
        + [jax.ShapeDtypeStruct(s.shape, s.dtype) for s in gather],
        input_output_aliases={8 + a: 3 + a for a in range(ng)},
        scratch_shapes=[pltpu.VMEM((hb, DN_DK, DN_DK), F32)] + (GatherJob.sems(ng) if ng else []),
        compiler_params=_cparams(("arbitrary", "arbitrary")),
    )(act_qkv, act_qkv, act_qkv, proj, proj, a_log3, dt_bias3, norm_w3, *gather)


def gdn_backward(act_qkv, proj, a_log3, dt_bias3, norm_w3, s_all, inv_all, d_mixed, *, n_pad, z_off, small_off,
                 exchange=()):
    t, w3 = act_qkv.shape
    wdn = w3 // 3
    heads = wdn // DN_DK
    hb = min(HEAD_BLOCK, heads)
    nhb = heads // hb
    nc = t // CHUNK
    bw = hb * DN_DK
    nqb = wdn // bw
    ne = len(exchange)

    def body(q_ref, k_ref, v_ref, z_ref, sm_ref, al_ref, db_ref, nw_ref, s_ref, inv_ref, do_ref, *rest):
        dq_ref, dk_ref, dv_ref, dz_ref, dsm_ref, dal_ref, ddb_ref, dnw_ref = rest[ne:ne + 8]
        ds_ref = rest[2 * ne + 8]
        i = pl.program_id(1)
        n = nc - 1 - i
        head0 = pl.program_id(0) * hb
        if ne:
            _run_beside(PairExchangeJob(rest[:ne], rest[ne + 8:2 * ne + 8], *rest[2 * ne + 9:]),
                        pl.program_id(0) * nc + i, nhb * nc)

        @pl.when(i == 0)
        def _():
            ds_ref[...] = jnp.zeros_like(ds_ref)
            dal_ref[...] = jnp.zeros_like(dal_ref)
            ddb_ref[...] = jnp.zeros_like(ddb_ref)
            dnw_ref[...] = jnp.zeros_like(dnw_ref)

        vm = _valid_rows(n, n_pad)
        step = functools.partial(_gdn_step, vm=vm, head0=head0, n_heads=heads, inv=inv_ref[...])
        _, vjp = jax.vjp(step, s_ref[...], _split_heads(q_ref[...], hb, DN_DK), _split_heads(k_ref[...], hb, DN_DK),
                         _split_heads(v_ref[...], hb, DN_DK), _split_heads(z_ref[...], hb, DN_DK),
                         sm_ref[...], al_ref[...], db_ref[...], nw_ref[...])
        ds, dq, dk, dv, dz, dsm, dal, ddb, dnw = vjp((ds_ref[...], _split_heads(do_ref[...], hb, DN_DK)))
        ds_ref[...] = ds
        dq_ref[...] = _merge_heads(dq)
        dk_ref[...] = _merge_heads(dk)
        dv_ref[...] = _merge_heads(dv)
        dz_ref[...] = _merge_heads(dz).astype(dz_ref.dtype)
        dsm_ref[...] = dsm
        dal_ref[...] += dal
        ddb_ref[...] += ddb
        dnw_ref[...] += dnw[0]

    rev = lambda n: nc - 1 - n
    par = pl.BlockSpec((1, LANES), lambda h, n: (0, 0))
    dpar = pl.BlockSpec((None, 1, LANES), lambda h, n: (h, 0, 0))
    blk = lambda off: pl.BlockSpec((CHUNK, bw), lambda h, n: (rev(n), off + h))
    return pl.pallas_call(
        body, name="gdn_bwd",
        grid=(nhb, nc),
        in_specs=[blk(0), blk(nqb), blk(2 * nqb), blk(z_off // bw),
                  pl.BlockSpec((CHUNK, LANES), lambda h, n: (rev(n), small_off // LANES)), par, par,
                  pl.BlockSpec((1, 1, DN_DK), lambda h, n: (0, 0, 0)),
                  pl.BlockSpec((None, hb, DN_DK, DN_DK), lambda h, n: (rev(n), h, 0, 0)),
                  pl.BlockSpec((None, hb, CHUNK, CHUNK), lambda h, n: (rev(n), h, 0, 0)),
                  blk(0)] + [ANY] * ne,
        out_specs=[blk(0), blk(0), blk(0), blk(z_off // bw),
                   pl.BlockSpec((None, CHUNK, LANES), lambda h, n: (h, rev(n), 0)), dpar, dpar,
                   pl.BlockSpec((None, 1, DN_DK), lambda h, n: (h, 0, 0))] + [ANY] * ne,
        out_shape=[jax.ShapeDtypeStruct((t, wdn), F32)] * 3
        + [jax.ShapeDtypeStruct((t, proj.shape[1]), BF16),
           jax.ShapeDtypeStruct((nhb, t, LANES), F32),
           jax.ShapeDtypeStruct((nhb, 1, LANES), F32), jax.ShapeDtypeStruct((nhb, 1, LANES), F32),
           jax.ShapeDtypeStruct((nhb, 1, DN_DK), F32)] + [_half_rows(s) for s in exchange],
        scratch_shapes=[pltpu.VMEM((hb, DN_DK, DN_DK), F32)] + (PairExchangeJob.sems(ne) if ne else []),
        compiler_params=_cparams(("arbitrary", "arbitrary")),
    )(act_qkv, act_qkv, act_qkv, proj, proj, a_log3, dt_bias3, norm_w3, s_all, inv_all, d_mixed, *exchange)


def _dot2(a, b, ca, cb):
    return lax.dot_general(a.astype(BF16), b.astype(BF16), (((ca,), (cb,)), ((), ())),
                           preferred_element_type=F32)


def _ssd_step(state, xa, bmat, cmat, z, small, a_log, dt_bias, dskip, norm_w, vm, lane0):
    causal, _, eye = _tri_masks()
    r_heads, p, n_state = state.shape
    gw = r_heads * p
    vm2 = vm[0]
    xa, bmat, cmat = xa * vm2, bmat * vm2, cmat * vm2
    dt_all = _softplus(small + dt_bias) * vm2
    a_all = dt_all * (-jnp.exp(a_log))
    acs_all = _running_sum(a_all)
    alast_all = jnp.sum(a_all, axis=0, keepdims=True)
    dt = _take_cols(dt_all, lane0, r_heads)
    acs = _take_cols(acs_all, lane0, r_heads)
    eacs = _take_cols(jnp.exp(acs_all), lane0, r_heads)
    etail = _take_cols(jnp.exp(alast_all - acs_all), lane0, r_heads)
    elast = _take_cols(jnp.exp(alast_all), lane0, r_heads)
    arow = _col2row(acs, eye)
    lmat = jnp.where(causal, jnp.exp(jnp.where(causal, acs - arow, 0.0)), 0.0)
    hsel = (lax.broadcasted_iota(jnp.int32, (r_heads, 1, gw), 2) // p
            == lax.broadcasted_iota(jnp.int32, (r_heads, 1, gw), 0))

    def spread(col):
        return jnp.sum(jnp.where(hsel, col, 0.0), axis=0)

    xdt = xa * spread(dt)
    cb = _dot2(cmat, bmat, 1, 1)
    m = (cb[None] * lmat).reshape(r_heads * CHUNK, CHUNK)
    yb = _dot2(m, xdt, 1, 0).reshape(r_heads, CHUNK, gw)
    y_diag = jnp.sum(jnp.where(hsel, yb, 0.0), axis=0)
    s2 = state.reshape(gw, n_state)
    y_off = _dot2(cmat, s2, 1, 1) * spread(eacs)
    upd = _dot2(xdt * spread(etail), bmat, 0, 0)
    new_state = state * elast + upd.reshape(r_heads, p, n_state)
    y = y_diag + y_off + xa * spread(dskip)
    y = y * _silu(z)
    y = y * lax.rsqrt(jnp.mean(y * y, -1, keepdims=True) + NORM_EPS) * norm_w
    return new_state, y


def _ssd_dims(act_xbc):
    t, wx = act_xbc.shape
    wm = wx - 2 * M2_GROUPS * M2_N
    gw = wm // M2_GROUPS
    return t, wm, gw, gw // M2_P, wm // M2_P, t // CHUNK


def ssd_forward(act_xbc, proj, a_log3, dt_bias3, dskip3, norm_w, mixed, *, n_pad, z_off, small_off, dt_lane,
                gather=()):
    t, wm, gw, rh, heads, nc = _ssd_dims(act_xbc)
    nb = wm // M2_N
    ob = (mixed.shape[1] - wm) // gw
    ng = len(gather)

    def body(x_ref, b_ref, c_ref, z_ref, dt_ref, al_ref, db_ref, dk_ref, nw_ref, _, *rest):
        out_ref, sall_ref = rest[ng:ng + 2]
        st_ref = rest[2 * ng + 2]
        n = pl.program_id(1)
        if ng:
            _run_beside(GatherJob(rest[ng + 2:2 * ng + 2], *rest[2 * ng + 3:]), pl.program_id(0) * nc + n,
                        M2_GROUPS * nc)

        @pl.when(n == 0)
        def _():
            st_ref[...] = jnp.zeros_like(st_ref)

        state = st_ref[...]
        sall_ref[...] = state
        new_state, y = _ssd_step(state, x_ref[...], b_ref[...], c_ref[...], z_ref[...], dt_ref[...],
                                 al_ref[...], db_ref[...], dk_ref[...], nw_ref[...], _valid_rows(n, n_pad),
                                 dt_lane + pl.program_id(0) * rh)
        st_ref[...] = new_state
        out_ref[...] = y.astype(out_ref.dtype)

    par = pl.BlockSpec((rh, 1, 1), lambda g, n: (g, 0, 0))
    row = pl.BlockSpec((1, LANES), lambda g, n: (0, 0))
    return pl.pallas_call(
        body, name="ssd_fwd",
        grid=(M2_GROUPS, nc),
        in_specs=[pl.BlockSpec((CHUNK, gw), lambda g, n: (n, g)),
                  pl.BlockSpec((CHUNK, M2_N), lambda g, n: (n, nb + g)),
                  pl.BlockSpec((CHUNK, M2_N), lambda g, n: (n, nb + M2_GROUPS + g)),
                  pl.BlockSpec((CHUNK, gw), lambda g, n: (n, z_off // gw + g)),
                  pl.BlockSpec((CHUNK, LANES), lambda g, n: (n, small_off // LANES)),
                  row, row, par,
                  pl.BlockSpec((1, gw), lambda g, n: (0, g)),
                  pl.BlockSpec(memory_space=pl.ANY)] + [ANY] * ng,
        out_specs=[pl.BlockSpec((CHUNK, gw), lambda g, n: (n, ob + g)),
                   pl.BlockSpec((None, rh, M2_P, M2_N), lambda g, n: (n, g, 0, 0))] + [ANY] * ng,
        out_shape=[jax.ShapeDtypeStruct(mixed.shape, mixed.dtype),
                   jax.ShapeDtypeStruct((nc, heads, M2_P, M2_N), F32)]
        + [jax.ShapeDtypeStruct(s.shape, s.dtype) for s in gather],
        input_output_aliases={9: 0, **{10 + a: 2 + a for a in range(ng)}},
        scratch_shapes=[pltpu.VMEM((rh, M2_P, M2_N), F32)] + (GatherJob.sems(ng) if ng else []),
        compiler_params=_cparams(("arbitrary", "arbitrary")),
    )(act_xbc, act_xbc, act_xbc, proj, proj, a_log3, dt_bias3, dskip3, norm_w, mixed, *gather)


def ssd_backward(act_xbc, proj, a_log3, dt_bias3, dskip3, norm_w, s_all, d_mixed, dproj, *, n_pad, z_off,
                 small_off, dt_lane, exchange=()):
    t, wm, gw, rh, heads, nc = _ssd_dims(act_xbc)
    nb = wm // M2_N
    ne = len(exchange)

    def body(x_ref, b_ref, c_ref, z_ref, dt_ref, al_ref, db_ref, dk_ref, nw_ref, s_ref, dy_ref, _, *rest):
        dx_ref, dbm_ref, dcm_ref, dz_ref, ddt_ref, dal_ref, ddb_ref, ddk_ref, dnw_ref = rest[ne:ne + 9]
        ds_ref = rest[2 * ne + 9]
        i = pl.program_id(1)
        n = nc - 1 - i
        if ne:
            _run_beside(ChipExchangeJob(rest[:ne], rest[ne + 9:2 * ne + 9], *rest[2 * ne + 10:]),
                        pl.program_id(0) * nc + i, M2_GROUPS * nc)

        @pl.when(i == 0)
        def _():
            ds_ref[...] = jnp.zeros_like(ds_ref)
            dal_ref[...] = jnp.zeros_like(dal_ref)
            ddb_ref[...] = jnp.zeros_like(ddb_ref)
            ddk_ref[...] = jnp.zeros_like(ddk_ref)
            dnw_ref[...] = jnp.zeros_like(dnw_ref)

        step = functools.partial(_ssd_step, vm=_valid_rows(n, n_pad), lane0=dt_lane + pl.program_id(0) * rh)
        _, vjp = jax.vjp(step, s_ref[...], x_ref[...], b_ref[...], c_ref[...], z_ref[...],
                         dt_ref[...], al_ref[...], db_ref[...], dk_ref[...], nw_ref[...])
        ds, dx, dbm, dcm, dz, ddt, dal, ddb, ddk, dnw = vjp((ds_ref[...], dy_ref[...]))
        ds_ref[...] = ds
        dx_ref[...] = dx
        dbm_ref[...] = dbm
        dcm_ref[...] = dcm
        dz_ref[...] = dz.astype(dz_ref.dtype)
        ddt_ref[...] = ddt
        dal_ref[...] += dal
        ddb_ref[...] += ddb
        ddk_ref[...] += ddk
        dnw_ref[...] += dnw

    rev = lambda n: nc - 1 - n
    par = pl.BlockSpec((rh, 1, 1), lambda g, n: (g, 0, 0))
    wide = lambda off: pl.BlockSpec((CHUNK, gw), lambda g, n: (rev(n), off + g))
    narrow = lambda off: pl.BlockSpec((CHUNK, M2_N), lambda g, n: (rev(n), off + g))
    col = pl.BlockSpec((None, CHUNK, LANES), lambda g, n: (g, rev(n), 0))
    row = pl.BlockSpec((1, LANES), lambda g, n: (0, 0))
    drow = pl.BlockSpec((None, 1, LANES), lambda g, n: (g, 0, 0))
    gn = M2_GROUPS * M2_N
    return pl.pallas_call(
        body, name="ssd_bwd",
        grid=(M2_GROUPS, nc),
        in_specs=[wide(0), narrow(nb), narrow(nb + M2_GROUPS), wide(z_off // gw),
                  pl.BlockSpec((CHUNK, LANES), lambda g, n: (rev(n), small_off // LANES)), row, row, par,
                  pl.BlockSpec((1, gw), lambda g, n: (0, g)),
                  pl.BlockSpec((None, rh, M2_P, M2_N), lambda g, n: (rev(n), g, 0, 0)),
                  wide(M2_GROUPS), pl.BlockSpec(memory_space=pl.ANY)] + [ANY] * ne,
        out_specs=[wide(0), narrow(0), narrow(0), wide(z_off // gw), col, drow, drow, par,
                   pl.BlockSpec((1, gw), lambda g, n: (0, g))] + [ANY] * ne,
        out_shape=[jax.ShapeDtypeStruct((t, wm), F32), jax.ShapeDtypeStruct((t, gn), F32),
                   jax.ShapeDtypeStruct((t, gn), F32), jax.ShapeDtypeStruct(dproj.shape, dproj.dtype),
                   jax.ShapeDtypeStruct((M2_GROUPS, t, LANES), F32),
                   jax.ShapeDtypeStruct((M2_GROUPS, 1, LANES), F32), jax.ShapeDtypeStruct((M2_GROUPS, 1, LANES), F32),
                   jax.ShapeDtypeStruct((heads, 1, 1), F32), jax.ShapeDtypeStruct((1, wm), F32)]
        + [jax.ShapeDtypeStruct((3,) + s.shape[1:], s.dtype) for s in exchange],
        input_output_aliases={11: 3},
        scratch_shapes=[pltpu.VMEM((rh, M2_P, M2_N), F32)] + (ChipExchangeJob.sems(ne) if ne else []),
        compiler_params=_cparams(("arbitrary", "arbitrary")),
    )(act_xbc, act_xbc, act_xbc, proj, proj, a_log3, dt_bias3, dskip3, norm_w, s_all, d_mixed, dproj, *exchange)


SUBLANES = 8
LANES = 128


def _pick(dim, target, align):
    best = None
    for d in range(align, min(dim, target) + 1, align):
        if dim % d == 0:
            best = d
    return dim if best is None else best


def _row_tile(t):
    return _pick(t, 512, 16)


def matmul(a, b, *, mode, out_dtype, name, tm=1056, tn=512, tk=2048, residual=None, out_shards=1,
           a_shards=1, b_shards=1, exchange=()):
    if mode == "tn":
        kd, m = a.shape
        n = b.shape[-1] * b_shards
    else:
        m, kd = a.shape[-2], a.shape[-1] * a_shards
        n = b.shape[1] if mode == "nn" else b.shape[0]
    tm = _pick(m, tm, LANES if mode == "tn" else 16)
    ks = kd // a_shards
    tk = _pick(ks, tk, LANES)
    nkb = ks // tk
    nk = kd // tk
    ns_o, ns_b = n // out_shards, n // b_shards
    tn = _pick(math.gcd(ns_o, ns_b), tn, LANES)
    npb_o, npb_b = ns_o // tn, ns_b // tn
    if mode == "tn":
        a_spec = pl.BlockSpec((tk, tm), lambda i, j, k: (k, i))
    elif a_shards == 1:
        a_spec = pl.BlockSpec((tm, tk), lambda i, j, k: (i, k))
    else:
        a_spec = pl.BlockSpec((None, tm, tk), lambda i, j, k: (k // nkb, i, k % nkb))
    contract = ((1,), (1,)) if mode == "nt" else ((1,), (0,))
    if mode == "nt":
        b_spec = pl.BlockSpec((tn, tk), lambda i, j, k: (j, k))
    elif b_shards == 1:
        b_spec = pl.BlockSpec((tk, tn), lambda i, j, k: (k, j))
    else:
        b_spec = pl.BlockSpec((None, tk, tn), lambda i, j, k: (j // npb_b, k, j % npb_b))
    has_res = residual is not None
    ne = len(exchange)
    grid = (m // tm, n // tn, nk)

    def body(*refs):
        refs = list(refs)
        a_ref, b_ref = refs[:2]
        del refs[:2]
        r_ref = refs.pop(0) if has_res else None
        ex_in = [refs.pop(0) for _ in range(ne)]
        o_ref = refs.pop(0)
        ex_out = [refs.pop(0) for _ in range(ne)]
        at_ref = refs.pop(0) if mode == "tn" else None
        acc_ref = refs.pop(0) if nk > 1 else None
        k = pl.program_id(2)
        if ne:
            step = (pl.program_id(0) * grid[1] + pl.program_id(1)) * nk + k
            _run_beside(ChipExchangeJob(ex_in, ex_out, *refs), step, grid[0] * grid[1] * nk)
        if mode == "tn":
            @pl.when(pl.program_id(1) == 0)
            def _():
                at_ref[k] = jnp.transpose(a_ref[...].astype(F32)).astype(BF16)

            lhs = at_ref[k]
        else:
            lhs = a_ref[...].astype(BF16)
        part = lax.dot_general(lhs, b_ref[...].astype(BF16), (contract, ((), ())), preferred_element_type=F32)

        def finish(total):
            if has_res:
                total = total + r_ref[...]
            o_ref[...] = total.astype(o_ref.dtype)

        if nk == 1:
            finish(part)
        else:
            @pl.when(k == 0)
            def _():
                acc_ref[...] = part

            @pl.when((k > 0) & (k < nk - 1))
            def _():
                acc_ref[...] += part

            @pl.when(k == nk - 1)
            def _():
                finish(acc_ref[...] + part)

    in_specs = [a_spec, b_spec]
    args = [a, b]
    if has_res:
        in_specs.append(pl.BlockSpec((tm, tn), lambda i, j, k: (i, j)))
        args.append(residual)
    if out_shards == 1:
        out_spec = pl.BlockSpec((tm, tn), lambda i, j, k: (i, j))
        out_shape = jax.ShapeDtypeStruct((m, n), out_dtype)
    else:
        out_spec = pl.BlockSpec((None, tm, tn), lambda i, j, k: (j // npb_o, i, j % npb_o))
        out_shape = jax.ShapeDtypeStruct((out_shards, m, ns_o), out_dtype)
    scratch = [pltpu.VMEM((nk, tm, tk), BF16)] if mode == "tn" else []
    if nk > 1:
        scratch.append(pltpu.VMEM((tm, tn), F32))
    if not ne:
        return pl.pallas_call(
            body, name=name, grid=grid,
            in_specs=in_specs, out_specs=out_spec, out_shape=out_shape, scratch_shapes=scratch,
            compiler_params=_cparams(("parallel", "arbitrary", "arbitrary")),
        )(*args)
    return pl.pallas_call(
        body, name=name, grid=grid,
        in_specs=in_specs + [ANY] * ne, out_specs=[out_spec] + [ANY] * ne,
        out_shape=[out_shape] + [jax.ShapeDtypeStruct((3,) + s.shape[1:], s.dtype) for s in exchange],
        scratch_shapes=scratch + ChipExchangeJob.sems(ne),
        compiler_params=_cparams(("arbitrary", "arbitrary", "arbitrary")),
    )(*args, *exchange)


def rmsnorm_forward(x, w, *, name):
    t, d = x.shape
    tm = _row_tile(t)

    def body(x_ref, w_ref, o_ref):
        xv = x_ref[...]
        r = lax.rsqrt(jnp.mean(xv * xv, -1, keepdims=True) + NORM_EPS)
        o_ref[...] = (xv * r * w_ref[...]).astype(o_ref.dtype)

    return pl.pallas_call(
        body, name=name, grid=(t // tm,),
        in_specs=[pl.BlockSpec((tm, d), lambda i: (i, 0)), pl.BlockSpec((1, d), lambda i: (0, 0))],
        out_specs=pl.BlockSpec((tm, d), lambda i: (i, 0)),
        out_shape=jax.ShapeDtypeStruct((t, d), BF16),
        compiler_params=_cparams(("parallel",)),
    )(x, w)


def _rmsnorm_grads(xv, wv, dy):
    r = lax.rsqrt(jnp.mean(xv * xv, -1, keepdims=True) + NORM_EPS)
    xh = xv * r
    g = dy * wv
    dx = r * (g - xh * jnp.mean(g * xh, -1, keepdims=True))
    return dx, jnp.sum(dy * xh, axis=0, keepdims=True)


def rmsnorm_backward(x, w, dy, dres, *, n_pad, name):
    t, d = x.shape
    tm = _row_tile(t)

    def body(x_ref, w_ref, dy_ref, dr_ref, dx_ref, dx16_ref, dw_ref):
        i = pl.program_id(0)

        @pl.when(i == 0)
        def _():
            dw_ref[...] = jnp.zeros_like(dw_ref)

        dx, dw = _rmsnorm_grads(x_ref[...], w_ref[...], dy_ref[...])
        rows = i * tm + lax.broadcasted_iota(jnp.int32, (tm, 1), 0)
        dx = jnp.where(rows >= n_pad, dx + dr_ref[...], 0.0)
        dx_ref[...] = dx
        dx16_ref[...] = dx.astype(BF16)
        dw_ref[...] += dw

    row = pl.BlockSpec((tm, d), lambda i: (i, 0))
    vec = pl.BlockSpec((1, d), lambda i: (0, 0))
    return pl.pallas_call(
        body, name=name, grid=(t // tm,),
        in_specs=[row, vec, row, row], out_specs=[row, row, vec],
        out_shape=[jax.ShapeDtypeStruct((t, d), F32), jax.ShapeDtypeStruct((t, d), BF16),
                   jax.ShapeDtypeStruct((1, d), F32)],
        compiler_params=_cparams(("arbitrary",)),
    )(x, w, dy, dres)


def loss_head(h, w, target, *, n_skip):
    t, d = h.shape
    tm = _row_tile(t)

    def body(x_ref, w_ref, y_ref, loss_ref, dx_ref, dx16_ref, dw_ref):
        i = pl.program_id(0)

        @pl.when(i == 0)
        def _():
            dw_ref[...] = jnp.zeros_like(dw_ref)
            loss_ref[...] = jnp.zeros_like(loss_ref)

        xv, wv = x_ref[...], w_ref[...]
        r = lax.rsqrt(jnp.mean(xv * xv, -1, keepdims=True) + NORM_EPS)
        rows = i * tm + lax.broadcasted_iota(jnp.int32, (tm, 1), 0)
        err = jnp.where(rows >= n_skip, xv * r * wv - y_ref[...], 0.0)
        loss_ref[...] += 0.5 * jnp.sum(jnp.mean(err * err, -1, keepdims=True))
        dx, dw = _rmsnorm_grads(xv, wv, err * (1.0 / d))
        dx_ref[...] = dx
        dx16_ref[...] = dx.astype(BF16)
        dw_ref[...] += dw

    row = pl.BlockSpec((tm, d), lambda i: (i, 0))
    vec = pl.BlockSpec((1, d), lambda i: (0, 0))
    return pl.pallas_call(
        body, name="loss_head", grid=(t // tm,),
        in_specs=[row, vec, row],
        out_specs=[pl.BlockSpec((1, LANES), lambda i: (0, 0)), row, row, vec],
        out_shape=[jax.ShapeDtypeStruct((1, LANES), F32), jax.ShapeDtypeStruct((t, d), F32),
                   jax.ShapeDtypeStruct((t, d), BF16), jax.ShapeDtypeStruct((1, d), F32)],
        compiler_params=_cparams(("arbitrary",)),
    )(h, w, target)


HALO = SUBLANES


STRIP = 32


def _taps(blk, kk, rows):
    return [blk[HALO - (kk - 1) + j:HALO - (kk - 1) + j + rows, :] for j in range(kk)]


def _fir(taps, w):
    acc = w[0:1, :] * taps[0]
    for j in range(1, len(taps)):
        acc = acc + w[j:j + 1, :] * taps[j]
    return acc


def _fir_transposed(dpre, w, rows):
    kk = w.shape[0]
    acc = w[0:1, :] * dpre[kk - 1:kk - 1 + rows, :]
    for j in range(1, kk):
        acc = acc + w[j:j + 1, :] * dpre[kk - 1 - j:kk - 1 - j + rows, :]
    return acc


def _fold8(v):
    return jnp.sum(v.reshape(v.shape[0] // SUBLANES, SUBLANES, v.shape[1]), axis=0)


def _strips(tm, body, init):
    def step(r, carry):
        return body(pl.multiple_of(r * STRIP, STRIP), carry)
    return lax.fori_loop(0, tm // STRIP, step, init)


def _dsilu(p):
    s = _sigmoid(p)
    return s * (1.0 + p * (1.0 - s))


def conv_silu(x, w, b, *, x_off, name):
    t = x.shape[0]
    kk, width = w.shape
    tm = _row_tile(t)
    tc = _pick(width, 512, LANES)
    ob, nh = x_off // tc, tm // HALO

    def body(prev_ref, x_ref, w_ref, b_ref, o_ref, scr):
        i = pl.program_id(1)
        scr[0:HALO, :] = jnp.where(i > 0, prev_ref[...], 0.0)
        scr[HALO:HALO + tm, :] = x_ref[...]
        wv, bv = w_ref[...], b_ref[...]

        def strip(base, carry):
            blk = scr[pl.ds(base, STRIP + HALO), :]
            o_ref[pl.ds(base, STRIP), :] = _silu(_fir(_taps(blk, kk, STRIP), wv) + bv)
            return carry

        _strips(tm, strip, 0)

    return pl.pallas_call(
        body, name=name, grid=(width // tc, t // tm),
        in_specs=[pl.BlockSpec((HALO, tc), lambda j, i: (jnp.maximum(i * nh - 1, 0), ob + j)),
                  pl.BlockSpec((tm, tc), lambda j, i: (i, ob + j)),
                  pl.BlockSpec((kk, tc), lambda j, i: (0, j)),
                  pl.BlockSpec((1, tc), lambda j, i: (0, j))],
        out_specs=pl.BlockSpec((tm, tc), lambda j, i: (i, j)),
        out_shape=jax.ShapeDtypeStruct((t, width), F32),
        scratch_shapes=[pltpu.VMEM((tm + HALO, tc), F32)],
        compiler_params=_cparams(("parallel", "arbitrary")),
    )(x, x, w, b)


def conv_silu_backward(x, w, b, dact, dst, *, x_off, name):
    t = x.shape[0]
    kk, width = w.shape
    tm = _row_tile(t)
    tc = _pick(width, 512, LANES)
    ob, nh, nt = x_off // tc, tm // HALO, t // tm
    last_h = t // HALO - 1

    def body(prev_ref, x_ref, next_ref, w_ref, b_ref, d_ref, dnext_ref, _, dx_ref, dw_ref, db_ref, scr_x, scr_d):
        i = pl.program_id(1)

        @pl.when(i == 0)
        def _():
            dw_ref[...] = jnp.zeros_like(dw_ref)
            db_ref[...] = jnp.zeros_like(db_ref)

        wv, bv = w_ref[...], b_ref[...]
        scr_x[0:HALO, :] = jnp.where(i > 0, prev_ref[...], 0.0)
        scr_x[HALO:HALO + tm, :] = x_ref[...]
        scr_x[HALO + tm:, :] = next_ref[...]
        scr_d[0:tm, :] = d_ref[...]
        scr_d[tm:, :] = jnp.where(i < nt - 1, dnext_ref[...], 0.0)

        def strip(base, carry):
            taps = _taps(scr_x[pl.ds(base, STRIP + 2 * HALO), :], kk, STRIP + HALO)
            dpre = scr_d[pl.ds(base, STRIP + HALO), :] * _dsilu(_fir(taps, wv) + bv)
            dx_ref[pl.ds(base, STRIP), :] = _fir_transposed(dpre, wv, STRIP).astype(dx_ref.dtype)
            d0 = dpre[0:STRIP, :]
            return tuple(c + _fold8(d0 * tap[0:STRIP, :]) for c, tap in zip(carry, taps)) + (carry[kk] + _fold8(d0),)

        sums = _strips(tm, strip, tuple(jnp.zeros((SUBLANES, tc), F32) for _ in range(kk + 1)))
        for j in range(kk):
            dw_ref[j:j + 1, :] += jnp.sum(sums[j], axis=0, keepdims=True)
        db_ref[0:1, :] += jnp.sum(sums[kk], axis=0, keepdims=True)

    nxt = lambda j, i: (jnp.minimum((i + 1) * nh, last_h), j)
    acc = pl.BlockSpec((SUBLANES, tc), lambda j, i: (0, j))
    return pl.pallas_call(
        body, name=name, grid=(width // tc, nt),
        in_specs=[pl.BlockSpec((HALO, tc), lambda j, i: (jnp.maximum(i * nh - 1, 0), ob + j)),
                  pl.BlockSpec((tm, tc), lambda j, i: (i, ob + j)),
                  pl.BlockSpec((HALO, tc), lambda j, i: (jnp.minimum((i + 1) * nh, last_h), ob + j)),
                  pl.BlockSpec((kk, tc), lambda j, i: (0, j)),
                  pl.BlockSpec((1, tc), lambda j, i: (0, j)),
                  pl.BlockSpec((tm, tc), lambda j, i: (i, j)),
                  pl.BlockSpec((HALO, tc), nxt),
                  pl.BlockSpec(memory_space=pl.ANY)],
        out_specs=[pl.BlockSpec((tm, tc), lambda j, i: (i, ob + j)), acc, acc],
        out_shape=[jax.ShapeDtypeStruct(dst.shape, dst.dtype), jax.ShapeDtypeStruct((SUBLANES, width), F32),
                   jax.ShapeDtypeStruct((SUBLANES, width), F32)],
        input_output_aliases={7: 0},
        scratch_shapes=[pltpu.VMEM((tm + 2 * HALO, tc), F32), pltpu.VMEM((tm + HALO, tc), F32)],
        compiler_params=_cparams(("parallel", "arbitrary")),
    )(x, x, x, w, b, dact, dact, dst)


def conv_glu(u, w, *, name):
    _, t, f = u.shape
    kk = w.shape[1]
    tm = _row_tile(t)
    tc = _pick(f, 512, LANES)
    nh = tm // HALO

    def body(prev_ref, x_ref, w_ref, o_ref, scr):
        i = pl.program_id(1)
        scr[:, 0:HALO, :] = jnp.where(i > 0, prev_ref[...], 0.0)
        scr[:, HALO:, :] = x_ref[...]
        wg, wv = w_ref[0], w_ref[1]

        def strip(base, carry):
            gate = _fir(_taps(scr[0, pl.ds(base, STRIP + HALO), :], kk, STRIP), wg)
            val = _fir(_taps(scr[1, pl.ds(base, STRIP + HALO), :], kk, STRIP), wv)
            o_ref[pl.ds(base, STRIP), :] = (_silu(gate) * val).astype(o_ref.dtype)
            return carry

        _strips(tm, strip, 0)

    return pl.pallas_call(
        body, name=name, grid=(f // tc, t // tm),
        in_specs=[pl.BlockSpec((2, HALO, tc), lambda j, i: (0, jnp.maximum(i * nh - 1, 0), j)),
                  pl.BlockSpec((2, tm, tc), lambda j, i: (0, i, j)),
                  pl.BlockSpec((2, kk, tc), lambda j, i: (0, 0, j))],
        out_specs=pl.BlockSpec((tm, tc), lambda j, i: (i, j)),
        out_shape=jax.ShapeDtypeStruct((t, f), BF16),
        scratch_shapes=[pltpu.VMEM((2, tm + HALO, tc), F32)],
        compiler_params=_cparams(("parallel", "arbitrary")),
    )(u, u, w)


def conv_glu_backward(u, w, dact, *, name):
    _, t, f = u.shape
    kk = w.shape[1]
    tm = _row_tile(t)
    tc = _pick(f, 512, LANES)
    nh, nt = tm // HALO, t // tm
    last_h = t // HALO - 1
    ext = tm + HALO
    first = HALO - (kk - 1)

    def body(prev_ref, x_ref, next_ref, w_ref, d_ref, dn_ref, du_ref, dw_ref, scr_x, scr_d):
        i = pl.program_id(1)

        @pl.when(i == 0)
        def _():
            dw_ref[...] = jnp.zeros_like(dw_ref)

        scr_x[:, 0:HALO, :] = jnp.where(i > 0, prev_ref[...], 0.0)
        scr_x[:, HALO:HALO + tm, :] = x_ref[...]
        scr_x[:, HALO + tm:, :] = next_ref[...]
        scr_d[0:tm, :] = d_ref[...]
        scr_d[tm:, :] = jnp.where(i < nt - 1, dn_ref[...], 0.0)
        ws = (w_ref[0], w_ref[1])

        def strip(base, carry):
            taps = [_taps(scr_x[h, pl.ds(base, STRIP + 2 * HALO), :], kk, STRIP + HALO) for h in range(2)]
            gate, val = _fir(taps[0], ws[0]), _fir(taps[1], ws[1])
            dact = scr_d[pl.ds(base, STRIP + HALO), :]
            s = _sigmoid(gate)
            dconv = (dact * val * (s * (1.0 + gate * (1.0 - s))), dact * (gate * s))
            out = []
            for h in range(2):
                du_ref[h, pl.ds(base, STRIP), :] = _fir_transposed(dconv[h], ws[h], STRIP).astype(du_ref.dtype)
                d0 = dconv[h][0:STRIP, :]
                out += [c + _fold8(d0 * tap[0:STRIP, :]) for c, tap in zip(carry[h * kk:(h + 1) * kk], taps[h])]
            return tuple(out)

        sums = _strips(tm, strip, tuple(jnp.zeros((SUBLANES, tc), F32) for _ in range(2 * kk)))
        for h in range(2):
            for j in range(kk):
                dw_ref[h, j:j + 1, :] += jnp.sum(sums[h * kk + j], axis=0, keepdims=True)

    return pl.pallas_call(
        body, name=name, grid=(f // tc, nt),
        in_specs=[pl.BlockSpec((2, HALO, tc), lambda j, i: (0, jnp.maximum(i * nh - 1, 0), j)),
                  pl.BlockSpec((2, tm, tc), lambda j, i: (0, i, j)),
                  pl.BlockSpec((2, HALO, tc), lambda j, i: (0, jnp.minimum((i + 1) * nh, last_h), j)),
                  pl.BlockSpec((2, kk, tc), lambda j, i: (0, 0, j)),
                  pl.BlockSpec((tm, tc), lambda j, i: (i, j)),
                  pl.BlockSpec((HALO, tc), lambda j, i: (jnp.minimum((i + 1) * nh, last_h), j))],
        out_specs=[pl.BlockSpec((2, tm, tc), lambda j, i: (0, i, j)),
                   pl.BlockSpec((2, SUBLANES, tc), lambda j, i: (0, 0, j))],
        out_shape=[jax.ShapeDtypeStruct((2, t, f), BF16), jax.ShapeDtypeStruct((2, SUBLANES, f), F32)],
        scratch_shapes=[pltpu.VMEM((2, tm + 2 * HALO, tc), F32), pltpu.VMEM((ext, tc), F32)],
        compiler_params=_cparams(("parallel", "arbitrary")),
    )(u, u, u, w, dact, dact)


ELEMENTWISE_BLOCK_BYTES = 3 * 1024 * 1024


def _rows_spec(a, tr):
    c = a.shape[-1]
    if a.ndim == 2:
        return pl.BlockSpec((tr, c), lambda i, *_: (i, 0))
    return pl.BlockSpec((None, tr, c), lambda i, *_: (0, i, 0))


def adamw(w, g, m, v, *, name):
    r, c = w.shape[-2:]
    tr = _pick(r, max(SUBLANES, ELEMENTWISE_BLOCK_BYTES // (4 * c) // SUBLANES * SUBLANES), SUBLANES)

    def body(w_ref, g_ref, m_ref, v_ref, d_ref, nm_ref, nv_ref):
        gv = g_ref[...]
        nm = ADAM_B1 * m_ref[...] + (1.0 - ADAM_B1) * gv
        nv = ADAM_B2 * v_ref[...] + (1.0 - ADAM_B2) * (gv * gv)
        m_hat = nm / (1.0 - ADAM_B1 ** ADAM_STEP)
        v_hat = nv / (1.0 - ADAM_B2 ** ADAM_STEP)
        d_ref[...] = -ADAM_LR * (m_hat / (jnp.sqrt(v_hat) + ADAM_EPS) + ADAM_WD * w_ref[...])
        nm_ref[...] = nm
        nv_ref[...] = nv

    shp = jax.ShapeDtypeStruct(w.shape, F32)
    return pl.pallas_call(
        body, name=name, grid=(r // tr,),
        in_specs=[_rows_spec(a, tr) for a in (w, g, m, v)], out_specs=[_rows_spec(w, tr)] * 3, out_shape=[shp] * 3,
        compiler_params=_cparams(("parallel",)),
    )(w, g, m, v)


def cast_into_slot(x, slot, n_slots, *, name):
    r, c = x.shape[-2:]
    tr = _pick(r, max(16, ELEMENTWISE_BLOCK_BYTES // (4 * c) // 16 * 16), 16)

    def body(s_ref, x_ref, o_ref):
        o_ref[...] = x_ref[...].astype(o_ref.dtype)

    grid_spec = pltpu.PrefetchScalarGridSpec(
        num_scalar_prefetch=1, grid=(r // tr,),
        in_specs=[_rows_spec(x, tr)],
        out_specs=pl.BlockSpec((None, tr, c), lambda i, s: (s[0], i, 0)))
    return pl.pallas_call(
        body, name=name, grid_spec=grid_spec,
        out_shape=jax.ShapeDtypeStruct((n_slots, r, c), BF16),
        compiler_params=_cparams(("arbitrary",)),
    )(jnp.reshape(slot, (1,)).astype(jnp.int32), x)


class Layout:
    def __init__(self, d):
        self.d = d
        self.h_dn = d // DN_DK
        self.h_m2 = d // M2_P
        self.gn = M2_GROUPS * M2_N
        self.w_xbc = d + 2 * self.gn
        self.z_off = 3 * d
        self.m2z_off = 4 * d
        self.xbc_off = 5 * d
        self.small_off = 5 * d + self.w_xbc
        self.n_small = 2 * self.h_dn + self.h_m2
        self.p = self.small_off + LANES
        self.p_orig = self.small_off + self.n_small

    def w_in_of_slots(self, slab):
        n_slots, rows, cs = slab.shape
        pieces = []
        for o0, _, ln in sorted(self._segments(), key=lambda seg: seg[1]):
            for s in range(n_slots):
                lo, hi = max(o0, s * cs), min(o0 + ln, (s + 1) * cs)
                if lo < hi:
                    pieces.append(slab[s, :, lo - s * cs:hi - s * cs])
        pieces.append(jnp.zeros((rows, LANES - self.n_small), slab.dtype))
        return jnp.concatenate(pieces, axis=1)

    def _segments(self):
        d, s2, so = self.d, 2 * self.h_dn, self.small_off
        return [(0, 0, 4 * d), (4 * d, so, s2), (4 * d + s2, 4 * d, so - 4 * d),
                (self.p_orig - self.h_m2, so + s2, self.h_m2)]

    def slots_of_w_in(self, w, n_slots):
        cs = self.p_orig // n_slots
        slots = []
        for s in range(n_slots):
            pieces = []
            for o0, k0, ln in self._segments():
                lo, hi = max(o0, s * cs), min(o0 + ln, (s + 1) * cs)
                if lo < hi:
                    pieces.append(w[:, k0 + lo - o0:k0 + hi - o0])
            slots.append(jnp.concatenate(pieces, axis=1))
        return jnp.stack(slots, axis=0)


def _on_lanes(v, first):
    return jnp.pad(v, ((0, 0), (first, LANES - first - v.shape[1])))


def local_step(h0, target, wts, slabs, chip, core, *, n_pad, n_meta):
    t, d = h0.shape
    lay = Layout(d)
    hd, hm = lay.h_dn, lay.h_m2
    zeros_b = jnp.zeros((1, 3 * d), F32)
    dn_al, dn_db = _on_lanes(wts["dn_a_log"], hd), _on_lanes(wts["dn_dt_bias"], hd)
    dn_nw = wts["dn_norm_w"].reshape(1, 1, DN_DK)
    m2_al, m2_db = _on_lanes(wts["m2_a_log"], 2 * hd), _on_lanes(wts["m2_dt_bias"], 2 * hd)
    m2_dk = wts["m2_d"].reshape(hm, 1, 1)

    hn1 = rmsnorm_forward(h0, wts["norm_mix_w"], name="norm_mix")
    proj = matmul(hn1, wts["w_in"], mode="nn", out_dtype=F32, name="in_proj", tn=1920)
    act_qkv = conv_silu(proj, wts["dn_conv_w"], zeros_b, x_off=0, name="dn_conv")
    act_xbc = conv_silu(proj, wts["m2_conv_w"], wts["m2_conv_b"], x_off=lay.xbc_off, name="m2_conv")
    small_at = dict(small_off=lay.small_off)
    dt_at = dict(small_off=lay.small_off, dt_lane=2 * hd)
    mixed, s_dn, inv_dn, ffn_up_all, ffn_down_all = gdn_forward(
        act_qkv, proj, dn_al, dn_db, dn_nw, n_pad=n_pad, z_off=lay.z_off, out_cols=2 * d,
        gather=[slabs["ffn_up"], slabs["ffn_down"]], **small_at)
    mixed, s_m2, w_out_all = ssd_forward(act_xbc, proj, m2_al, m2_db, m2_dk, wts["m2_norm_w"], mixed,
                                         n_pad=n_pad, z_off=lay.m2z_off, gather=[slabs["w_out"]], **dt_at)
    wts = dict(wts, w_out=w_out_all.reshape(-1, d), ffn_down=ffn_down_all.reshape(-1, d),
               ffn_up=jnp.transpose(ffn_up_all, (1, 0, 2)).reshape(d, -1))
    h1 = matmul(mixed, wts["w_out"], mode="nn", out_dtype=F32, name="out_proj", tk=2 * d, residual=h0)
    hn2 = rmsnorm_forward(h1, wts["norm_ffn_w"], name="norm_ffn")
    up = matmul(hn2, wts["ffn_up"], mode="nn", out_dtype=F32, name="ffn_up", tn=1408, out_shards=2)
    kf, f = wts["ffn_conv_w"].shape[0], wts["ffn_conv_w"].shape[1] // 2
    w_glu = jnp.transpose(wts["ffn_conv_w"].reshape(kf, 2, f), (1, 0, 2))
    act = conv_glu(up, w_glu, name="ffn_conv")
    h2 = matmul(act, wts["ffn_down"], mode="nn", out_dtype=F32, name="ffn_down", tk=f // 2, residual=h1)
    loss, dh2, dh2_16, d_nfw = loss_head(h2, wts["norm_final_w"].reshape(1, d), target, n_skip=n_pad + n_meta)

    g = {}
    d_act = matmul(dh2_16, wts["ffn_down"], mode="nt", out_dtype=F32, name="d_ffn_act", tn=1408)
    g["ffn_down"] = matmul(act, dh2_16, mode="tn", out_dtype=F32, name="dw_ffn_down", tm=512, tn=1024, tk=t)
    dup, d_fcw = conv_glu_backward(up, w_glu, d_act, name="d_ffn_conv")
    g["ffn_conv_w"] = jnp.transpose(d_fcw[:, :kf], (1, 0, 2)).reshape(kf, 2 * f)
    dhn2 = matmul(dup, wts["ffn_up"], mode="nt", out_dtype=F32, name="d_norm_ffn_out", tk=f, a_shards=2)
    g["ffn_up"] = matmul(hn2, dup, mode="tn", out_dtype=F32, name="dw_ffn_up", tm=512, tn=1408, tk=t,
                         b_shards=2, out_shards=4)
    dh1, dh1_16, g["norm_ffn_w"] = rmsnorm_backward(h1, wts["norm_ffn_w"], dhn2, dh2, n_pad=n_pad, name="d_norm_ffn")
    d_mixed = matmul(dh1_16, wts["w_out"], mode="nt", out_dtype=F32, name="d_mixed", tn=512)
    g["w_out"] = matmul(mixed, dh1_16, mode="tn", out_dtype=F32, name="dw_out", tm=512, tn=1024, tk=t)

    early = ("w_out", "ffn_up", "ffn_down")
    slots = [g[k] if g[k].ndim == 3 else g[k].reshape(N_CHIPS, -1, g[k].shape[1]) for k in early]
    dq, dk, dv, dproj, dsm_dn, g_al, g_db, g_nw, *from_sibling = gdn_backward(
        act_qkv, proj, dn_al, dn_db, dn_nw, s_dn, inv_dn, d_mixed, n_pad=n_pad, z_off=lay.z_off, exchange=slots,
        **small_at)
    g["dn_a_log"], g["dn_dt_bias"] = (jnp.sum(v, axis=0)[:, hd:2 * hd] for v in (g_al, g_db))
    g["dn_norm_w"] = jnp.sum(g_nw, axis=0)
    p16, own = [], []
    for k, slot, rb in zip(early, slots, from_sibling):
        a, b = pair_add(slot, rb, chip, core, name="grad_pair_add_" + k)
        p16.append(a)
        own.append(b)
    dxs, dbm, dcm, dproj, dsm_m2, g_al, g_db, g_dk, g["m2_norm_w"], *from_chips = ssd_backward(
        act_xbc, proj, m2_al, m2_db, m2_dk, wts["m2_norm_w"], s_m2, d_mixed, dproj,
        n_pad=n_pad, z_off=lay.m2z_off, exchange=p16, **dt_at)
    for k, o, q in zip(early, own, from_chips):
        g[k] = (o, q)
    g["m2_a_log"], g["m2_dt_bias"] = (jnp.sum(v, axis=0)[:, 2 * hd:2 * hd + hm] for v in (g_al, g_db))
    g["m2_d"] = g_dk.reshape(1, hm)

    kc = wts["dn_conv_w"].shape[0]
    dw_parts = []
    for idx, dpart in enumerate((dq, dk, dv)):
        dproj, dw, _ = conv_silu_backward(proj, wts["dn_conv_w"][:, idx * d:(idx + 1) * d], zeros_b[:, :d], dpart,
                                          dproj, x_off=idx * d, name=f"d_dn_conv{idx}")
        dw_parts.append(dw[:kc])
    g["dn_conv_w"] = jnp.concatenate(dw_parts, axis=1)
    dw_parts, db_parts = [], []
    off = 0
    for idx, dpart in enumerate((dxs, dbm, dcm)):
        wd = dpart.shape[1]
        dproj, dw, db = conv_silu_backward(proj, wts["m2_conv_w"][:, off:off + wd], wts["m2_conv_b"][:, off:off + wd],
                                           dpart, dproj, x_off=lay.xbc_off + off, name=f"d_m2_conv{idx}")
        dw_parts.append(dw[:kc])
        db_parts.append(db[:1])
        off += wd
    g["m2_conv_w"] = jnp.concatenate(dw_parts, axis=1)
    g["m2_conv_b"] = jnp.concatenate(db_parts, axis=1)
    dsmall = jnp.sum(dsm_dn, axis=0) + jnp.sum(dsm_m2, axis=0)
    dproj = lax.dynamic_update_slice(dproj, dsmall.astype(BF16), (0, lay.small_off))

    dw_in = matmul(hn1, dproj, mode="tn", out_dtype=F32, name="dw_in", tm=512, tn=896, tk=t)
    w_in_slots = lay.slots_of_w_in(dw_in, N_CHIPS)
    (from_sibling,) = pair_exchange([w_in_slots], name="grad_pair_exchange_w_in")
    p16_w_in, own_w_in = pair_add(w_in_slots, from_sibling, chip, core, name="grad_pair_add_w_in")
    dhn1, from_chips = matmul(dproj, wts["w_in"], mode="nt", out_dtype=F32, name="d_norm_mix_out", tk=lay.p // 3,
                              exchange=[p16_w_in])
    g["w_in"] = (own_w_in, from_chips)
    dh0, _, g["norm_mix_w"] = rmsnorm_backward(h0, wts["norm_mix_w"], dhn1, dh1, n_pad=n_pad, name="d_norm_mix")
    g["norm_final_w"] = d_nfw
    return loss[0, 0], dh0, g


MESH = pl.DeviceIdType.MESH
ANY = pl.BlockSpec(memory_space=pl.ANY)
N_CHIPS = 4
N_DEV = 8


def _mesh_pos():
    return lax.axis_index("x"), lax.axis_index("y"), lax.axis_index("c")


def _other_chips(x, y):
    return [(1 - x, y), (x, 1 - y), (1 - x, 1 - y)]


def _rcopy(src, dst, send_sems, recv_sems, k, to):
    return pltpu.make_async_remote_copy(src_ref=src, dst_ref=dst, send_sem=send_sems.at[k], recv_sem=recv_sems.at[k],
                                        device_id=to, device_id_type=MESH)


class GatherJob:
    def __init__(self, slabs, send_sems, recv_sems):
        self.slabs, self.send, self.recv = slabs, send_sems, recv_sems
        self.x, self.y, self.c = _mesh_pos()

    @staticmethod
    def sems(n):
        return [pltpu.SemaphoreType.DMA((6 * n,)), pltpu.SemaphoreType.DMA((6 * n,))]

    def _pieces(self):
        x, y, c = self.x, self.y, self.c
        for a, slab in enumerate(self.slabs):
            half = slab.shape[1] // 2
            for j, (px, py) in enumerate(_other_chips(x, y)):
                yield a, j, (px, py), slab, pl.ds(c * half, half), pl.ds((1 - c) * half, half)

    def _ici(self, a, j, chip, ref):
        return _rcopy(ref, ref, self.send, self.recv, 6 * a + j, (chip[0], chip[1], self.c))

    def _d2d(self, a, j, ref):
        return _rcopy(ref, ref, self.send, self.recv, 6 * a + 3 + j, (self.x, self.y, 1 - self.c))

    def begin(self):
        for a, j, chip, slab, mine, _ in self._pieces():
            self._ici(a, j, chip, slab.at[2 * self.x + self.y, mine]).start()

    def pass_on(self):
        for a, j, chip, slab, mine, _ in self._pieces():
            landed = slab.at[2 * chip[0] + chip[1], mine]
            self._ici(a, j, chip, landed).wait_recv()
            self._d2d(a, j, landed).start()

    def end(self):
        for a, j, chip, slab, mine, theirs in self._pieces():
            self._d2d(a, j, slab.at[2 * chip[0] + chip[1], theirs]).wait_recv()
        for a, j, chip, slab, mine, _ in self._pieces():
            self._ici(a, j, chip, slab.at[2 * self.x + self.y, mine]).wait_send()
            self._d2d(a, j, slab.at[2 * chip[0] + chip[1], mine]).wait_send()


class ChipExchangeJob:
    def __init__(self, parts, outs, send_sems, recv_sems):
        self.parts, self.outs, self.send, self.recv = parts, outs, send_sems, recv_sems
        self.x, self.y, self.c = _mesh_pos()

    @staticmethod
    def sems(n):
        return [pltpu.SemaphoreType.DMA((3 * n,)), pltpu.SemaphoreType.DMA((3 * n,))]

    def _copies(self):
        for a, (part, out) in enumerate(zip(self.parts, self.outs)):
            for j, (px, py) in enumerate(_other_chips(self.x, self.y)):
                yield _rcopy(part.at[2 * px + py], out.at[j], self.send, self.recv, 3 * a + j, (px, py, self.c))

    def begin(self):
        for cp in self._copies():
            cp.start()

    def end(self):
        for cp in self._copies():
            cp.wait()


class PairExchangeJob:
    def __init__(self, grads, outs, send_sems, recv_sems):
        self.grads, self.outs, self.send, self.recv = grads, outs, send_sems, recv_sems
        self.x, self.y, self.c = _mesh_pos()

    @staticmethod
    def sems(n):
        return [pltpu.SemaphoreType.DMA((n,)), pltpu.SemaphoreType.DMA((n,))]

    def _copies(self):
        for a, (grad, out) in enumerate(zip(self.grads, self.outs)):
            half = grad.shape[1] // 2
            yield _rcopy(grad.at[:, pl.ds((1 - self.c) * half, half), :], out, self.send, self.recv, a,
                         (self.x, self.y, 1 - self.c))

    def begin(self):
        for cp in self._copies():
            cp.start()

    def end(self):
        for cp in self._copies():
            cp.wait()


def gather_shards(slabs, *, name):
    n = len(slabs)

    def body(*refs):
        job = GatherJob(refs[n:2 * n], *refs[2 * n:])
        job.begin()
        job.pass_on()
        job.end()

    return pl.pallas_call(
        body, name=name,
        in_specs=[ANY] * n, out_specs=[ANY] * n,
        out_shape=[jax.ShapeDtypeStruct(s.shape, s.dtype) for s in slabs],
        input_output_aliases={a: a for a in range(n)},
        scratch_shapes=GatherJob.sems(n),
        compiler_params=pltpu.CompilerParams(has_side_effects=True),
    )(*slabs)


def _half_rows(s):
    return jax.ShapeDtypeStruct((s.shape[0], s.shape[1] // 2, s.shape[2]), s.dtype)


def pair_exchange(grads, *, name):
    n = len(grads)

    def body(*refs):
        job = PairExchangeJob(refs[:n], refs[n:2 * n], *refs[2 * n:])
        job.begin()
        job.end()

    return pl.pallas_call(
        body, name=name, in_specs=[ANY] * n, out_specs=[ANY] * n,
        out_shape=[_half_rows(s) for s in grads],
        scratch_shapes=PairExchangeJob.sems(n),
        compiler_params=pltpu.CompilerParams(has_side_effects=True),
    )(*grads)


def pair_join(wholes, *, name):
    n = len(wholes)

    def body(*refs):
        outs = refs[n:2 * n]
        send_sems, recv_sems = refs[2 * n:]
        x, y, c = _mesh_pos()
        cps = []
        for a in range(n):
            half = outs[a].shape[0] // 2
            rows = outs[a].at[pl.ds(c * half, half)]
            cp = _rcopy(rows, rows, send_sems, recv_sems, a, (x, y, 1 - c))
            cp.start()
            cps.append(cp)
        for a, cp in enumerate(cps):
            cp.wait_send()
            half = outs[a].shape[0] // 2
            theirs = outs[a].at[pl.ds((1 - c) * half, half)]
            _rcopy(theirs, theirs, send_sems, recv_sems, a, (x, y, 1 - c)).wait_recv()

    return pl.pallas_call(
        body, name=name, in_specs=[ANY] * n, out_specs=[ANY] * n,
        out_shape=[jax.ShapeDtypeStruct(s.shape, s.dtype) for s in wholes],
        input_output_aliases={a: a for a in range(n)},
        scratch_shapes=[pltpu.SemaphoreType.DMA((n,)), pltpu.SemaphoreType.DMA((n,))],
        compiler_params=pltpu.CompilerParams(has_side_effects=True),
    )(*wholes)


def gather_all(v, *, name):
    def body(in_ref, out_ref, send_sems, recv_sems, local_sem):
        x, y, c = _mesh_pos()
        mine = out_ref.at[4 * x + 2 * y + c]
        lc = pltpu.make_async_copy(in_ref, mine, local_sem)
        lc.start()
        cps = []
        for k in range(1, N_DEV):
            flip = lambda v, bit: 1 - v if (k >> bit) & 1 else v
            cp = _rcopy(in_ref, mine, send_sems, recv_sems, k - 1, (flip(x, 2), flip(y, 1), flip(c, 0)))
            cp.start()
            cps.append(cp)
        for cp in cps:
            cp.wait()
        lc.wait()

    return pl.pallas_call(
        body, name=name, in_specs=[ANY], out_specs=ANY,
        out_shape=jax.ShapeDtypeStruct((N_DEV,) + v.shape, v.dtype),
        scratch_shapes=[pltpu.SemaphoreType.DMA((N_DEV - 1,)), pltpu.SemaphoreType.DMA((N_DEV - 1,)),
                        pltpu.SemaphoreType.DMA(())],
        compiler_params=pltpu.CompilerParams(has_side_effects=True),
    )(v)


def _sum_tile(rows, cols):
    return _pick(rows, max(16, ELEMENTWISE_BLOCK_BYTES // (4 * cols) // 16 * 16), 16)


def pair_add(g, rb, chip, c, *, name):
    _, r, cols = g.shape
    half = r // 2
    tr = _sum_tile(half, cols)
    nrt = half // tr

    def body(s_ref, g_ref, rb_ref, p16_ref, own_ref):
        v = g_ref[...] + rb_ref[...]
        p16_ref[...] = v.astype(p16_ref.dtype)

        @pl.when(pl.program_id(1) == s_ref[0])
        def _():
            own_ref[...] = v

    grid_spec = pltpu.PrefetchScalarGridSpec(
        num_scalar_prefetch=1, grid=(nrt, N_CHIPS),
        in_specs=[pl.BlockSpec((None, tr, cols), lambda i, k, s: (k, s[1] * nrt + i, 0)),
                  pl.BlockSpec((None, tr, cols), lambda i, k, s: (k, i, 0))],
        out_specs=[pl.BlockSpec((None, tr, cols), lambda i, k, s: (k, i, 0)),
                   pl.BlockSpec((tr, cols), lambda i, k, s: (i, 0))])
    return pl.pallas_call(
        body, name=name, grid_spec=grid_spec,
        out_shape=[jax.ShapeDtypeStruct((N_CHIPS, half, cols), BF16), jax.ShapeDtypeStruct((half, cols), F32)],
        compiler_params=_cparams(("arbitrary", "arbitrary")),
    )(jnp.stack([chip, c]).astype(jnp.int32), g, rb)


def chip_add(own, q, c, *, name):
    r, cols = own.shape
    tr = _sum_tile(r, cols)
    nrt = r // tr

    def body(s_ref, own_ref, q_ref, o_ref):
        o_ref[...] = ((own_ref[...] + q_ref[0].astype(F32)) + q_ref[1].astype(F32)) + q_ref[2].astype(F32)

    grid_spec = pltpu.PrefetchScalarGridSpec(
        num_scalar_prefetch=1, grid=(nrt,),
        in_specs=[pl.BlockSpec((tr, cols), lambda i, s: (i, 0)), pl.BlockSpec((3, tr, cols), lambda i, s: (0, i, 0))],
        out_specs=pl.BlockSpec((tr, cols), lambda i, s: (s[0] * nrt + i, 0)))
    return pl.pallas_call(
        body, name=name, grid_spec=grid_spec,
        out_shape=jax.ShapeDtypeStruct((2 * r, cols), F32),
        compiler_params=_cparams(("arbitrary",)),
    )(jnp.reshape(c, (1,)).astype(jnp.int32), own, q)


def sum_slots(v, *, name):
    n, r, cols = v.shape
    tr = _sum_tile(r, cols)

    def body(v_ref, o_ref):
        acc = v_ref[0]
        for k in range(1, n):
            acc = acc + v_ref[k]
        o_ref[...] = acc

    return pl.pallas_call(
        body, name=name, grid=(r // tr,),
        in_specs=[pl.BlockSpec((n, tr, cols), lambda i: (0, i, 0))],
        out_specs=pl.BlockSpec((tr, cols), lambda i: (i, 0)),
        out_shape=jax.ShapeDtypeStruct((r, cols), F32),
        compiler_params=_cparams(("parallel",)),
    )(v)


PACK_ROWS = 16


def _pack(arrays):
    parts = []
    for a in arrays:
        flat = a.reshape(-1).astype(F32)
        size = PACK_ROWS * LANES
        pad = (-flat.shape[0]) % size
        parts.append(jnp.pad(flat, (0, pad)))
    return jnp.concatenate(parts).reshape(-1, LANES)


def _unpack(slab, shapes):
    out, row = [], 0
    for shp in shapes:
        n = 1
        for s in shp:
            n *= s
        rows = -(-n // (PACK_ROWS * LANES)) * PACK_ROWS
        out.append(slab[row:row + rows].reshape(-1)[:n].reshape(shp))
        row += rows
    return out


WEIGHT_NAMES = ("meta_tokens", "norm_mix_w", "w_in", "dn_conv_w", "dn_a_log", "dn_dt_bias", "dn_norm_w", "m2_conv_w",
                "m2_conv_b", "m2_a_log", "m2_dt_bias", "m2_d", "m2_norm_w", "w_out", "norm_ffn_w", "ffn_up",
                "ffn_conv_w", "ffn_down", "norm_final_w")
BIG = ("w_in", "w_out", "ffn_up", "ffn_down")
SMALL_SHARDED = ("meta_tokens", "dn_conv_w", "m2_conv_w", "ffn_conv_w")
SMALL = tuple(n for n in WEIGHT_NAMES if n not in BIG)


def kernel(x, meta_tokens, norm_mix_w, w_in, dn_conv_w, dn_a_log, dn_dt_bias, dn_norm_w, m2_conv_w, m2_conv_b, m2_a_log, m2_dt_bias, m2_d, m2_norm_w, w_out, norm_ffn_w, ffn_up, ffn_conv_w, ffn_down, norm_final_w, loss_target, m_meta_tokens, m_norm_mix_w, m_w_in, m_dn_conv_w, m_dn_a_log, m_dn_dt_bias, m_dn_norm_w, m_m2_conv_w, m_m2_conv_b, m_m2_a_log, m_m2_dt_bias, m_m2_d, m_m2_norm_w, m_w_out, m_norm_ffn_w, m_ffn_up, m_ffn_conv_w, m_ffn_down, m_norm_final_w, v_meta_tokens, v_norm_mix_w, v_w_in, v_dn_conv_w, v_dn_a_log, v_dn_dt_bias, v_dn_norm_w, v_m2_conv_w, v_m2_conv_b, v_m2_a_log, v_m2_dt_bias, v_m2_d, v_m2_norm_w, v_w_out, v_norm_ffn_w, v_ffn_up, v_ffn_conv_w, v_ffn_down, v_norm_final_w):
    args = tuple(locals().values())
    nw = len(WEIGHT_NAMES)
    wt = dict(zip(WEIGHT_NAMES, args[1:1 + nw]))
    mom = dict(zip(WEIGHT_NAMES, args[2 + nw:2 + 2 * nw]))
    var = dict(zip(WEIGHT_NAMES, args[2 + 2 * nw:2 + 3 * nw]))
    xi, yi, ci = _mesh_pos()
    chip = 2 * xi + yi
    seq, d = x.shape[1], x.shape[2]
    n_meta = wt["meta_tokens"].shape[0]
    n_pad = (-(n_meta + seq)) % ROW_ALIGN
    lay = Layout(d)

    small_local = [wt[k].reshape(wt[k].shape[-2:]) for k in SMALL_SHARDED]
    small_slab = _pack(small_local)
    small_slab = lax.dynamic_update_slice(jnp.zeros((N_CHIPS,) + small_slab.shape, F32), small_slab[None], (chip, 0, 0))
    slabs = {k: cast_into_slot(wt[k], chip, N_CHIPS, name="cast_" + k) for k in BIG}
    w_in_all, small_all = gather_shards([slabs.pop("w_in"), small_slab], name="gather_w_in")
    full = {"w_in": lay.w_in_of_slots(w_in_all)}
    per_chip = [_unpack(small_all[s], [a.shape for a in small_local]) for s in range(N_CHIPS)]
    for idx, k in enumerate(SMALL_SHARDED):
        full[k] = jnp.concatenate([per_chip[s][idx] for s in range(N_CHIPS)], axis=-1)
    for k in SMALL:
        if k not in SMALL_SHARDED:
            full[k] = wt[k]

    h0 = jnp.concatenate([jnp.zeros((n_pad, d), F32), full["meta_tokens"], x[0]], axis=0)
    target = jnp.concatenate([jnp.zeros((n_pad + n_meta, d), F32), loss_target[0]], axis=0)
    loss_local, dh0, g = local_step(h0, target, full, slabs, chip, ci, n_pad=n_pad, n_meta=n_meta)
    grad_x = dh0[n_pad + n_meta:][None]
    g["meta_tokens"] = dh0[n_pad:n_pad + n_meta]
    loss = lax.psum(loss_local, ("x", "y", "c"))

    wholes = [chip_add(*g[k], ci, name="grad_chip_add_" + k) for k in BIG]
    grad = dict(zip(BIG, pair_join(wholes, name="grad_pair_join")))

    small_shapes = [g[k].shape for k in SMALL]
    summed = sum_slots(gather_all(_pack([g[k] for k in SMALL]), name="small_grad_gather"), name="small_grad_sum")
    for k, v in zip(SMALL, _unpack(summed, small_shapes)):
        if k in SMALL_SHARDED:
            width = wt[k].shape[-1]
            v = lax.dynamic_slice_in_dim(v, chip * width, width, axis=v.ndim - 1)
        grad[k] = v

    delta, new_m, new_v = {}, {}, {}
    for k in BIG:
        delta[k], new_m[k], new_v[k] = adamw(wt[k], grad[k], mom[k], var[k], name="adamw_" + k)
    shapes = [wt[k].shape for k in SMALL]
    packed = adamw(_pack([wt[k] for k in SMALL]), _pack([grad[k] for k in SMALL]), _pack([mom[k] for k in SMALL]),
                   _pack([var[k] for k in SMALL]), name="adamw_small")
    for res, slab in zip((delta, new_m, new_v), packed):
        for k, v in zip(SMALL, _unpack(slab, shapes)):
            res[k] = v
    outs = [loss, grad_x]
    for res in (grad, delta, new_m, new_v):
        outs += [res[k].reshape(wt[k].shape) for k in WEIGHT_NAMES]
    return tuple(outs)
```

```python
import functools
import math

import jax
import jax.numpy as jnp
from jax import lax
from jax.experimental import pallas as pl
from jax.experimental.pallas import tpu as pltpu

F32 = jnp.float32
BF16 = jnp.bfloat16

CHUNK = 64
ROW_ALIGN = 128
NORM_EPS = 1e-6
DN_DK = 128
M2_P = 64
M2_N = 128
M2_GROUPS = 4
HEAD_BLOCK = 8
VMEM_LIMIT = 56 * 1024 * 1024

ADAM_LR, ADAM_B1, ADAM_B2, ADAM_EPS, ADAM_WD, ADAM_STEP = 0.001, 0.9, 0.999, 1e-08, 0.01, 10


def _cparams(sem=None):
    return pltpu.CompilerParams(dimension_semantics=sem, vmem_limit_bytes=VMEM_LIMIT)


def _silu(x):
    return x / (1.0 + jnp.exp(-x))


def _sigmoid(x):
    return 1.0 / (1.0 + jnp.exp(-x))


def _softplus(x):
    return jnp.maximum(x, 0.0) + jnp.log(1.0 + jnp.exp(-jnp.abs(x)))


def _tri_masks():
    r = lax.broadcasted_iota(jnp.int32, (CHUNK, CHUNK), 0)
    c = lax.broadcasted_iota(jnp.int32, (CHUNK, CHUNK), 1)
    return (r >= c)[None], (r > c)[None], (r == c)[None]


def _col2row(col, eye):
    return jnp.sum(jnp.where(eye, col, 0.0), axis=1, keepdims=True)


def _tri_sum(v, suffix):
    r = lax.broadcasted_iota(jnp.int32, (v.shape[0], v.shape[0]), 0)
    c = lax.broadcasted_iota(jnp.int32, (v.shape[0], v.shape[0]), 1)
    tri = jnp.where((r <= c) if suffix else (r >= c), 1.0, 0.0).astype(BF16)
    hi = v.astype(BF16)
    rest = v - hi.astype(F32)
    mid = rest.astype(BF16)
    lo = (rest - mid.astype(F32)).astype(BF16)
    dot = lambda part: jnp.dot(tri, part, preferred_element_type=F32)
    return dot(hi) + (dot(mid) + dot(lo))


@jax.custom_vjp
def _running_sum(v):
    return _tri_sum(v, False)


_running_sum.defvjp(lambda v: (_tri_sum(v, False), None), lambda _, g: (_tri_sum(g, True),))


def _bdot(a, b, dims):
    return lax.dot_general(a.astype(BF16), b.astype(BF16), (((dims[0],), (dims[1],)), ((0,), (0,))),
                           preferred_element_type=F32)


def _dot3(a, b, ca, cb):
    def split(v):
        hi = v.astype(BF16)
        return hi, (v - hi.astype(F32)).astype(BF16)

    def dot(p, q):
        return lax.dot_general(p, q, (((ca,), (cb,)), ((0,), (0,))), preferred_element_type=F32)

    (ah, al), (bh, bl) = split(a), split(b)
    return dot(ah, bh) + (dot(ah, bl) + dot(al, bh))


@jax.custom_vjp
def _bmm3(a, b):
    return _dot3(a, b, 2, 1)


def _bmm3_fwd(a, b):
    return _dot3(a, b, 2, 1), (a, b)


def _bmm3_bwd(res, g):
    a, b = res
    return _dot3(g, b, 2, 2), _dot3(a, g, 1, 1)


_bmm3.defvjp(_bmm3_fwd, _bmm3_bwd)


def _unit_lower_inverse(a_mat, eye):
    inv = jnp.where(eye, 1.0, 0.0) - a_mat
    pw = a_mat
    n = 2
    while n < CHUNK:
        pw = _bmm3(pw, pw)
        inv = inv + _bmm3(inv, pw)
        n *= 2
    return inv


@jax.custom_vjp
def _known_inverse(a_mat, inv):
    return inv


def _known_inverse_fwd(a_mat, inv):
    return inv, inv


def _known_inverse_bwd(inv, g):
    return -_dot3(_dot3(inv, g, 1, 1), inv, 2, 2), jnp.zeros_like(inv)


_known_inverse.defvjp(_known_inverse_fwd, _known_inverse_bwd)


def _gdn_step(state, qa, ka, va, z, small, a_log, dt_bias, norm_w, vm, head0, n_heads, inv=None, want_inv=False):
    causal, strict, eye = _tri_masks()
    nh = state.shape[0]
    qa, ka, va = qa * vm, ka * vm, va * vm
    q = qa * lax.rsqrt(jnp.sum(qa * qa, -1, keepdims=True) + NORM_EPS) * (DN_DK ** -0.5)
    k = ka * lax.rsqrt(jnp.sum(ka * ka, -1, keepdims=True) + NORM_EPS)
    g_all = -jnp.exp(a_log) * _softplus(small + dt_bias) * vm[0]
    gcum_all = _running_sum(g_all)
    glast_all = jnp.sum(g_all, axis=0, keepdims=True)
    beta = _take_cols(_sigmoid(small) * vm[0], head0, nh)
    gcum = _take_cols(gcum_all, n_heads + head0, nh)
    egc = _take_cols(jnp.exp(gcum_all), n_heads + head0, nh)
    etail = _take_cols(jnp.exp(glast_all - gcum_all), n_heads + head0, nh)
    elast = _take_cols(jnp.exp(glast_all), n_heads + head0, nh)
    grow = _col2row(gcum, eye)
    decay = jnp.where(causal, jnp.exp(jnp.where(causal, gcum - grow, 0.0)), 0.0)
    kk = _bdot(k, k, (2, 2))
    a_mat = jnp.where(strict, beta * kk * decay, 0.0)
    tinv = _unit_lower_inverse(a_mat, eye) if inv is None else _known_inverse(a_mat, inv)
    u = _bmm3(tinv, va * beta)
    w = _bmm3(tinv, k * (beta * egc))
    v_new = u - _bdot(w, state, (2, 1))
    o_inter = _bdot(q * egc, state, (2, 1))
    qk = _bdot(q, k, (2, 2)) * decay
    o = o_inter + _bdot(qk, v_new, (2, 1))
    new_state = state * elast + _bdot(k * etail, v_new, (1, 1))
    o = o * lax.rsqrt(jnp.mean(o * o, -1, keepdims=True) + NORM_EPS) * norm_w * _silu(z)
    return (new_state, o, tinv) if want_inv else (new_state, o)


def _valid_rows(chunk, n_pad):
    r = chunk * CHUNK + lax.broadcasted_iota(jnp.int32, (1, CHUNK, 1), 1)
    return jnp.where(r >= n_pad, 1.0, 0.0).astype(F32)


def _split_heads(x, n, w):
    return jnp.stack([x[:, i * w:(i + 1) * w] for i in range(n)], axis=0)


def _merge_heads(x):
    return jnp.concatenate([x[i] for i in range(x.shape[0])], axis=-1)


def _run_beside(job, step, n_steps):
    @pl.when(step == 0)
    def _():
        job.begin()

    if hasattr(job, "pass_on"):
        @pl.when(step == (7 * n_steps) // 8)
        def _():
            job.pass_on()

    @pl.when(step == n_steps - 1)
    def _():
        job.end()


def _take_cols(slab, first, n):
    lane = lax.broadcasted_iota(jnp.int32, (n, 1, slab.shape[1]), 2)
    col = lax.broadcasted_iota(jnp.int32, (n, 1, slab.shape[1]), 0)
    return jnp.sum(jnp.where(lane == first + col, slab[None], 0.0), axis=2, keepdims=True)


def gdn_forward(act_qkv, proj, a_log3, dt_bias3, norm_w3, *, n_pad, z_off, small_off, out_cols, gather=()):
    t, w3 = act_qkv.shape
    wdn = w3 // 3
    heads = wdn // DN_DK
    hb = min(HEAD_BLOCK, heads)
    nhb = heads // hb
    nc = t // CHUNK
    bw = hb * DN_DK
    nqb = wdn // bw
    ng = len(gather)

    def body(q_ref, k_ref, v_ref, z_ref, sm_ref, al_ref, db_ref, nw_ref, *rest):
        out_ref, sall_ref, inv_ref = rest[ng:ng + 3]
        st_ref = rest[2 * ng + 3]
        n = pl.program_id(1)
        if ng:
            _run_beside(GatherJob(rest[ng + 3:2 * ng + 3], *rest[2 * ng + 4:]), pl.program_id(0) * nc + n, nhb * nc)

        @pl.when(n == 0)
        def _():
            st_ref[...] = jnp.zeros_like(st_ref)

        state = st_ref[...]
        sall_ref[...] = state
        vm = _valid_rows(n, n_pad)
        head0 = pl.program_id(0) * hb
        new_state, o, tinv = _gdn_step(
            state, _split_heads(q_ref[...], hb, DN_DK), _split_heads(k_ref[...], hb, DN_DK),
            _split_heads(v_ref[...], hb, DN_DK), _split_heads(z_ref[...], hb, DN_DK),
            sm_ref[...], al_ref[...], db_ref[...], nw_ref[...], vm, head0, heads, want_inv=True)
        st_ref[...] = new_state
        inv_ref[...] = tinv
        out_ref[...] = _merge_heads(o).astype(out_ref.dtype)

    par = pl.BlockSpec((1, LANES), lambda h, n: (0, 0))
    return pl.pallas_call(
        body, name="gdn_fwd",
        grid=(nhb, nc),
        in_specs=[pl.BlockSpec((CHUNK, bw), lambda h, n: (n, h)),
                  pl.BlockSpec((CHUNK, bw), lambda h, n: (n, nqb + h)),
                  pl.BlockSpec((CHUNK, bw), lambda h, n: (n, 2 * nqb + h)),
                  pl.BlockSpec((CHUNK, bw), lambda h, n: (n, z_off // bw + h)),
                  pl.BlockSpec((CHUNK, LANES), lambda h, n: (n, small_off // LANES)),
                  par, par,
                  pl.BlockSpec((1, 1, DN_DK), lambda h, n: (0, 0, 0))] + [ANY] * ng,
        out_specs=[pl.BlockSpec((CHUNK, bw), lambda h, n: (n, h)),
                   pl.BlockSpec((None, hb, DN_DK, DN_DK), lambda h, n: (n, h, 0, 0)),
                   pl.BlockSpec((None, hb, CHUNK, CHUNK), lambda h, n: (n, h, 0, 0))] + [ANY] * ng,
        out_shape=[jax.ShapeDtypeStruct((t, out_cols), BF16),
                   jax.ShapeDtypeStruct((nc, heads, DN_DK, DN_DK), F32),
                   jax.ShapeDtypeStruct((nc, heads, CHUNK, CHUNK), F32)]
        + [jax.ShapeDtypeStruct(s.shape, s.dtype) for s in gather],
        input_output_aliases={8 + a: 3 + a for a in range(ng)},
        scratch_shapes=[pltpu.VMEM((hb, DN_DK, DN_DK), F32)] + (GatherJob.sems(ng) if ng else []),
        compiler_params=_cparams(("arbitrary", "arbitrary")),
    )(act_qkv, act_qkv, act_qkv, proj, proj, a_log3, dt_bias3, norm_w3, *gather)


def gdn_backward(act_qkv, proj, a_log3, dt_bias3, norm_w3, s_all, inv_all, d_mixed, *, n_pad, z_off, small_off,
                 exchange=()):
    t, w3 = act_qkv.shape
    wdn = w3 // 3
    heads = wdn // DN_DK
    hb = min(HEAD_BLOCK, heads)
    nhb = heads // hb
    nc = t // CHUNK
    bw = hb * DN_DK
    nqb = wdn // bw
    ne = len(exchange)

    def body(q_ref, k_ref, v_ref, z_ref, sm_ref, al_ref, db_ref, nw_ref, s_ref, inv_ref, do_ref, *rest):
        dq_ref, dk_ref, dv_ref, dz_ref, dsm_ref, dal_ref, ddb_ref, dnw_ref = rest[ne:ne + 8]
        ds_ref = rest[2 * ne + 8]
        i = pl.program_id(1)
        n = nc - 1 - i
        head0 = pl.program_id(0) * hb
        if ne:
            _run_beside(PairExchangeJob(rest[:ne], rest[ne + 8:2 * ne + 8], *rest[2 * ne + 9:]),
                        pl.program_id(0) * nc + i, nhb * nc)

        @pl.when(i == 0)
        def _():
            ds_ref[...] = jnp.zeros_like(ds_ref)
            dal_ref[...] = jnp.zeros_like(dal_ref)
            ddb_ref[...] = jnp.zeros_like(ddb_ref)
            dnw_ref[...] = jnp.zeros_like(dnw_ref)

        vm = _valid_rows(n, n_pad)
        step = functools.partial(_gdn_step, vm=vm, head0=head0, n_heads=heads, inv=inv_ref[...])
        _, vjp = jax.vjp(step, s_ref[...], _split_heads(q_ref[...], hb, DN_DK), _split_heads(k_ref[...], hb, DN_DK),
                         _split_heads(v_ref[...], hb, DN_DK), _split_heads(z_ref[...], hb, DN_DK),
                         sm_ref[...], al_ref[...], db_ref[...], nw_ref[...])
        ds, dq, dk, dv, dz, dsm, dal, ddb, dnw = vjp((ds_ref[...], _split_heads(do_ref[...], hb, DN_DK)))
        ds_ref[...] = ds
        dq_ref[...] = _merge_heads(dq)
        dk_ref[...] = _merge_heads(dk)
        dv_ref[...] = _merge_heads(dv)
        dz_ref[...] = _merge_heads(dz).astype(dz_ref.dtype)
        dsm_ref[...] = dsm
        dal_ref[...] += dal
        ddb_ref[...] += ddb
        dnw_ref[...] += dnw[0]

    rev = lambda n: nc - 1 - n
    par = pl.BlockSpec((1, LANES), lambda h, n: (0, 0))
    dpar = pl.BlockSpec((None, 1, LANES), lambda h, n: (h, 0, 0))
    blk = lambda off: pl.BlockSpec((CHUNK, bw), lambda h, n: (rev(n), off + h))
    return pl.pallas_call(
        body, name="gdn_bwd",
        grid=(nhb, nc),
        in_specs=[blk(0), blk(nqb), blk(2 * nqb), blk(z_off // bw),
                  pl.BlockSpec((CHUNK, LANES), lambda h, n: (rev(n), small_off // LANES)), par, par,
                  pl.BlockSpec((1, 1, DN_DK), lambda h, n: (0, 0, 0)),
                  pl.BlockSpec((None, hb, DN_DK, DN_DK), lambda h, n: (rev(n), h, 0, 0)),
                  pl.BlockSpec((None, hb, CHUNK, CHUNK), lambda h, n: (rev(n), h, 0, 0)),
                  blk(0)] + [ANY] * ne,
        out_specs=[blk(0), blk(0), blk(0), blk(z_off // bw),
                   pl.BlockSpec((None, CHUNK, LANES), lambda h, n: (h, rev(n), 0)), dpar, dpar,
                   pl.BlockSpec((None, 1, DN_DK), lambda h, n: (h, 0, 0))] + [ANY] * ne,
        out_shape=[jax.ShapeDtypeStruct((t, wdn), F32)] * 3
        + [jax.ShapeDtypeStruct((t, proj.shape[1]), BF16),
           jax.ShapeDtypeStruct((nhb, t, LANES), F32),
           jax.ShapeDtypeStruct((nhb, 1, LANES), F32), jax.ShapeDtypeStruct((nhb, 1, LANES), F32),
           jax.ShapeDtypeStruct((nhb, 1, DN_DK), F32)] + [_half_rows(s) for s in exchange],
        scratch_shapes=[pltpu.VMEM((hb, DN_DK, DN_DK), F32)] + (PairExchangeJob.sems(ne) if ne else []),
        compiler_params=_cparams(("arbitrary", "arbitrary")),
    )(act_qkv, act_qkv, act_qkv, proj, proj, a_log3, dt_bias3, norm_w3, s_all, inv_all, d_mixed, *exchange)


def _dot2(a, b, ca, cb):
    return lax.dot_general(a.astype(BF16), b.astype(BF16), (((ca,), (cb,)), ((), ())),
                           preferred_element_type=F32)


def _move_lanes(v, onto, back):
    hi = v.astype(BF16)
    rest = v - hi.astype(F32)
    mid = rest.astype(BF16)
    lo = (rest - mid.astype(F32)).astype(BF16)
    dims = (((1,), (1,)), ((), ())) if back else (((1,), (0,)), ((), ()))
    dot = lambda part: lax.dot_general(part, onto, dims, preferred_element_type=F32)
    return dot(hi) + (dot(mid) + dot(lo))


@jax.custom_vjp
def _spread_lanes(v, onto):
    return _move_lanes(v, onto, False)


_spread_lanes.defvjp(lambda v, onto: (_move_lanes(v, onto, False), onto),
                     lambda onto, g: (_move_lanes(g, onto, True), jnp.zeros_like(onto)))


def _ssd_step(state, xa, bmat, cmat, z, small, a_log, dt_bias, dskip, norm_w, vm, lane0):
    causal, _, eye = _tri_masks()
    r_heads, p, n_state = state.shape
    gw = r_heads * p
    vm2 = vm[0]
    xa, bmat, cmat = xa * vm2, bmat * vm2, cmat * vm2
    dt_all = _softplus(small + dt_bias) * vm2
    a_all = dt_all * (-jnp.exp(a_log))
    acs_all = _running_sum(a_all)
    alast_all = jnp.sum(a_all, axis=0, keepdims=True)
    acs = _take_cols(acs_all, lane0, r_heads)
    elast = _take_cols(jnp.exp(alast_all), lane0, r_heads)
    arow = _col2row(acs, eye)
    lmat = jnp.where(causal, jnp.exp(jnp.where(causal, acs - arow, 0.0)), 0.0)
    hsel = (lax.broadcasted_iota(jnp.int32, (r_heads, 1, gw), 2) // p
            == lax.broadcasted_iota(jnp.int32, (r_heads, 1, gw), 0))
    onto = jnp.where(lax.broadcasted_iota(jnp.int32, (LANES, gw), 0)
                     == lane0 + lax.broadcasted_iota(jnp.int32, (LANES, gw), 1) // p, 1.0, 0.0).astype(BF16)
    spread = lambda v: _spread_lanes(v, onto)

    xdt = xa * spread(dt_all)
    cb = _dot2(cmat, bmat, 1, 1)
    m = (cb[None] * lmat).reshape(r_heads * CHUNK, CHUNK)
    yb = _dot2(m, xdt, 1, 0).reshape(r_heads, CHUNK, gw)
    y_diag = jnp.sum(jnp.where(hsel, yb, 0.0), axis=0)
    s2 = state.reshape(gw, n_state)
    y_off = _dot2(cmat, s2, 1, 1) * spread(jnp.exp(acs_all))
    upd = _dot2(xdt * spread(jnp.exp(alast_all - acs_all)), bmat, 0, 0)
    new_state = state * elast + upd.reshape(r_heads, p, n_state)
    y = y_diag + y_off + xa * jnp.sum(jnp.where(hsel, dskip, 0.0), axis=0)
    y = y * _silu(z)
    y = y * lax.rsqrt(jnp.mean(y * y, -1, keepdims=True) + NORM_EPS) * norm_w
    return new_state, y


def _ssd_dims(act_xbc):
    t, wx = act_xbc.shape
    wm = wx - 2 * M2_GROUPS * M2_N
    gw = wm // M2_GROUPS
    return t, wm, gw, gw // M2_P, wm // M2_P, t // CHUNK


def ssd_forward(act_xbc, proj, a_log3, dt_bias3, dskip3, norm_w, mixed, *, n_pad, z_off, small_off, dt_lane,
                gather=()):
    t, wm, gw, rh, heads, nc = _ssd_dims(act_xbc)
    nb = wm // M2_N
    ob = (mixed.shape[1] - wm) // gw
    ng = len(gather)

    def body(x_ref, b_ref, c_ref, z_ref, dt_ref, al_ref, db_ref, dk_ref, nw_ref, _, *rest):
        out_ref, sall_ref = rest[ng:ng + 2]
        st_ref = rest[2 * ng + 2]
        n = pl.program_id(1)
        if ng:
            _run_beside(GatherJob(rest[ng + 2:2 * ng + 2], *rest[2 * ng + 3:]), pl.program_id(0) * nc + n,
                        M2_GROUPS * nc)

        @pl.when(n == 0)
        def _():
            st_ref[...] = jnp.zeros_like(st_ref)

        state = st_ref[...]
        sall_ref[...] = state
        new_state, y = _ssd_step(state, x_ref[...], b_ref[...], c_ref[...], z_ref[...], dt_ref[...],
                                 al_ref[...], db_ref[...], dk_ref[...], nw_ref[...], _valid_rows(n, n_pad),
                                 dt_lane + pl.program_id(0) * rh)
        st_ref[...] = new_state
        out_ref[...] = y.astype(out_ref.dtype)

    par = pl.BlockSpec((rh, 1, 1), lambda g, n: (g, 0, 0))
    row = pl.BlockSpec((1, LANES), lambda g, n: (0, 0))
    return pl.pallas_call(
        body, name="ssd_fwd",
        grid=(M2_GROUPS, nc),
        in_specs=[pl.BlockSpec((CHUNK, gw), lambda g, n: (n, g)),
                  pl.BlockSpec((CHUNK, M2_N), lambda g, n: (n, nb + g)),
                  pl.BlockSpec((CHUNK, M2_N), lambda g, n: (n, nb + M2_GROUPS + g)),
                  pl.BlockSpec((CHUNK, gw), lambda g, n: (n, z_off // gw + g)),
                  pl.BlockSpec((CHUNK, LANES), lambda g, n: (n, small_off // LANES)),
                  row, row, par,
                  pl.BlockSpec((1, gw), lambda g, n: (0, g)),
                  pl.BlockSpec(memory_space=pl.ANY)] + [ANY] * ng,
        out_specs=[pl.BlockSpec((CHUNK, gw), lambda g, n: (n, ob + g)),
                   pl.BlockSpec((None, rh, M2_P, M2_N), lambda g, n: (n, g, 0, 0))] + [ANY] * ng,
        out_shape=[jax.ShapeDtypeStruct(mixed.shape, mixed.dtype),
                   jax.ShapeDtypeStruct((nc, heads, M2_P, M2_N), F32)]
        + [jax.ShapeDtypeStruct(s.shape, s.dtype) for s in gather],
        input_output_aliases={9: 0, **{10 + a: 2 + a for a in range(ng)}},
        scratch_shapes=[pltpu.VMEM((rh, M2_P, M2_N), F32)] + (GatherJob.sems(ng) if ng else []),
        compiler_params=_cparams(("arbitrary", "arbitrary")),
    )(act_xbc, act_xbc, act_xbc, proj, proj, a_log3, dt_bias3, dskip3, norm_w, mixed, *gather)


def ssd_backward(act_xbc, proj, a_log3, dt_bias3, dskip3, norm_w, s_all, d_mixed, dproj, *, n_pad, z_off,
                 small_off, dt_lane, exchange=()):
    t, wm, gw, rh, heads, nc = _ssd_dims(act_xbc)
    nb = wm // M2_N
    ne = len(exchange)

    def body(x_ref, b_ref, c_ref, z_ref, dt_ref, al_ref, db_ref, dk_ref, nw_ref, s_ref, dy_ref, _, *rest):
        dx_ref, dbm_ref, dcm_ref, dz_ref, ddt_ref, dal_ref, ddb_ref, ddk_ref, dnw_ref = rest[ne:ne + 9]
        ds_ref = rest[2 * ne + 9]
        i = pl.program_id(1)
        n = nc - 1 - i
        if ne:
            _run_beside(ChipExchangeJob(rest[:ne], rest[ne + 9:2 * ne + 9], *rest[2 * ne + 10:]),
                        pl.program_id(0) * nc + i, M2_GROUPS * nc)

        @pl.when(i == 0)
        def _():
            ds_ref[...] = jnp.zeros_like(ds_ref)
            dal_ref[...] = jnp.zeros_like(dal_ref)
            ddb_ref[...] = jnp.zeros_like(ddb_ref)
            ddk_ref[...] = jnp.zeros_like(ddk_ref)
            dnw_ref[...] = jnp.zeros_like(dnw_ref)

        step = functools.partial(_ssd_step, vm=_valid_rows(n, n_pad), lane0=dt_lane + pl.program_id(0) * rh)
        _, vjp = jax.vjp(step, s_ref[...], x_ref[...], b_ref[...], c_ref[...], z_ref[...],
                         dt_ref[...], al_ref[...], db_ref[...], dk_ref[...], nw_ref[...])
        ds, dx, dbm, dcm, dz, ddt, dal, ddb, ddk, dnw = vjp((ds_ref[...], dy_ref[...]))
        ds_ref[...] = ds
        dx_ref[...] = dx
        dbm_ref[...] = dbm
        dcm_ref[...] = dcm
        dz_ref[...] = dz.astype(dz_ref.dtype)
        ddt_ref[...] = ddt
        dal_ref[...] += dal
        ddb_ref[...] += ddb
        ddk_ref[...] += ddk
        dnw_ref[...] += dnw

    rev = lambda n: nc - 1 - n
    par = pl.BlockSpec((rh, 1, 1), lambda g, n: (g, 0, 0))
    wide = lambda off: pl.BlockSpec((CHUNK, gw), lambda g, n: (rev(n), off + g))
    narrow = lambda off: pl.BlockSpec((CHUNK, M2_N), lambda g, n: (rev(n), off + g))
    col = pl.BlockSpec((None, CHUNK, LANES), lambda g, n: (g, rev(n), 0))
    row = pl.BlockSpec((1, LANES), lambda g, n: (0, 0))
    drow = pl.BlockSpec((None, 1, LANES), lambda g, n: (g, 0, 0))
    gn = M2_GROUPS * M2_N
    return pl.pallas_call(
        body, name="ssd_bwd",
        grid=(M2_GROUPS, nc),
        in_specs=[wide(0), narrow(nb), narrow(nb + M2_GROUPS), wide(z_off // gw),
                  pl.BlockSpec((CHUNK, LANES), lambda g, n: (rev(n), small_off // LANES)), row, row, par,
                  pl.BlockSpec((1, gw), lambda g, n: (0, g)),
                  pl.BlockSpec((None, rh, M2_P, M2_N), lambda g, n: (rev(n), g, 0, 0)),
                  wide(M2_GROUPS), pl.BlockSpec(memory_space=pl.ANY)] + [ANY] * ne,
        out_specs=[wide(0), narrow(0), narrow(0), wide(z_off // gw), col, drow, drow, par,
                   pl.BlockSpec((1, gw), lambda g, n: (0, g))] + [ANY] * ne,
        out_shape=[jax.ShapeDtypeStruct((t, wm), F32), jax.ShapeDtypeStruct((t, gn), F32),
                   jax.ShapeDtypeStruct((t, gn), F32), jax.ShapeDtypeStruct(dproj.shape, dproj.dtype),
                   jax.ShapeDtypeStruct((M2_GROUPS, t, LANES), F32),
                   jax.ShapeDtypeStruct((M2_GROUPS, 1, LANES), F32), jax.ShapeDtypeStruct((M2_GROUPS, 1, LANES), F32),
                   jax.ShapeDtypeStruct((heads, 1, 1), F32), jax.ShapeDtypeStruct((1, wm), F32)]
        + [jax.ShapeDtypeStruct((3,) + s.shape[1:], s.dtype) for s in exchange],
        input_output_aliases={11: 3},
        scratch_shapes=[pltpu.VMEM((rh, M2_P, M2_N), F32)] + (ChipExchangeJob.sems(ne) if ne else []),
        compiler_params=_cparams(("arbitrary", "arbitrary")),
    )(act_xbc, act_xbc, act_xbc, proj, proj, a_log3, dt_bias3, dskip3, norm_w, s_all, d_mixed, dproj, *exchange)


SUBLANES = 8
LANES = 128


def _pick(dim, target, align):
    best = None
    for d in range(align, min(dim, target) + 1, align):
        if dim % d == 0:
            best = d
    return dim if best is None else best


def _row_tile(t):
    return _pick(t, 512, 16)


def matmul(a, b, *, mode, out_dtype, name, tm=1056, tn=512, tk=2048, residual=None, out_shards=1,
           a_shards=1, b_shards=1, exchange=()):
    if mode == "tn":
        kd, m = a.shape
        n = b.shape[-1] * b_shards
    else:
        m, kd = a.shape[-2], a.shape[-1] * a_shards
        n = b.shape[1] if mode == "nn" else b.shape[0]
    tm = _pick(m, tm, LANES if mode == "tn" else 16)
    ks = kd // a_shards
    tk = _pick(ks, tk, LANES)
    nkb = ks // tk
    nk = kd // tk
    ns_o, ns_b = n // out_shards, n // b_shards
    tn = _pick(math.gcd(ns_o, ns_b), tn, LANES)
    npb_o, npb_b = ns_o // tn, ns_b // tn
    if mode == "tn":
        a_spec = pl.BlockSpec((tk, tm), lambda i, j, k: (k, i))
    elif a_shards == 1:
        a_spec = pl.BlockSpec((tm, tk), lambda i, j, k: (i, k))
    else:
        a_spec = pl.BlockSpec((None, tm, tk), lambda i, j, k: (k // nkb, i, k % nkb))
    contract = ((1,), (1,)) if mode == "nt" else ((1,), (0,))
    if mode == "nt":
        b_spec = pl.BlockSpec((tn, tk), lambda i, j, k: (j, k))
    elif b_shards == 1:
        b_spec = pl.BlockSpec((tk, tn), lambda i, j, k: (k, j))
    else:
        b_spec = pl.BlockSpec((None, tk, tn), lambda i, j, k: (j // npb_b, k, j % npb_b))
    has_res = residual is not None
    ne = len(exchange)
    grid = (m // tm, n // tn, nk)

    def body(*refs):
        refs = list(refs)
        a_ref, b_ref = refs[:2]
        del refs[:2]
        r_ref = refs.pop(0) if has_res else None
        ex_in = [refs.pop(0) for _ in range(ne)]
        o_ref = refs.pop(0)
        ex_out = [refs.pop(0) for _ in range(ne)]
        at_ref = refs.pop(0) if mode == "tn" else None
        acc_ref = refs.pop(0) if nk > 1 else None
        k = pl.program_id(2)
        if ne:
            step = (pl.program_id(0) * grid[1] + pl.program_id(1)) * nk + k
            _run_beside(ChipExchangeJob(ex_in, ex_out, *refs), step, grid[0] * grid[1] * nk)
        if mode == "tn":
            @pl.when(pl.program_id(1) == 0)
            def _():
                at_ref[k] = jnp.transpose(a_ref[...].astype(F32)).astype(BF16)

            lhs = at_ref[k]
        else:
            lhs = a_ref[...].astype(BF16)
        part = lax.dot_general(lhs, b_ref[...].astype(BF16), (contract, ((), ())), preferred_element_type=F32)

        def finish(total):
            if has_res:
                total = total + r_ref[...]
            o_ref[...] = total.astype(o_ref.dtype)

        if nk == 1:
            finish(part)
        else:
            @pl.when(k == 0)
            def _():
                acc_ref[...] = part

            @pl.when((k > 0) & (k < nk - 1))
            def _():
                acc_ref[...] += part

            @pl.when(k == nk - 1)
            def _():
                finish(acc_ref[...] + part)

    in_specs = [a_spec, b_spec]
    args = [a, b]
    if has_res:
        in_specs.append(pl.BlockSpec((tm, tn), lambda i, j, k: (i, j)))
        args.append(residual)
    if out_shards == 1:
        out_spec = pl.BlockSpec((tm, tn), lambda i, j, k: (i, j))
        out_shape = jax.ShapeDtypeStruct((m, n), out_dtype)
    else:
        out_spec = pl.BlockSpec((None, tm, tn), lambda i, j, k: (j // npb_o, i, j % npb_o))
        out_shape = jax.ShapeDtypeStruct((out_shards, m, ns_o), out_dtype)
    scratch = [pltpu.VMEM((nk, tm, tk), BF16)] if mode == "tn" else []
    if nk > 1:
        scratch.append(pltpu.VMEM((tm, tn), F32))
    if not ne:
        return pl.pallas_call(
            body, name=name, grid=grid,
            in_specs=in_specs, out_specs=out_spec, out_shape=out_shape, scratch_shapes=scratch,
            compiler_params=_cparams(("parallel", "arbitrary", "arbitrary")),
        )(*args)
    return pl.pallas_call(
        body, name=name, grid=grid,
        in_specs=in_specs + [ANY] * ne, out_specs=[out_spec] + [ANY] * ne,
        out_shape=[out_shape] + [jax.ShapeDtypeStruct((3,) + s.shape[1:], s.dtype) for s in exchange],
        scratch_shapes=scratch + ChipExchangeJob.sems(ne),
        compiler_params=_cparams(("arbitrary", "arbitrary", "arbitrary")),
    )(*args, *exchange)


def rmsnorm_forward(x, w, *, name):
    t, d = x.shape
    tm = _row_tile(t)

    def body(x_ref, w_ref, o_ref):
        xv = x_ref[...]
        r = lax.rsqrt(jnp.mean(xv * xv, -1, keepdims=True) + NORM_EPS)
        o_ref[...] = (xv * r * w_ref[...]).astype(o_ref.dtype)

    return pl.pallas_call(
        body, name=name, grid=(t // tm,),
        in_specs=[pl.BlockSpec((tm, d), lambda i: (i, 0)), pl.BlockSpec((1, d), lambda i: (0, 0))],
        out_specs=pl.BlockSpec((tm, d), lambda i: (i, 0)),
        out_shape=jax.ShapeDtypeStruct((t, d), BF16),
        compiler_params=_cparams(("parallel",)),
    )(x, w)


def _rmsnorm_grads(xv, wv, dy):
    r = lax.rsqrt(jnp.mean(xv * xv, -1, keepdims=True) + NORM_EPS)
    xh = xv * r
    g = dy * wv
    dx = r * (g - xh * jnp.mean(g * xh, -1, keepdims=True))
    return dx, jnp.sum(dy * xh, axis=0, keepdims=True)


def rmsnorm_backward(x, w, dy, dres, *, n_pad, name):
    t, d = x.shape
    tm = _row_tile(t)

    def body(x_ref, w_ref, dy_ref, dr_ref, dx_ref, dx16_ref, dw_ref):
        i = pl.program_id(0)

        @pl.when(i == 0)
        def _():
            dw_ref[...] = jnp.zeros_like(dw_ref)

        dx, dw = _rmsnorm_grads(x_ref[...], w_ref[...], dy_ref[...])
        rows = i * tm + lax.broadcasted_iota(jnp.int32, (tm, 1), 0)
        dx = jnp.where(rows >= n_pad, dx + dr_ref[...], 0.0)
        dx_ref[...] = dx
        dx16_ref[...] = dx.astype(BF16)
        dw_ref[...] += dw

    row = pl.BlockSpec((tm, d), lambda i: (i, 0))
    vec = pl.BlockSpec((1, d), lambda i: (0, 0))
    return pl.pallas_call(
        body, name=name, grid=(t // tm,),
        in_specs=[row, vec, row, row], out_specs=[row, row, vec],
        out_shape=[jax.ShapeDtypeStruct((t, d), F32), jax.ShapeDtypeStruct((t, d), BF16),
                   jax.ShapeDtypeStruct((1, d), F32)],
        compiler_params=_cparams(("arbitrary",)),
    )(x, w, dy, dres)


def loss_head(h, w, target, *, n_skip):
    t, d = h.shape
    tm = _row_tile(t)

    def body(x_ref, w_ref, y_ref, loss_ref, dx_ref, dx16_ref, dw_ref):
        i = pl.program_id(0)

        @pl.when(i == 0)
        def _():
            dw_ref[...] = jnp.zeros_like(dw_ref)
            loss_ref[...] = jnp.zeros_like(loss_ref)

        xv, wv = x_ref[...], w_ref[...]
        r = lax.rsqrt(jnp.mean(xv * xv, -1, keepdims=True) + NORM_EPS)
        rows = i * tm + lax.broadcasted_iota(jnp.int32, (tm, 1), 0)
        err = jnp.where(rows >= n_skip, xv * r * wv - y_ref[...], 0.0)
        loss_ref[...] += 0.5 * jnp.sum(jnp.mean(err * err, -1, keepdims=True))
        dx, dw = _rmsnorm_grads(xv, wv, err * (1.0 / d))
        dx_ref[...] = dx
        dx16_ref[...] = dx.astype(BF16)
        dw_ref[...] += dw

    row = pl.BlockSpec((tm, d), lambda i: (i, 0))
    vec = pl.BlockSpec((1, d), lambda i: (0, 0))
    return pl.pallas_call(
        body, name="loss_head", grid=(t // tm,),
        in_specs=[row, vec, row],
        out_specs=[pl.BlockSpec((1, LANES), lambda i: (0, 0)), row, row, vec],
        out_shape=[jax.ShapeDtypeStruct((1, LANES), F32), jax.ShapeDtypeStruct((t, d), F32),
                   jax.ShapeDtypeStruct((t, d), BF16), jax.ShapeDtypeStruct((1, d), F32)],
        compiler_params=_cparams(("arbitrary",)),
    )(h, w, target)


HALO = SUBLANES


STRIP = 32


def _taps(blk, kk, rows):
    return [blk[HALO - (kk - 1) + j:HALO - (kk - 1) + j + rows, :] for j in range(kk)]


def _fir(taps, w):
    acc = w[0:1, :] * taps[0]
    for j in range(1, len(taps)):
        acc = acc + w[j:j + 1, :] * taps[j]
    return acc


def _fir_transposed(dpre, w, rows):
    kk = w.shape[0]
    acc = w[0:1, :] * dpre[kk - 1:kk - 1 + rows, :]
    for j in range(1, kk):
        acc = acc + w[j:j + 1, :] * dpre[kk - 1 - j:kk - 1 - j + rows, :]
    return acc


def _fold8(v):
    return jnp.sum(v.reshape(v.shape[0] // SUBLANES, SUBLANES, v.shape[1]), axis=0)


def _strips(tm, body, init):
    def step(r, carry):
        return body(pl.multiple_of(r * STRIP, STRIP), carry)
    return lax.fori_loop(0, tm // STRIP, step, init)


def _dsilu(p):
    s = _sigmoid(p)
    return s * (1.0 + p * (1.0 - s))


def conv_silu(x, w, b, *, x_off, name):
    t = x.shape[0]
    kk, width = w.shape
    tm = _row_tile(t)
    tc = _pick(width, 512, LANES)
    ob, nh = x_off // tc, tm // HALO

    def body(prev_ref, x_ref, w_ref, b_ref, o_ref, scr):
        i = pl.program_id(1)
        scr[0:HALO, :] = jnp.where(i > 0, prev_ref[...], 0.0)
        scr[HALO:HALO + tm, :] = x_ref[...]
        wv, bv = w_ref[...], b_ref[...]

        def strip(base, carry):
            blk = scr[pl.ds(base, STRIP + HALO), :]
            o_ref[pl.ds(base, STRIP), :] = _silu(_fir(_taps(blk, kk, STRIP), wv) + bv)
            return carry

        _strips(tm, strip, 0)

    return pl.pallas_call(
        body, name=name, grid=(width // tc, t // tm),
        in_specs=[pl.BlockSpec((HALO, tc), lambda j, i: (jnp.maximum(i * nh - 1, 0), ob + j)),
                  pl.BlockSpec((tm, tc), lambda j, i: (i, ob + j)),
                  pl.BlockSpec((kk, tc), lambda j, i: (0, j)),
                  pl.BlockSpec((1, tc), lambda j, i: (0, j))],
        out_specs=pl.BlockSpec((tm, tc), lambda j, i: (i, j)),
        out_shape=jax.ShapeDtypeStruct((t, width), F32),
        scratch_shapes=[pltpu.VMEM((tm + HALO, tc), F32)],
        compiler_params=_cparams(("parallel", "arbitrary")),
    )(x, x, w, b)


def conv_silu_backward(x, w, b, dact, dst, *, x_off, name):
    t = x.shape[0]
    kk, width = w.shape
    tm = _row_tile(t)
    tc = _pick(width, 512, LANES)
    ob, nh, nt = x_off // tc, tm // HALO, t // tm
    last_h = t // HALO - 1

    def body(prev_ref, x_ref, next_ref, w_ref, b_ref, d_ref, dnext_ref, _, dx_ref, dw_ref, db_ref, scr_x, scr_d):
        i = pl.program_id(1)

        @pl.when(i == 0)
        def _():
            dw_ref[...] = jnp.zeros_like(dw_ref)
            db_ref[...] = jnp.zeros_like(db_ref)

        wv, bv = w_ref[...], b_ref[...]
        scr_x[0:HALO, :] = jnp.where(i > 0, prev_ref[...], 0.0)
        scr_x[HALO:HALO + tm, :] = x_ref[...]
        scr_x[HALO + tm:, :] = next_ref[...]
        scr_d[0:tm, :] = d_ref[...]
        scr_d[tm:, :] = jnp.where(i < nt - 1, dnext_ref[...], 0.0)

        def strip(base, carry):
            taps = _taps(scr_x[pl.ds(base, STRIP + 2 * HALO), :], kk, STRIP + HALO)
            dpre = scr_d[pl.ds(base, STRIP + HALO), :] * _dsilu(_fir(taps, wv) + bv)
            dx_ref[pl.ds(base, STRIP), :] = _fir_transposed(dpre, wv, STRIP).astype(dx_ref.dtype)
            d0 = dpre[0:STRIP, :]
            return tuple(c + _fold8(d0 * tap[0:STRIP, :]) for c, tap in zip(carry, taps)) + (carry[kk] + _fold8(d0),)

        sums = _strips(tm, strip, tuple(jnp.zeros((SUBLANES, tc), F32) for _ in range(kk + 1)))
        for j in range(kk):
            dw_ref[j:j + 1, :] += jnp.sum(sums[j], axis=0, keepdims=True)
        db_ref[0:1, :] += jnp.sum(sums[kk], axis=0, keepdims=True)

    nxt = lambda j, i: (jnp.minimum((i + 1) * nh, last_h), j)
    acc = pl.BlockSpec((SUBLANES, tc), lambda j, i: (0, j))
    return pl.pallas_call(
        body, name=name, grid=(width // tc, nt),
        in_specs=[pl.BlockSpec((HALO, tc), lambda j, i: (jnp.maximum(i * nh - 1, 0), ob + j)),
                  pl.BlockSpec((tm, tc), lambda j, i: (i, ob + j)),
                  pl.BlockSpec((HALO, tc), lambda j, i: (jnp.minimum((i + 1) * nh, last_h), ob + j)),
                  pl.BlockSpec((kk, tc), lambda j, i: (0, j)),
                  pl.BlockSpec((1, tc), lambda j, i: (0, j)),
                  pl.BlockSpec((tm, tc), lambda j, i: (i, j)),
                  pl.BlockSpec((HALO, tc), nxt),
                  pl.BlockSpec(memory_space=pl.ANY)],
        out_specs=[pl.BlockSpec((tm, tc), lambda j, i: (i, ob + j)), acc, acc],
        out_shape=[jax.ShapeDtypeStruct(dst.shape, dst.dtype), jax.ShapeDtypeStruct((SUBLANES, width), F32),
                   jax.ShapeDtypeStruct((SUBLANES, width), F32)],
        input_output_aliases={7: 0},
        scratch_shapes=[pltpu.VMEM((tm + 2 * HALO, tc), F32), pltpu.VMEM((tm + HALO, tc), F32)],
        compiler_params=_cparams(("parallel", "arbitrary")),
    )(x, x, x, w, b, dact, dact, dst)


def conv_glu(u, w, *, name):
    _, t, f = u.shape
    kk = w.shape[1]
    tm = _row_tile(t)
    tc = _pick(f, 512, LANES)
    nh = tm // HALO

    def body(prev_ref, x_ref, w_ref, o_ref, scr):
        i = pl.program_id(1)
        scr[:, 0:HALO, :] = jnp.where(i > 0, prev_ref[...], 0.0)
        scr[:, HALO:, :] = x_ref[...]
        wg, wv = w_ref[0], w_ref[1]

        def strip(base, carry):
            gate = _fir(_taps(scr[0, pl.ds(base, STRIP + HALO), :], kk, STRIP), wg)
            val = _fir(_taps(scr[1, pl.ds(base, STRIP + HALO), :], kk, STRIP), wv)
            o_ref[pl.ds(base, STRIP), :] = (_silu(gate) * val).astype(o_ref.dtype)
            return carry

        _strips(tm, strip, 0)

    return pl.pallas_call(
        body, name=name, grid=(f // tc, t // tm),
        in_specs=[pl.BlockSpec((2, HALO, tc), lambda j, i: (0, jnp.maximum(i * nh - 1, 0), j)),
                  pl.BlockSpec((2, tm, tc), lambda j, i: (0, i, j)),
                  pl.BlockSpec((2, kk, tc), lambda j, i: (0, 0, j))],
        out_specs=pl.BlockSpec((tm, tc), lambda j, i: (i, j)),
        out_shape=jax.ShapeDtypeStruct((t, f), BF16),
        scratch_shapes=[pltpu.VMEM((2, tm + HALO, tc), F32)],
        compiler_params=_cparams(("parallel", "arbitrary")),
    )(u, u, w)


def conv_glu_backward(u, w, dact, *, name):
    _, t, f = u.shape
    kk = w.shape[1]
    tm = _row_tile(t)
    tc = _pick(f, 512, LANES)
    nh, nt = tm // HALO, t // tm
    last_h = t // HALO - 1
    ext = tm + HALO
    first = HALO - (kk - 1)

    def body(prev_ref, x_ref, next_ref, w_ref, d_ref, dn_ref, du_ref, dw_ref, scr_x, scr_d):
        i = pl.program_id(1)

        @pl.when(i == 0)
        def _():
            dw_ref[...] = jnp.zeros_like(dw_ref)

        scr_x[:, 0:HALO, :] = jnp.where(i > 0, prev_ref[...], 0.0)
        scr_x[:, HALO:HALO + tm, :] = x_ref[...]
        scr_x[:, HALO + tm:, :] = next_ref[...]
        scr_d[0:tm, :] = d_ref[...]
        scr_d[tm:, :] = jnp.where(i < nt - 1, dn_ref[...], 0.0)
        ws = (w_ref[0], w_ref[1])

        def strip(base, carry):
            taps = [_taps(scr_x[h, pl.ds(base, STRIP + 2 * HALO), :], kk, STRIP + HALO) for h in range(2)]
            gate, val = _fir(taps[0], ws[0]), _fir(taps[1], ws[1])
            dact = scr_d[pl.ds(base, STRIP + HALO), :]
            s = _sigmoid(gate)
            dconv = (dact * val * (s * (1.0 + gate * (1.0 - s))), dact * (gate * s))
            out = []
            for h in range(2):
                du_ref[h, pl.ds(base, STRIP), :] = _fir_transposed(dconv[h], ws[h], STRIP).astype(du_ref.dtype)
                d0 = dconv[h][0:STRIP, :]
                out += [c + _fold8(d0 * tap[0:STRIP, :]) for c, tap in zip(carry[h * kk:(h + 1) * kk], taps[h])]
            return tuple(out)

        sums = _strips(tm, strip, tuple(jnp.zeros((SUBLANES, tc), F32) for _ in range(2 * kk)))
        for h in range(2):
            for j in range(kk):
                dw_ref[h, j:j + 1, :] += jnp.sum(sums[h * kk + j], axis=0, keepdims=True)

    return pl.pallas_call(
        body, name=name, grid=(f // tc, nt),
        in_specs=[pl.BlockSpec((2, HALO, tc), lambda j, i: (0, jnp.maximum(i * nh - 1, 0), j)),
                  pl.BlockSpec((2, tm, tc), lambda j, i: (0, i, j)),
                  pl.BlockSpec((2, HALO, tc), lambda j, i: (0, jnp.minimum((i + 1) * nh, last_h), j)),
                  pl.BlockSpec((2, kk, tc), lambda j, i: (0, 0, j)),
                  pl.BlockSpec((tm, tc), lambda j, i: (i, j)),
                  pl.BlockSpec((HALO, tc), lambda j, i: (jnp.minimum((i + 1) * nh, last_h), j))],
        out_specs=[pl.BlockSpec((2, tm, tc), lambda j, i: (0, i, j)),
                   pl.BlockSpec((2, SUBLANES, tc), lambda j, i: (0, 0, j))],
        out_shape=[jax.ShapeDtypeStruct((2, t, f), BF16), jax.ShapeDtypeStruct((2, SUBLANES, f), F32)],
        scratch_shapes=[pltpu.VMEM((2, tm + 2 * HALO, tc), F32), pltpu.VMEM((ext, tc), F32)],
        compiler_params=_cparams(("parallel", "arbitrary")),
    )(u, u, u, w, dact, dact)


ELEMENTWISE_BLOCK_BYTES = 3 * 1024 * 1024


def _rows_spec(a, tr):
    c = a.shape[-1]
    if a.ndim == 2:
        return pl.BlockSpec((tr, c), lambda i, *_: (i, 0))
    return pl.BlockSpec((None, tr, c), lambda i, *_: (0, i, 0))


def adamw(w, g, m, v, *, name):
    r, c = w.shape[-2:]
    tr = _pick(r, max(SUBLANES, ELEMENTWISE_BLOCK_BYTES // (4 * c) // SUBLANES * SUBLANES), SUBLANES)

    def body(w_ref, g_ref, m_ref, v_ref, d_ref, nm_ref, nv_ref):
        gv = g_ref[...]
        nm = ADAM_B1 * m_ref[...] + (1.0 - ADAM_B1) * gv
        nv = ADAM_B2 * v_ref[...] + (1.0 - ADAM_B2) * (gv * gv)
        m_hat = nm / (1.0 - ADAM_B1 ** ADAM_STEP)
        v_hat = nv / (1.0 - ADAM_B2 ** ADAM_STEP)
        d_ref[...] = -ADAM_LR * (m_hat / (jnp.sqrt(v_hat) + ADAM_EPS) + ADAM_WD * w_ref[...])
        nm_ref[...] = nm
        nv_ref[...] = nv

    shp = jax.ShapeDtypeStruct(w.shape, F32)
    return pl.pallas_call(
        body, name=name, grid=(r // tr,),
        in_specs=[_rows_spec(a, tr) for a in (w, g, m, v)], out_specs=[_rows_spec(w, tr)] * 3, out_shape=[shp] * 3,
        compiler_params=_cparams(("parallel",)),
    )(w, g, m, v)


def cast_into_slot(x, slot, n_slots, *, name):
    r, c = x.shape[-2:]
    tr = _pick(r, max(16, ELEMENTWISE_BLOCK_BYTES // (4 * c) // 16 * 16), 16)

    def body(s_ref, x_ref, o_ref):
        o_ref[...] = x_ref[...].astype(o_ref.dtype)

    grid_spec = pltpu.PrefetchScalarGridSpec(
        num_scalar_prefetch=1, grid=(r // tr,),
        in_specs=[_rows_spec(x, tr)],
        out_specs=pl.BlockSpec((None, tr, c), lambda i, s: (s[0], i, 0)))
    return pl.pallas_call(
        body, name=name, grid_spec=grid_spec,
        out_shape=jax.ShapeDtypeStruct((n_slots, r, c), BF16),
        compiler_params=_cparams(("arbitrary",)),
    )(jnp.reshape(slot, (1,)).astype(jnp.int32), x)


class Layout:
    def __init__(self, d):
        self.d = d
        self.h_dn = d // DN_DK
        self.h_m2 = d // M2_P
        self.gn = M2_GROUPS * M2_N
        self.w_xbc = d + 2 * self.gn
        self.z_off = 3 * d
        self.m2z_off = 4 * d
        self.xbc_off = 5 * d
        self.small_off = 5 * d + self.w_xbc
        self.n_small = 2 * self.h_dn + self.h_m2
        self.p = self.small_off + LANES
        self.p_orig = self.small_off + self.n_small

    def w_in_of_slots(self, slab):
        n_slots, rows, cs = slab.shape
        pieces = []
        for o0, _, ln in sorted(self._segments(), key=lambda seg: seg[1]):
            for s in range(n_slots):
                lo, hi = max(o0, s * cs), min(o0 + ln, (s + 1) * cs)
                if lo < hi:
                    pieces.append(slab[s, :, lo - s * cs:hi - s * cs])
        pieces.append(jnp.zeros((rows, LANES - self.n_small), slab.dtype))
        return jnp.concatenate(pieces, axis=1)

    def _segments(self):
        d, s2, so = self.d, 2 * self.h_dn, self.small_off
        return [(0, 0, 4 * d), (4 * d, so, s2), (4 * d + s2, 4 * d, so - 4 * d),
                (self.p_orig - self.h_m2, so + s2, self.h_m2)]

    def slots_of_w_in(self, w, n_slots):
        cs = self.p_orig // n_slots
        slots = []
        for s in range(n_slots):
            pieces = []
            for o0, k0, ln in self._segments():
                lo, hi = max(o0, s * cs), min(o0 + ln, (s + 1) * cs)
                if lo < hi:
                    pieces.append(w[:, k0 + lo - o0:k0 + hi - o0])
            slots.append(jnp.concatenate(pieces, axis=1))
        return jnp.stack(slots, axis=0)


def _on_lanes(v, first):
    return jnp.pad(v, ((0, 0), (first, LANES - first - v.shape[1])))


def local_step(h0, target, wts, slabs, chip, core, *, n_pad, n_meta):
    t, d = h0.shape
    lay = Layout(d)
    hd, hm = lay.h_dn, lay.h_m2
    zeros_b = jnp.zeros((1, 3 * d), F32)
    dn_al, dn_db = _on_lanes(wts["dn_a_log"], hd), _on_lanes(wts["dn_dt_bias"], hd)
    dn_nw = wts["dn_norm_w"].reshape(1, 1, DN_DK)
    m2_al, m2_db = _on_lanes(wts["m2_a_log"], 2 * hd), _on_lanes(wts["m2_dt_bias"], 2 * hd)
    m2_dk = wts["m2_d"].reshape(hm, 1, 1)

    hn1 = rmsnorm_forward(h0, wts["norm_mix_w"], name="norm_mix")
    proj = matmul(hn1, wts["w_in"], mode="nn", out_dtype=F32, name="in_proj", tn=1920)
    act_qkv = conv_silu(proj, wts["dn_conv_w"], zeros_b, x_off=0, name="dn_conv")
    act_xbc = conv_silu(proj, wts["m2_conv_w"], wts["m2_conv_b"], x_off=lay.xbc_off, name="m2_conv")
    small_at = dict(small_off=lay.small_off)
    dt_at = dict(small_off=lay.small_off, dt_lane=2 * hd)
    mixed, s_dn, inv_dn, ffn_up_all, ffn_down_all = gdn_forward(
        act_qkv, proj, dn_al, dn_db, dn_nw, n_pad=n_pad, z_off=lay.z_off, out_cols=2 * d,
        gather=[slabs["ffn_up"], slabs["ffn_down"]], **small_at)
    mixed, s_m2, w_out_all = ssd_forward(act_xbc, proj, m2_al, m2_db, m2_dk, wts["m2_norm_w"], mixed,
                                         n_pad=n_pad, z_off=lay.m2z_off, gather=[slabs["w_out"]], **dt_at)
    wts = dict(wts, w_out=w_out_all.reshape(-1, d), ffn_down=ffn_down_all.reshape(-1, d),
               ffn_up=jnp.transpose(ffn_up_all, (1, 0, 2)).reshape(d, -1))
    h1 = matmul(mixed, wts["w_out"], mode="nn", out_dtype=F32, name="out_proj", tk=2 * d, residual=h0)
    hn2 = rmsnorm_forward(h1, wts["norm_ffn_w"], name="norm_ffn")
    up = matmul(hn2, wts["ffn_up"], mode="nn", out_dtype=F32, name="ffn_up", tn=1408, out_shards=2)
    kf, f = wts["ffn_conv_w"].shape[0], wts["ffn_conv_w"].shape[1] // 2
    w_glu = jnp.transpose(wts["ffn_conv_w"].reshape(kf, 2, f), (1, 0, 2))
    act = conv_glu(up, w_glu, name="ffn_conv")
    h2 = matmul(act, wts["ffn_down"], mode="nn", out_dtype=F32, name="ffn_down", tk=f // 2, residual=h1)
    loss, dh2, dh2_16, d_nfw = loss_head(h2, wts["norm_final_w"].reshape(1, d), target, n_skip=n_pad + n_meta)

    g = {}
    d_act = matmul(dh2_16, wts["ffn_down"], mode="nt", out_dtype=F32, name="d_ffn_act", tn=1408)
    g["ffn_down"] = matmul(act, dh2_16, mode="tn", out_dtype=F32, name="dw_ffn_down", tm=512, tn=1024, tk=t)
    dup, d_fcw = conv_glu_backward(up, w_glu, d_act, name="d_ffn_conv")
    g["ffn_conv_w"] = jnp.transpose(d_fcw[:, :kf], (1, 0, 2)).reshape(kf, 2 * f)
    dhn2 = matmul(dup, wts["ffn_up"], mode="nt", out_dtype=F32, name="d_norm_ffn_out", tk=f, a_shards=2)
    g["ffn_up"] = matmul(hn2, dup, mode="tn", out_dtype=F32, name="dw_ffn_up", tm=512, tn=1408, tk=t,
                         b_shards=2, out_shards=4)
    dh1, dh1_16, g["norm_ffn_w"] = rmsnorm_backward(h1, wts["norm_ffn_w"], dhn2, dh2, n_pad=n_pad, name="d_norm_ffn")
    d_mixed = matmul(dh1_16, wts["w_out"], mode="nt", out_dtype=F32, name="d_mixed", tn=512)
    g["w_out"] = matmul(mixed, dh1_16, mode="tn", out_dtype=F32, name="dw_out", tm=512, tn=1024, tk=t)

    early = ("w_out", "ffn_up", "ffn_down")
    slots = [g[k] if g[k].ndim == 3 else g[k].reshape(N_CHIPS, -1, g[k].shape[1]) for k in early]
    dq, dk, dv, dproj, dsm_dn, g_al, g_db, g_nw, *from_sibling = gdn_backward(
        act_qkv, proj, dn_al, dn_db, dn_nw, s_dn, inv_dn, d_mixed, n_pad=n_pad, z_off=lay.z_off, exchange=slots,
        **small_at)
    g["dn_a_log"], g["dn_dt_bias"] = (jnp.sum(v, axis=0)[:, hd:2 * hd] for v in (g_al, g_db))
    g["dn_norm_w"] = jnp.sum(g_nw, axis=0)
    p16, own = [], []
    for k, slot, rb in zip(early, slots, from_sibling):
        a, b = pair_add(slot, rb, chip, core, name="grad_pair_add_" + k)
        p16.append(a)
        own.append(b)
    dxs, dbm, dcm, dproj, dsm_m2, g_al, g_db, g_dk, g["m2_norm_w"], *from_chips = ssd_backward(
        act_xbc, proj, m2_al, m2_db, m2_dk, wts["m2_norm_w"], s_m2, d_mixed, dproj,
        n_pad=n_pad, z_off=lay.m2z_off, exchange=p16, **dt_at)
    for k, o, q in zip(early, own, from_chips):
        g[k] = (o, q)
    g["m2_a_log"], g["m2_dt_bias"] = (jnp.sum(v, axis=0)[:, 2 * hd:2 * hd + hm] for v in (g_al, g_db))
    g["m2_d"] = g_dk.reshape(1, hm)

    kc = wts["dn_conv_w"].shape[0]
    dw_parts = []
    for idx, dpart in enumerate((dq, dk, dv)):
        dproj, dw, _ = conv_silu_backward(proj, wts["dn_conv_w"][:, idx * d:(idx + 1) * d], zeros_b[:, :d], dpart,
                                          dproj, x_off=idx * d, name=f"d_dn_conv{idx}")
        dw_parts.append(dw[:kc])
    g["dn_conv_w"] = jnp.concatenate(dw_parts, axis=1)
    dw_parts, db_parts = [], []
    off = 0
    for idx, dpart in enumerate((dxs, dbm, dcm)):
        wd = dpart.shape[1]
        dproj, dw, db = conv_silu_backward(proj, wts["m2_conv_w"][:, off:off + wd], wts["m2_conv_b"][:, off:off + wd],
                                           dpart, dproj, x_off=lay.xbc_off + off, name=f"d_m2_conv{idx}")
        dw_parts.append(dw[:kc])
        db_parts.append(db[:1])
        off += wd
    g["m2_conv_w"] = jnp.concatenate(dw_parts, axis=1)
    g["m2_conv_b"] = jnp.concatenate(db_parts, axis=1)
    dsmall = jnp.sum(dsm_dn, axis=0) + jnp.sum(dsm_m2, axis=0)
    dproj = lax.dynamic_update_slice(dproj, dsmall.astype(BF16), (0, lay.small_off))

    dw_in = matmul(hn1, dproj, mode="tn", out_dtype=F32, name="dw_in", tm=512, tn=896, tk=t)
    w_in_slots = lay.slots_of_w_in(dw_in, N_CHIPS)
    (from_sibling,) = pair_exchange([w_in_slots], name="grad_pair_exchange_w_in")
    p16_w_in, own_w_in = pair_add(w_in_slots, from_sibling, chip, core, name="grad_pair_add_w_in")
    dhn1, from_chips = matmul(dproj, wts["w_in"], mode="nt", out_dtype=F32, name="d_norm_mix_out", tk=lay.p // 3,
                              exchange=[p16_w_in])
    g["w_in"] = (own_w_in, from_chips)
    dh0, _, g["norm_mix_w"] = rmsnorm_backward(h0, wts["norm_mix_w"], dhn1, dh1, n_pad=n_pad, name="d_norm_mix")
    g["norm_final_w"] = d_nfw
    return loss[0, 0], dh0, g


MESH = pl.DeviceIdType.MESH
ANY = pl.BlockSpec(memory_space=pl.ANY)
N_CHIPS = 4
N_DEV = 8


def _mesh_pos():
    return lax.axis_index("x"), lax.axis_index("y"), lax.axis_index("c")


def _other_chips(x, y):
    return [(1 - x, y), (x, 1 - y), (1 - x, 1 - y)]


def _rcopy(src, dst, send_sems, recv_sems, k, to):
    return pltpu.make_async_remote_copy(src_ref=src, dst_ref=dst, send_sem=send_sems.at[k], recv_sem=recv_sems.at[k],
                                        device_id=to, device_id_type=MESH)


class GatherJob:
    def __init__(self, slabs, send_sems, recv_sems):
        self.slabs, self.send, self.recv = slabs, send_sems, recv_sems
        self.x, self.y, self.c = _mesh_pos()

    @staticmethod
    def sems(n):
        return [pltpu.SemaphoreType.DMA((6 * n,)), pltpu.SemaphoreType.DMA((6 * n,))]

    def _pieces(self):
        x, y, c = self.x, self.y, self.c
        for a, slab in enumerate(self.slabs):
            half = slab.shape[1] // 2
            for j, (px, py) in enumerate(_other_chips(x, y)):
                yield a, j, (px, py), slab, pl.ds(c * half, half), pl.ds((1 - c) * half, half)

    def _ici(self, a, j, chip, ref):
        return _rcopy(ref, ref, self.send, self.recv, 6 * a + j, (chip[0], chip[1], self.c))

    def _d2d(self, a, j, ref):
        return _rcopy(ref, ref, self.send, self.recv, 6 * a + 3 + j, (self.x, self.y, 1 - self.c))

    def begin(self):
        for a, j, chip, slab, mine, _ in self._pieces():
            self._ici(a, j, chip, slab.at[2 * self.x + self.y, mine]).start()

    def pass_on(self):
        for a, j, chip, slab, mine, _ in self._pieces():
            landed = slab.at[2 * chip[0] + chip[1], mine]
            self._ici(a, j, chip, landed).wait_recv()
            self._d2d(a, j, landed).start()

    def end(self):
        for a, j, chip, slab, mine, theirs in self._pieces():
            self._d2d(a, j, slab.at[2 * chip[0] + chip[1], theirs]).wait_recv()
        for a, j, chip, slab, mine, _ in self._pieces():
            self._ici(a, j, chip, slab.at[2 * self.x + self.y, mine]).wait_send()
            self._d2d(a, j, slab.at[2 * chip[0] + chip[1], mine]).wait_send()


class ChipExchangeJob:
    def __init__(self, parts, outs, send_sems, recv_sems):
        self.parts, self.outs, self.send, self.recv = parts, outs, send_sems, recv_sems
        self.x, self.y, self.c = _mesh_pos()

    @staticmethod
    def sems(n):
        return [pltpu.SemaphoreType.DMA((3 * n,)), pltpu.SemaphoreType.DMA((3 * n,))]

    def _copies(self):
        for a, (part, out) in enumerate(zip(self.parts, self.outs)):
            for j, (px, py) in enumerate(_other_chips(self.x, self.y)):
                yield _rcopy(part.at[2 * px + py], out.at[j], self.send, self.recv, 3 * a + j, (px, py, self.c))

    def begin(self):
        for cp in self._copies():
            cp.start()

    def end(self):
        for cp in self._copies():
            cp.wait()


class PairExchangeJob:
    def __init__(self, grads, outs, send_sems, recv_sems):
        self.grads, self.outs, self.send, self.recv = grads, outs, send_sems, recv_sems
        self.x, self.y, self.c = _mesh_pos()

    @staticmethod
    def sems(n):
        return [pltpu.SemaphoreType.DMA((n,)), pltpu.SemaphoreType.DMA((n,))]

    def _copies(self):
        for a, (grad, out) in enumerate(zip(self.grads, self.outs)):
            half = grad.shape[1] // 2
            yield _rcopy(grad.at[:, pl.ds((1 - self.c) * half, half), :], out, self.send, self.recv, a,
                         (self.x, self.y, 1 - self.c))

    def begin(self):
        for cp in self._copies():
            cp.start()

    def end(self):
        for cp in self._copies():
            cp.wait()


def gather_shards(slabs, *, name):
    n = len(slabs)

    def body(*refs):
        job = GatherJob(refs[n:2 * n], *refs[2 * n:])
        job.begin()
        job.pass_on()
        job.end()

    return pl.pallas_call(
        body, name=name,
        in_specs=[ANY] * n, out_specs=[ANY] * n,
        out_shape=[jax.ShapeDtypeStruct(s.shape, s.dtype) for s in slabs],
        input_output_aliases={a: a for a in range(n)},
        scratch_shapes=GatherJob.sems(n),
        compiler_params=pltpu.CompilerParams(has_side_effects=True),
    )(*slabs)


def _half_rows(s):
    return jax.ShapeDtypeStruct((s.shape[0], s.shape[1] // 2, s.shape[2]), s.dtype)


def pair_exchange(grads, *, name):
    n = len(grads)

    def body(*refs):
        job = PairExchangeJob(refs[:n], refs[n:2 * n], *refs[2 * n:])
        job.begin()
        job.end()

    return pl.pallas_call(
        body, name=name, in_specs=[ANY] * n, out_specs=[ANY] * n,
        out_shape=[_half_rows(s) for s in grads],
        scratch_shapes=PairExchangeJob.sems(n),
        compiler_params=pltpu.CompilerParams(has_side_effects=True),
    )(*grads)


def pair_join(wholes, *, name):
    n = len(wholes)

    def body(*refs):
        outs = refs[n:2 * n]
        send_sems, recv_sems = refs[2 * n:]
        x, y, c = _mesh_pos()
        cps = []
        for a in range(n):
            half = outs[a].shape[0] // 2
            rows = outs[a].at[pl.ds(c * half, half)]
            cp = _rcopy(rows, rows, send_sems, recv_sems, a, (x, y, 1 - c))
            cp.start()
            cps.append(cp)
        for a, cp in enumerate(cps):
            cp.wait_send()
            half = outs[a].shape[0] // 2
            theirs = outs[a].at[pl.ds((1 - c) * half, half)]
            _rcopy(theirs, theirs, send_sems, recv_sems, a, (x, y, 1 - c)).wait_recv()

    return pl.pallas_call(
        body, name=name, in_specs=[ANY] * n, out_specs=[ANY] * n,
        out_shape=[jax.ShapeDtypeStruct(s.shape, s.dtype) for s in wholes],
        input_output_aliases={a: a for a in range(n)},
        scratch_shapes=[pltpu.SemaphoreType.DMA((n,)), pltpu.SemaphoreType.DMA((n,))],
        compiler_params=pltpu.CompilerParams(has_side_effects=True),
    )(*wholes)


def gather_all(v, *, name):
    def body(in_ref, out_ref, send_sems, recv_sems, local_sem):
        x, y, c = _mesh_pos()
        mine = out_ref.at[4 * x + 2 * y + c]
        lc = pltpu.make_async_copy(in_ref, mine, local_sem)
        lc.start()
        cps = []
        for k in range(1, N_DEV):
            flip = lambda v, bit: 1 - v if (k >> bit) & 1 else v
            cp = _rcopy(in_ref, mine, send_sems, recv_sems, k - 1, (flip(x, 2), flip(y, 1), flip(c, 0)))
            cp.start()
            cps.append(cp)
        for cp in cps:
            cp.wait()
        lc.wait()

    return pl.pallas_call(
        body, name=name, in_specs=[ANY], out_specs=ANY,
        out_shape=jax.ShapeDtypeStruct((N_DEV,) + v.shape, v.dtype),
        scratch_shapes=[pltpu.SemaphoreType.DMA((N_DEV - 1,)), pltpu.SemaphoreType.DMA((N_DEV - 1,)),
                        pltpu.SemaphoreType.DMA(())],
        compiler_params=pltpu.CompilerParams(has_side_effects=True),
    )(v)


def _sum_tile(rows, cols):
    return _pick(rows, max(16, ELEMENTWISE_BLOCK_BYTES // (4 * cols) // 16 * 16), 16)


def pair_add(g, rb, chip, c, *, name):
    _, r, cols = g.shape
    half = r // 2
    tr = _sum_tile(half, cols)
    nrt = half // tr

    def body(s_ref, g_ref, rb_ref, p16_ref, own_ref):
        v = g_ref[...] + rb_ref[...]
        p16_ref[...] = v.astype(p16_ref.dtype)

        @pl.when(pl.program_id(1) == s_ref[0])
        def _():
            own_ref[...] = v

    grid_spec = pltpu.PrefetchScalarGridSpec(
        num_scalar_prefetch=1, grid=(nrt, N_CHIPS),
        in_specs=[pl.BlockSpec((None, tr, cols), lambda i, k, s: (k, s[1] * nrt + i, 0)),
                  pl.BlockSpec((None, tr, cols), lambda i, k, s: (k, i, 0))],
        out_specs=[pl.BlockSpec((None, tr, cols), lambda i, k, s: (k, i, 0)),
                   pl.BlockSpec((tr, cols), lambda i, k, s: (i, 0))])
    return pl.pallas_call(
        body, name=name, grid_spec=grid_spec,
        out_shape=[jax.ShapeDtypeStruct((N_CHIPS, half, cols), BF16), jax.ShapeDtypeStruct((half, cols), F32)],
        compiler_params=_cparams(("arbitrary", "arbitrary")),
    )(jnp.stack([chip, c]).astype(jnp.int32), g, rb)


def chip_add(own, q, c, *, name):
    r, cols = own.shape
    tr = _sum_tile(r, cols)
    nrt = r // tr

    def body(s_ref, own_ref, q_ref, o_ref):
        o_ref[...] = ((own_ref[...] + q_ref[0].astype(F32)) + q_ref[1].astype(F32)) + q_ref[2].astype(F32)

    grid_spec = pltpu.PrefetchScalarGridSpec(
        num_scalar_prefetch=1, grid=(nrt,),
        in_specs=[pl.BlockSpec((tr, cols), lambda i, s: (i, 0)), pl.BlockSpec((3, tr, cols), lambda i, s: (0, i, 0))],
        out_specs=pl.BlockSpec((tr, cols), lambda i, s: (s[0] * nrt + i, 0)))
    return pl.pallas_call(
        body, name=name, grid_spec=grid_spec,
        out_shape=jax.ShapeDtypeStruct((2 * r, cols), F32),
        compiler_params=_cparams(("arbitrary",)),
    )(jnp.reshape(c, (1,)).astype(jnp.int32), own, q)


def sum_slots(v, *, name):
    n, r, cols = v.shape
    tr = _sum_tile(r, cols)

    def body(v_ref, o_ref):
        acc = v_ref[0]
        for k in range(1, n):
            acc = acc + v_ref[k]
        o_ref[...] = acc

    return pl.pallas_call(
        body, name=name, grid=(r // tr,),
        in_specs=[pl.BlockSpec((n, tr, cols), lambda i: (0, i, 0))],
        out_specs=pl.BlockSpec((tr, cols), lambda i: (i, 0)),
        out_shape=jax.ShapeDtypeStruct((r, cols), F32),
        compiler_params=_cparams(("parallel",)),
    )(v)


PACK_ROWS = 16


def _pack(arrays):
    parts = []
    for a in arrays:
        flat = a.reshape(-1).astype(F32)
        size = PACK_ROWS * LANES
        pad = (-flat.shape[0]) % size
        parts.append(jnp.pad(flat, (0, pad)))
    return jnp.concatenate(parts).reshape(-1, LANES)


def _unpack(slab, shapes):
    out, row = [], 0
    for shp in shapes:
        n = 1
        for s in shp:
            n *= s
        rows = -(-n // (PACK_ROWS * LANES)) * PACK_ROWS
        out.append(slab[row:row + rows].reshape(-1)[:n].reshape(shp))
        row += rows
    return out


WEIGHT_NAMES = ("meta_tokens", "norm_mix_w", "w_in", "dn_conv_w", "dn_a_log", "dn_dt_bias", "dn_norm_w", "m2_conv_w",
                "m2_conv_b", "m2_a_log", "m2_dt_bias", "m2_d", "m2_norm_w", "w_out", "norm_ffn_w", "ffn_up",
                "ffn_conv_w", "ffn_down", "norm_final_w")
BIG = ("w_in", "w_out", "ffn_up", "ffn_down")
SMALL_SHARDED = ("meta_tokens", "dn_conv_w", "m2_conv_w", "ffn_conv_w")
SMALL = tuple(n for n in WEIGHT_NAMES if n not in BIG)


def kernel(x, meta_tokens, norm_mix_w, w_in, dn_conv_w, dn_a_log, dn_dt_bias, dn_norm_w, m2_conv_w, m2_conv_b, m2_a_log, m2_dt_bias, m2_d, m2_norm_w, w_out, norm_ffn_w, ffn_up, ffn_conv_w, ffn_down, norm_final_w, loss_target, m_meta_tokens, m_norm_mix_w, m_w_in, m_dn_conv_w, m_dn_a_log, m_dn_dt_bias, m_dn_norm_w, m_m2_conv_w, m_m2_conv_b, m_m2_a_log, m_m2_dt_bias, m_m2_d, m_m2_norm_w, m_w_out, m_norm_ffn_w, m_ffn_up, m_ffn_conv_w, m_ffn_down, m_norm_final_w, v_meta_tokens, v_norm_mix_w, v_w_in, v_dn_conv_w, v_dn_a_log, v_dn_dt_bias, v_dn_norm_w, v_m2_conv_w, v_m2_conv_b, v_m2_a_log, v_m2_dt_bias, v_m2_d, v_m2_norm_w, v_w_out, v_norm_ffn_w, v_ffn_up, v_ffn_conv_w, v_ffn_down, v_norm_final_w):
    args = tuple(locals().values())
    nw = len(WEIGHT_NAMES)
    wt = dict(zip(WEIGHT_NAMES, args[1:1 + nw]))
    mom = dict(zip(WEIGHT_NAMES, args[2 + nw:2 + 2 * nw]))
    var = dict(zip(WEIGHT_NAMES, args[2 + 2 * nw:2 + 3 * nw]))
    xi, yi, ci = _mesh_pos()
    chip = 2 * xi + yi
    seq, d = x.shape[1], x.shape[2]
    n_meta = wt["meta_tokens"].shape[0]
    n_pad = (-(n_meta + seq)) % ROW_ALIGN
    lay = Layout(d)

    small_local = [wt[k].reshape(wt[k].shape[-2:]) for k in SMALL_SHARDED]
    small_slab = _pack(small_local)
    small_slab = lax.dynamic_update_slice(jnp.zeros((N_CHIPS,) + small_slab.shape, F32), small_slab[None], (chip, 0, 0))
    slabs = {k: cast_into_slot(wt[k], chip, N_CHIPS, name="cast_" + k) for k in BIG}
    w_in_all, small_all = gather_shards([slabs.pop("w_in"), small_slab], name="gather_w_in")
    full = {"w_in": lay.w_in_of_slots(w_in_all)}
    per_chip = [_unpack(small_all[s], [a.shape for a in small_local]) for s in range(N_CHIPS)]
    for idx, k in enumerate(SMALL_SHARDED):
        full[k] = jnp.concatenate([per_chip[s][idx] for s in range(N_CHIPS)], axis=-1)
    for k in SMALL:
        if k not in SMALL_SHARDED:
            full[k] = wt[k]

    h0 = jnp.concatenate([jnp.zeros((n_pad, d), F32), full["meta_tokens"], x[0]], axis=0)
    target = jnp.concatenate([jnp.zeros((n_pad + n_meta, d), F32), loss_target[0]], axis=0)
    loss_local, dh0, g = local_step(h0, target, full, slabs, chip, ci, n_pad=n_pad, n_meta=n_meta)
    grad_x = dh0[n_pad + n_meta:][None]
    g["meta_tokens"] = dh0[n_pad:n_pad + n_meta]
    loss = lax.psum(loss_local, ("x", "y", "c"))

    wholes = [chip_add(*g[k], ci, name="grad_chip_add_" + k) for k in BIG]
    grad = dict(zip(BIG, pair_join(wholes, name="grad_pair_join")))

    small_shapes = [g[k].shape for k in SMALL]
    summed = sum_slots(gather_all(_pack([g[k] for k in SMALL]), name="small_grad_gather"), name="small_grad_sum")
    for k, v in zip(SMALL, _unpack(summed, small_shapes)):
        if k in SMALL_SHARDED:
            width = wt[k].shape[-1]
            v = lax.dynamic_slice_in_dim(v, chip * width, width, axis=v.ndim - 1)
        grad[k] = v

    delta, new_m, new_v = {}, {}, {}
    for k in BIG:
        delta[k], new_m[k], new_v[k] = adamw(wt[k], grad[k], mom[k], var[k], name="adamw_" + k)
    shapes = [wt[k].shape for k in SMALL]
    packed = adamw(_pack([wt[k] for k in SMALL]), _pack([grad[k] for k in SMALL]), _pack([mom[k] for k in SMALL]),
                   _pack([var[k] for k in SMALL]), name="adamw_small")
    for res, slab in zip((delta, new_m, new_v), packed):
        for k, v in zip(SMALL, _unpack(slab, shapes)):
            res[k] = v
    outs = [loss, grad_x]
    for res in (grad, delta, new_m, new_v):
        outs += [res[k].reshape(wt[k].shape) for k in WEIGHT_NAMES]
    return tuple(outs)
```

```python
import functools
import math

import jax
import jax.numpy as jnp
from jax import lax
from jax.experimental import pallas as pl
from jax.experimental.pallas import tpu as pltpu

F32 = jnp.float32
BF16 = jnp.bfloat16

CHUNK = 64
ROW_ALIGN = 128
NORM_EPS = 1e-6
DN_DK = 128
M2_P = 64
M2_N = 128
M2_GROUPS = 4
HEAD_BLOCK = 16
VMEM_LIMIT = 56 * 1024 * 1024

ADAM_LR, ADAM_B1, ADAM_B2, ADAM_EPS, ADAM_WD, ADAM_STEP = 0.001, 0.9, 0.999, 1e-08, 0.01, 10


def _cparams(sem=None):
    return pltpu.CompilerParams(dimension_semantics=sem, vmem_limit_bytes=VMEM_LIMIT)


def _silu(x):
    return x / (1.0 + jnp.exp(-x))


def _sigmoid(x):
    return 1.0 / (1.0 + jnp.exp(-x))


def _softplus(x):
    return jnp.maximum(x, 0.0) + jnp.log(1.0 + jnp.exp(-jnp.abs(x)))


def _tri_masks():
    r = lax.broadcasted_iota(jnp.int32, (CHUNK, CHUNK), 0)
    c = lax.broadcasted_iota(jnp.int32, (CHUNK, CHUNK), 1)
    return (r >= c)[None], (r > c)[None], (r == c)[None]


def _col2row(col, eye):
    return jnp.sum(jnp.where(eye, col, 0.0), axis=1, keepdims=True)


def _tri_sum(v, suffix):
    r = lax.broadcasted_iota(jnp.int32, (v.shape[0], v.shape[0]), 0)
    c = lax.broadcasted_iota(jnp.int32, (v.shape[0], v.shape[0]), 1)
    tri = jnp.where((r <= c) if suffix else (r >= c), 1.0, 0.0).astype(BF16)
    hi = v.astype(BF16)
    rest = v - hi.astype(F32)
    mid = rest.astype(BF16)
    lo = (rest - mid.astype(F32)).astype(BF16)
    dot = lambda part: jnp.dot(tri, part, preferred_element_type=F32)
    return dot(hi) + (dot(mid) + dot(lo))


@jax.custom_vjp
def _running_sum(v):
    return _tri_sum(v, False)


_running_sum.defvjp(lambda v: (_tri_sum(v, False), None), lambda _, g: (_tri_sum(g, True),))


def _bdot(a, b, dims):
    return lax.dot_general(a.astype(BF16), b.astype(BF16), (((dims[0],), (dims[1],)), ((0,), (0,))),
                           preferred_element_type=F32)


def _dot3(a, b, ca, cb):
    def split(v):
        hi = v.astype(BF16)
        return hi, (v - hi.astype(F32)).astype(BF16)

    def dot(p, q):
        return lax.dot_general(p, q, (((ca,), (cb,)), ((0,), (0,))), preferred_element_type=F32)

    (ah, al), (bh, bl) = split(a), split(b)
    return dot(ah, bh) + (dot(ah, bl) + dot(al, bh))


@jax.custom_vjp
def _bmm3(a, b):
    return _dot3(a, b, 2, 1)


def _bmm3_fwd(a, b):
    return _dot3(a, b, 2, 1), (a, b)


def _bmm3_bwd(res, g):
    a, b = res
    return _dot3(g, b, 2, 2), _dot3(a, g, 1, 1)


_bmm3.defvjp(_bmm3_fwd, _bmm3_bwd)


def _unit_lower_inverse(a_mat, eye):
    inv = jnp.where(eye, 1.0, 0.0) - a_mat
    pw = a_mat
    n = 2
    while n < CHUNK:
        pw = _bmm3(pw, pw)
        inv = inv + _bmm3(inv, pw)
        n *= 2
    return inv


@jax.custom_vjp
def _known_inverse(a_mat, inv):
    return inv


def _known_inverse_fwd(a_mat, inv):
    return inv, inv


def _known_inverse_bwd(inv, g):
    return -_dot3(_dot3(inv, g, 1, 1), inv, 2, 2), jnp.zeros_like(inv)


_known_inverse.defvjp(_known_inverse_fwd, _known_inverse_bwd)


def _gdn_step(state, qa, ka, va, z, small, a_log, dt_bias, norm_w, vm, head0, n_heads, inv=None, want_inv=False):
    causal, strict, eye = _tri_masks()
    nh = state.shape[0]
    qa, ka, va = qa * vm, ka * vm, va * vm
    q = qa * lax.rsqrt(jnp.sum(qa * qa, -1, keepdims=True) + NORM_EPS) * (DN_DK ** -0.5)
    k = ka * lax.rsqrt(jnp.sum(ka * ka, -1, keepdims=True) + NORM_EPS)
    g_all = -jnp.exp(a_log) * _softplus(small + dt_bias) * vm[0]
    gcum_all = _running_sum(g_all)
    glast_all = jnp.sum(g_all, axis=0, keepdims=True)
    beta = _take_cols(_sigmoid(small) * vm[0], head0, nh)
    gcum = _take_cols(gcum_all, n_heads + head0, nh)
    egc = _take_cols(jnp.exp(gcum_all), n_heads + head0, nh)
    etail = _take_cols(jnp.exp(glast_all - gcum_all), n_heads + head0, nh)
    elast = _take_cols(jnp.exp(glast_all), n_heads + head0, nh)
    grow = _col2row(gcum, eye)
    decay = jnp.where(causal, jnp.exp(jnp.where(causal, gcum - grow, 0.0)), 0.0)
    kk = _bdot(k, k, (2, 2))
    a_mat = jnp.where(strict, beta * kk * decay, 0.0)
    tinv = _unit_lower_inverse(a_mat, eye) if inv is None else _known_inverse(a_mat, inv)
    u = _bmm3(tinv, va * beta)
    w = _bmm3(tinv, k * (beta * egc))
    v_new = u - _bdot(w, state, (2, 1))
    o_inter = _bdot(q * egc, state, (2, 1))
    qk = _bdot(q, k, (2, 2)) * decay
    o = o_inter + _bdot(qk, v_new, (2, 1))
    new_state = state * elast + _bdot(k * etail, v_new, (1, 1))
    o = o * lax.rsqrt(jnp.mean(o * o, -1, keepdims=True) + NORM_EPS) * norm_w * _silu(z)
    return (new_state, o, tinv) if want_inv else (new_state, o)


def _valid_rows(chunk, n_pad):
    r = chunk * CHUNK + lax.broadcasted_iota(jnp.int32, (1, CHUNK, 1), 1)
    return jnp.where(r >= n_pad, 1.0, 0.0).astype(F32)


def _split_heads(x, n, w):
    return jnp.stack([x[:, i * w:(i + 1) * w] for i in range(n)], axis=0)


def _merge_heads(x):
    return jnp.concatenate([x[i] for i in range(x.shape[0])], axis=-1)


def _run_beside(job, step, n_steps):
    @pl.when(step == 0)
    def _():
        job.begin()

    if hasattr(job, "pass_on"):
        @pl.when(step == (7 * n_steps) // 8)
        def _():
            job.pass_on()

    @pl.when(step == n_steps - 1)
    def _():
        job.end()


def _take_cols(slab, first, n):
    lane = lax.broadcasted_iota(jnp.int32, (n, 1, slab.shape[1]), 2)
    col = lax.broadcasted_iota(jnp.int32, (n, 1, slab.shape[1]), 0)
    return jnp.sum(jnp.where(lane == first + col, slab[None], 0.0), axis=2, keepdims=True)


def gdn_forward(act_qkv, proj, a_log3, dt_bias3, norm_w3, *, n_pad, z_off, small_off, out_cols, gather=()):
    t, w3 = act_qkv.shape
    wdn = w3 // 3
    heads = wdn // DN_DK
    hb = min(HEAD_BLOCK, heads)
    nhb = heads // hb
    nc = t // CHUNK
    bw = hb * DN_DK
    nqb = wdn // bw
    ng = len(gather)

    def body(q_ref, k_ref, v_ref, z_ref, sm_ref, al_ref, db_ref, nw_ref, *rest):
        out_ref, sall_ref, inv_ref = rest[ng:ng + 3]
        st_ref = rest[2 * ng + 3]
        n = pl.program_id(1)
        if ng:
            _run_beside(GatherJob(rest[ng + 3:2 * ng + 3], *rest[2 * ng + 4:]), pl.program_id(0) * nc + n, nhb * nc)

        @pl.when(n == 0)
        def _():
            st_ref[...] = jnp.zeros_like(st_ref)

        state = st_ref[...]
        sall_ref[...] = state
        vm = _valid_rows(n, n_pad)
        head0 = pl.program_id(0) * hb
        new_state, o, tinv = _gdn_step(
            state, _split_heads(q_ref[...], hb, DN_DK), _split_heads(k_ref[...], hb, DN_DK),
            _split_heads(v_ref[...], hb, DN_DK), _split_heads(z_ref[...], hb, DN_DK),
            sm_ref[...], al_ref[...], db_ref[...], nw_ref[...], vm, head0, heads, want_inv=True)
        st_ref[...] = new_state
        inv_ref[...] = tinv
        out_ref[...] = _merge_heads(o).astype(out_ref.dtype)

    par = pl.BlockSpec((1, LANES), lambda h, n: (0, 0))
    return pl.pallas_call(
        body, name="gdn_fwd",
        grid=(nhb, nc),
        in_specs=[pl.BlockSpec((CHUNK, bw), lambda h, n: (n, h)),
                  pl.BlockSpec((CHUNK, bw), lambda h, n: (n, nqb + h)),
                  pl.BlockSpec((CHUNK, bw), lambda h, n: (n, 2 * nqb + h)),
                  pl.BlockSpec((CHUNK, bw), lambda h, n: (n, z_off // bw + h)),
                  pl.BlockSpec((CHUNK, LANES), lambda h, n: (n, small_off // LANES)),
                  par, par,
                  pl.BlockSpec((1, 1, DN_DK), lambda h, n: (0, 0, 0))] + [ANY] * ng,
        out_specs=[pl.BlockSpec((CHUNK, bw), lambda h, n: (n, h)),
                   pl.BlockSpec((None, hb, DN_DK, DN_DK), lambda h, n: (n, h, 0, 0)),
                   pl.BlockSpec((None, hb, CHUNK, CHUNK), lambda h, n: (n, h, 0, 0))] + [ANY] * ng,
        out_shape=[jax.ShapeDtypeStruct((t, out_cols), BF16),
                   jax.ShapeDtypeStruct((nc, heads, DN_DK, DN_DK), F32),
                   jax.ShapeDtypeStruct((nc, heads, CHUNK, CHUNK), F32)]
        + [jax.ShapeDtypeStruct(s.shape, s.dtype) for s in gather],
        input_output_aliases={8 + a: 3 + a for a in range(ng)},
        scratch_shapes=[pltpu.VMEM((hb, DN_DK, DN_DK), F32)] + (GatherJob.sems(ng) if ng else []),
        compiler_params=_cparams(("arbitrary", "arbitrary")),
    )(act_qkv, act_qkv, act_qkv, proj, proj, a_log3, dt_bias3, norm_w3, *gather)


def gdn_backward(act_qkv, proj, a_log3, dt_bias3, norm_w3, s_all, inv_all, d_mixed, *, n_pad, z_off, small_off,
                 exchange=()):
    t, w3 = act_qkv.shape
    wdn = w3 // 3
    heads = wdn // DN_DK
    hb = min(HEAD_BLOCK, heads)
    nhb = heads // hb
    nc = t // CHUNK
    bw = hb * DN_DK
    nqb = wdn // bw
    ne = len(exchange)

    def body(q_ref, k_ref, v_ref, z_ref, sm_ref, al_ref, db_ref, nw_ref, s_ref, inv_ref, do_ref, *rest):
        dq_ref, dk_ref, dv_ref, dz_ref, dsm_ref, dal_ref, ddb_ref, dnw_ref = rest[ne:ne + 8]
        ds_ref = rest[2 * ne + 8]
        i = pl.program_id(1)
        n = nc - 1 - i
        head0 = pl.program_id(0) * hb
        if ne:
            _run_beside(PairExchangeJob(rest[:ne], rest[ne + 8:2 * ne + 8], *rest[2 * ne + 9:]),
                        pl.program_id(0) * nc + i, nhb * nc)

        @pl.when(i == 0)
        def _():
            ds_ref[...] = jnp.zeros_like(ds_ref)
            dal_ref[...] = jnp.zeros_like(dal_ref)
            ddb_ref[...] = jnp.zeros_like(ddb_ref)
            dnw_ref[...] = jnp.zeros_like(dnw_ref)

        vm = _valid_rows(n, n_pad)
        step = functools.partial(_gdn_step, vm=vm, head0=head0, n_heads=heads, inv=inv_ref[...])
        _, vjp = jax.vjp(step, s_ref[...], _split_heads(q_ref[...], hb, DN_DK), _split_heads(k_ref[...], hb, DN_DK),
                         _split_heads(v_ref[...], hb, DN_DK), _split_heads(z_ref[...], hb, DN_DK),
                         sm_ref[...], al_ref[...], db_ref[...], nw_ref[...])
        ds, dq, dk, dv, dz, dsm, dal, ddb, dnw = vjp((ds_ref[...], _split_heads(do_ref[...], hb, DN_DK)))
        ds_ref[...] = ds
        dq_ref[...] = _merge_heads(dq)
        dk_ref[...] = _merge_heads(dk)
        dv_ref[...] = _merge_heads(dv)
        dz_ref[...] = _merge_heads(dz).astype(dz_ref.dtype)
        dsm_ref[...] = dsm
        dal_ref[...] += dal
        ddb_ref[...] += ddb
        dnw_ref[...] += dnw[0]

    rev = lambda n: nc - 1 - n
    par = pl.BlockSpec((1, LANES), lambda h, n: (0, 0))
    dpar = pl.BlockSpec((None, 1, LANES), lambda h, n: (h, 0, 0))
    blk = lambda off: pl.BlockSpec((CHUNK, bw), lambda h, n: (rev(n), off + h))
    return pl.pallas_call(
        body, name="gdn_bwd",
        grid=(nhb, nc),
        in_specs=[blk(0), blk(nqb), blk(2 * nqb), blk(z_off // bw),
                  pl.BlockSpec((CHUNK, LANES), lambda h, n: (rev(n), small_off // LANES)), par, par,
                  pl.BlockSpec((1, 1, DN_DK), lambda h, n: (0, 0, 0)),
                  pl.BlockSpec((None, hb, DN_DK, DN_DK), lambda h, n: (rev(n), h, 0, 0)),
                  pl.BlockSpec((None, hb, CHUNK, CHUNK), lambda h, n: (rev(n), h, 0, 0)),
                  blk(0)] + [ANY] * ne,
        out_specs=[blk(0), blk(0), blk(0), blk(z_off // bw),
                   pl.BlockSpec((None, CHUNK, LANES), lambda h, n: (h, rev(n), 0)), dpar, dpar,
                   pl.BlockSpec((None, 1, DN_DK), lambda h, n: (h, 0, 0))] + [ANY] * ne,
        out_shape=[jax.ShapeDtypeStruct((t, wdn), F32)] * 3
        + [jax.ShapeDtypeStruct((t, proj.shape[1]), BF16),
           jax.ShapeDtypeStruct((nhb, t, LANES), F32),
           jax.ShapeDtypeStruct((nhb, 1, LANES), F32), jax.ShapeDtypeStruct((nhb, 1, LANES), F32),
           jax.ShapeDtypeStruct((nhb, 1, DN_DK), F32)] + [_half_rows(s) for s in exchange],
        scratch_shapes=[pltpu.VMEM((hb, DN_DK, DN_DK), F32)] + (PairExchangeJob.sems(ne) if ne else []),
        compiler_params=_cparams(("arbitrary", "arbitrary")),
    )(act_qkv, act_qkv, act_qkv, proj, proj, a_log3, dt_bias3, norm_w3, s_all, inv_all, d_mixed, *exchange)


def _dot2(a, b, ca, cb):
    return lax.dot_general(a.astype(BF16), b.astype(BF16), (((ca,), (cb,)), ((), ())),
                           preferred_element_type=F32)


def _ssd_step(state, xa, bmat, cmat, z, small, a_log, dt_bias, dskip, norm_w, vm, lane0):
    causal, _, eye = _tri_masks()
    r_heads, p, n_state = state.shape
    gw = r_heads * p
    vm2 = vm[0]
    xa, bmat, cmat = xa * vm2, bmat * vm2, cmat * vm2
    dt_all = _softplus(small + dt_bias) * vm2
    a_all = dt_all * (-jnp.exp(a_log))
    acs_all = _running_sum(a_all)
    alast_all = jnp.sum(a_all, axis=0, keepdims=True)
    dt = _take_cols(dt_all, lane0, r_heads)
    acs = _take_cols(acs_all, lane0, r_heads)
    eacs = _take_cols(jnp.exp(acs_all), lane0, r_heads)
    etail = _take_cols(jnp.exp(alast_all - acs_all), lane0, r_heads)
    elast = _take_cols(jnp.exp(alast_all), lane0, r_heads)
    arow = _col2row(acs, eye)
    lmat = jnp.where(causal, jnp.exp(jnp.where(causal, acs - arow, 0.0)), 0.0)
    hsel = (lax.broadcasted_iota(jnp.int32, (r_heads, 1, gw), 2) // p
            == lax.broadcasted_iota(jnp.int32, (r_heads, 1, gw), 0))

    def spread(col):
        return jnp.sum(jnp.where(hsel, col, 0.0), axis=0)

    xdt = xa * spread(dt)
    cb = _dot2(cmat, bmat, 1, 1)
    m = (cb[None] * lmat).reshape(r_heads * CHUNK, CHUNK)
    yb = _dot2(m, xdt, 1, 0).reshape(r_heads, CHUNK, gw)
    y_diag = jnp.sum(jnp.where(hsel, yb, 0.0), axis=0)
    s2 = state.reshape(gw, n_state)
    y_off = _dot2(cmat, s2, 1, 1) * spread(eacs)
    upd = _dot2(xdt * spread(etail), bmat, 0, 0)
    new_state = state * elast + upd.reshape(r_heads, p, n_state)
    y = y_diag + y_off + xa * spread(dskip)
    y = y * _silu(z)
    y = y * lax.rsqrt(jnp.mean(y * y, -1, keepdims=True) + NORM_EPS) * norm_w
    return new_state, y


def _ssd_dims(act_xbc):
    t, wx = act_xbc.shape
    wm = wx - 2 * M2_GROUPS * M2_N
    gw = wm // M2_GROUPS
    return t, wm, gw, gw // M2_P, wm // M2_P, t // CHUNK


def ssd_forward(act_xbc, proj, a_log3, dt_bias3, dskip3, norm_w, mixed, *, n_pad, z_off, small_off, dt_lane,
                gather=()):
    t, wm, gw, rh, heads, nc = _ssd_dims(act_xbc)
    nb = wm // M2_N
    ob = (mixed.shape[1] - wm) // gw
    ng = len(gather)

    def body(x_ref, b_ref, c_ref, z_ref, dt_ref, al_ref, db_ref, dk_ref, nw_ref, _, *rest):
        out_ref, sall_ref = rest[ng:ng + 2]
        st_ref = rest[2 * ng + 2]
        n = pl.program_id(1)
        if ng:
            _run_beside(GatherJob(rest[ng + 2:2 * ng + 2], *rest[2 * ng + 3:]), pl.program_id(0) * nc + n,
                        M2_GROUPS * nc)

        @pl.when(n == 0)
        def _():
            st_ref[...] = jnp.zeros_like(st_ref)

        state = st_ref[...]
        sall_ref[...] = state
        new_state, y = _ssd_step(state, x_ref[...], b_ref[...], c_ref[...], z_ref[...], dt_ref[...],
                                 al_ref[...], db_ref[...], dk_ref[...], nw_ref[...], _valid_rows(n, n_pad),
                                 dt_lane + pl.program_id(0) * rh)
        st_ref[...] = new_state
        out_ref[...] = y.astype(out_ref.dtype)

    par = pl.BlockSpec((rh, 1, 1), lambda g, n: (g, 0, 0))
    row = pl.BlockSpec((1, LANES), lambda g, n: (0, 0))
    return pl.pallas_call(
        body, name="ssd_fwd",
        grid=(M2_GROUPS, nc),
        in_specs=[pl.BlockSpec((CHUNK, gw), lambda g, n: (n, g)),
                  pl.BlockSpec((CHUNK, M2_N), lambda g, n: (n, nb + g)),
                  pl.BlockSpec((CHUNK, M2_N), lambda g, n: (n, nb + M2_GROUPS + g)),
                  pl.BlockSpec((CHUNK, gw), lambda g, n: (n, z_off // gw + g)),
                  pl.BlockSpec((CHUNK, LANES), lambda g, n: (n, small_off // LANES)),
                  row, row, par,
                  pl.BlockSpec((1, gw), lambda g, n: (0, g)),
                  pl.BlockSpec(memory_space=pl.ANY)] + [ANY] * ng,
        out_specs=[pl.BlockSpec((CHUNK, gw), lambda g, n: (n, ob + g)),
                   pl.BlockSpec((None, rh, M2_P, M2_N), lambda g, n: (n, g, 0, 0))] + [ANY] * ng,
        out_shape=[jax.ShapeDtypeStruct(mixed.shape, mixed.dtype),
                   jax.ShapeDtypeStruct((nc, heads, M2_P, M2_N), F32)]
        + [jax.ShapeDtypeStruct(s.shape, s.dtype) for s in gather],
        input_output_aliases={9: 0, **{10 + a: 2 + a for a in range(ng)}},
        scratch_shapes=[pltpu.VMEM((rh, M2_P, M2_N), F32)] + (GatherJob.sems(ng) if ng else []),
        compiler_params=_cparams(("arbitrary", "arbitrary")),
    )(act_xbc, act_xbc, act_xbc, proj, proj, a_log3, dt_bias3, dskip3, norm_w, mixed, *gather)


def ssd_backward(act_xbc, proj, a_log3, dt_bias3, dskip3, norm_w, s_all, d_mixed, dproj, *, n_pad, z_off,
                 small_off, dt_lane, exchange=()):
    t, wm, gw, rh, heads, nc = _ssd_dims(act_xbc)
    nb = wm // M2_N
    ne = len(exchange)

    def body(x_ref, b_ref, c_ref, z_ref, dt_ref, al_ref, db_ref, dk_ref, nw_ref, s_ref, dy_ref, _, *rest):
        dx_ref, dbm_ref, dcm_ref, dz_ref, ddt_ref, dal_ref, ddb_ref, ddk_ref, dnw_ref = rest[ne:ne + 9]
        ds_ref = rest[2 * ne + 9]
        i = pl.program_id(1)
        n = nc - 1 - i
        if ne:
            _run_beside(ChipExchangeJob(rest[:ne], rest[ne + 9:2 * ne + 9], *rest[2 * ne + 10:]),
                        pl.program_id(0) * nc + i, M2_GROUPS * nc)

        @pl.when(i == 0)
        def _():
            ds_ref[...] = jnp.zeros_like(ds_ref)
            dal_ref[...] = jnp.zeros_like(dal_ref)
            ddb_ref[...] = jnp.zeros_like(ddb_ref)
            ddk_ref[...] = jnp.zeros_like(ddk_ref)
            dnw_ref[...] = jnp.zeros_like(dnw_ref)

        step = functools.partial(_ssd_step, vm=_valid_rows(n, n_pad), lane0=dt_lane + pl.program_id(0) * rh)
        _, vjp = jax.vjp(step, s_ref[...], x_ref[...], b_ref[...], c_ref[...], z_ref[...],
                         dt_ref[...], al_ref[...], db_ref[...], dk_ref[...], nw_ref[...])
        ds, dx, dbm, dcm, dz, ddt, dal, ddb, ddk, dnw = vjp((ds_ref[...], dy_ref[...]))
        ds_ref[...] = ds
        dx_ref[...] = dx
        dbm_ref[...] = dbm
        dcm_ref[...] = dcm
        dz_ref[...] = dz.astype(dz_ref.dtype)
        ddt_ref[...] = ddt
        dal_ref[...] += dal
        ddb_ref[...] += ddb
        ddk_ref[...] += ddk
        dnw_ref[...] += dnw

    rev = lambda n: nc - 1 - n
    par = pl.BlockSpec((rh, 1, 1), lambda g, n: (g, 0, 0))
    wide = lambda off: pl.BlockSpec((CHUNK, gw), lambda g, n: (rev(n), off + g))
    narrow = lambda off: pl.BlockSpec((CHUNK, M2_N), lambda g, n: (rev(n), off + g))
    col = pl.BlockSpec((None, CHUNK, LANES), lambda g, n: (g, rev(n), 0))
    row = pl.BlockSpec((1, LANES), lambda g, n: (0, 0))
    drow = pl.BlockSpec((None, 1, LANES), lambda g, n: (g, 0, 0))
    gn = M2_GROUPS * M2_N
    return pl.pallas_call(
        body, name="ssd_bwd",
        grid=(M2_GROUPS, nc),
        in_specs=[wide(0), narrow(nb), narrow(nb + M2_GROUPS), wide(z_off // gw),
                  pl.BlockSpec((CHUNK, LANES), lambda g, n: (rev(n), small_off // LANES)), row, row, par,
                  pl.BlockSpec((1, gw), lambda g, n: (0, g)),
                  pl.BlockSpec((None, rh, M2_P, M2_N), lambda g, n: (rev(n), g, 0, 0)),
                  wide(M2_GROUPS), pl.BlockSpec(memory_space=pl.ANY)] + [ANY] * ne,
        out_specs=[wide(0), narrow(0), narrow(0), wide(z_off // gw), col, drow, drow, par,
                   pl.BlockSpec((1, gw), lambda g, n: (0, g))] + [ANY] * ne,
        out_shape=[jax.ShapeDtypeStruct((t, wm), F32), jax.ShapeDtypeStruct((t, gn), F32),
                   jax.ShapeDtypeStruct((t, gn), F32), jax.ShapeDtypeStruct(dproj.shape, dproj.dtype),
                   jax.ShapeDtypeStruct((M2_GROUPS, t, LANES), F32),
                   jax.ShapeDtypeStruct((M2_GROUPS, 1, LANES), F32), jax.ShapeDtypeStruct((M2_GROUPS, 1, LANES), F32),
                   jax.ShapeDtypeStruct((heads, 1, 1), F32), jax.ShapeDtypeStruct((1, wm), F32)]
        + [jax.ShapeDtypeStruct((3,) + s.shape[1:], s.dtype) for s in exchange],
        input_output_aliases={11: 3},
        scratch_shapes=[pltpu.VMEM((rh, M2_P, M2_N), F32)] + (ChipExchangeJob.sems(ne) if ne else []),
        compiler_params=_cparams(("arbitrary", "arbitrary")),
    )(act_xbc, act_xbc, act_xbc, proj, proj, a_log3, dt_bias3, dskip3, norm_w, s_all, d_mixed, dproj, *exchange)


SUBLANES = 8
LANES = 128


def _pick(dim, target, align):
    best = None
    for d in range(align, min(dim, target) + 1, align):
        if dim % d == 0:
            best = d
    return dim if best is None else best


def _row_tile(t):
    return _pick(t, 512, 16)


def matmul(a, b, *, mode, out_dtype, name, tm=1056, tn=512, tk=2048, residual=None, out_shards=1,
           a_shards=1, b_shards=1, exchange=()):
    if mode == "tn":
        kd, m = a.shape
        n = b.shape[-1] * b_shards
    else:
        m, kd = a.shape[-2], a.shape[-1] * a_shards
        n = b.shape[1] if mode == "nn" else b.shape[0]
    tm = _pick(m, tm, LANES if mode == "tn" else 16)
    ks = kd // a_shards
    tk = _pick(ks, tk, LANES)
    nkb = ks // tk
    nk = kd // tk
    ns_o, ns_b = n // out_shards, n // b_shards
    tn = _pick(math.gcd(ns_o, ns_b), tn, LANES)
    npb_o, npb_b = ns_o // tn, ns_b // tn
    if mode == "tn":
        a_spec = pl.BlockSpec((tk, tm), lambda i, j, k: (k, i))
    elif a_shards == 1:
        a_spec = pl.BlockSpec((tm, tk), lambda i, j, k: (i, k))
    else:
        a_spec = pl.BlockSpec((None, tm, tk), lambda i, j, k: (k // nkb, i, k % nkb))
    contract = ((1,), (1,)) if mode == "nt" else ((1,), (0,))
    if mode == "nt":
        b_spec = pl.BlockSpec((tn, tk), lambda i, j, k: (j, k))
    elif b_shards == 1:
        b_spec = pl.BlockSpec((tk, tn), lambda i, j, k: (k, j))
    else:
        b_spec = pl.BlockSpec((None, tk, tn), lambda i, j, k: (j // npb_b, k, j % npb_b))
    has_res = residual is not None
    ne = len(exchange)
    grid = (m // tm, n // tn, nk)

    def body(*refs):
        refs = list(refs)
        a_ref, b_ref = refs[:2]
        del refs[:2]
        r_ref = refs.pop(0) if has_res else None
        ex_in = [refs.pop(0) for _ in range(ne)]
        o_ref = refs.pop(0)
        ex_out = [refs.pop(0) for _ in range(ne)]
        at_ref = refs.pop(0) if mode == "tn" else None
        acc_ref = refs.pop(0) if nk > 1 else None
        k = pl.program_id(2)
        if ne:
            step = (pl.program_id(0) * grid[1] + pl.program_id(1)) * nk + k
            _run_beside(ChipExchangeJob(ex_in, ex_out, *refs), step, grid[0] * grid[1] * nk)
        if mode == "tn":
            @pl.when(pl.program_id(1) == 0)
            def _():
                at_ref[k] = jnp.transpose(a_ref[...].astype(F32)).astype(BF16)

            lhs = at_ref[k]
        else:
            lhs = a_ref[...].astype(BF16)
        part = lax.dot_general(lhs, b_ref[...].astype(BF16), (contract, ((), ())), preferred_element_type=F32)

        def finish(total):
            if has_res:
                total = total + r_ref[...]
            o_ref[...] = total.astype(o_ref.dtype)

        if nk == 1:
            finish(part)
        else:
            @pl.when(k == 0)
            def _():
                acc_ref[...] = part

            @pl.when((k > 0) & (k < nk - 1))
            def _():
                acc_ref[...] += part

            @pl.when(k == nk - 1)
            def _():
                finish(acc_ref[...] + part)

    in_specs = [a_spec, b_spec]
    args = [a, b]
    if has_res:
        in_specs.append(pl.BlockSpec((tm, tn), lambda i, j, k: (i, j)))
        args.append(residual)
    if out_shards == 1:
        out_spec = pl.BlockSpec((tm, tn), lambda i, j, k: (i, j))
        out_shape = jax.ShapeDtypeStruct((m, n), out_dtype)
    else:
        out_spec = pl.BlockSpec((None, tm, tn), lambda i, j, k: (j // npb_o, i, j % npb_o))
        out_shape = jax.ShapeDtypeStruct((out_shards, m, ns_o), out_dtype)
    scratch = [pltpu.VMEM((nk, tm, tk), BF16)] if mode == "tn" else []
    if nk > 1:
        scratch.append(pltpu.VMEM((tm, tn), F32))
    if not ne:
        return pl.pallas_call(
            body, name=name, grid=grid,
            in_specs=in_specs, out_specs=out_spec, out_shape=out_shape, scratch_shapes=scratch,
            compiler_params=_cparams(("parallel", "arbitrary", "arbitrary")),
        )(*args)
    return pl.pallas_call(
        body, name=name, grid=grid,
        in_specs=in_specs + [ANY] * ne, out_specs=[out_spec] + [ANY] * ne,
        out_shape=[out_shape] + [jax.ShapeDtypeStruct((3,) + s.shape[1:], s.dtype) for s in exchange],
        scratch_shapes=scratch + ChipExchangeJob.sems(ne),
        compiler_params=_cparams(("arbitrary", "arbitrary", "arbitrary")),
    )(*args, *exchange)


def rmsnorm_forward(x, w, *, name):
    t, d = x.shape
    tm = _row_tile(t)

    def body(x_ref, w_ref, o_ref):
        xv = x_ref[...]
        r = lax.rsqrt(jnp.mean(xv * xv, -1, keepdims=True) + NORM_EPS)
        o_ref[...] = (xv * r * w_ref[...]).astype(o_ref.dtype)

    return pl.pallas_call(
        body, name=name, grid=(t // tm,),
        in_specs=[pl.BlockSpec((tm, d), lambda i: (i, 0)), pl.BlockSpec((1, d), lambda i: (0, 0))],
        out_specs=pl.BlockSpec((tm, d), lambda i: (i, 0)),
        out_shape=jax.ShapeDtypeStruct((t, d), BF16),
        compiler_params=_cparams(("parallel",)),
    )(x, w)


def _rmsnorm_grads(xv, wv, dy):
    r = lax.rsqrt(jnp.mean(xv * xv, -1, keepdims=True) + NORM_EPS)
    xh = xv * r
    g = dy * wv
    dx = r * (g - xh * jnp.mean(g * xh, -1, keepdims=True))
    return dx, jnp.sum(dy * xh, axis=0, keepdims=True)


def rmsnorm_backward(x, w, dy, dres, *, n_pad, name):
    t, d = x.shape
    tm = _row_tile(t)

    def body(x_ref, w_ref, dy_ref, dr_ref, dx_ref, dx16_ref, dw_ref):
        i = pl.program_id(0)

        @pl.when(i == 0)
        def _():
            dw_ref[...] = jnp.zeros_like(dw_ref)

        dx, dw = _rmsnorm_grads(x_ref[...], w_ref[...], dy_ref[...])
        rows = i * tm + lax.broadcasted_iota(jnp.int32, (tm, 1), 0)
        dx = jnp.where(rows >= n_pad, dx + dr_ref[...], 0.0)
        dx_ref[...] = dx
        dx16_ref[...] = dx.astype(BF16)
        dw_ref[...] += dw

    row = pl.BlockSpec((tm, d), lambda i: (i, 0))
    vec = pl.BlockSpec((1, d), lambda i: (0, 0))
    return pl.pallas_call(
        body, name=name, grid=(t // tm,),
        in_specs=[row, vec, row, row], out_specs=[row, row, vec],
        out_shape=[jax.ShapeDtypeStruct((t, d), F32), jax.ShapeDtypeStruct((t, d), BF16),
                   jax.ShapeDtypeStruct((1, d), F32)],
        compiler_params=_cparams(("arbitrary",)),
    )(x, w, dy, dres)


def loss_head(h, w, target, *, n_skip):
    t, d = h.shape
    tm = _row_tile(t)

    def body(x_ref, w_ref, y_ref, loss_ref, dx_ref, dx16_ref, dw_ref):
        i = pl.program_id(0)

        @pl.when(i == 0)
        def _():
            dw_ref[...] = jnp.zeros_like(dw_ref)
            loss_ref[...] = jnp.zeros_like(loss_ref)

        xv, wv = x_ref[...], w_ref[...]
        r = lax.rsqrt(jnp.mean(xv * xv, -1, keepdims=True) + NORM_EPS)
        rows = i * tm + lax.broadcasted_iota(jnp.int32, (tm, 1), 0)
        err = jnp.where(rows >= n_skip, xv * r * wv - y_ref[...], 0.0)
        loss_ref[...] += 0.5 * jnp.sum(jnp.mean(err * err, -1, keepdims=True))
        dx, dw = _rmsnorm_grads(xv, wv, err * (1.0 / d))
        dx_ref[...] = dx
        dx16_ref[...] = dx.astype(BF16)
        dw_ref[...] += dw

    row = pl.BlockSpec((tm, d), lambda i: (i, 0))
    vec = pl.BlockSpec((1, d), lambda i: (0, 0))
    return pl.pallas_call(
        body, name="loss_head", grid=(t // tm,),
        in_specs=[row, vec, row],
        out_specs=[pl.BlockSpec((1, LANES), lambda i: (0, 0)), row, row, vec],
        out_shape=[jax.ShapeDtypeStruct((1, LANES), F32), jax.ShapeDtypeStruct((t, d), F32),
                   jax.ShapeDtypeStruct((t, d), BF16), jax.ShapeDtypeStruct((1, d), F32)],
        compiler_params=_cparams(("arbitrary",)),
    )(h, w, target)


HALO = SUBLANES


STRIP = 32


def _taps(blk, kk, rows):
    return [blk[HALO - (kk - 1) + j:HALO - (kk - 1) + j + rows, :] for j in range(kk)]


def _fir(taps, w):
    acc = w[0:1, :] * taps[0]
    for j in range(1, len(taps)):
        acc = acc + w[j:j + 1, :] * taps[j]
    return acc


def _fir_transposed(dpre, w, rows):
    kk = w.shape[0]
    acc = w[0:1, :] * dpre[kk - 1:kk - 1 + rows, :]
    for j in range(1, kk):
        acc = acc + w[j:j + 1, :] * dpre[kk - 1 - j:kk - 1 - j + rows, :]
    return acc


def _fold8(v):
    return jnp.sum(v.reshape(v.shape[0] // SUBLANES, SUBLANES, v.shape[1]), axis=0)


def _strips(tm, body, init):
    def step(r, carry):
        return body(pl.multiple_of(r * STRIP, STRIP), carry)
    return lax.fori_loop(0, tm // STRIP, step, init)


def _dsilu(p):
    s = _sigmoid(p)
    return s * (1.0 + p * (1.0 - s))


def conv_silu(x, w, b, *, x_off, name):
    t = x.shape[0]
    kk, width = w.shape
    tm = _row_tile(t)
    tc = _pick(width, 512, LANES)
    ob, nh = x_off // tc, tm // HALO

    def body(prev_ref, x_ref, w_ref, b_ref, o_ref, scr):
        i = pl.program_id(1)
        scr[0:HALO, :] = jnp.where(i > 0, prev_ref[...], 0.0)
        scr[HALO:HALO + tm, :] = x_ref[...]
        wv, bv = w_ref[...], b_ref[...]

        def strip(base, carry):
            blk = scr[pl.ds(base, STRIP + HALO), :]
            o_ref[pl.ds(base, STRIP), :] = _silu(_fir(_taps(blk, kk, STRIP), wv) + bv)
            return carry

        _strips(tm, strip, 0)

    return pl.pallas_call(
        body, name=name, grid=(width // tc, t // tm),
        in_specs=[pl.BlockSpec((HALO, tc), lambda j, i: (jnp.maximum(i * nh - 1, 0), ob + j)),
                  pl.BlockSpec((tm, tc), lambda j, i: (i, ob + j)),
                  pl.BlockSpec((kk, tc), lambda j, i: (0, j)),
                  pl.BlockSpec((1, tc), lambda j, i: (0, j))],
        out_specs=pl.BlockSpec((tm, tc), lambda j, i: (i, j)),
        out_shape=jax.ShapeDtypeStruct((t, width), F32),
        scratch_shapes=[pltpu.VMEM((tm + HALO, tc), F32)],
        compiler_params=_cparams(("parallel", "arbitrary")),
    )(x, x, w, b)


def conv_silu_backward(x, w, b, dact, dst, *, x_off, name):
    t = x.shape[0]
    kk, width = w.shape
    tm = _row_tile(t)
    tc = _pick(width, 512, LANES)
    ob, nh, nt = x_off // tc, tm // HALO, t // tm
    last_h = t // HALO - 1

    def body(prev_ref, x_ref, next_ref, w_ref, b_ref, d_ref, dnext_ref, _, dx_ref, dw_ref, db_ref, scr_x, scr_d):
        i = pl.program_id(1)

        @pl.when(i == 0)
        def _():
            dw_ref[...] = jnp.zeros_like(dw_ref)
            db_ref[...] = jnp.zeros_like(db_ref)

        wv, bv = w_ref[...], b_ref[...]
        scr_x[0:HALO, :] = jnp.where(i > 0, prev_ref[...], 0.0)
        scr_x[HALO:HALO + tm, :] = x_ref[...]
        scr_x[HALO + tm:, :] = next_ref[...]
        scr_d[0:tm, :] = d_ref[...]
        scr_d[tm:, :] = jnp.where(i < nt - 1, dnext_ref[...], 0.0)

        def strip(base, carry):
            taps = _taps(scr_x[pl.ds(base, STRIP + 2 * HALO), :], kk, STRIP + HALO)
            dpre = scr_d[pl.ds(base, STRIP + HALO), :] * _dsilu(_fir(taps, wv) + bv)
            dx_ref[pl.ds(base, STRIP), :] = _fir_transposed(dpre, wv, STRIP).astype(dx_ref.dtype)
            d0 = dpre[0:STRIP, :]
            return tuple(c + _fold8(d0 * tap[0:STRIP, :]) for c, tap in zip(carry, taps)) + (carry[kk] + _fold8(d0),)

        sums = _strips(tm, strip, tuple(jnp.zeros((SUBLANES, tc), F32) for _ in range(kk + 1)))
        for j in range(kk):
            dw_ref[j:j + 1, :] += jnp.sum(sums[j], axis=0, keepdims=True)
        db_ref[0:1, :] += jnp.sum(sums[kk], axis=0, keepdims=True)

    nxt = lambda j, i: (jnp.minimum((i + 1) * nh, last_h), j)
    acc = pl.BlockSpec((SUBLANES, tc), lambda j, i: (0, j))
    return pl.pallas_call(
        body, name=name, grid=(width // tc, nt),
        in_specs=[pl.BlockSpec((HALO, tc), lambda j, i: (jnp.maximum(i * nh - 1, 0), ob + j)),
                  pl.BlockSpec((tm, tc), lambda j, i: (i, ob + j)),
                  pl.BlockSpec((HALO, tc), lambda j, i: (jnp.minimum((i + 1) * nh, last_h), ob + j)),
                  pl.BlockSpec((kk, tc), lambda j, i: (0, j)),
                  pl.BlockSpec((1, tc), lambda j, i: (0, j)),
                  pl.BlockSpec((tm, tc), lambda j, i: (i, j)),
                  pl.BlockSpec((HALO, tc), nxt),
                  pl.BlockSpec(memory_space=pl.ANY)],
        out_specs=[pl.BlockSpec((tm, tc), lambda j, i: (i, ob + j)), acc, acc],
        out_shape=[jax.ShapeDtypeStruct(dst.shape, dst.dtype), jax.ShapeDtypeStruct((SUBLANES, width), F32),
                   jax.ShapeDtypeStruct((SUBLANES, width), F32)],
        input_output_aliases={7: 0},
        scratch_shapes=[pltpu.VMEM((tm + 2 * HALO, tc), F32), pltpu.VMEM((tm + HALO, tc), F32)],
        compiler_params=_cparams(("parallel", "arbitrary")),
    )(x, x, x, w, b, dact, dact, dst)


def conv_glu(u, w, *, name):
    _, t, f = u.shape
    kk = w.shape[1]
    tm = _row_tile(t)
    tc = _pick(f, 512, LANES)
    nh = tm // HALO

    def body(prev_ref, x_ref, w_ref, o_ref, scr):
        i = pl.program_id(1)
        scr[:, 0:HALO, :] = jnp.where(i > 0, prev_ref[...], 0.0)
        scr[:, HALO:, :] = x_ref[...]
        wg, wv = w_ref[0], w_ref[1]

        def strip(base, carry):
            gate = _fir(_taps(scr[0, pl.ds(base, STRIP + HALO), :], kk, STRIP), wg)
            val = _fir(_taps(scr[1, pl.ds(base, STRIP + HALO), :], kk, STRIP), wv)
            o_ref[pl.ds(base, STRIP), :] = (_silu(gate) * val).astype(o_ref.dtype)
            return carry

        _strips(tm, strip, 0)

    return pl.pallas_call(
        body, name=name, grid=(f // tc, t // tm),
        in_specs=[pl.BlockSpec((2, HALO, tc), lambda j, i: (0, jnp.maximum(i * nh - 1, 0), j)),
                  pl.BlockSpec((2, tm, tc), lambda j, i: (0, i, j)),
                  pl.BlockSpec((2, kk, tc), lambda j, i: (0, 0, j))],
        out_specs=pl.BlockSpec((tm, tc), lambda j, i: (i, j)),
        out_shape=jax.ShapeDtypeStruct((t, f), BF16),
        scratch_shapes=[pltpu.VMEM((2, tm + HALO, tc), F32)],
        compiler_params=_cparams(("parallel", "arbitrary")),
    )(u, u, w)


def conv_glu_backward(u, w, dact, *, name):
    _, t, f = u.shape
    kk = w.shape[1]
    tm = _row_tile(t)
    tc = _pick(f, 512, LANES)
    nh, nt = tm // HALO, t // tm
    last_h = t // HALO - 1
    ext = tm + HALO
    first = HALO - (kk - 1)

    def body(prev_ref, x_ref, next_ref, w_ref, d_ref, dn_ref, du_ref, dw_ref, scr_x, scr_d):
        i = pl.program_id(1)

        @pl.when(i == 0)
        def _():
            dw_ref[...] = jnp.zeros_like(dw_ref)

        scr_x[:, 0:HALO, :] = jnp.where(i > 0, prev_ref[...], 0.0)
        scr_x[:, HALO:HALO + tm, :] = x_ref[...]
        scr_x[:, HALO + tm:, :] = next_ref[...]
        scr_d[0:tm, :] = d_ref[...]
        scr_d[tm:, :] = jnp.where(i < nt - 1, dn_ref[...], 0.0)
        ws = (w_ref[0], w_ref[1])

        def strip(base, carry):
            taps = [_taps(scr_x[h, pl.ds(base, STRIP + 2 * HALO), :], kk, STRIP + HALO) for h in range(2)]
            gate, val = _fir(taps[0], ws[0]), _fir(taps[1], ws[1])
            dact = scr_d[pl.ds(base, STRIP + HALO), :]
            s = _sigmoid(gate)
            dconv = (dact * val * (s * (1.0 + gate * (1.0 - s))), dact * (gate * s))
            out = []
            for h in range(2):
                du_ref[h, pl.ds(base, STRIP), :] = _fir_transposed(dconv[h], ws[h], STRIP).astype(du_ref.dtype)
                d0 = dconv[h][0:STRIP, :]
                out += [c + _fold8(d0 * tap[0:STRIP, :]) for c, tap in zip(carry[h * kk:(h + 1) * kk], taps[h])]
            return tuple(out)

        sums = _strips(tm, strip, tuple(jnp.zeros((SUBLANES, tc), F32) for _ in range(2 * kk)))
        for h in range(2):
            for j in range(kk):
                dw_ref[h, j:j + 1, :] += jnp.sum(sums[h * kk + j], axis=0, keepdims=True)

    return pl.pallas_call(
        body, name=name, grid=(f // tc, nt),
        in_specs=[pl.BlockSpec((2, HALO, tc), lambda j, i: (0, jnp.maximum(i * nh - 1, 0), j)),
                  pl.BlockSpec((2, tm, tc), lambda j, i: (0, i, j)),
                  pl.BlockSpec((2, HALO, tc), lambda j, i: (0, jnp.minimum((i + 1) * nh, last_h), j)),
                  pl.BlockSpec((2, kk, tc), lambda j, i: (0, 0, j)),
                  pl.BlockSpec((tm, tc), lambda j, i: (i, j)),
                  pl.BlockSpec((HALO, tc), lambda j, i: (jnp.minimum((i + 1) * nh, last_h), j))],
        out_specs=[pl.BlockSpec((2, tm, tc), lambda j, i: (0, i, j)),
                   pl.BlockSpec((2, SUBLANES, tc), lambda j, i: (0, 0, j))],
        out_shape=[jax.ShapeDtypeStruct((2, t, f), BF16), jax.ShapeDtypeStruct((2, SUBLANES, f), F32)],
        scratch_shapes=[pltpu.VMEM((2, tm + 2 * HALO, tc), F32), pltpu.VMEM((ext, tc), F32)],
        compiler_params=_cparams(("parallel", "arbitrary")),
    )(u, u, u, w, dact, dact)


ELEMENTWISE_BLOCK_BYTES = 3 * 1024 * 1024


def _rows_spec(a, tr):
    c = a.shape[-1]
    if a.ndim == 2:
        return pl.BlockSpec((tr, c), lambda i, *_: (i, 0))
    return pl.BlockSpec((None, tr, c), lambda i, *_: (0, i, 0))


def adamw(w, g, m, v, *, name):
    r, c = w.shape[-2:]
    tr = _pick(r, max(SUBLANES, ELEMENTWISE_BLOCK_BYTES // (4 * c) // SUBLANES * SUBLANES), SUBLANES)

    def body(w_ref, g_ref, m_ref, v_ref, d_ref, nm_ref, nv_ref):
        gv = g_ref[...]
        nm = ADAM_B1 * m_ref[...] + (1.0 - ADAM_B1) * gv
        nv = ADAM_B2 * v_ref[...] + (1.0 - ADAM_B2) * (gv * gv)
        m_hat = nm / (1.0 - ADAM_B1 ** ADAM_STEP)
        v_hat = nv / (1.0 - ADAM_B2 ** ADAM_STEP)
        d_ref[...] = -ADAM_LR * (m_hat / (jnp.sqrt(v_hat) + ADAM_EPS) + ADAM_WD * w_ref[...])
        nm_ref[...] = nm
        nv_ref[...] = nv

    shp = jax.ShapeDtypeStruct(w.shape, F32)
    return pl.pallas_call(
        body, name=name, grid=(r // tr,),
        in_specs=[_rows_spec(a, tr) for a in (w, g, m, v)], out_specs=[_rows_spec(w, tr)] * 3, out_shape=[shp] * 3,
        compiler_params=_cparams(("parallel",)),
    )(w, g, m, v)


def cast_into_slot(x, slot, n_slots, *, name):
    r, c = x.shape[-2:]
    tr = _pick(r, max(16, ELEMENTWISE_BLOCK_BYTES // (4 * c) // 16 * 16), 16)

    def body(s_ref, x_ref, o_ref):
        o_ref[...] = x_ref[...].astype(o_ref.dtype)

    grid_spec = pltpu.PrefetchScalarGridSpec(
        num_scalar_prefetch=1, grid=(r // tr,),
        in_specs=[_rows_spec(x, tr)],
        out_specs=pl.BlockSpec((None, tr, c), lambda i, s: (s[0], i, 0)))
    return pl.pallas_call(
        body, name=name, grid_spec=grid_spec,
        out_shape=jax.ShapeDtypeStruct((n_slots, r, c), BF16),
        compiler_params=_cparams(("arbitrary",)),
    )(jnp.reshape(slot, (1,)).astype(jnp.int32), x)


class Layout:
    def __init__(self, d):
        self.d = d
        self.h_dn = d // DN_DK
        self.h_m2 = d // M2_P
        self.gn = M2_GROUPS * M2_N
        self.w_xbc = d + 2 * self.gn
        self.z_off = 3 * d
        self.m2z_off = 4 * d
        self.xbc_off = 5 * d
        self.small_off = 5 * d + self.w_xbc
        self.n_small = 2 * self.h_dn + self.h_m2
        self.p = self.small_off + LANES
        self.p_orig = self.small_off + self.n_small

    def w_in_of_slots(self, slab):
        n_slots, rows, cs = slab.shape
        pieces = []
        for o0, _, ln in sorted(self._segments(), key=lambda seg: seg[1]):
            for s in range(n_slots):
                lo, hi = max(o0, s * cs), min(o0 + ln, (s + 1) * cs)
                if lo < hi:
                    pieces.append(slab[s, :, lo - s * cs:hi - s * cs])
        pieces.append(jnp.zeros((rows, LANES - self.n_small), slab.dtype))
        return jnp.concatenate(pieces, axis=1)

    def _segments(self):
        d, s2, so = self.d, 2 * self.h_dn, self.small_off
        return [(0, 0, 4 * d), (4 * d, so, s2), (4 * d + s2, 4 * d, so - 4 * d),
                (self.p_orig - self.h_m2, so + s2, self.h_m2)]

    def slots_of_w_in(self, w, n_slots):
        cs = self.p_orig // n_slots
        slots = []
        for s in range(n_slots):
            pieces = []
            for o0, k0, ln in self._segments():
                lo, hi = max(o0, s * cs), min(o0 + ln, (s + 1) * cs)
                if lo < hi:
                    pieces.append(w[:, k0 + lo - o0:k0 + hi - o0])
            slots.append(jnp.concatenate(pieces, axis=1))
        return jnp.stack(slots, axis=0)


def _on_lanes(v, first):
    return jnp.pad(v, ((0, 0), (first, LANES - first - v.shape[1])))


def local_step(h0, target, wts, slabs, chip, core, *, n_pad, n_meta):
    t, d = h0.shape
    lay = Layout(d)
    hd, hm = lay.h_dn, lay.h_m2
    zeros_b = jnp.zeros((1, 3 * d), F32)
    dn_al, dn_db = _on_lanes(wts["dn_a_log"], hd), _on_lanes(wts["dn_dt_bias"], hd)
    dn_nw = wts["dn_norm_w"].reshape(1, 1, DN_DK)
    m2_al, m2_db = _on_lanes(wts["m2_a_log"], 2 * hd), _on_lanes(wts["m2_dt_bias"], 2 * hd)
    m2_dk = wts["m2_d"].reshape(hm, 1, 1)

    hn1 = rmsnorm_forward(h0, wts["norm_mix_w"], name="norm_mix")
    proj = matmul(hn1, wts["w_in"], mode="nn", out_dtype=F32, name="in_proj", tn=1920)
    act_qkv = conv_silu(proj, wts["dn_conv_w"], zeros_b, x_off=0, name="dn_conv")
    act_xbc = conv_silu(proj, wts["m2_conv_w"], wts["m2_conv_b"], x_off=lay.xbc_off, name="m2_conv")
    small_at = dict(small_off=lay.small_off)
    dt_at = dict(small_off=lay.small_off, dt_lane=2 * hd)
    mixed, s_dn, inv_dn, ffn_up_all, ffn_down_all = gdn_forward(
        act_qkv, proj, dn_al, dn_db, dn_nw, n_pad=n_pad, z_off=lay.z_off, out_cols=2 * d,
        gather=[slabs["ffn_up"], slabs["ffn_down"]], **small_at)
    mixed, s_m2, w_out_all = ssd_forward(act_xbc, proj, m2_al, m2_db, m2_dk, wts["m2_norm_w"], mixed,
                                         n_pad=n_pad, z_off=lay.m2z_off, gather=[slabs["w_out"]], **dt_at)
    wts = dict(wts, w_out=w_out_all.reshape(-1, d), ffn_down=ffn_down_all.reshape(-1, d),
               ffn_up=jnp.transpose(ffn_up_all, (1, 0, 2)).reshape(d, -1))
    h1 = matmul(mixed, wts["w_out"], mode="nn", out_dtype=F32, name="out_proj", tk=2 * d, residual=h0)
    hn2 = rmsnorm_forward(h1, wts["norm_ffn_w"], name="norm_ffn")
    up = matmul(hn2, wts["ffn_up"], mode="nn", out_dtype=F32, name="ffn_up", tn=1408, out_shards=2)
    kf, f = wts["ffn_conv_w"].shape[0], wts["ffn_conv_w"].shape[1] // 2
    w_glu = jnp.transpose(wts["ffn_conv_w"].reshape(kf, 2, f), (1, 0, 2))
    act = conv_glu(up, w_glu, name="ffn_conv")
    h2 = matmul(act, wts["ffn_down"], mode="nn", out_dtype=F32, name="ffn_down", tk=f // 2, residual=h1)
    loss, dh2, dh2_16, d_nfw = loss_head(h2, wts["norm_final_w"].reshape(1, d), target, n_skip=n_pad + n_meta)

    g = {}
    d_act = matmul(dh2_16, wts["ffn_down"], mode="nt", out_dtype=F32, name="d_ffn_act", tn=1408)
    g["ffn_down"] = matmul(act, dh2_16, mode="tn", out_dtype=F32, name="dw_ffn_down", tm=512, tn=1024, tk=t)
    dup, d_fcw = conv_glu_backward(up, w_glu, d_act, name="d_ffn_conv")
    g["ffn_conv_w"] = jnp.transpose(d_fcw[:, :kf], (1, 0, 2)).reshape(kf, 2 * f)
    dhn2 = matmul(dup, wts["ffn_up"], mode="nt", out_dtype=F32, name="d_norm_ffn_out", tk=f, a_shards=2)
    g["ffn_up"] = matmul(hn2, dup, mode="tn", out_dtype=F32, name="dw_ffn_up", tm=512, tn=1408, tk=t,
                         b_shards=2, out_shards=4)
    dh1, dh1_16, g["norm_ffn_w"] = rmsnorm_backward(h1, wts["norm_ffn_w"], dhn2, dh2, n_pad=n_pad, name="d_norm_ffn")
    d_mixed = matmul(dh1_16, wts["w_out"], mode="nt", out_dtype=F32, name="d_mixed", tn=512)
    g["w_out"] = matmul(mixed, dh1_16, mode="tn", out_dtype=F32, name="dw_out", tm=512, tn=1024, tk=t)

    early = ("w_out", "ffn_up", "ffn_down")
    slots = [g[k] if g[k].ndim == 3 else g[k].reshape(N_CHIPS, -1, g[k].shape[1]) for k in early]
    dq, dk, dv, dproj, dsm_dn, g_al, g_db, g_nw, *from_sibling = gdn_backward(
        act_qkv, proj, dn_al, dn_db, dn_nw, s_dn, inv_dn, d_mixed, n_pad=n_pad, z_off=lay.z_off, exchange=slots,
        **small_at)
    g["dn_a_log"], g["dn_dt_bias"] = (jnp.sum(v, axis=0)[:, hd:2 * hd] for v in (g_al, g_db))
    g["dn_norm_w"] = jnp.sum(g_nw, axis=0)
    p16, own = [], []
    for k, slot, rb in zip(early, slots, from_sibling):
        a, b = pair_add(slot, rb, chip, core, name="grad_pair_add_" + k)
        p16.append(a)
        own.append(b)
    dxs, dbm, dcm, dproj, dsm_m2, g_al, g_db, g_dk, g["m2_norm_w"], *from_chips = ssd_backward(
        act_xbc, proj, m2_al, m2_db, m2_dk, wts["m2_norm_w"], s_m2, d_mixed, dproj,
        n_pad=n_pad, z_off=lay.m2z_off, exchange=p16, **dt_at)
    for k, o, q in zip(early, own, from_chips):
        g[k] = (o, q)
    g["m2_a_log"], g["m2_dt_bias"] = (jnp.sum(v, axis=0)[:, 2 * hd:2 * hd + hm] for v in (g_al, g_db))
    g["m2_d"] = g_dk.reshape(1, hm)

    kc = wts["dn_conv_w"].shape[0]
    dw_parts = []
    for idx, dpart in enumerate((dq, dk, dv)):
        dproj, dw, _ = conv_silu_backward(proj, wts["dn_conv_w"][:, idx * d:(idx + 1) * d], zeros_b[:, :d], dpart,
                                          dproj, x_off=idx * d, name=f"d_dn_conv{idx}")
        dw_parts.append(dw[:kc])
    g["dn_conv_w"] = jnp.concatenate(dw_parts, axis=1)
    dw_parts, db_parts = [], []
    off = 0
    for idx, dpart in enumerate((dxs, dbm, dcm)):
        wd = dpart.shape[1]
        dproj, dw, db = conv_silu_backward(proj, wts["m2_conv_w"][:, off:off + wd], wts["m2_conv_b"][:, off:off + wd],
                                           dpart, dproj, x_off=lay.xbc_off + off, name=f"d_m2_conv{idx}")
        dw_parts.append(dw[:kc])
        db_parts.append(db[:1])
        off += wd
    g["m2_conv_w"] = jnp.concatenate(dw_parts, axis=1)
    g["m2_conv_b"] = jnp.concatenate(db_parts, axis=1)
    dsmall = jnp.sum(dsm_dn, axis=0) + jnp.sum(dsm_m2, axis=0)
    dproj = lax.dynamic_update_slice(dproj, dsmall.astype(BF16), (0, lay.small_off))

    dw_in = matmul(hn1, dproj, mode="tn", out_dtype=F32, name="dw_in", tm=512, tn=896, tk=t)
    w_in_slots = lay.slots_of_w_in(dw_in, N_CHIPS)
    (from_sibling,) = pair_exchange([w_in_slots], name="grad_pair_exchange_w_in")
    p16_w_in, own_w_in = pair_add(w_in_slots, from_sibling, chip, core, name="grad_pair_add_w_in")
    dhn1, from_chips = matmul(dproj, wts["w_in"], mode="nt", out_dtype=F32, name="d_norm_mix_out", tk=lay.p // 3,
                              exchange=[p16_w_in])
    g["w_in"] = (own_w_in, from_chips)
    dh0, _, g["norm_mix_w"] = rmsnorm_backward(h0, wts["norm_mix_w"], dhn1, dh1, n_pad=n_pad, name="d_norm_mix")
    g["norm_final_w"] = d_nfw
    return loss[0, 0], dh0, g


MESH = pl.DeviceIdType.MESH
ANY = pl.BlockSpec(memory_space=pl.ANY)
N_CHIPS = 4
N_DEV = 8


def _mesh_pos():
    return lax.axis_index("x"), lax.axis_index("y"), lax.axis_index("c")


def _other_chips(x, y):
    return [(1 - x, y), (x, 1 - y), (1 - x, 1 - y)]


def _rcopy(src, dst, send_sems, recv_sems, k, to):
    return pltpu.make_async_remote_copy(src_ref=src, dst_ref=dst, send_sem=send_sems.at[k], recv_sem=recv_sems.at[k],
                                        device_id=to, device_id_type=MESH)


class GatherJob:
    def __init__(self, slabs, send_sems, recv_sems):
        self.slabs, self.send, self.recv = slabs, send_sems, recv_sems
        self.x, self.y, self.c = _mesh_pos()

    @staticmethod
    def sems(n):
        return [pltpu.SemaphoreType.DMA((6 * n,)), pltpu.SemaphoreType.DMA((6 * n,))]

    def _pieces(self):
        x, y, c = self.x, self.y, self.c
        for a, slab in enumerate(self.slabs):
            half = slab.shape[1] // 2
            for j, (px, py) in enumerate(_other_chips(x, y)):
                yield a, j, (px, py), slab, pl.ds(c * half, half), pl.ds((1 - c) * half, half)

    def _ici(self, a, j, chip, ref):
        return _rcopy(ref, ref, self.send, self.recv, 6 * a + j, (chip[0], chip[1], self.c))

    def _d2d(self, a, j, ref):
        return _rcopy(ref, ref, self.send, self.recv, 6 * a + 3 + j, (self.x, self.y, 1 - self.c))

    def begin(self):
        for a, j, chip, slab, mine, _ in self._pieces():
            self._ici(a, j, chip, slab.at[2 * self.x + self.y, mine]).start()

    def pass_on(self):
        for a, j, chip, slab, mine, _ in self._pieces():
            landed = slab.at[2 * chip[0] + chip[1], mine]
            self._ici(a, j, chip, landed).wait_recv()
            self._d2d(a, j, landed).start()

    def end(self):
        for a, j, chip, slab, mine, theirs in self._pieces():
            self._d2d(a, j, slab.at[2 * chip[0] + chip[1], theirs]).wait_recv()
        for a, j, chip, slab, mine, _ in self._pieces():
            self._ici(a, j, chip, slab.at[2 * self.x + self.y, mine]).wait_send()
            self._d2d(a, j, slab.at[2 * chip[0] + chip[1], mine]).wait_send()


class ChipExchangeJob:
    def __init__(self, parts, outs, send_sems, recv_sems):
        self.parts, self.outs, self.send, self.recv = parts, outs, send_sems, recv_sems
        self.x, self.y, self.c = _mesh_pos()

    @staticmethod
    def sems(n):
        return [pltpu.SemaphoreType.DMA((3 * n,)), pltpu.SemaphoreType.DMA((3 * n,))]

    def _copies(self):
        for a, (part, out) in enumerate(zip(self.parts, self.outs)):
            for j, (px, py) in enumerate(_other_chips(self.x, self.y)):
                yield _rcopy(part.at[2 * px + py], out.at[j], self.send, self.recv, 3 * a + j, (px, py, self.c))

    def begin(self):
        for cp in self._copies():
            cp.start()

    def end(self):
        for cp in self._copies():
            cp.wait()


class PairExchangeJob:
    def __init__(self, grads, outs, send_sems, recv_sems):
        self.grads, self.outs, self.send, self.recv = grads, outs, send_sems, recv_sems
        self.x, self.y, self.c = _mesh_pos()

    @staticmethod
    def sems(n):
        return [pltpu.SemaphoreType.DMA((n,)), pltpu.SemaphoreType.DMA((n,))]

    def _copies(self):
        for a, (grad, out) in enumerate(zip(self.grads, self.outs)):
            half = grad.shape[1] // 2
            yield _rcopy(grad.at[:, pl.ds((1 - self.c) * half, half), :], out, self.send, self.recv, a,
                         (self.x, self.y, 1 - self.c))

    def begin(self):
        for cp in self._copies():
            cp.start()

    def end(self):
        for cp in self._copies():
            cp.wait()


def gather_shards(slabs, *, name):
    n = len(slabs)

    def body(*refs):
        job = GatherJob(refs[n:2 * n], *refs[2 * n:])
        job.begin()
        job.pass_on()
        job.end()

    return pl.pallas_call(
        body, name=name,
        in_specs=[ANY] * n, out_specs=[ANY] * n,
        out_shape=[jax.ShapeDtypeStruct(s.shape, s.dtype) for s in slabs],
        input_output_aliases={a: a for a in range(n)},
        scratch_shapes=GatherJob.sems(n),
        compiler_params=pltpu.CompilerParams(has_side_effects=True),
    )(*slabs)


def _half_rows(s):
    return jax.ShapeDtypeStruct((s.shape[0], s.shape[1] // 2, s.shape[2]), s.dtype)


def pair_exchange(grads, *, name):
    n = len(grads)

    def body(*refs):
        job = PairExchangeJob(refs[:n], refs[n:2 * n], *refs[2 * n:])
        job.begin()
        job.end()

    return pl.pallas_call(
        body, name=name, in_specs=[ANY] * n, out_specs=[ANY] * n,
        out_shape=[_half_rows(s) for s in grads],
        scratch_shapes=PairExchangeJob.sems(n),
        compiler_params=pltpu.CompilerParams(has_side_effects=True),
    )(*grads)


def pair_join(wholes, *, name):
    n = len(wholes)

    def body(*refs):
        outs = refs[n:2 * n]
        send_sems, recv_sems = refs[2 * n:]
        x, y, c = _mesh_pos()
        cps = []
        for a in range(n):
            half = outs[a].shape[0] // 2
            rows = outs[a].at[pl.ds(c * half, half)]
            cp = _rcopy(rows, rows, send_sems, recv_sems, a, (x, y, 1 - c))
            cp.start()
            cps.append(cp)
        for a, cp in enumerate(cps):
            cp.wait_send()
            half = outs[a].shape[0] // 2
            theirs = outs[a].at[pl.ds((1 - c) * half, half)]
            _rcopy(theirs, theirs, send_sems, recv_sems, a, (x, y, 1 - c)).wait_recv()

    return pl.pallas_call(
        body, name=name, in_specs=[ANY] * n, out_specs=[ANY] * n,
        out_shape=[jax.ShapeDtypeStruct(s.shape, s.dtype) for s in wholes],
        input_output_aliases={a: a for a in range(n)},
        scratch_shapes=[pltpu.SemaphoreType.DMA((n,)), pltpu.SemaphoreType.DMA((n,))],
        compiler_params=pltpu.CompilerParams(has_side_effects=True),
    )(*wholes)


def gather_all(v, *, name):
    def body(in_ref, out_ref, send_sems, recv_sems, local_sem):
        x, y, c = _mesh_pos()
        mine = out_ref.at[4 * x + 2 * y + c]
        lc = pltpu.make_async_copy(in_ref, mine, local_sem)
        lc.start()
        cps = []
        for k in range(1, N_DEV):
            flip = lambda v, bit: 1 - v if (k >> bit) & 1 else v
            cp = _rcopy(in_ref, mine, send_sems, recv_sems, k - 1, (flip(x, 2), flip(y, 1), flip(c, 0)))
            cp.start()
            cps.append(cp)
        for cp in cps:
            cp.wait()
        lc.wait()

    return pl.pallas_call(
        body, name=name, in_specs=[ANY], out_specs=ANY,
        out_shape=jax.ShapeDtypeStruct((N_DEV,) + v.shape, v.dtype),
        scratch_shapes=[pltpu.SemaphoreType.DMA((N_DEV - 1,)), pltpu.SemaphoreType.DMA((N_DEV - 1,)),
                        pltpu.SemaphoreType.DMA(())],
        compiler_params=pltpu.CompilerParams(has_side_effects=True),
    )(v)


def _sum_tile(rows, cols):
    return _pick(rows, max(16, ELEMENTWISE_BLOCK_BYTES // (4 * cols) // 16 * 16), 16)


def pair_add(g, rb, chip, c, *, name):
    _, r, cols = g.shape
    half = r // 2
    tr = _sum_tile(half, cols)
    nrt = half // tr

    def body(s_ref, g_ref, rb_ref, p16_ref, own_ref):
        v = g_ref[...] + rb_ref[...]
        p16_ref[...] = v.astype(p16_ref.dtype)

        @pl.when(pl.program_id(1) == s_ref[0])
        def _():
            own_ref[...] = v

    grid_spec = pltpu.PrefetchScalarGridSpec(
        num_scalar_prefetch=1, grid=(nrt, N_CHIPS),
        in_specs=[pl.BlockSpec((None, tr, cols), lambda i, k, s: (k, s[1] * nrt + i, 0)),
                  pl.BlockSpec((None, tr, cols), lambda i, k, s: (k, i, 0))],
        out_specs=[pl.BlockSpec((None, tr, cols), lambda i, k, s: (k, i, 0)),
                   pl.BlockSpec((tr, cols), lambda i, k, s: (i, 0))])
    return pl.pallas_call(
        body, name=name, grid_spec=grid_spec,
        out_shape=[jax.ShapeDtypeStruct((N_CHIPS, half, cols), BF16), jax.ShapeDtypeStruct((half, cols), F32)],
        compiler_params=_cparams(("arbitrary", "arbitrary")),
    )(jnp.stack([chip, c]).astype(jnp.int32), g, rb)


def chip_add(own, q, c, *, name):
    r, cols = own.shape
    tr = _sum_tile(r, cols)
    nrt = r // tr

    def body(s_ref, own_ref, q_ref, o_ref):
        o_ref[...] = ((own_ref[...] + q_ref[0].astype(F32)) + q_ref[1].astype(F32)) + q_ref[2].astype(F32)

    grid_spec = pltpu.PrefetchScalarGridSpec(
        num_scalar_prefetch=1, grid=(nrt,),
        in_specs=[pl.BlockSpec((tr, cols), lambda i, s: (i, 0)), pl.BlockSpec((3, tr, cols), lambda i, s: (0, i, 0))],
        out_specs=pl.BlockSpec((tr, cols), lambda i, s: (s[0] * nrt + i, 0)))
    return pl.pallas_call(
        body, name=name, grid_spec=grid_spec,
        out_shape=jax.ShapeDtypeStruct((2 * r, cols), F32),
        compiler_params=_cparams(("arbitrary",)),
    )(jnp.reshape(c, (1,)).astype(jnp.int32), own, q)


def sum_slots(v, *, name):
    n, r, cols = v.shape
    tr = _sum_tile(r, cols)

    def body(v_ref, o_ref):
        acc = v_ref[0]
        for k in range(1, n):
            acc = acc + v_ref[k]
        o_ref[...] = acc

    return pl.pallas_call(
        body, name=name, grid=(r // tr,),
        in_specs=[pl.BlockSpec((n, tr, cols), lambda i: (0, i, 0))],
        out_specs=pl.BlockSpec((tr, cols), lambda i: (i, 0)),
        out_shape=jax.ShapeDtypeStruct((r, cols), F32),
        compiler_params=_cparams(("parallel",)),
    )(v)


PACK_ROWS = 16


def _pack(arrays):
    parts = []
    for a in arrays:
        flat = a.reshape(-1).astype(F32)
        size = PACK_ROWS * LANES
        pad = (-flat.shape[0]) % size
        parts.append(jnp.pad(flat, (0, pad)))
    return jnp.concatenate(parts).reshape(-1, LANES)


def _unpack(slab, shapes):
    out, row = [], 0
    for shp in shapes:
        n = 1
        for s in shp:
            n *= s
        rows = -(-n // (PACK_ROWS * LANES)) * PACK_ROWS
        out.append(slab[row:row + rows].reshape(-1)[:n].reshape(shp))
        row += rows
    return out


WEIGHT_NAMES = ("meta_tokens", "norm_mix_w", "w_in", "dn_conv_w", "dn_a_log", "dn_dt_bias", "dn_norm_w", "m2_conv_w",
                "m2_conv_b", "m2_a_log", "m2_dt_bias", "m2_d", "m2_norm_w", "w_out", "norm_ffn_w", "ffn_up",
                "ffn_conv_w", "ffn_down", "norm_final_w")
BIG = ("w_in", "w_out", "ffn_up", "ffn_down")
SMALL_SHARDED = ("meta_tokens", "dn_conv_w", "m2_conv_w", "ffn_conv_w")
SMALL = tuple(n for n in WEIGHT_NAMES if n not in BIG)


def kernel(x, meta_tokens, norm_mix_w, w_in, dn_conv_w, dn_a_log, dn_dt_bias, dn_norm_w, m2_conv_w, m2_conv_b, m2_a_log, m2_dt_bias, m2_d, m2_norm_w, w_out, norm_ffn_w, ffn_up, ffn_conv_w, ffn_down, norm_final_w, loss_target, m_meta_tokens, m_norm_mix_w, m_w_in, m_dn_conv_w, m_dn_a_log, m_dn_dt_bias, m_dn_norm_w, m_m2_conv_w, m_m2_conv_b, m_m2_a_log, m_m2_dt_bias, m_m2_d, m_m2_norm_w, m_w_out, m_norm_ffn_w, m_ffn_up, m_ffn_conv_w, m_ffn_down, m_norm_final_w, v_meta_tokens, v_norm_mix_w, v_w_in, v_dn_conv_w, v_dn_a_log, v_dn_dt_bias, v_dn_norm_w, v_m2_conv_w, v_m2_conv_b, v_m2_a_log, v_m2_dt_bias, v_m2_d, v_m2_norm_w, v_w_out, v_norm_ffn_w, v_ffn_up, v_ffn_conv_w, v_ffn_down, v_norm_final_w):
    args = tuple(locals().values())
    nw = len(WEIGHT_NAMES)
    wt = dict(zip(WEIGHT_NAMES, args[1:1 + nw]))
    mom = dict(zip(WEIGHT_NAMES, args[2 + nw:2 + 2 * nw]))
    var = dict(zip(WEIGHT_NAMES, args[2 + 2 * nw:2 + 3 * nw]))
    xi, yi, ci = _mesh_pos()
    chip = 2 * xi + yi
    seq, d = x.shape[1], x.shape[2]
    n_meta = wt["meta_tokens"].shape[0]
    n_pad = (-(n_meta + seq)) % ROW_ALIGN
    lay = Layout(d)

    small_local = [wt[k].reshape(wt[k].shape[-2:]) for k in SMALL_SHARDED]
    small_slab = _pack(small_local)
    small_slab = lax.dynamic_update_slice(jnp.zeros((N_CHIPS,) + small_slab.shape, F32), small_slab[None], (chip, 0, 0))
    slabs = {k: cast_into_slot(wt[k], chip, N_CHIPS, name="cast_" + k) for k in BIG}
    w_in_all, small_all = gather_shards([slabs.pop("w_in"), small_slab], name="gather_w_in")
    full = {"w_in": lay.w_in_of_slots(w_in_all)}
    per_chip = [_unpack(small_all[s], [a.shape for a in small_local]) for s in range(N_CHIPS)]
    for idx, k in enumerate(SMALL_SHARDED):
        full[k] = jnp.concatenate([per_chip[s][idx] for s in range(N_CHIPS)], axis=-1)
    for k in SMALL:
        if k not in SMALL_SHARDED:
            full[k] = wt[k]

    h0 = jnp.concatenate([jnp.zeros((n_pad, d), F32), full["meta_tokens"], x[0]], axis=0)
    target = jnp.concatenate([jnp.zeros((n_pad + n_meta, d), F32), loss_target[0]], axis=0)
    loss_local, dh0, g = local_step(h0, target, full, slabs, chip, ci, n_pad=n_pad, n_meta=n_meta)
    grad_x = dh0[n_pad + n_meta:][None]
    g["meta_tokens"] = dh0[n_pad:n_pad + n_meta]
    loss = lax.psum(loss_local, ("x", "y", "c"))

    wholes = [chip_add(*g[k], ci, name="grad_chip_add_" + k) for k in BIG]
    grad = dict(zip(BIG, pair_join(wholes, name="grad_pair_join")))

    small_shapes = [g[k].shape for k in SMALL]
    summed = sum_slots(gather_all(_pack([g[k] for k in SMALL]), name="small_grad_gather"), name="small_grad_sum")
    for k, v in zip(SMALL, _unpack(summed, small_shapes)):
        if k in SMALL_SHARDED:
            width = wt[k].shape[-1]
            v = lax.dynamic_slice_in_dim(v, chip * width, width, axis=v.ndim - 1)
        grad[k] = v

    delta, new_m, new_v = {}, {}, {}
    for k in BIG:
        delta[k], new_m[k], new_v[k] = adamw(wt[k], grad[k], mom[k], var[k], name="adamw_" + k)
    shapes = [wt[k].shape for k in SMALL]
    packed = adamw(_pack([wt[k] for k in SMALL]), _pack([grad[k] for k in SMALL]), _pack([mom[k] for k in SMALL]),
                   _pack([var[k] for k in SMALL]), name="adamw_small")
    for res, slab in zip((delta, new_m, new_v), packed):
        for k, v in zip(SMALL, _unpack(slab, shapes)):
            res[k] = v
    outs = [loss, grad_x]
    for res in (grad, delta, new_m, new_v):
        outs += [res[k].reshape(wt[k].shape) for k in WEIGHT_NAMES]
    return tuple(outs)
```

```python
import functools
import math

import jax
import jax.numpy as jnp
from jax import lax
from jax.experimental import pallas as pl
from jax.experimental.pallas import tpu as pltpu

F32 = jnp.float32
BF16 = jnp.bfloat16

CHUNK = 64
ROW_ALIGN = 128
NORM_EPS = 1e-6
DN_DK = 128
M2_P = 64
M2_N = 128
M2_GROUPS = 4
HEAD_BLOCK = 16
VMEM_LIMIT = 56 * 1024 * 1024

ADAM_LR, ADAM_B1, ADAM_B2, ADAM_EPS, ADAM_WD, ADAM_STEP = 0.001, 0.9, 0.999, 1e-08, 0.01, 10


def _cparams(sem=None):
    return pltpu.CompilerParams(dimension_semantics=sem, vmem_limit_bytes=VMEM_LIMIT)


def _silu(x):
    return x / (1.0 + jnp.exp(-x))


def _sigmoid(x):
    return 1.0 / (1.0 + jnp.exp(-x))


def _softplus(x):
    return jnp.maximum(x, 0.0) + jnp.log(1.0 + jnp.exp(-jnp.abs(x)))


def _tri_masks():
    r = lax.broadcasted_iota(jnp.int32, (CHUNK, CHUNK), 0)
    c = lax.broadcasted_iota(jnp.int32, (CHUNK, CHUNK), 1)
    return (r >= c)[None], (r > c)[None], (r == c)[None]


def _col2row(col, eye):
    return jnp.sum(jnp.where(eye, col, 0.0), axis=1, keepdims=True)


def _tri_sum(v, suffix):
    r = lax.broadcasted_iota(jnp.int32, (v.shape[0], v.shape[0]), 0)
    c = lax.broadcasted_iota(jnp.int32, (v.shape[0], v.shape[0]), 1)
    tri = jnp.where((r <= c) if suffix else (r >= c), 1.0, 0.0).astype(BF16)
    hi = v.astype(BF16)
    rest = v - hi.astype(F32)
    mid = rest.astype(BF16)
    lo = (rest - mid.astype(F32)).astype(BF16)
    dot = lambda part: jnp.dot(tri, part, preferred_element_type=F32)
    return dot(hi) + (dot(mid) + dot(lo))


@jax.custom_vjp
def _running_sum(v):
    return _tri_sum(v, False)


_running_sum.defvjp(lambda v: (_tri_sum(v, False), None), lambda _, g: (_tri_sum(g, True),))


def _bdot(a, b, dims):
    return lax.dot_general(a.astype(BF16), b.astype(BF16), (((dims[0],), (dims[1],)), ((0,), (0,))),
                           preferred_element_type=F32)


def _dot3(a, b, ca, cb):
    def split(v):
        hi = v.astype(BF16)
        return hi, (v - hi.astype(F32)).astype(BF16)

    def dot(p, q):
        return lax.dot_general(p, q, (((ca,), (cb,)), ((0,), (0,))), preferred_element_type=F32)

    (ah, al), (bh, bl) = split(a), split(b)
    return dot(ah, bh) + (dot(ah, bl) + dot(al, bh))


@jax.custom_vjp
def _bmm3(a, b):
    return _dot3(a, b, 2, 1)


def _bmm3_fwd(a, b):
    return _dot3(a, b, 2, 1), (a, b)


def _bmm3_bwd(res, g):
    a, b = res
    return _dot3(g, b, 2, 2), _dot3(a, g, 1, 1)


_bmm3.defvjp(_bmm3_fwd, _bmm3_bwd)


def _unit_lower_inverse(a_mat, eye):
    inv = jnp.where(eye, 1.0, 0.0) - a_mat
    pw = a_mat
    n = 2
    while n < CHUNK:
        pw = _bmm3(pw, pw)
        inv = inv + _bmm3(inv, pw)
        n *= 2
    return inv


@jax.custom_vjp
def _known_inverse(a_mat, inv):
    return inv


def _known_inverse_fwd(a_mat, inv):
    return inv, inv


def _known_inverse_bwd(inv, g):
    return -_dot3(_dot3(inv, g, 1, 1), inv, 2, 2), jnp.zeros_like(inv)


_known_inverse.defvjp(_known_inverse_fwd, _known_inverse_bwd)


def _gdn_step(state, qa, ka, va, z, small, a_log, dt_bias, norm_w, vm, head0, n_heads, inv=None, want_inv=False):
    causal, strict, eye = _tri_masks()
    nh = state.shape[0]
    qa, ka, va = qa * vm, ka * vm, va * vm
    q = qa * lax.rsqrt(jnp.sum(qa * qa, -1, keepdims=True) + NORM_EPS) * (DN_DK ** -0.5)
    k = ka * lax.rsqrt(jnp.sum(ka * ka, -1, keepdims=True) + NORM_EPS)
    g_all = -jnp.exp(a_log) * _softplus(small + dt_bias) * vm[0]
    gcum_all = _running_sum(g_all)
    glast_all = jnp.sum(g_all, axis=0, keepdims=True)
    beta = _take_cols(_sigmoid(small) * vm[0], head0, nh)
    gcum = _take_cols(gcum_all, n_heads + head0, nh)
    egc = _take_cols(jnp.exp(gcum_all), n_heads + head0, nh)
    etail = _take_cols(jnp.exp(glast_all - gcum_all), n_heads + head0, nh)
    elast = _take_cols(jnp.exp(glast_all), n_heads + head0, nh)
    grow = _col2row(gcum, eye)
    decay = jnp.where(causal, jnp.exp(jnp.where(causal, gcum - grow, 0.0)), 0.0)
    kk = _bdot(k, k, (2, 2))
    a_mat = jnp.where(strict, beta * kk * decay, 0.0)
    tinv = _unit_lower_inverse(a_mat, eye) if inv is None else _known_inverse(a_mat, inv)
    u = _bmm3(tinv, va * beta)
    w = _bmm3(tinv, k * (beta * egc))
    v_new = u - _bdot(w, state, (2, 1))
    o_inter = _bdot(q * egc, state, (2, 1))
    qk = _bdot(q, k, (2, 2)) * decay
    o = o_inter + _bdot(qk, v_new, (2, 1))
    new_state = state * elast + _bdot(k * etail, v_new, (1, 1))
    o = o * lax.rsqrt(jnp.mean(o * o, -1, keepdims=True) + NORM_EPS) * norm_w * _silu(z)
    return (new_state, o, tinv) if want_inv else (new_state, o)


def _valid_rows(chunk, n_pad):
    r = chunk * CHUNK + lax.broadcasted_iota(jnp.int32, (1, CHUNK, 1), 1)
    return jnp.where(r >= n_pad, 1.0, 0.0).astype(F32)


def _split_heads(x, n, w):
    return jnp.stack([x[:, i * w:(i + 1) * w] for i in range(n)], axis=0)


def _merge_heads(x):
    return jnp.concatenate([x[i] for i in range(x.shape[0])], axis=-1)


def _run_beside(job, step, n_steps):
    @pl.when(step == 0)
    def _():
        job.begin()

    if hasattr(job, "pass_on"):
        @pl.when(step == (7 * n_steps) // 8)
        def _():
            job.pass_on()

    @pl.when(step == n_steps - 1)
    def _():
        job.end()


def _take_cols(slab, first, n):
    lane = lax.broadcasted_iota(jnp.int32, (n, 1, slab.shape[1]), 2)
    col = lax.broadcasted_iota(jnp.int32, (n, 1, slab.shape[1]), 0)
    return jnp.sum(jnp.where(lane == first + col, slab[None], 0.0), axis=2, keepdims=True)


def gdn_forward(act_qkv, proj, a_log3, dt_bias3, norm_w3, *, n_pad, z_off, small_off, out_cols, gather=()):
    t, w3 = act_qkv.shape
    wdn = w3 // 3
    heads = wdn // DN_DK
    hb = min(HEAD_BLOCK, heads)
    nhb = heads // hb
    nc = t // CHUNK
    bw = hb * DN_DK
    nqb = wdn // bw
    ng = len(gather)

    def body(q_ref, k_ref, v_ref, z_ref, sm_ref, al_ref, db_ref, nw_ref, *rest):
        out_ref, sall_ref, inv_ref = rest[ng:ng + 3]
        st_ref = rest[2 * ng + 3]
        n = pl.program_id(1)
        if ng:
            _run_beside(GatherJob(rest[ng + 3:2 * ng + 3], *rest[2 * ng + 4:]), pl.program_id(0) * nc + n, nhb * nc)

        @pl.when(n == 0)
        def _():
            st_ref[...] = jnp.zeros_like(st_ref)

        state = st_ref[...]
        sall_ref[...] = state
        vm = _valid_rows(n, n_pad)
        head0 = pl.program_id(0) * hb
        new_state, o, tinv = _gdn_step(
            state, _split_heads(q_ref[...], hb, DN_DK), _split_heads(k_ref[...], hb, DN_DK),
            _split_heads(v_ref[...], hb, DN_DK), _split_heads(z_ref[...], hb, DN_DK),
            sm_ref[...], al_ref[...], db_ref[...], nw_ref[...], vm, head0, heads, want_inv=True)
        st_ref[...] = new_state
        inv_ref[...] = tinv
        out_ref[...] = _merge_heads(o).astype(out_ref.dtype)

    par = pl.BlockSpec((1, LANES), lambda h, n: (0, 0))
    return pl.pallas_call(
        body, name="gdn_fwd",
        grid=(nhb, nc),
        in_specs=[pl.BlockSpec((CHUNK, bw), lambda h, n: (n, h)),
                  pl.BlockSpec((CHUNK, bw), lambda h, n: (n, nqb + h)),
                  pl.BlockSpec((CHUNK, bw), lambda h, n: (n, 2 * nqb + h)),
                  pl.BlockSpec((CHUNK, bw), lambda h, n: (n, z_off // bw + h)),
                  pl.BlockSpec((CHUNK, LANES), lambda h, n: (n, small_off // LANES)),
                  par, par,
                  pl.BlockSpec((1, 1, DN_DK), lambda h, n: (0, 0, 0))] + [ANY] * ng,
        out_specs=[pl.BlockSpec((CHUNK, bw), lambda h, n: (n, h)),
                   pl.BlockSpec((None, hb, DN_DK, DN_DK), lambda h, n: (n, h, 0, 0)),
                   pl.BlockSpec((None, hb, CHUNK, CHUNK), lambda h, n: (n, h, 0, 0))] + [ANY] * ng,
        out_shape=[jax.ShapeDtypeStruct((t, out_cols), BF16),
                   jax.ShapeDtypeStruct((nc, heads, DN_DK, DN_DK), F32),
                   jax.ShapeDtypeStruct((nc, heads, CHUNK, CHUNK), F32)]
        + [jax.ShapeDtypeStruct(s.shape, s.dtype) for s in gather],
        input_output_aliases={8 + a: 3 + a for a in range(ng)},
        scratch_shapes=[pltpu.VMEM((hb, DN_DK, DN_DK), F32)] + (GatherJob.sems(ng) if ng else []),
        compiler_params=_cparams(("arbitrary", "arbitrary")),
    )(act_qkv, act_qkv, act_qkv, proj, proj, a_log3, dt_bias3, norm_w3, *gather)


def gdn_backward(act_qkv, proj, a_log3, dt_bias3, norm_w3, s_all, inv_all, d_mixed, *, n_pad, z_off, small_off,
                 exchange=()):
    t, w3 = act_qkv.shape
    wdn = w3 // 3
    heads = wdn // DN_DK
    hb = min(HEAD_BLOCK, heads)
    nhb = heads // hb
    nc = t // CHUNK
    bw = hb * DN_DK
    nqb = wdn // bw
    ne = len(exchange)

    def body(q_ref, k_ref, v_ref, z_ref, sm_ref, al_ref, db_ref, nw_ref, s_ref, inv_ref, do_ref, *rest):
        dq_ref, dk_ref, dv_ref, dz_ref, dsm_ref, dal_ref, ddb_ref, dnw_ref = rest[ne:ne + 8]
        ds_ref = rest[2 * ne + 8]
        i = pl.program_id(1)
        n = nc - 1 - i
        head0 = pl.program_id(0) * hb
        if ne:
            _run_beside(PairExchangeJob(rest[:ne], rest[ne + 8:2 * ne + 8], *rest[2 * ne + 9:]),
                        pl.program_id(0) * nc + i, nhb * nc)

        @pl.when(i == 0)
        def _():
            ds_ref[...] = jnp.zeros_like(ds_ref)
            dal_ref[...] = jnp.zeros_like(dal_ref)
            ddb_ref[...] = jnp.zeros_like(ddb_ref)
            dnw_ref[...] = jnp.zeros_like(dnw_ref)

        vm = _valid_rows(n, n_pad)
        step = functools.partial(_gdn_step, vm=vm, head0=head0, n_heads=heads, inv=inv_ref[...])
        _, vjp = jax.vjp(step, s_ref[...], _split_heads(q_ref[...], hb, DN_DK), _split_heads(k_ref[...], hb, DN_DK),
                         _split_heads(v_ref[...], hb, DN_DK), _split_heads(z_ref[...], hb, DN_DK),
                         sm_ref[...], al_ref[...], db_ref[...], nw_ref[...])
        ds, dq, dk, dv, dz, dsm, dal, ddb, dnw = vjp((ds_ref[...], _split_heads(do_ref[...], hb, DN_DK)))
        ds_ref[...] = ds
        dq_ref[...] = _merge_heads(dq)
        dk_ref[...] = _merge_heads(dk)
        dv_ref[...] = _merge_heads(dv)
        dz_ref[...] = _merge_heads(dz).astype(dz_ref.dtype)
        dsm_ref[...] = dsm
        dal_ref[...] += dal
        ddb_ref[...] += ddb
        dnw_ref[...] += dnw[0]

    rev = lambda n: nc - 1 - n
    par = pl.BlockSpec((1, LANES), lambda h, n: (0, 0))
    dpar = pl.BlockSpec((None, 1, LANES), lambda h, n: (h, 0, 0))
    blk = lambda off: pl.BlockSpec((CHUNK, bw), lambda h, n: (rev(n), off + h))
    return pl.pallas_call(
        body, name="gdn_bwd",
        grid=(nhb, nc),
        in_specs=[blk(0), blk(nqb), blk(2 * nqb), blk(z_off // bw),
                  pl.BlockSpec((CHUNK, LANES), lambda h, n: (rev(n), small_off // LANES)), par, par,
                  pl.BlockSpec((1, 1, DN_DK), lambda h, n: (0, 0, 0)),
                  pl.BlockSpec((None, hb, DN_DK, DN_DK), lambda h, n: (rev(n), h, 0, 0)),
                  pl.BlockSpec((None, hb, CHUNK, CHUNK), lambda h, n: (rev(n), h, 0, 0)),
                  blk(0)] + [ANY] * ne,
        out_specs=[blk(0), blk(0), blk(0), blk(z_off // bw),
                   pl.BlockSpec((None, CHUNK, LANES), lambda h, n: (h, rev(n), 0)), dpar, dpar,
                   pl.BlockSpec((None, 1, DN_DK), lambda h, n: (h, 0, 0))] + [ANY] * ne,
        out_shape=[jax.ShapeDtypeStruct((t, wdn), F32)] * 3
        + [jax.ShapeDtypeStruct((t, proj.shape[1]), BF16),
           jax.ShapeDtypeStruct((nhb, t, LANES), F32),
           jax.ShapeDtypeStruct((nhb, 1, LANES), F32), jax.ShapeDtypeStruct((nhb, 1, LANES), F32),
           jax.ShapeDtypeStruct((nhb, 1, DN_DK), F32)] + [_half_rows(s) for s in exchange],
        scratch_shapes=[pltpu.VMEM((hb, DN_DK, DN_DK), F32)] + (PairExchangeJob.sems(ne) if ne else []),
        compiler_params=_cparams(("arbitrary", "arbitrary")),
    )(act_qkv, act_qkv, act_qkv, proj, proj, a_log3, dt_bias3, norm_w3, s_all, inv_all, d_mixed, *exchange)


def _dot2(a, b, ca, cb):
    return lax.dot_general(a.astype(BF16), b.astype(BF16), (((ca,), (cb,)), ((), ())),
                           preferred_element_type=F32)


def _ssd_step(state, xa, bmat, cmat, z, small, a_log, dt_bias, dskip, norm_w, vm, lane0):
    causal, _, eye = _tri_masks()
    r_heads, p, n_state = state.shape
    gw = r_heads * p
    vm2 = vm[0]
    xa, bmat, cmat = xa * vm2, bmat * vm2, cmat * vm2
    dt_all = _softplus(small + dt_bias) * vm2
    a_all = dt_all * (-jnp.exp(a_log))
    acs_all = _running_sum(a_all)
    alast_all = jnp.sum(a_all, axis=0, keepdims=True)
    dt = _take_cols(dt_all, lane0, r_heads)
    acs = _take_cols(acs_all, lane0, r_heads)
    eacs = _take_cols(jnp.exp(acs_all), lane0, r_heads)
    etail = _take_cols(jnp.exp(alast_all - acs_all), lane0, r_heads)
    elast = _take_cols(jnp.exp(alast_all), lane0, r_heads)
    arow = _col2row(acs, eye)
    lmat = jnp.where(causal, jnp.exp(jnp.where(causal, acs - arow, 0.0)), 0.0)
    hsel = (lax.broadcasted_iota(jnp.int32, (r_heads, 1, gw), 2) // p
            == lax.broadcasted_iota(jnp.int32, (r_heads, 1, gw), 0))

    def spread(col):
        return jnp.sum(jnp.where(hsel, col, 0.0), axis=0)

    xdt = xa * spread(dt)
    cb = _dot2(cmat, bmat, 1, 1)
    m = (cb[None] * lmat).reshape(r_heads * CHUNK, CHUNK)
    yb = _dot2(m, xdt, 1, 0).reshape(r_heads, CHUNK, gw)
    y_diag = jnp.sum(jnp.where(hsel, yb, 0.0), axis=0)
    s2 = state.reshape(gw, n_state)
    y_off = _dot2(cmat, s2, 1, 1) * spread(eacs)
    upd = _dot2(xdt * spread(etail), bmat, 0, 0)
    new_state = state * elast + upd.reshape(r_heads, p, n_state)
    y = y_diag + y_off + xa * spread(dskip)
    y = y * _silu(z)
    y = y * lax.rsqrt(jnp.mean(y * y, -1, keepdims=True) + NORM_EPS) * norm_w
    return new_state, y


def _ssd_dims(act_xbc):
    t, wx = act_xbc.shape
    wm = wx - 2 * M2_GROUPS * M2_N
    gw = wm // M2_GROUPS
    return t, wm, gw, gw // M2_P, wm // M2_P, t // CHUNK


def ssd_forward(act_xbc, proj, a_log3, dt_bias3, dskip3, norm_w, mixed, *, n_pad, z_off, small_off, dt_lane,
                gather=()):
    t, wm, gw, rh, heads, nc = _ssd_dims(act_xbc)
    nb = wm // M2_N
    ob = (mixed.shape[1] - wm) // gw
    ng = len(gather)

    def body(x_ref, b_ref, c_ref, z_ref, dt_ref, al_ref, db_ref, dk_ref, nw_ref, _, *rest):
        out_ref, sall_ref = rest[ng:ng + 2]
        st_ref = rest[2 * ng + 2]
        n = pl.program_id(1)
        if ng:
            _run_beside(GatherJob(rest[ng + 2:2 * ng + 2], *rest[2 * ng + 3:]), pl.program_id(0) * nc + n,
                        M2_GROUPS * nc)

        @pl.when(n == 0)
        def _():
            st_ref[...] = jnp.zeros_like(st_ref)

        state = st_ref[...]
        sall_ref[...] = state
        new_state, y = _ssd_step(state, x_ref[...], b_ref[...], c_ref[...], z_ref[...], dt_ref[...],
                                 al_ref[...], db_ref[...], dk_ref[...], nw_ref[...], _valid_rows(n, n_pad),
                                 dt_lane + pl.program_id(0) * rh)
        st_ref[...] = new_state
        out_ref[...] = y.astype(out_ref.dtype)

    par = pl.BlockSpec((rh, 1, 1), lambda g, n: (g, 0, 0))
    row = pl.BlockSpec((1, LANES), lambda g, n: (0, 0))
    return pl.pallas_call(
        body, name="ssd_fwd",
        grid=(M2_GROUPS, nc),
        in_specs=[pl.BlockSpec((CHUNK, gw), lambda g, n: (n, g)),
                  pl.BlockSpec((CHUNK, M2_N), lambda g, n: (n, nb + g)),
                  pl.BlockSpec((CHUNK, M2_N), lambda g, n: (n, nb + M2_GROUPS + g)),
                  pl.BlockSpec((CHUNK, gw), lambda g, n: (n, z_off // gw + g)),
                  pl.BlockSpec((CHUNK, LANES), lambda g, n: (n, small_off // LANES)),
                  row, row, par,
                  pl.BlockSpec((1, gw), lambda g, n: (0, g)),
                  pl.BlockSpec(memory_space=pl.ANY)] + [ANY] * ng,
        out_specs=[pl.BlockSpec((CHUNK, gw), lambda g, n: (n, ob + g)),
                   pl.BlockSpec((None, rh, M2_P, M2_N), lambda g, n: (n, g, 0, 0))] + [ANY] * ng,
        out_shape=[jax.ShapeDtypeStruct(mixed.shape, mixed.dtype),
                   jax.ShapeDtypeStruct((nc, heads, M2_P, M2_N), F32)]
        + [jax.ShapeDtypeStruct(s.shape, s.dtype) for s in gather],
        input_output_aliases={9: 0, **{10 + a: 2 + a for a in range(ng)}},
        scratch_shapes=[pltpu.VMEM((rh, M2_P, M2_N), F32)] + (GatherJob.sems(ng) if ng else []),
        compiler_params=_cparams(("arbitrary", "arbitrary")),
    )(act_xbc, act_xbc, act_xbc, proj, proj, a_log3, dt_bias3, dskip3, norm_w, mixed, *gather)


def ssd_backward(act_xbc, proj, a_log3, dt_bias3, dskip3, norm_w, s_all, d_mixed, dproj, *, n_pad, z_off,
                 small_off, dt_lane, exchange=()):
    t, wm, gw, rh, heads, nc = _ssd_dims(act_xbc)
    nb = wm // M2_N
    ne = len(exchange)

    def body(x_ref, b_ref, c_ref, z_ref, dt_ref, al_ref, db_ref, dk_ref, nw_ref, s_ref, dy_ref, _, *rest):
        dx_ref, dbm_ref, dcm_ref, dz_ref, ddt_ref, dal_ref, ddb_ref, ddk_ref, dnw_ref = rest[ne:ne + 9]
        ds_ref = rest[2 * ne + 9]
        i = pl.program_id(1)
        n = nc - 1 - i
        if ne:
            _run_beside(ChipExchangeJob(rest[:ne], rest[ne + 9:2 * ne + 9], *rest[2 * ne + 10:]),
                        pl.program_id(0) * nc + i, M2_GROUPS * nc)

        @pl.when(i == 0)
        def _():
            ds_ref[...] = jnp.zeros_like(ds_ref)
            dal_ref[...] = jnp.zeros_like(dal_ref)
            ddb_ref[...] = jnp.zeros_like(ddb_ref)
            ddk_ref[...] = jnp.zeros_like(ddk_ref)
            dnw_ref[...] = jnp.zeros_like(dnw_ref)

        step = functools.partial(_ssd_step, vm=_valid_rows(n, n_pad), lane0=dt_lane + pl.program_id(0) * rh)
        _, vjp = jax.vjp(step, s_ref[...], x_ref[...], b_ref[...], c_ref[...], z_ref[...],
                         dt_ref[...], al_ref[...], db_ref[...], dk_ref[...], nw_ref[...])
        ds, dx, dbm, dcm, dz, ddt, dal, ddb, ddk, dnw = vjp((ds_ref[...], dy_ref[...]))
        ds_ref[...] = ds
        dx_ref[...] = dx
        dbm_ref[...] = dbm
        dcm_ref[...] = dcm
        dz_ref[...] = dz.astype(dz_ref.dtype)
        ddt_ref[...] = ddt
        dal_ref[...] += dal
        ddb_ref[...] += ddb
        ddk_ref[...] += ddk
        dnw_ref[...] += dnw

    rev = lambda n: nc - 1 - n
    par = pl.BlockSpec((rh, 1, 1), lambda g, n: (g, 0, 0))
    wide = lambda off: pl.BlockSpec((CHUNK, gw), lambda g, n: (rev(n), off + g))
    narrow = lambda off: pl.BlockSpec((CHUNK, M2_N), lambda g, n: (rev(n), off + g))
    col = pl.BlockSpec((None, CHUNK, LANES), lambda g, n: (g, rev(n), 0))
    row = pl.BlockSpec((1, LANES), lambda g, n: (0, 0))
    drow = pl.BlockSpec((None, 1, LANES), lambda g, n: (g, 0, 0))
    gn = M2_GROUPS * M2_N
    return pl.pallas_call(
        body, name="ssd_bwd",
        grid=(M2_GROUPS, nc),
        in_specs=[wide(0), narrow(nb), narrow(nb + M2_GROUPS), wide(z_off // gw),
                  pl.BlockSpec((CHUNK, LANES), lambda g, n: (rev(n), small_off // LANES)), row, row, par,
                  pl.BlockSpec((1, gw), lambda g, n: (0, g)),
                  pl.BlockSpec((None, rh, M2_P, M2_N), lambda g, n: (rev(n), g, 0, 0)),
                  wide(M2_GROUPS), pl.BlockSpec(memory_space=pl.ANY)] + [ANY] * ne,
        out_specs=[wide(0), narrow(0), narrow(0), wide(z_off // gw), col, drow, drow, par,
                   pl.BlockSpec((1, gw), lambda g, n: (0, g))] + [ANY] * ne,
        out_shape=[jax.ShapeDtypeStruct((t, wm), F32), jax.ShapeDtypeStruct((t, gn), F32),
                   jax.ShapeDtypeStruct((t, gn), F32), jax.ShapeDtypeStruct(dproj.shape, dproj.dtype),
                   jax.ShapeDtypeStruct((M2_GROUPS, t, LANES), F32),
                   jax.ShapeDtypeStruct((M2_GROUPS, 1, LANES), F32), jax.ShapeDtypeStruct((M2_GROUPS, 1, LANES), F32),
                   jax.ShapeDtypeStruct((heads, 1, 1), F32), jax.ShapeDtypeStruct((1, wm), F32)]
        + [jax.ShapeDtypeStruct((3,) + s.shape[1:], s.dtype) for s in exchange],
        input_output_aliases={11: 3},
        scratch_shapes=[pltpu.VMEM((rh, M2_P, M2_N), F32)] + (ChipExchangeJob.sems(ne) if ne else []),
        compiler_params=_cparams(("arbitrary", "arbitrary")),
    )(act_xbc, act_xbc, act_xbc, proj, proj, a_log3, dt_bias3, dskip3, norm_w, s_all, d_mixed, dproj, *exchange)


SUBLANES = 8
LANES = 128


def _pick(dim, target, align):
    best = None
    for d in range(align, min(dim, target) + 1, align):
        if dim % d == 0:
            best = d
    return dim if best is None else best


def _row_tile(t):
    return _pick(t, 512, 16)


def matmul(a, b, *, mode, out_dtype, name, tm=1056, tn=512, tk=2048, residual=None, out_shards=1,
           a_shards=1, b_shards=1, exchange=()):
    if mode == "tn":
        kd, m = a.shape
        n = b.shape[-1] * b_shards
    else:
        m, kd = a.shape[-2], a.shape[-1] * a_shards
        n = b.shape[1] if mode == "nn" else b.shape[0]
    tm = _pick(m, tm, LANES if mode == "tn" else 16)
    ks = kd // a_shards
    tk = _pick(ks, tk, LANES)
    nkb = ks // tk
    nk = kd // tk
    ns_o, ns_b = n // out_shards, n // b_shards
    tn = _pick(math.gcd(ns_o, ns_b), tn, LANES)
    npb_o, npb_b = ns_o // tn, ns_b // tn
    if mode == "tn":
        a_spec = pl.BlockSpec((tk, tm), lambda i, j, k: (k, i))
    elif a_shards == 1:
        a_spec = pl.BlockSpec((tm, tk), lambda i, j, k: (i, k))
    else:
        a_spec = pl.BlockSpec((None, tm, tk), lambda i, j, k: (k // nkb, i, k % nkb))
    contract = ((1,), (1,)) if mode == "nt" else ((1,), (0,))
    if mode == "nt":
        b_spec = pl.BlockSpec((tn, tk), lambda i, j, k: (j, k))
    elif b_shards == 1:
        b_spec = pl.BlockSpec((tk, tn), lambda i, j, k: (k, j))
    else:
        b_spec = pl.BlockSpec((None, tk, tn), lambda i, j, k: (j // npb_b, k, j % npb_b))
    has_res = residual is not None
    ne = len(exchange)
    grid = (m // tm, n // tn, nk)

    def body(*refs):
        refs = list(refs)
        a_ref, b_ref = refs[:2]
        del refs[:2]
        r_ref = refs.pop(0) if has_res else None
        ex_in = [refs.pop(0) for _ in range(ne)]
        o_ref = refs.pop(0)
        ex_out = [refs.pop(0) for _ in range(ne)]
        at_ref = refs.pop(0) if mode == "tn" else None
        acc_ref = refs.pop(0) if nk > 1 else None
        k = pl.program_id(2)
        if ne:
            step = (pl.program_id(0) * grid[1] + pl.program_id(1)) * nk + k
            _run_beside(ChipExchangeJob(ex_in, ex_out, *refs), step, grid[0] * grid[1] * nk)
        if mode == "tn":
            @pl.when(pl.program_id(1) == 0)
            def _():
                at_ref[k] = jnp.transpose(a_ref[...].astype(F32)).astype(BF16)

            lhs = at_ref[k]
        else:
            lhs = a_ref[...].astype(BF16)
        part = lax.dot_general(lhs, b_ref[...].astype(BF16), (contract, ((), ())), preferred_element_type=F32)

        def finish(total):
            if has_res:
                total = total + r_ref[...]
            o_ref[...] = total.astype(o_ref.dtype)

        if nk == 1:
            finish(part)
        else:
            @pl.when(k == 0)
            def _():
                acc_ref[...] = part

            @pl.when((k > 0) & (k < nk - 1))
            def _():
                acc_ref[...] += part

            @pl.when(k == nk - 1)
            def _():
                finish(acc_ref[...] + part)

    in_specs = [a_spec, b_spec]
    args = [a, b]
    if has_res:
        in_specs.append(pl.BlockSpec((tm, tn), lambda i, j, k: (i, j)))
        args.append(residual)
    if out_shards == 1:
        out_spec = pl.BlockSpec((tm, tn), lambda i, j, k: (i, j))
        out_shape = jax.ShapeDtypeStruct((m, n), out_dtype)
    else:
        out_spec = pl.BlockSpec((None, tm, tn), lambda i, j, k: (j // npb_o, i, j % npb_o))
        out_shape = jax.ShapeDtypeStruct((out_shards, m, ns_o), out_dtype)
    scratch = [pltpu.VMEM((nk, tm, tk), BF16)] if mode == "tn" else []
    if nk > 1:
        scratch.append(pltpu.VMEM((tm, tn), F32))
    if not ne:
        return pl.pallas_call(
            body, name=name, grid=grid,
            in_specs=in_specs, out_specs=out_spec, out_shape=out_shape, scratch_shapes=scratch,
            compiler_params=_cparams(("parallel", "arbitrary", "arbitrary")),
        )(*args)
    return pl.pallas_call(
        body, name=name, grid=grid,
        in_specs=in_specs + [ANY] * ne, out_specs=[out_spec] + [ANY] * ne,
        out_shape=[out_shape] + [jax.ShapeDtypeStruct((3,) + s.shape[1:], s.dtype) for s in exchange],
        scratch_shapes=scratch + ChipExchangeJob.sems(ne),
        compiler_params=_cparams(("arbitrary", "arbitrary", "arbitrary")),
    )(*args, *exchange)


def rmsnorm_forward(x, w, *, name):
    t, d = x.shape
    tm = _row_tile(t)

    def body(x_ref, w_ref, o_ref):
        xv = x_ref[...]
        r = lax.rsqrt(jnp.mean(xv * xv, -1, keepdims=True) + NORM_EPS)
        o_ref[...] = (xv * r * w_ref[...]).astype(o_ref.dtype)

    return pl.pallas_call(
        body, name=name, grid=(t // tm,),
        in_specs=[pl.BlockSpec((tm, d), lambda i: (i, 0)), pl.BlockSpec((1, d), lambda i: (0, 0))],
        out_specs=pl.BlockSpec((tm, d), lambda i: (i, 0)),
        out_shape=jax.ShapeDtypeStruct((t, d), BF16),
        compiler_params=_cparams(("parallel",)),
    )(x, w)


def _rmsnorm_grads(xv, wv, dy):
    r = lax.rsqrt(jnp.mean(xv * xv, -1, keepdims=True) + NORM_EPS)
    xh = xv * r
    g = dy * wv
    dx = r * (g - xh * jnp.mean(g * xh, -1, keepdims=True))
    return dx, jnp.sum(dy * xh, axis=0, keepdims=True)


def rmsnorm_backward(x, w, dy, dres, *, n_pad, name):
    t, d = x.shape
    tm = _row_tile(t)

    def body(x_ref, w_ref, dy_ref, dr_ref, dx_ref, dx16_ref, dw_ref):
        i = pl.program_id(0)

        @pl.when(i == 0)
        def _():
            dw_ref[...] = jnp.zeros_like(dw_ref)

        dx, dw = _rmsnorm_grads(x_ref[...], w_ref[...], dy_ref[...])
        rows = i * tm + lax.broadcasted_iota(jnp.int32, (tm, 1), 0)
        dx = jnp.where(rows >= n_pad, dx + dr_ref[...], 0.0)
        dx_ref[...] = dx
        dx16_ref[...] = dx.astype(BF16)
        dw_ref[...] += dw

    row = pl.BlockSpec((tm, d), lambda i: (i, 0))
    vec = pl.BlockSpec((1, d), lambda i: (0, 0))
    return pl.pallas_call(
        body, name=name, grid=(t // tm,),
        in_specs=[row, vec, row, row], out_specs=[row, row, vec],
        out_shape=[jax.ShapeDtypeStruct((t, d), F32), jax.ShapeDtypeStruct((t, d), BF16),
                   jax.ShapeDtypeStruct((1, d), F32)],
        compiler_params=_cparams(("arbitrary",)),
    )(x, w, dy, dres)


def loss_head(h, w, target, *, n_skip):
    t, d = h.shape
    tm = _row_tile(t)

    def body(x_ref, w_ref, y_ref, loss_ref, dx_ref, dx16_ref, dw_ref):
        i = pl.program_id(0)

        @pl.when(i == 0)
        def _():
            dw_ref[...] = jnp.zeros_like(dw_ref)
            loss_ref[...] = jnp.zeros_like(loss_ref)

        xv, wv = x_ref[...], w_ref[...]
        r = lax.rsqrt(jnp.mean(xv * xv, -1, keepdims=True) + NORM_EPS)
        rows = i * tm + lax.broadcasted_iota(jnp.int32, (tm, 1), 0)
        err = jnp.where(rows >= n_skip, xv * r * wv - y_ref[...], 0.0)
        loss_ref[...] += 0.5 * jnp.sum(jnp.mean(err * err, -1, keepdims=True))
        dx, dw = _rmsnorm_grads(xv, wv, err * (1.0 / d))
        dx_ref[...] = dx
        dx16_ref[...] = dx.astype(BF16)
        dw_ref[...] += dw

    row = pl.BlockSpec((tm, d), lambda i: (i, 0))
    vec = pl.BlockSpec((1, d), lambda i: (0, 0))
    return pl.pallas_call(
        body, name="loss_head", grid=(t // tm,),
        in_specs=[row, vec, row],
        out_specs=[pl.BlockSpec((1, LANES), lambda i: (0, 0)), row, row, vec],
        out_shape=[jax.ShapeDtypeStruct((1, LANES), F32), jax.ShapeDtypeStruct((t, d), F32),
                   jax.ShapeDtypeStruct((t, d), BF16), jax.ShapeDtypeStruct((1, d), F32)],
        compiler_params=_cparams(("arbitrary",)),
    )(h, w, target)


HALO = SUBLANES


STRIP = 64


def _taps(blk, kk, rows):
    return [blk[HALO - (kk - 1) + j:HALO - (kk - 1) + j + rows, :] for j in range(kk)]


def _fir(taps, w):
    acc = w[0:1, :] * taps[0]
    for j in range(1, len(taps)):
        acc = acc + w[j:j + 1, :] * taps[j]
    return acc


def _fir_transposed(dpre, w, rows):
    kk = w.shape[0]
    acc = w[0:1, :] * dpre[kk - 1:kk - 1 + rows, :]
    for j in range(1, kk):
        acc = acc + w[j:j + 1, :] * dpre[kk - 1 - j:kk - 1 - j + rows, :]
    return acc


def _fold8(v):
    return jnp.sum(v.reshape(v.shape[0] // SUBLANES, SUBLANES, v.shape[1]), axis=0)


def _strips(tm, body, init):
    def step(r, carry):
        return body(pl.multiple_of(r * STRIP, STRIP), carry)
    return lax.fori_loop(0, tm // STRIP, step, init)


def _dsilu(p):
    s = _sigmoid(p)
    return s * (1.0 + p * (1.0 - s))


def conv_silu(x, w, b, *, x_off, name):
    t = x.shape[0]
    kk, width = w.shape
    tm = _row_tile(t)
    tc = _pick(width, 512, LANES)
    ob, nh = x_off // tc, tm // HALO

    def body(prev_ref, x_ref, w_ref, b_ref, o_ref, scr):
        i = pl.program_id(1)
        scr[0:HALO, :] = jnp.where(i > 0, prev_ref[...], 0.0)
        scr[HALO:HALO + tm, :] = x_ref[...]
        wv, bv = w_ref[...], b_ref[...]

        def strip(base, carry):
            blk = scr[pl.ds(base, STRIP + HALO), :]
            o_ref[pl.ds(base, STRIP), :] = _silu(_fir(_taps(blk, kk, STRIP), wv) + bv)
            return carry

        _strips(tm, strip, 0)

    return pl.pallas_call(
        body, name=name, grid=(width // tc, t // tm),
        in_specs=[pl.BlockSpec((HALO, tc), lambda j, i: (jnp.maximum(i * nh - 1, 0), ob + j)),
                  pl.BlockSpec((tm, tc), lambda j, i: (i, ob + j)),
                  pl.BlockSpec((kk, tc), lambda j, i: (0, j)),
                  pl.BlockSpec((1, tc), lambda j, i: (0, j))],
        out_specs=pl.BlockSpec((tm, tc), lambda j, i: (i, j)),
        out_shape=jax.ShapeDtypeStruct((t, width), F32),
        scratch_shapes=[pltpu.VMEM((tm + HALO, tc), F32)],
        compiler_params=_cparams(("parallel", "arbitrary")),
    )(x, x, w, b)


def conv_silu_backward(x, w, b, dact, dst, *, x_off, name):
    t = x.shape[0]
    kk, width = w.shape
    tm = _row_tile(t)
    tc = _pick(width, 512, LANES)
    ob, nh, nt = x_off // tc, tm // HALO, t // tm
    last_h = t // HALO - 1

    def body(prev_ref, x_ref, next_ref, w_ref, b_ref, d_ref, dnext_ref, _, dx_ref, dw_ref, db_ref, scr_x, scr_d):
        i = pl.program_id(1)

        @pl.when(i == 0)
        def _():
            dw_ref[...] = jnp.zeros_like(dw_ref)
            db_ref[...] = jnp.zeros_like(db_ref)

        wv, bv = w_ref[...], b_ref[...]
        scr_x[0:HALO, :] = jnp.where(i > 0, prev_ref[...], 0.0)
        scr_x[HALO:HALO + tm, :] = x_ref[...]
        scr_x[HALO + tm:, :] = next_ref[...]
        scr_d[0:tm, :] = d_ref[...]
        scr_d[tm:, :] = jnp.where(i < nt - 1, dnext_ref[...], 0.0)

        def strip(base, carry):
            taps = _taps(scr_x[pl.ds(base, STRIP + 2 * HALO), :], kk, STRIP + HALO)
            dpre = scr_d[pl.ds(base, STRIP + HALO), :] * _dsilu(_fir(taps, wv) + bv)
            dx_ref[pl.ds(base, STRIP), :] = _fir_transposed(dpre, wv, STRIP).astype(dx_ref.dtype)
            d0 = dpre[0:STRIP, :]
            return tuple(c + _fold8(d0 * tap[0:STRIP, :]) for c, tap in zip(carry, taps)) + (carry[kk] + _fold8(d0),)

        sums = _strips(tm, strip, tuple(jnp.zeros((SUBLANES, tc), F32) for _ in range(kk + 1)))
        for j in range(kk):
            dw_ref[j:j + 1, :] += jnp.sum(sums[j], axis=0, keepdims=True)
        db_ref[0:1, :] += jnp.sum(sums[kk], axis=0, keepdims=True)

    nxt = lambda j, i: (jnp.minimum((i + 1) * nh, last_h), j)
    acc = pl.BlockSpec((SUBLANES, tc), lambda j, i: (0, j))
    return pl.pallas_call(
        body, name=name, grid=(width // tc, nt),
        in_specs=[pl.BlockSpec((HALO, tc), lambda j, i: (jnp.maximum(i * nh - 1, 0), ob + j)),
                  pl.BlockSpec((tm, tc), lambda j, i: (i, ob + j)),
                  pl.BlockSpec((HALO, tc), lambda j, i: (jnp.minimum((i + 1) * nh, last_h), ob + j)),
                  pl.BlockSpec((kk, tc), lambda j, i: (0, j)),
                  pl.BlockSpec((1, tc), lambda j, i: (0, j)),
                  pl.BlockSpec((tm, tc), lambda j, i: (i, j)),
                  pl.BlockSpec((HALO, tc), nxt),
                  pl.BlockSpec(memory_space=pl.ANY)],
        out_specs=[pl.BlockSpec((tm, tc), lambda j, i: (i, ob + j)), acc, acc],
        out_shape=[jax.ShapeDtypeStruct(dst.shape, dst.dtype), jax.ShapeDtypeStruct((SUBLANES, width), F32),
                   jax.ShapeDtypeStruct((SUBLANES, width), F32)],
        input_output_aliases={7: 0},
        scratch_shapes=[pltpu.VMEM((tm + 2 * HALO, tc), F32), pltpu.VMEM((tm + HALO, tc), F32)],
        compiler_params=_cparams(("parallel", "arbitrary")),
    )(x, x, x, w, b, dact, dact, dst)


def conv_glu(u, w, *, name):
    _, t, f = u.shape
    kk = w.shape[1]
    tm = _row_tile(t)
    tc = _pick(f, 512, LANES)
    nh = tm // HALO

    def body(prev_ref, x_ref, w_ref, o_ref, scr):
        i = pl.program_id(1)
        scr[:, 0:HALO, :] = jnp.where(i > 0, prev_ref[...], 0.0)
        scr[:, HALO:, :] = x_ref[...]
        wg, wv = w_ref[0], w_ref[1]

        def strip(base, carry):
            gate = _fir(_taps(scr[0, pl.ds(base, STRIP + HALO), :], kk, STRIP), wg)
            val = _fir(_taps(scr[1, pl.ds(base, STRIP + HALO), :], kk, STRIP), wv)
            o_ref[pl.ds(base, STRIP), :] = (_silu(gate) * val).astype(o_ref.dtype)
            return carry

        _strips(tm, strip, 0)

    return pl.pallas_call(
        body, name=name, grid=(f // tc, t // tm),
        in_specs=[pl.BlockSpec((2, HALO, tc), lambda j, i: (0, jnp.maximum(i * nh - 1, 0), j)),
                  pl.BlockSpec((2, tm, tc), lambda j, i: (0, i, j)),
                  pl.BlockSpec((2, kk, tc), lambda j, i: (0, 0, j))],
        out_specs=pl.BlockSpec((tm, tc), lambda j, i: (i, j)),
        out_shape=jax.ShapeDtypeStruct((t, f), BF16),
        scratch_shapes=[pltpu.VMEM((2, tm + HALO, tc), F32)],
        compiler_params=_cparams(("parallel", "arbitrary")),
    )(u, u, w)


def conv_glu_backward(u, w, dact, *, name):
    _, t, f = u.shape
    kk = w.shape[1]
    tm = _row_tile(t)
    tc = _pick(f, 512, LANES)
    nh, nt = tm // HALO, t // tm
    last_h = t // HALO - 1
    ext = tm + HALO
    first = HALO - (kk - 1)

    def body(prev_ref, x_ref, next_ref, w_ref, d_ref, dn_ref, du_ref, dw_ref, scr_x, scr_d):
        i = pl.program_id(1)

        @pl.when(i == 0)
        def _():
            dw_ref[...] = jnp.zeros_like(dw_ref)

        scr_x[:, 0:HALO, :] = jnp.where(i > 0, prev_ref[...], 0.0)
        scr_x[:, HALO:HALO + tm, :] = x_ref[...]
        scr_x[:, HALO + tm:, :] = next_ref[...]
        scr_d[0:tm, :] = d_ref[...]
        scr_d[tm:, :] = jnp.where(i < nt - 1, dn_ref[...], 0.0)
        ws = (w_ref[0], w_ref[1])

        def strip(base, carry):
            taps = [_taps(scr_x[h, pl.ds(base, STRIP + 2 * HALO), :], kk, STRIP + HALO) for h in range(2)]
            gate, val = _fir(taps[0], ws[0]), _fir(taps[1], ws[1])
            dact = scr_d[pl.ds(base, STRIP + HALO), :]
            s = _sigmoid(gate)
            dconv = (dact * val * (s * (1.0 + gate * (1.0 - s))), dact * (gate * s))
            out = []
            for h in range(2):
                du_ref[h, pl.ds(base, STRIP), :] = _fir_transposed(dconv[h], ws[h], STRIP).astype(du_ref.dtype)
                d0 = dconv[h][0:STRIP, :]
                out += [c + _fold8(d0 * tap[0:STRIP, :]) for c, tap in zip(carry[h * kk:(h + 1) * kk], taps[h])]
            return tuple(out)

        sums = _strips(tm, strip, tuple(jnp.zeros((SUBLANES, tc), F32) for _ in range(2 * kk)))
        for h in range(2):
            for j in range(kk):
                dw_ref[h, j:j + 1, :] += jnp.sum(sums[h * kk + j], axis=0, keepdims=True)

    return pl.pallas_call(
        body, name=name, grid=(f // tc, nt),
        in_specs=[pl.BlockSpec((2, HALO, tc), lambda j, i: (0, jnp.maximum(i * nh - 1, 0), j)),
                  pl.BlockSpec((2, tm, tc), lambda j, i: (0, i, j)),
                  pl.BlockSpec((2, HALO, tc), lambda j, i: (0, jnp.minimum((i + 1) * nh, last_h), j)),
                  pl.BlockSpec((2, kk, tc), lambda j, i: (0, 0, j)),
                  pl.BlockSpec((tm, tc), lambda j, i: (i, j)),
                  pl.BlockSpec((HALO, tc), lambda j, i: (jnp.minimum((i + 1) * nh, last_h), j))],
        out_specs=[pl.BlockSpec((2, tm, tc), lambda j, i: (0, i, j)),
                   pl.BlockSpec((2, SUBLANES, tc), lambda j, i: (0, 0, j))],
        out_shape=[jax.ShapeDtypeStruct((2, t, f), BF16), jax.ShapeDtypeStruct((2, SUBLANES, f), F32)],
        scratch_shapes=[pltpu.VMEM((2, tm + 2 * HALO, tc), F32), pltpu.VMEM((ext, tc), F32)],
        compiler_params=_cparams(("parallel", "arbitrary")),
    )(u, u, u, w, dact, dact)


ELEMENTWISE_BLOCK_BYTES = 3 * 1024 * 1024


def _rows_spec(a, tr):
    c = a.shape[-1]
    if a.ndim == 2:
        return pl.BlockSpec((tr, c), lambda i, *_: (i, 0))
    return pl.BlockSpec((None, tr, c), lambda i, *_: (0, i, 0))


def adamw(w, g, m, v, *, name):
    r, c = w.shape[-2:]
    tr = _pick(r, max(SUBLANES, ELEMENTWISE_BLOCK_BYTES // (4 * c) // SUBLANES * SUBLANES), SUBLANES)

    def body(w_ref, g_ref, m_ref, v_ref, d_ref, nm_ref, nv_ref):
        gv = g_ref[...]
        nm = ADAM_B1 * m_ref[...] + (1.0 - ADAM_B1) * gv
        nv = ADAM_B2 * v_ref[...] + (1.0 - ADAM_B2) * (gv * gv)
        m_hat = nm / (1.0 - ADAM_B1 ** ADAM_STEP)
        v_hat = nv / (1.0 - ADAM_B2 ** ADAM_STEP)
        d_ref[...] = -ADAM_LR * (m_hat / (jnp.sqrt(v_hat) + ADAM_EPS) + ADAM_WD * w_ref[...])
        nm_ref[...] = nm
        nv_ref[...] = nv

    shp = jax.ShapeDtypeStruct(w.shape, F32)
    return pl.pallas_call(
        body, name=name, grid=(r // tr,),
        in_specs=[_rows_spec(a, tr) for a in (w, g, m, v)], out_specs=[_rows_spec(w, tr)] * 3, out_shape=[shp] * 3,
        compiler_params=_cparams(("parallel",)),
    )(w, g, m, v)


def cast_into_slot(x, slot, n_slots, *, name):
    r, c = x.shape[-2:]
    tr = _pick(r, max(16, ELEMENTWISE_BLOCK_BYTES // (4 * c) // 16 * 16), 16)

    def body(s_ref, x_ref, o_ref):
        o_ref[...] = x_ref[...].astype(o_ref.dtype)

    grid_spec = pltpu.PrefetchScalarGridSpec(
        num_scalar_prefetch=1, grid=(r // tr,),
        in_specs=[_rows_spec(x, tr)],
        out_specs=pl.BlockSpec((None, tr, c), lambda i, s: (s[0], i, 0)))
    return pl.pallas_call(
        body, name=name, grid_spec=grid_spec,
        out_shape=jax.ShapeDtypeStruct((n_slots, r, c), BF16),
        compiler_params=_cparams(("arbitrary",)),
    )(jnp.reshape(slot, (1,)).astype(jnp.int32), x)


class Layout:
    def __init__(self, d):
        self.d = d
        self.h_dn = d // DN_DK
        self.h_m2 = d // M2_P
        self.gn = M2_GROUPS * M2_N
        self.w_xbc = d + 2 * self.gn
        self.z_off = 3 * d
        self.m2z_off = 4 * d
        self.xbc_off = 5 * d
        self.small_off = 5 * d + self.w_xbc
        self.n_small = 2 * self.h_dn + self.h_m2
        self.p = self.small_off + LANES
        self.p_orig = self.small_off + self.n_small

    def w_in_of_slots(self, slab):
        n_slots, rows, cs = slab.shape
        pieces = []
        for o0, _, ln in sorted(self._segments(), key=lambda seg: seg[1]):
            for s in range(n_slots):
                lo, hi = max(o0, s * cs), min(o0 + ln, (s + 1) * cs)
                if lo < hi:
                    pieces.append(slab[s, :, lo - s * cs:hi - s * cs])
        pieces.append(jnp.zeros((rows, LANES - self.n_small), slab.dtype))
        return jnp.concatenate(pieces, axis=1)

    def _segments(self):
        d, s2, so = self.d, 2 * self.h_dn, self.small_off
        return [(0, 0, 4 * d), (4 * d, so, s2), (4 * d + s2, 4 * d, so - 4 * d),
                (self.p_orig - self.h_m2, so + s2, self.h_m2)]

    def slots_of_w_in(self, w, n_slots):
        cs = self.p_orig // n_slots
        slots = []
        for s in range(n_slots):
            pieces = []
            for o0, k0, ln in self._segments():
                lo, hi = max(o0, s * cs), min(o0 + ln, (s + 1) * cs)
                if lo < hi:
                    pieces.append(w[:, k0 + lo - o0:k0 + hi - o0])
            slots.append(jnp.concatenate(pieces, axis=1))
        return jnp.stack(slots, axis=0)


def _on_lanes(v, first):
    return jnp.pad(v, ((0, 0), (first, LANES - first - v.shape[1])))


def local_step(h0, target, wts, slabs, chip, core, *, n_pad, n_meta):
    t, d = h0.shape
    lay = Layout(d)
    hd, hm = lay.h_dn, lay.h_m2
    zeros_b = jnp.zeros((1, 3 * d), F32)
    dn_al, dn_db = _on_lanes(wts["dn_a_log"], hd), _on_lanes(wts["dn_dt_bias"], hd)
    dn_nw = wts["dn_norm_w"].reshape(1, 1, DN_DK)
    m2_al, m2_db = _on_lanes(wts["m2_a_log"], 2 * hd), _on_lanes(wts["m2_dt_bias"], 2 * hd)
    m2_dk = wts["m2_d"].reshape(hm, 1, 1)

    hn1 = rmsnorm_forward(h0, wts["norm_mix_w"], name="norm_mix")
    proj = matmul(hn1, wts["w_in"], mode="nn", out_dtype=F32, name="in_proj", tn=1920)
    act_qkv = conv_silu(proj, wts["dn_conv_w"], zeros_b, x_off=0, name="dn_conv")
    act_xbc = conv_silu(proj, wts["m2_conv_w"], wts["m2_conv_b"], x_off=lay.xbc_off, name="m2_conv")
    small_at = dict(small_off=lay.small_off)
    dt_at = dict(small_off=lay.small_off, dt_lane=2 * hd)
    mixed, s_dn, inv_dn, ffn_up_all, ffn_down_all = gdn_forward(
        act_qkv, proj, dn_al, dn_db, dn_nw, n_pad=n_pad, z_off=lay.z_off, out_cols=2 * d,
        gather=[slabs["ffn_up"], slabs["ffn_down"]], **small_at)
    mixed, s_m2, w_out_all = ssd_forward(act_xbc, proj, m2_al, m2_db, m2_dk, wts["m2_norm_w"], mixed,
                                         n_pad=n_pad, z_off=lay.m2z_off, gather=[slabs["w_out"]], **dt_at)
    wts = dict(wts, w_out=w_out_all.reshape(-1, d), ffn_down=ffn_down_all.reshape(-1, d),
               ffn_up=jnp.transpose(ffn_up_all, (1, 0, 2)).reshape(d, -1))
    h1 = matmul(mixed, wts["w_out"], mode="nn", out_dtype=F32, name="out_proj", tk=2 * d, residual=h0)
    hn2 = rmsnorm_forward(h1, wts["norm_ffn_w"], name="norm_ffn")
    up = matmul(hn2, wts["ffn_up"], mode="nn", out_dtype=F32, name="ffn_up", tn=1408, out_shards=2)
    kf, f = wts["ffn_conv_w"].shape[0], wts["ffn_conv_w"].shape[1] // 2
    w_glu = jnp.transpose(wts["ffn_conv_w"].reshape(kf, 2, f), (1, 0, 2))
    act = conv_glu(up, w_glu, name="ffn_conv")
    h2 = matmul(act, wts["ffn_down"], mode="nn", out_dtype=F32, name="ffn_down", tk=f // 2, residual=h1)
    loss, dh2, dh2_16, d_nfw = loss_head(h2, wts["norm_final_w"].reshape(1, d), target, n_skip=n_pad + n_meta)

    g = {}
    d_act = matmul(dh2_16, wts["ffn_down"], mode="nt", out_dtype=F32, name="d_ffn_act", tn=1408)
    g["ffn_down"] = matmul(act, dh2_16, mode="tn", out_dtype=F32, name="dw_ffn_down", tm=512, tn=1024, tk=t)
    dup, d_fcw = conv_glu_backward(up, w_glu, d_act, name="d_ffn_conv")
    g["ffn_conv_w"] = jnp.transpose(d_fcw[:, :kf], (1, 0, 2)).reshape(kf, 2 * f)
    dhn2 = matmul(dup, wts["ffn_up"], mode="nt", out_dtype=F32, name="d_norm_ffn_out", tk=f, a_shards=2)
    g["ffn_up"] = matmul(hn2, dup, mode="tn", out_dtype=F32, name="dw_ffn_up", tm=512, tn=1408, tk=t,
                         b_shards=2, out_shards=4)
    dh1, dh1_16, g["norm_ffn_w"] = rmsnorm_backward(h1, wts["norm_ffn_w"], dhn2, dh2, n_pad=n_pad, name="d_norm_ffn")
    d_mixed = matmul(dh1_16, wts["w_out"], mode="nt", out_dtype=F32, name="d_mixed", tn=512)
    g["w_out"] = matmul(mixed, dh1_16, mode="tn", out_dtype=F32, name="dw_out", tm=512, tn=1024, tk=t)

    early = ("w_out", "ffn_up", "ffn_down")
    slots = [g[k] if g[k].ndim == 3 else g[k].reshape(N_CHIPS, -1, g[k].shape[1]) for k in early]
    dq, dk, dv, dproj, dsm_dn, g_al, g_db, g_nw, *from_sibling = gdn_backward(
        act_qkv, proj, dn_al, dn_db, dn_nw, s_dn, inv_dn, d_mixed, n_pad=n_pad, z_off=lay.z_off, exchange=slots,
        **small_at)
    g["dn_a_log"], g["dn_dt_bias"] = (jnp.sum(v, axis=0)[:, hd:2 * hd] for v in (g_al, g_db))
    g["dn_norm_w"] = jnp.sum(g_nw, axis=0)
    p16, own = [], []
    for k, slot, rb in zip(early, slots, from_sibling):
        a, b = pair_add(slot, rb, chip, core, name="grad_pair_add_" + k)
        p16.append(a)
        own.append(b)
    dxs, dbm, dcm, dproj, dsm_m2, g_al, g_db, g_dk, g["m2_norm_w"], *from_chips = ssd_backward(
        act_xbc, proj, m2_al, m2_db, m2_dk, wts["m2_norm_w"], s_m2, d_mixed, dproj,
        n_pad=n_pad, z_off=lay.m2z_off, exchange=p16, **dt_at)
    for k, o, q in zip(early, own, from_chips):
        g[k] = (o, q)
    g["m2_a_log"], g["m2_dt_bias"] = (jnp.sum(v, axis=0)[:, 2 * hd:2 * hd + hm] for v in (g_al, g_db))
    g["m2_d"] = g_dk.reshape(1, hm)

    kc = wts["dn_conv_w"].shape[0]
    dw_parts = []
    for idx, dpart in enumerate((dq, dk, dv)):
        dproj, dw, _ = conv_silu_backward(proj, wts["dn_conv_w"][:, idx * d:(idx + 1) * d], zeros_b[:, :d], dpart,
                                          dproj, x_off=idx * d, name=f"d_dn_conv{idx}")
        dw_parts.append(dw[:kc])
    g["dn_conv_w"] = jnp.concatenate(dw_parts, axis=1)
    dw_parts, db_parts = [], []
    off = 0
    for idx, dpart in enumerate((dxs, dbm, dcm)):
        wd = dpart.shape[1]
        dproj, dw, db = conv_silu_backward(proj, wts["m2_conv_w"][:, off:off + wd], wts["m2_conv_b"][:, off:off + wd],
                                           dpart, dproj, x_off=lay.xbc_off + off, name=f"d_m2_conv{idx}")
        dw_parts.append(dw[:kc])
        db_parts.append(db[:1])
        off += wd
    g["m2_conv_w"] = jnp.concatenate(dw_parts, axis=1)
    g["m2_conv_b"] = jnp.concatenate(db_parts, axis=1)
    dsmall = jnp.sum(dsm_dn, axis=0) + jnp.sum(dsm_m2, axis=0)
    dproj = lax.dynamic_update_slice(dproj, dsmall.astype(BF16), (0, lay.small_off))

    dw_in = matmul(hn1, dproj, mode="tn", out_dtype=F32, name="dw_in", tm=512, tn=896, tk=t)
    w_in_slots = lay.slots_of_w_in(dw_in, N_CHIPS)
    (from_sibling,) = pair_exchange([w_in_slots], name="grad_pair_exchange_w_in")
    p16_w_in, own_w_in = pair_add(w_in_slots, from_sibling, chip, core, name="grad_pair_add_w_in")
    dhn1, from_chips = matmul(dproj, wts["w_in"], mode="nt", out_dtype=F32, name="d_norm_mix_out", tk=lay.p // 3,
                              exchange=[p16_w_in])
    g["w_in"] = (own_w_in, from_chips)
    dh0, _, g["norm_mix_w"] = rmsnorm_backward(h0, wts["norm_mix_w"], dhn1, dh1, n_pad=n_pad, name="d_norm_mix")
    g["norm_final_w"] = d_nfw
    return loss[0, 0], dh0, g


MESH = pl.DeviceIdType.MESH
ANY = pl.BlockSpec(memory_space=pl.ANY)
N_CHIPS = 4
N_DEV = 8


def _mesh_pos():
    return lax.axis_index("x"), lax.axis_index("y"), lax.axis_index("c")


def _other_chips(x, y):
    return [(1 - x, y), (x, 1 - y), (1 - x, 1 - y)]


def _rcopy(src, dst, send_sems, recv_sems, k, to):
    return pltpu.make_async_remote_copy(src_ref=src, dst_ref=dst, send_sem=send_sems.at[k], recv_sem=recv_sems.at[k],
                                        device_id=to, device_id_type=MESH)


class GatherJob:
    def __init__(self, slabs, send_sems, recv_sems):
        self.slabs, self.send, self.recv = slabs, send_sems, recv_sems
        self.x, self.y, self.c = _mesh_pos()

    @staticmethod
    def sems(n):
        return [pltpu.SemaphoreType.DMA((6 * n,)), pltpu.SemaphoreType.DMA((6 * n,))]

    def _pieces(self):
        x, y, c = self.x, self.y, self.c
        for a, slab in enumerate(self.slabs):
            half = slab.shape[1] // 2
            for j, (px, py) in enumerate(_other_chips(x, y)):
                yield a, j, (px, py), slab, pl.ds(c * half, half), pl.ds((1 - c) * half, half)

    def _ici(self, a, j, chip, ref):
        return _rcopy(ref, ref, self.send, self.recv, 6 * a + j, (chip[0], chip[1], self.c))

    def _d2d(self, a, j, ref):
        return _rcopy(ref, ref, self.send, self.recv, 6 * a + 3 + j, (self.x, self.y, 1 - self.c))

    def begin(self):
        for a, j, chip, slab, mine, _ in self._pieces():
            self._ici(a, j, chip, slab.at[2 * self.x + self.y, mine]).start()

    def pass_on(self):
        for a, j, chip, slab, mine, _ in self._pieces():
            landed = slab.at[2 * chip[0] + chip[1], mine]
            self._ici(a, j, chip, landed).wait_recv()
            self._d2d(a, j, landed).start()

    def end(self):
        for a, j, chip, slab, mine, theirs in self._pieces():
            self._d2d(a, j, slab.at[2 * chip[0] + chip[1], theirs]).wait_recv()
        for a, j, chip, slab, mine, _ in self._pieces():
            self._ici(a, j, chip, slab.at[2 * self.x + self.y, mine]).wait_send()
            self._d2d(a, j, slab.at[2 * chip[0] + chip[1], mine]).wait_send()


class ChipExchangeJob:
    def __init__(self, parts, outs, send_sems, recv_sems):
        self.parts, self.outs, self.send, self.recv = parts, outs, send_sems, recv_sems
        self.x, self.y, self.c = _mesh_pos()

    @staticmethod
    def sems(n):
        return [pltpu.SemaphoreType.DMA((3 * n,)), pltpu.SemaphoreType.DMA((3 * n,))]

    def _copies(self):
        for a, (part, out) in enumerate(zip(self.parts, self.outs)):
            for j, (px, py) in enumerate(_other_chips(self.x, self.y)):
                yield _rcopy(part.at[2 * px + py], out.at[j], self.send, self.recv, 3 * a + j, (px, py, self.c))

    def begin(self):
        for cp in self._copies():
            cp.start()

    def end(self):
        for cp in self._copies():
            cp.wait()


class PairExchangeJob:
    def __init__(self, grads, outs, send_sems, recv_sems):
        self.grads, self.outs, self.send, self.recv = grads, outs, send_sems, recv_sems
        self.x, self.y, self.c = _mesh_pos()

    @staticmethod
    def sems(n):
        return [pltpu.SemaphoreType.DMA((n,)), pltpu.SemaphoreType.DMA((n,))]

    def _copies(self):
        for a, (grad, out) in enumerate(zip(self.grads, self.outs)):
            half = grad.shape[1] // 2
            yield _rcopy(grad.at[:, pl.ds((1 - self.c) * half, half), :], out, self.send, self.recv, a,
                         (self.x, self.y, 1 - self.c))

    def begin(self):
        for cp in self._copies():
            cp.start()

    def end(self):
        for cp in self._copies():
            cp.wait()


def gather_shards(slabs, *, name):
    n = len(slabs)

    def body(*refs):
        job = GatherJob(refs[n:2 * n], *refs[2 * n:])
        job.begin()
        job.pass_on()
        job.end()

    return pl.pallas_call(
        body, name=name,
        in_specs=[ANY] * n, out_specs=[ANY] * n,
        out_shape=[jax.ShapeDtypeStruct(s.shape, s.dtype) for s in slabs],
        input_output_aliases={a: a for a in range(n)},
        scratch_shapes=GatherJob.sems(n),
        compiler_params=pltpu.CompilerParams(has_side_effects=True),
    )(*slabs)


def _half_rows(s):
    return jax.ShapeDtypeStruct((s.shape[0], s.shape[1] // 2, s.shape[2]), s.dtype)


def pair_exchange(grads, *, name):
    n = len(grads)

    def body(*refs):
        job = PairExchangeJob(refs[:n], refs[n:2 * n], *refs[2 * n:])
        job.begin()
        job.end()

    return pl.pallas_call(
        body, name=name, in_specs=[ANY] * n, out_specs=[ANY] * n,
        out_shape=[_half_rows(s) for s in grads],
        scratch_shapes=PairExchangeJob.sems(n),
        compiler_params=pltpu.CompilerParams(has_side_effects=True),
    )(*grads)


def pair_join(wholes, *, name):
    n = len(wholes)

    def body(*refs):
        outs = refs[n:2 * n]
        send_sems, recv_sems = refs[2 * n:]
        x, y, c = _mesh_pos()
        cps = []
        for a in range(n):
            half = outs[a].shape[0] // 2
            rows = outs[a].at[pl.ds(c * half, half)]
            cp = _rcopy(rows, rows, send_sems, recv_sems, a, (x, y, 1 - c))
            cp.start()
            cps.append(cp)
        for a, cp in enumerate(cps):
            cp.wait_send()
            half = outs[a].shape[0] // 2
            theirs = outs[a].at[pl.ds((1 - c) * half, half)]
            _rcopy(theirs, theirs, send_sems, recv_sems, a, (x, y, 1 - c)).wait_recv()

    return pl.pallas_call(
        body, name=name, in_specs=[ANY] * n, out_specs=[ANY] * n,
        out_shape=[jax.ShapeDtypeStruct(s.shape, s.dtype) for s in wholes],
        input_output_aliases={a: a for a in range(n)},
        scratch_shapes=[pltpu.SemaphoreType.DMA((n,)), pltpu.SemaphoreType.DMA((n,))],
        compiler_params=pltpu.CompilerParams(has_side_effects=True),
    )(*wholes)


def gather_all(v, *, name):
    def body(in_ref, out_ref, send_sems, recv_sems, local_sem):
        x, y, c = _mesh_pos()
        mine = out_ref.at[4 * x + 2 * y + c]
        lc = pltpu.make_async_copy(in_ref, mine, local_sem)
        lc.start()
        cps = []
        for k in range(1, N_DEV):
            flip = lambda v, bit: 1 - v if (k >> bit) & 1 else v
            cp = _rcopy(in_ref, mine, send_sems, recv_sems, k - 1, (flip(x, 2), flip(y, 1), flip(c, 0)))
            cp.start()
            cps.append(cp)
        for cp in cps:
            cp.wait()
        lc.wait()

    return pl.pallas_call(
        body, name=name, in_specs=[ANY], out_specs=ANY,
        out_shape=jax.ShapeDtypeStruct((N_DEV,) + v.shape, v.dtype),
        scratch_shapes=[pltpu.SemaphoreType.DMA((N_DEV - 1,)), pltpu.SemaphoreType.DMA((N_DEV - 1,)),
                        pltpu.SemaphoreType.DMA(())],
        compiler_params=pltpu.CompilerParams(has_side_effects=True),
    )(v)


def _sum_tile(rows, cols):
    return _pick(rows, max(16, ELEMENTWISE_BLOCK_BYTES // (4 * cols) // 16 * 16), 16)


def pair_add(g, rb, chip, c, *, name):
    _, r, cols = g.shape
    half = r // 2
    tr = _sum_tile(half, cols)
    nrt = half // tr

    def body(s_ref, g_ref, rb_ref, p16_ref, own_ref):
        v = g_ref[...] + rb_ref[...]
        p16_ref[...] = v.astype(p16_ref.dtype)

        @pl.when(pl.program_id(1) == s_ref[0])
        def _():
            own_ref[...] = v

    grid_spec = pltpu.PrefetchScalarGridSpec(
        num_scalar_prefetch=1, grid=(nrt, N_CHIPS),
        in_specs=[pl.BlockSpec((None, tr, cols), lambda i, k, s: (k, s[1] * nrt + i, 0)),
                  pl.BlockSpec((None, tr, cols), lambda i, k, s: (k, i, 0))],
        out_specs=[pl.BlockSpec((None, tr, cols), lambda i, k, s: (k, i, 0)),
                   pl.BlockSpec((tr, cols), lambda i, k, s: (i, 0))])
    return pl.pallas_call(
        body, name=name, grid_spec=grid_spec,
        out_shape=[jax.ShapeDtypeStruct((N_CHIPS, half, cols), BF16), jax.ShapeDtypeStruct((half, cols), F32)],
        compiler_params=_cparams(("arbitrary", "arbitrary")),
    )(jnp.stack([chip, c]).astype(jnp.int32), g, rb)


def chip_add(own, q, c, *, name):
    r, cols = own.shape
    tr = _sum_tile(r, cols)
    nrt = r // tr

    def body(s_ref, own_ref, q_ref, o_ref):
        o_ref[...] = ((own_ref[...] + q_ref[0].astype(F32)) + q_ref[1].astype(F32)) + q_ref[2].astype(F32)

    grid_spec = pltpu.PrefetchScalarGridSpec(
        num_scalar_prefetch=1, grid=(nrt,),
        in_specs=[pl.BlockSpec((tr, cols), lambda i, s: (i, 0)), pl.BlockSpec((3, tr, cols), lambda i, s: (0, i, 0))],
        out_specs=pl.BlockSpec((tr, cols), lambda i, s: (s[0] * nrt + i, 0)))
    return pl.pallas_call(
        body, name=name, grid_spec=grid_spec,
        out_shape=jax.ShapeDtypeStruct((2 * r, cols), F32),
        compiler_params=_cparams(("arbitrary",)),
    )(jnp.reshape(c, (1,)).astype(jnp.int32), own, q)


def sum_slots(v, *, name):
    n, r, cols = v.shape
    tr = _sum_tile(r, cols)

    def body(v_ref, o_ref):
        acc = v_ref[0]
        for k in range(1, n):
            acc = acc + v_ref[k]
        o_ref[...] = acc

    return pl.pallas_call(
        body, name=name, grid=(r // tr,),
        in_specs=[pl.BlockSpec((n, tr, cols), lambda i: (0, i, 0))],
        out_specs=pl.BlockSpec((tr, cols), lambda i: (i, 0)),
        out_shape=jax.ShapeDtypeStruct((r, cols), F32),
        compiler_params=_cparams(("parallel",)),
    )(v)


PACK_ROWS = 16


def _pack(arrays):
    parts = []
    for a in arrays:
        flat = a.reshape(-1).astype(F32)
        size = PACK_ROWS * LANES
        pad = (-flat.shape[0]) % size
        parts.append(jnp.pad(flat, (0, pad)))
    return jnp.concatenate(parts).reshape(-1, LANES)


def _unpack(slab, shapes):
    out, row = [], 0
    for shp in shapes:
        n = 1
        for s in shp:
            n *= s
        rows = -(-n // (PACK_ROWS * LANES)) * PACK_ROWS
        out.append(slab[row:row + rows].reshape(-1)[:n].reshape(shp))
        row += rows
    return out


WEIGHT_NAMES = ("meta_tokens", "norm_mix_w", "w_in", "dn_conv_w", "dn_a_log", "dn_dt_bias", "dn_norm_w", "m2_conv_w",
                "m2_conv_b", "m2_a_log", "m2_dt_bias", "m2_d", "m2_norm_w", "w_out", "norm_ffn_w", "ffn_up",
                "ffn_conv_w", "ffn_down", "norm_final_w")
BIG = ("w_in", "w_out", "ffn_up", "ffn_down")
SMALL_SHARDED = ("meta_tokens", "dn_conv_w", "m2_conv_w", "ffn_conv_w")
SMALL = tuple(n for n in WEIGHT_NAMES if n not in BIG)


def kernel(x, meta_tokens, norm_mix_w, w_in, dn_conv_w, dn_a_log, dn_dt_bias, dn_norm_w, m2_conv_w, m2_conv_b, m2_a_log, m2_dt_bias, m2_d, m2_norm_w, w_out, norm_ffn_w, ffn_up, ffn_conv_w, ffn_down, norm_final_w, loss_target, m_meta_tokens, m_norm_mix_w, m_w_in, m_dn_conv_w, m_dn_a_log, m_dn_dt_bias, m_dn_norm_w, m_m2_conv_w, m_m2_conv_b, m_m2_a_log, m_m2_dt_bias, m_m2_d, m_m2_norm_w, m_w_out, m_norm_ffn_w, m_ffn_up, m_ffn_conv_w, m_ffn_down, m_norm_final_w, v_meta_tokens, v_norm_mix_w, v_w_in, v_dn_conv_w, v_dn_a_log, v_dn_dt_bias, v_dn_norm_w, v_m2_conv_w, v_m2_conv_b, v_m2_a_log, v_m2_dt_bias, v_m2_d, v_m2_norm_w, v_w_out, v_norm_ffn_w, v_ffn_up, v_ffn_conv_w, v_ffn_down, v_norm_final_w):
    args = tuple(locals().values())
    nw = len(WEIGHT_NAMES)
    wt = dict(zip(WEIGHT_NAMES, args[1:1 + nw]))
    mom = dict(zip(WEIGHT_NAMES, args[2 + nw:2 + 2 * nw]))
    var = dict(zip(WEIGHT_NAMES, args[2 + 2 * nw:2 + 3 * nw]))
    xi, yi, ci = _mesh_pos()
    chip = 2 * xi + yi
    seq, d = x.shape[1], x.shape[2]
    n_meta = wt["meta_tokens"].shape[0]
    n_pad = (-(n_meta + seq)) % ROW_ALIGN
    lay = Layout(d)

    small_local = [wt[k].reshape(wt[k].shape[-2:]) for k in SMALL_SHARDED]
    small_slab = _pack(small_local)
    small_slab = lax.dynamic_update_slice(jnp.zeros((N_CHIPS,) + small_slab.shape, F32), small_slab[None], (chip, 0, 0))
    slabs = {k: cast_into_slot(wt[k], chip, N_CHIPS, name="cast_" + k) for k in BIG}
    w_in_all, small_all = gather_shards([slabs.pop("w_in"), small_slab], name="gather_w_in")
    full = {"w_in": lay.w_in_of_slots(w_in_all)}
    per_chip = [_unpack(small_all[s], [a.shape for a in small_local]) for s in range(N_CHIPS)]
    for idx, k in enumerate(SMALL_SHARDED):
        full[k] = jnp.concatenate([per_chip[s][idx] for s in range(N_CHIPS)], axis=-1)
    for k in SMALL:
        if k not in SMALL_SHARDED:
            full[k] = wt[k]

    h0 = jnp.concatenate([jnp.zeros((n_pad, d), F32), full["meta_tokens"], x[0]], axis=0)
    target = jnp.concatenate([jnp.zeros((n_pad + n_meta, d), F32), loss_target[0]], axis=0)
    loss_local, dh0, g = local_step(h0, target, full, slabs, chip, ci, n_pad=n_pad, n_meta=n_meta)
    grad_x = dh0[n_pad + n_meta:][None]
    g["meta_tokens"] = dh0[n_pad:n_pad + n_meta]
    loss = lax.psum(loss_local, ("x", "y", "c"))

    wholes = [chip_add(*g[k], ci, name="grad_chip_add_" + k) for k in BIG]
    grad = dict(zip(BIG, pair_join(wholes, name="grad_pair_join")))

    small_shapes = [g[k].shape for k in SMALL]
    summed = sum_slots(gather_all(_pack([g[k] for k in SMALL]), name="small_grad_gather"), name="small_grad_sum")
    for k, v in zip(SMALL, _unpack(summed, small_shapes)):
        if k in SMALL_SHARDED:
            width = wt[k].shape[-1]
            v = lax.dynamic_slice_in_dim(v, chip * width, width, axis=v.ndim - 1)
        grad[k] = v

    delta, new_m, new_v = {}, {}, {}
    for k in BIG:
        delta[k], new_m[k], new_v[k] = adamw(wt[k], grad[k], mom[k], var[k], name="adamw_" + k)
    shapes = [wt[k].shape for k in SMALL]
    packed = adamw(_pack([wt[k] for k in SMALL]), _pack([grad[k] for k in SMALL]), _pack([mom[k] for k in SMALL]),
                   _pack([var[k] for k in SMALL]), name="adamw_small")
    for res, slab in zip((delta, new_m, new_v), packed):
        for k, v in zip(SMALL, _unpack(slab, shapes)):
            res[k] = v
    outs = [loss, grad_x]
    for res in (grad, delta, new_m, new_v):
        outs += [res[k].reshape(wt[k].shape) for k in WEIGHT_NAMES]
    return tuple(outs)
```
